```python
import jax
import jax.numpy as jnp
from jax import lax
import numpy as np

D_MODEL = 1024
BATCH = 2
SEQ = 16384
DEPTH = 4
DEC_BATCH = 16
DEC_SEQ = 32
PAST_LEN = 2048

CHUNK = 64
RWKV_HEAD = 64
RWKV_HEADS = D_MODEL // RWKV_HEAD
RWKV_W = RWKV_HEADS * RWKV_HEAD
LORA_W = 64
LORA_A = 64
RWKV_SHIFT_W = 3 * RWKV_W + LORA_W + LORA_A
S5_GROUP_CH = 16
S5_W = D_MODEL
S5_GROUPS = S5_W // S5_GROUP_CH
S5_STATE = 64
LRU_W = D_MODEL
LRU_BLOCKS = 16
LRU_BLOCK = LRU_W // LRU_BLOCKS
CONV_W = 4
LRU_C = 8.0
N_BRANCH = 3
RMS_EPS = 1e-6
GN_EPS = 64e-5

OFF_RWKV_G = RWKV_SHIFT_W
OFF_S5 = OFF_RWKV_G + RWKV_W
OFF_S5_G = OFF_S5 + S5_W
OFF_LRU = OFF_S5_G + S5_W
OFF_LRU_G = OFF_LRU + LRU_W
OFF_MERGE = OFF_LRU_G + LRU_W
IN_W = OFF_MERGE + N_BRANCH * D_MODEL

kernel_name = 'hybrid_rwkv7_s5_rglru_stream_step'


def rms_norm(x, g):
    xf = x.astype(jnp.float32)
    y = xf * lax.rsqrt(jnp.mean(xf * xf, axis=-1, keepdims=True) + RMS_EPS)
    return (y * g.astype(jnp.float32)).astype(x.dtype)


def diag_assoc_scan(h0, a, b):
    def combine(left, right):
        a_l, b_l = left
        a_r, b_r = right
        return a_l * a_r, a_r * b_l + b_r
    a_cum, b_cum = lax.associative_scan(combine, (a, b), axis=1)
    return b_cum + a_cum * h0[:, None]


def scan_in_blocks(step, carry0, xs):
    bsz, t = xs[0].shape[:2]
    blk = CHUNK if t % CHUNK == 0 else t
    nblk = t // blk
    xs_b = tuple(jnp.swapaxes(z.reshape((bsz, nblk, blk) + z.shape[2:]), 0, 1) for z in xs)
    carry, ys = lax.scan(step, carry0, xs_b)
    ys = jnp.swapaxes(ys, 0, 1).reshape((bsz, t) + ys.shape[3:])
    return carry, ys


def rwkv7_mixer(p, prev_row, wkv0, mu, w0, w2, a0, a2, k_k, k_a, r_k, ln_g, ln_b):
    bsz, t, _ = p.shape
    f32 = jnp.float32
    pf = p.astype(f32)
    prev = jnp.concatenate([prev_row[:, None].astype(f32), pf[:, :-1]], axis=1)
    pm = pf + (prev - pf) * mu.astype(f32)
    r = pm[..., :RWKV_W]
    k = pm[..., RWKV_W:2 * RWKV_W]
    v = pm[..., 2 * RWKV_W:3 * RWKV_W]
    w_down = pm[..., 3 * RWKV_W:3 * RWKV_W + LORA_W]
    a_down = pm[..., 3 * RWKV_W + LORA_W:]
    w_log = -jax.nn.softplus(-(w0.astype(f32) + jnp.tanh(w_down) @ w2.astype(f32))) - 0.5
    decay = jnp.exp(-jnp.exp(w_log))
    a = jax.nn.sigmoid(a0.astype(f32) + a_down @ a2.astype(f32))

    def heads(z):
        return z.reshape(bsz, t, RWKV_HEADS, RWKV_HEAD)

    kk = heads(k * k_k.astype(f32))
    kk = kk / jnp.maximum(jnp.sqrt(jnp.sum(kk * kk, axis=-1, keepdims=True)), 1e-12)
    k = k * (1.0 + (a - 1.0) * k_a.astype(f32))
    r_h, k_h, v_h, a_h, d_h = heads(r), heads(k), heads(v), heads(a), heads(decay)

    def step(s, inp):
        r_t, d_t, k_t, v_t, kk_t, a_t = inp
        sa = jnp.einsum('bhvk,bhk->bhv', s, -kk_t)
        s = (s * d_t[:, :, None, :] + sa[..., None] * (kk_t * a_t)[:, :, None, :]
             + v_t[..., None] * k_t[:, :, None, :])
        return s, jnp.einsum('bhvk,bhk->bhv', s, r_t)

    xs = tuple(jnp.moveaxis(z, 1, 0) for z in (r_h, d_h, k_h, v_h, kk, a_h))
    s_last, ys = lax.scan(step, wkv0.astype(f32), xs)
    ys = jnp.moveaxis(ys, 0, 1)
    mean = jnp.mean(ys, axis=-1, keepdims=True)
    var = jnp.mean(jnp.square(ys - mean), axis=-1, keepdims=True)
    yn = ((ys - mean) * lax.rsqrt(var + GN_EPS)).reshape(bsz, t, RWKV_W) * ln_g.astype(f32) + ln_b.astype(f32)
    bonus = jnp.sum(r_h * k_h * r_k.astype(f32), axis=-1, keepdims=True) * v_h
    y = yn + bonus.reshape(bsz, t, RWKV_W)
    return y, p[:, -1], s_last


def s5_mixer(u, s_re, s_im, a_re, a_im, log_step, b_re, b_im, c_re, c_im, d_skip, glu_w, glu_b):
    bsz, t, _ = u.shape
    f32 = jnp.float32
    lam = lax.complex(a_re.astype(f32), a_im.astype(f32))
    dt = jnp.exp(log_step.astype(f32))[:, None]
    a_bar = jnp.exp(lam * dt)
    b_bar = ((a_bar - 1.0) / lam)[..., None] * lax.complex(b_re.astype(f32), b_im.astype(f32))
    c_mat = lax.complex(c_re.astype(f32), c_im.astype(f32))
    uf = u.astype(f32)
    ug = uf.reshape(bsz, t, S5_GROUPS, S5_GROUP_CH)

    def step(h0, blk):
        (u_blk,) = blk
        bu = jnp.einsum('blgi,gpi->blgp', u_blk.astype(b_bar.dtype), b_bar)
        h = diag_assoc_scan(h0, jnp.broadcast_to(a_bar, bu.shape), bu)
        return h[:, -1], jnp.einsum('blgp,gop->blgo', h, c_mat).real

    h_last, y = scan_in_blocks(step, lax.complex(s_re.astype(f32), s_im.astype(f32)), (ug,))
    y = y.reshape(bsz, t, S5_W) + d_skip.astype(f32) * uf
    y = jax.nn.gelu(y)
    y = y * jax.nn.sigmoid(y @ glu_w.astype(f32) + glu_b.astype(f32))
    return y, h_last.real, h_last.imag


def rglru_mixer(xl, conv_buf, h0, conv_w, conv_b, wa, ba, wx, bx, lam):
    bsz, t, _ = xl.shape
    f32 = jnp.float32
    xp = jnp.concatenate([conv_buf.astype(xl.dtype), xl], axis=1)
    xpf = xp.astype(f32)
    xc = conv_b.astype(f32) + xpf[:, CONV_W - 1:] * conv_w[CONV_W - 1].astype(f32)
    for j in range(CONV_W - 1):
        xc = xc + xpf[:, j:j + t] * conv_w[j].astype(f32)
    xb = xc.reshape(bsz, t, LRU_BLOCKS, LRU_BLOCK)
    gate_r = jax.nn.sigmoid(jnp.einsum('btnd,nde->btne', xb, wa.astype(f32)).reshape(bsz, t, LRU_W) + ba.astype(f32))
    gate_i = jax.nn.sigmoid(jnp.einsum('btnd,nde->btne', xb, wx.astype(f32)).reshape(bsz, t, LRU_W) + bx.astype(f32))
    log_a = -LRU_C * gate_r * jax.nn.softplus(-lam.astype(f32))
    a = jnp.exp(log_a)
    b = jnp.sqrt(-jnp.expm1(2.0 * log_a)) * (gate_i * xc)

    def step(h, blk):
        a_b, b_b = blk
        hs = diag_assoc_scan(h, a_b, b_b)
        return hs[:, -1], hs

    h_last, hs = scan_in_blocks(step, h0.astype(f32), (a, b))
    return hs, h_last, xp[:, -(CONV_W - 1):]


def zero_state(bsz, dtype):
    f32 = jnp.float32
    return (jnp.zeros((bsz, RWKV_SHIFT_W), dtype),
            jnp.zeros((bsz, RWKV_HEADS, RWKV_HEAD, RWKV_HEAD), f32),
            jnp.zeros((bsz, S5_GROUPS, S5_STATE), f32),
            jnp.zeros((bsz, S5_GROUPS, S5_STATE), f32),
            jnp.zeros((bsz, LRU_W), f32),
            jnp.zeros((bsz, CONV_W - 1, LRU_W), dtype))


def trunk_layer(x, c, state, common, rwkv_p, s5_p, lru_p):
    shift_row, wkv, s_re, s_im, lru_h, lru_conv = state
    ada_w, ada_b, g_pre, g_post, w_in, w_out = common
    mod = jax.nn.silu(c) @ ada_w + ada_b
    shift, scale, gate = jnp.split(mod, 3, axis=-1)
    h = rms_norm(x, g_pre) * (1.0 + scale[:, None]) + shift[:, None]
    p = h @ w_in
    y_r, shift_new, wkv_new = rwkv7_mixer(p[..., :OFF_RWKV_G], shift_row, wkv, *rwkv_p)
    y_s, s_re_new, s_im_new = s5_mixer(p[..., OFF_S5:OFF_S5_G], s_re, s_im, *s5_p)
    y_l, h_new, conv_new = rglru_mixer(p[..., OFF_LRU:OFF_LRU_G], lru_conv, lru_h, *lru_p)
    z_r = jax.nn.silu(p[..., OFF_RWKV_G:OFF_S5].astype(jnp.float32))
    z_s = jax.nn.silu(p[..., OFF_S5_G:OFF_LRU].astype(jnp.float32))
    z_l = jax.nn.silu(p[..., OFF_LRU_G:OFF_MERGE].astype(jnp.float32))
    m = jax.nn.sigmoid(p[..., OFF_MERGE:].astype(jnp.float32))
    merged = (m[..., :D_MODEL] * (y_r * z_r) + m[..., D_MODEL:2 * D_MODEL] * (y_s * z_s)
              + m[..., 2 * D_MODEL:] * (y_l * z_l)).astype(x.dtype)
    out = rms_norm(merged @ w_out, g_post)
    x = x + gate[:, None] * out
    return x, (shift_new, wkv_new, s_re_new, s_im_new, h_new, conv_new)


def setup_inputs(seed: int = 0) -> dict:
    key = jax.random.key(seed)
    ks = iter(jax.random.split(key, 64))
    f32 = jnp.float32

    def nrm(shape, scale):
        return scale * jax.random.normal(next(ks), shape, f32)

    def unif(shape, lo, hi):
        return jax.random.uniform(next(ks), shape, f32, lo, hi)

    L = DEPTH
    d_inv = D_MODEL ** -0.5
    ramp = jnp.linspace(0.0, 1.0, RWKV_W)
    a_c = unif((L, LRU_W), 0.9, 0.999)
    a_base = a_c ** (1.0 / LRU_C)
    return {
        'x_prompt': nrm((BATCH, SEQ, D_MODEL), 1.0),
        'x_sample': nrm((DEC_BATCH, DEC_SEQ, D_MODEL), 1.0),
        'state_rwkv_shift': nrm((L, DEC_BATCH, RWKV_SHIFT_W), 1.0),
        'state_rwkv_wkv': nrm((L, DEC_BATCH, RWKV_HEADS, RWKV_HEAD, RWKV_HEAD), 0.5),
        'state_s5_re': nrm((L, DEC_BATCH, S5_GROUPS, S5_STATE), 0.5),
        'state_s5_im': nrm((L, DEC_BATCH, S5_GROUPS, S5_STATE), 0.5),
        'state_lru_h': nrm((L, DEC_BATCH, LRU_W), 0.5),
        'state_lru_conv': nrm((L, DEC_BATCH, CONV_W - 1, LRU_W), 1.0),
        'c_prompt': nrm((BATCH, D_MODEL), 1.0),
        'c_sample': nrm((DEC_BATCH, D_MODEL), 1.0),
        'ada_w': nrm((L, D_MODEL, 3 * D_MODEL), d_inv),
        'ada_b': nrm((L, 3 * D_MODEL), 0.02),
        'norm_pre': 1.0 + nrm((L, D_MODEL), 0.02),
        'norm_post': 1.0 + nrm((L, D_MODEL), 0.02),
        'w_in': nrm((L, D_MODEL, IN_W), d_inv),
        'w_out': nrm((L, D_MODEL, D_MODEL), d_inv),
        'rwkv_mu': unif((L, RWKV_SHIFT_W), 0.0, 1.0),
        'rwkv_w0': (-6.0 + 5.0 * ramp ** 0.85)[None] + nrm((L, RWKV_W), 0.1),
        'rwkv_w2': nrm((L, LORA_W, RWKV_W), 0.1 * LORA_W ** -0.5),
        'rwkv_a0': nrm((L, RWKV_W), 0.1),
        'rwkv_a2': nrm((L, LORA_A, RWKV_W), 0.1 * LORA_A ** -0.5),
        'rwkv_k_k': 0.85 + nrm((L, RWKV_W), 0.02),
        'rwkv_k_a': 1.0 + nrm((L, RWKV_W), 0.02),
        'rwkv_r_k': nrm((L, RWKV_HEADS, RWKV_HEAD), 0.1),
        'rwkv_ln_g': 1.0 + nrm((L, RWKV_W), 0.02),
        'rwkv_ln_b': nrm((L, RWKV_W), 0.02),
        's5_a_re': -0.5 + nrm((L, S5_GROUPS, S5_STATE), 0.01),
        's5_a_im': jnp.pi * jnp.arange(S5_STATE, dtype=f32)[None, None] + nrm((L, S5_GROUPS, S5_STATE), 0.01),
        's5_log_step': unif((L, S5_GROUPS), float(np.log(1e-3)), float(np.log(1e-1))),
        's5_b_re': nrm((L, S5_GROUPS, S5_STATE, S5_GROUP_CH), (2 * S5_GROUP_CH) ** -0.5),
        's5_b_im': nrm((L, S5_GROUPS, S5_STATE, S5_GROUP_CH), (2 * S5_GROUP_CH) ** -0.5),
        's5_c_re': nrm((L, S5_GROUPS, S5_GROUP_CH, S5_STATE), (2 * S5_STATE) ** -0.5),
        's5_c_im': nrm((L, S5_GROUPS, S5_GROUP_CH, S5_STATE), (2 * S5_STATE) ** -0.5),
        's5_d': nrm((L, S5_W), 1.0),
        's5_glu_w': nrm((L, S5_W, S5_W), S5_W ** -0.5),
        's5_glu_b': nrm((L, S5_W), 0.02),
        'lru_conv_w': nrm((L, CONV_W, LRU_W), CONV_W ** -0.5),
        'lru_conv_b': nrm((L, LRU_W), 0.02),
        'lru_wa': nrm((L, LRU_BLOCKS, LRU_BLOCK, LRU_BLOCK), LRU_BLOCK ** -0.5),
        'lru_ba': nrm((L, LRU_W), 0.02),
        'lru_wx': nrm((L, LRU_BLOCKS, LRU_BLOCK, LRU_BLOCK), LRU_BLOCK ** -0.5),
        'lru_bx': nrm((L, LRU_W), 0.02),
        'lru_lambda': jnp.log(a_base) - jnp.log1p(-a_base),
    }


def reference(x_prompt, x_sample, state_rwkv_shift, state_rwkv_wkv, state_s5_re, state_s5_im,
              state_lru_h, state_lru_conv, c_prompt, c_sample, ada_w, ada_b, norm_pre, norm_post,
              w_in, w_out, rwkv_mu, rwkv_w0, rwkv_w2, rwkv_a0, rwkv_a2, rwkv_k_k, rwkv_k_a, rwkv_r_k,
              rwkv_ln_g, rwkv_ln_b, s5_a_re, s5_a_im, s5_log_step, s5_b_re, s5_b_im, s5_c_re, s5_c_im,
              s5_d, s5_glu_w, s5_glu_b, lru_conv_w, lru_conv_b, lru_wa, lru_ba, lru_wx, lru_bx, lru_lambda):
    assert x_sample.shape[1] <= CHUNK
    xp, xs = x_prompt, x_sample
    new_p = [[] for _ in range(6)]
    new_s = [[] for _ in range(6)]
    for l in range(DEPTH):
        common = (ada_w[l], ada_b[l], norm_pre[l], norm_post[l], w_in[l], w_out[l])
        rwkv_p = (rwkv_mu[l], rwkv_w0[l], rwkv_w2[l], rwkv_a0[l], rwkv_a2[l], rwkv_k_k[l], rwkv_k_a[l],
                  rwkv_r_k[l], rwkv_ln_g[l], rwkv_ln_b[l])
        s5_p = (s5_a_re[l], s5_a_im[l], s5_log_step[l], s5_b_re[l], s5_b_im[l], s5_c_re[l], s5_c_im[l],
                s5_d[l], s5_glu_w[l], s5_glu_b[l])
        lru_p = (lru_conv_w[l], lru_conv_b[l], lru_wa[l], lru_ba[l], lru_wx[l], lru_bx[l], lru_lambda[l])
        xp, st_p = trunk_layer(xp, c_prompt, zero_state(xp.shape[0], xp.dtype), common, rwkv_p, s5_p, lru_p)
        st_in = (state_rwkv_shift[l], state_rwkv_wkv[l], state_s5_re[l], state_s5_im[l],
                 state_lru_h[l], state_lru_conv[l])
        xs, st_s = trunk_layer(xs, c_sample, st_in, common, rwkv_p, s5_p, lru_p)
        for i in range(6):
            new_p[i].append(st_p[i])
            new_s[i].append(st_s[i])
    sp = [jnp.stack(z, axis=0) for z in new_p]
    ss = [jnp.stack(z, axis=0) for z in new_s]
    return (xp, xs, sp[0], sp[1], sp[2], sp[3], sp[4], sp[5], ss[0], ss[1], ss[2], ss[3], ss[4], ss[5])
```

```python
import functools

import jax
import jax.numpy as jnp
from jax import lax
from jax.experimental import pallas as pl
from jax.experimental.pallas import tpu as pltpu

F32 = jnp.float32
BF16 = jnp.bfloat16
HIGHEST = lax.Precision.HIGHEST

D_MODEL = 1024
HEAD = 64
HEADS = D_MODEL // HEAD
PAIRS = HEADS // 2
LORA = 64
SHIFT_W = 3 * D_MODEL + 2 * LORA
S5_STATE_W = 64 * 64
S5_SETS = 4
CONV_W = 4
LRU_C = 8.0
RMS_EPS = 1e-6
GN_EPS = 64e-5
RWKV_CHUNK = 64
SUBLANES = 8
LANES = 128
VMEM_LIMIT = 56 * 1024 * 1024


def _silu(x):
    return x * jax.nn.sigmoid(x)


def _softplus(x):
    return jnp.maximum(x, 0.0) + jnp.log1p(jnp.exp(-jnp.abs(x)))


def _norm_mod(x, g, mod):
    ms = jnp.mean(x * x, axis=-1, keepdims=True)
    y = (x * lax.rsqrt(ms + RMS_EPS)) * g
    return y * (1.0 + mod[1:2, :]) + mod[0:1, :]


def _dot(a, b):
    return jnp.dot(a, b, preferred_element_type=F32)


def _dot_hi(a, b):
    return jnp.dot(a, b, preferred_element_type=F32, precision=HIGHEST)


def _split2(x):
    hi = x.astype(BF16)
    lo = (x - hi.astype(F32)).astype(BF16)
    return hi, lo


def _head_sum(x, ones_blk):
    outs = []
    for s in range(D_MODEL // 256):
        hi, lo = _split2(x[:, 256 * s:256 * (s + 1)])
        outs.append(_dot(hi, ones_blk) + _dot(lo, ones_blk))
    return jnp.concatenate(outs, axis=1)


def _block_ones(n, blk):
    ri = lax.broadcasted_iota(jnp.int32, (n, n), 0)
    ci = lax.broadcasted_iota(jnp.int32, (n, n), 1)
    sh = blk.bit_length() - 1
    return ((ri >> sh) == (ci >> sh)).astype(F32).astype(BF16)


def _mod_kernel(c_ref, w_ref, b_ref, o_ref):
    s = _silu(c_ref[...])
    o_ref[...] = _dot_hi(s, w_ref[...]) + b_ref[...]


def _modulation(c_all, ada_w, ada_b):
    depth = ada_w.shape[0]
    rows = c_all.shape[0]
    return pl.pallas_call(
        _mod_kernel,
        grid=(depth, 3),
        in_specs=[
            pl.BlockSpec((rows, D_MODEL), lambda l, j: (0, 0)),
            pl.BlockSpec((None, D_MODEL, D_MODEL), lambda l, j: (l, 0, j)),
            pl.BlockSpec((None, 1, D_MODEL), lambda l, j: (l, 0, j)),
        ],
        out_specs=pl.BlockSpec((None, rows, D_MODEL), lambda l, j: (l, 0, j)),
        out_shape=jax.ShapeDtypeStruct((depth, rows, 3 * D_MODEL), F32),
        name="adaln_mod",
    )(c_all, ada_w, ada_b.reshape(depth, 1, 3 * D_MODEL))


def _s5_prep_kernel(are_ref, aim_ref, ls_ref, bre_ref, bim_ref, pwr_ref, pwi_ref, bbr_ref, bbi_ref):
    are = are_ref[...]
    aim = aim_ref[...]
    dt = jnp.exp(ls_ref[...])
    for j in range(SUBLANES):
        mag = jnp.exp(are * dt * (j + 1.0))
        th = aim * dt * (j + 1.0)
        pwr_ref[j] = mag * jnp.cos(th)
        pwi_ref[j] = mag * jnp.sin(th)
    nr = pwr_ref[0] - 1.0
    ni = pwi_ref[0]
    den = are * are + aim * aim
    cr = (nr * are + ni * aim) / den
    ci = (ni * are - nr * aim) / den
    bre = bre_ref[...]
    bim = bim_ref[...]
    bbr_ref[...] = cr[:, None, :] * bre - ci[:, None, :] * bim
    bbi_ref[...] = cr[:, None, :] * bim + ci[:, None, :] * bre


def _s5_prep(a_re, a_im, log_step, b_re, b_im):
    g, p = a_re.shape
    i = b_re.shape[-1]
    return pl.pallas_call(
        _s5_prep_kernel,
        out_shape=(
            jax.ShapeDtypeStruct((SUBLANES, g, p), F32),
            jax.ShapeDtypeStruct((SUBLANES, g, p), F32),
            jax.ShapeDtypeStruct((g, i, p), F32),
            jax.ShapeDtypeStruct((g, i, p), F32),
        ),
        name="s5_discretise",
    )(a_re, a_im, log_step.reshape(g, 1), jnp.swapaxes(b_re, 1, 2), jnp.swapaxes(b_im, 1, 2))


def _rwkv_kernel(x_ref, mod_ref, g_ref, wr_ref, wg_ref, mu_ref, w0_ref, w2a_ref, a0_ref, kk_ref, ka_ref,
                 rk_ref, lng_ref, lnb_ref, shift0_ref, wkv0_ref,
                 yz_ref, shift_out_ref, wkv_out_ref,
                 carry_ref, st_ref, at_ref, rt_ref, bt_ref, kt_ref, bh_ref, kh_ref, v_ref, pl_ref, y_ref,
                 *, tile, chunk, t_valid):
    t = pl.program_id(1)
    n_chunks = tile // chunk
    two = 2 * chunk

    @pl.when(t == 0)
    def _():
        carry_ref[...] = shift0_ref[...]
        st_ref[...] = wkv0_ref[...]

    h = _norm_mod(x_ref[...], g_ref[...], mod_ref[...]).astype(BF16)
    p = _dot(h, wr_ref[...])
    zg = _silu(_dot(h, wg_ref[...]))

    rows = lax.broadcasted_iota(jnp.int32, (tile, 1), 0)
    prev = jnp.where(rows == 0, carry_ref[...], pltpu.roll(p, 1, 0))
    last = (t_valid - 1) % tile
    carry_ref[...] = p[last:last + 1, :]
    pm = p + (prev - p) * mu_ref[...]

    r = pm[:, 0:D_MODEL]
    k = pm[:, D_MODEL:2 * D_MODEL]
    v = pm[:, 2 * D_MODEL:3 * D_MODEL]
    wa = pm[:, 3 * D_MODEL:SHIFT_W]
    lane = lax.broadcasted_iota(jnp.int32, (1, LANES), 1)
    wa = jnp.where(lane < LORA, jnp.tanh(wa), wa)
    lora = _dot(wa.astype(BF16), w2a_ref[...])
    w_log = -_softplus(-(w0_ref[...] + lora[:, :D_MODEL])) - 0.5
    logd = -jnp.exp(w_log)
    a_sig = jax.nn.sigmoid(a0_ref[...] + lora[:, D_MODEL:])

    ones_blk = _block_ones(256, HEAD)
    kk = k * kk_ref[...]
    kk = kk / jnp.maximum(jnp.sqrt(_head_sum(kk * kk, ones_blk)), 1e-12)
    k2 = k * (1.0 + (a_sig - 1.0) * ka_ref[...])
    bvec = kk * a_sig
    if t_valid % tile != 0:
        ok = rows < t_valid
        logd = jnp.where(ok, logd, 0.0)
        bvec = jnp.where(ok, bvec, 0.0)
        k2 = jnp.where(ok, k2, 0.0)

    rmod = rows & (chunk - 1)
    cum = logd
    s = 1
    while s < chunk:
        cum = cum + jnp.where(rmod >= s, pltpu.roll(cum, s, 0), 0.0)
        s *= 2
    cum_last = jnp.concatenate(
        [jnp.broadcast_to(cum[(j + 1) * chunk - 1:(j + 1) * chunk, :], (chunk, D_MODEL)) for j in range(n_chunks)],
        axis=0)
    p_inc = jnp.exp(cum)
    p_inv = jnp.exp(-cum)
    p_exc = jnp.exp(cum - logd)
    p_end = jnp.exp(cum_last - cum)

    def to_pairs(ref, val):
        for q in range(PAIRS):
            ref[q] = val[:, LANES * q:LANES * (q + 1)]

    to_pairs(at_ref, -kk * p_exc)
    to_pairs(rt_ref, r * p_inc)
    to_pairs(bt_ref, bvec * p_inv)
    to_pairs(kt_ref, k2 * p_inv)
    to_pairs(bh_ref, bvec * p_end)
    to_pairs(kh_ref, k2 * p_end)
    to_pairs(v_ref, v)
    for j in range(n_chunks):
        row = jnp.exp(cum[(j + 1) * chunk - 1:(j + 1) * chunk, :])
        for q in range(PAIRS):
            pl_ref[q, j] = jnp.broadcast_to(row[:, LANES * q:LANES * (q + 1)], (SUBLANES, LANES))

    m_a = (lane < HEAD).astype(F32)
    m_b = 1.0 - m_a
    ri = lax.broadcasted_iota(jnp.int32, (two, two), 0)
    ci = lax.broadcasted_iota(jnp.int32, (two, two), 1)
    csh = chunk.bit_length() - 1
    same_head = (ri >> csh) == (ci >> csh)
    strict = jnp.logical_and(same_head, (ri & (chunk - 1)) > (ci & (chunk - 1)))
    incl = jnp.logical_and(same_head, (ri & (chunk - 1)) >= (ci & (chunk - 1)))
    eye = ri == ci
    nt_dims = (((1,), (1,)), ((), ()))
    tn_dims = (((0,), (0,)), ((), ()))

    def stack(x):
        return jnp.concatenate([x * m_a, x * m_b], axis=0)

    def pair_body(q, carry):
        st = st_ref[q]
        for j in range(n_chunks):
            sl = pl.ds(j * chunk, chunk)
            atm = stack(at_ref[q, sl, :])
            rtm = stack(rt_ref[q, sl, :])
            bt = bt_ref[q, sl, :]
            kt = kt_ref[q, sl, :]
            vm = stack(v_ref[q, sl, :])
            o1 = lax.dot_general(jnp.concatenate([atm, rtm], axis=0), jnp.concatenate([bt, bt, kt, kt], axis=0),
                                 nt_dims, preferred_element_type=F32, precision=HIGHEST)
            n_ab = jnp.where(strict, o1[:two, :two], 0.0)
            a_ak = jnp.where(strict, o1[:two, two:], 0.0)
            a_rb = jnp.where(incl, o1[two:, :two], 0.0)
            a_rk = jnp.where(incl, o1[two:, two:], 0.0)
            z = jnp.concatenate([atm, _dot_hi(a_ak, vm)], axis=1)
            npow = n_ab
            span = 1
            while span < chunk:
                z = z + _dot_hi(npow, z)
                span *= 2
                if span < chunk:
                    npow = _dot_hi(npow, npow)
            rhs2 = jnp.concatenate([z, jnp.concatenate([jnp.zeros_like(vm), vm], axis=1)], axis=0)
            top = lax.dot_general(jnp.concatenate([stack(bh_ref[q, sl, :]), stack(kh_ref[q, sl, :])], axis=0), rhs2,
                                  tn_dims, preferred_element_type=F32, precision=HIGHEST)
            bot = _dot_hi(jnp.concatenate([a_rb, a_rk], axis=1), rhs2)
            g_mat = top[:, :two] + jnp.where(eye, jnp.broadcast_to(pl_ref[q, j][0:1, :], (two, two)), 0.0)
            q_eff = bot[:, :two] + rtm
            ym = _dot_hi(q_eff, st) + bot[:, two:]
            y_ref[q, sl, :] = ym[:chunk, :] + ym[chunk:, :]
            st = _dot_hi(g_mat, st) + top[:, two:]
        st_ref[q] = st
        return carry

    lax.fori_loop(0, PAIRS, pair_body, 0)

    ys = jnp.concatenate([y_ref[q] for q in range(PAIRS)], axis=1)
    mean = _head_sum(ys, ones_blk) * (1.0 / HEAD)
    yc = ys - mean
    var = _head_sum(yc * yc, ones_blk) * (1.0 / HEAD)
    yn = yc * lax.rsqrt(var + GN_EPS) * lng_ref[...] + lnb_ref[...]
    bonus = _head_sum(r * k2 * rk_ref[...], ones_blk) * v
    yz_ref[...] = (yn + bonus) * zg

    @pl.when(t == pl.num_programs(1) - 1)
    def _():
        shift_out_ref[...] = carry_ref[...]
        wkv_out_ref[...] = st_ref[...]


def _rwkv_mixer(x, mod, g_pre, wr, wg, vecs, w2a, shift0, wkv0, *, tile, t_valid):
    bsz, tp, _ = x.shape
    chunk = RWKV_CHUNK
    assert tp % tile == 0 and tile % chunk == 0
    assert t_valid == tp or tp == tile
    nt = tp // tile
    mu, w0, a0, k_k, k_a, r_k, ln_g, ln_b = vecs
    row = lambda n: pl.BlockSpec((1, n), lambda b, t: (0, 0))
    full = lambda a: pl.BlockSpec(a.shape, lambda b, t: (0,) * a.ndim)
    pair_buf = pltpu.VMEM((PAIRS, tile, LANES), F32)
    kern = functools.partial(_rwkv_kernel, tile=tile, chunk=chunk, t_valid=t_valid)
    return pl.pallas_call(
        kern,
        grid=(bsz, nt),
        in_specs=[
            pl.BlockSpec((None, tile, D_MODEL), lambda b, t: (b, t, 0)),
            pl.BlockSpec((None, 3, D_MODEL), lambda b, t: (b, 0, 0)),
            row(D_MODEL), full(wr), full(wg), row(SHIFT_W), row(D_MODEL), full(w2a), row(D_MODEL),
            row(D_MODEL), row(D_MODEL), row(D_MODEL), row(D_MODEL), row(D_MODEL),
            pl.BlockSpec((None, 1, SHIFT_W), lambda b, t: (b, 0, 0)),
            pl.BlockSpec((None, PAIRS, LANES, LANES), lambda b, t: (b, 0, 0, 0)),
        ],
        out_specs=[
            pl.BlockSpec((None, tile, D_MODEL), lambda b, t: (b, t, 0)),
            pl.BlockSpec((None, 1, SHIFT_W), lambda b, t: (b, 0, 0)),
            pl.BlockSpec((None, PAIRS, LANES, LANES), lambda b, t: (b, 0, 0, 0)),
        ],
        out_shape=[
            jax.ShapeDtypeStruct((bsz, tp, D_MODEL), F32),
            jax.ShapeDtypeStruct((bsz, 1, SHIFT_W), F32),
            jax.ShapeDtypeStruct((bsz, PAIRS, LANES, LANES), F32),
        ],
        scratch_shapes=[
            pltpu.VMEM((1, SHIFT_W), F32),
            pltpu.VMEM((PAIRS, LANES, LANES), F32),
            pair_buf, pair_buf, pair_buf, pair_buf, pair_buf, pair_buf, pair_buf,
            pltpu.VMEM((PAIRS, tile // chunk, SUBLANES, LANES), F32),
            pair_buf,
        ],
        compiler_params=pltpu.CompilerParams(
            dimension_semantics=("arbitrary", "arbitrary"), vmem_limit_bytes=VMEM_LIMIT),
        name="rwkv7_mixer",
    )(x, mod, g_pre, wr, wg, mu, w0, w2a, a0, k_k, k_a, r_k, ln_g, ln_b, shift0, wkv0)


def _s5_kernel(x_ref, mod_ref, g_ref, wu_ref, wg_ref, bbr_ref, bbi_ref, cre_ref, cmi_ref, pwr_ref, pwi_ref,
               d_ref, gluw_ref, glub_ref, s0r_ref, s0i_ref,
               yz_ref, sr_out_ref, si_out_ref,
               hr_ref, hi_ref, cr_ref, ci_ref, *, tile, col_w):
    t = pl.program_id(1)

    @pl.when(t == 0)
    def _():
        cr_ref[...] = s0r_ref[...]
        ci_ref[...] = s0i_ref[...]

    h = _norm_mod(x_ref[...], g_ref[...], mod_ref[...]).astype(BF16)
    u = _dot(h, wu_ref[...])
    zg = _silu(_dot(h, wg_ref[...]))
    set_w = S5_STATE_W // S5_SETS
    ch_w = D_MODEL // S5_SETS
    for s in range(S5_SETS):
        ub = u[:, ch_w * s:ch_w * (s + 1)].astype(BF16)
        hr_ref[:, set_w * s:set_w * (s + 1)] = _dot(ub, bbr_ref[s])
        hi_ref[:, set_w * s:set_w * (s + 1)] = _dot(ub, bbi_ref[s])

    row8 = lax.broadcasted_iota(jnp.int32, (SUBLANES, 1), 0)
    for c in range(S5_STATE_W // col_w):
        cs = slice(c * col_w, (c + 1) * col_w)
        pwr = pwr_ref[:, cs]
        pwi = pwi_ref[:, cs]
        steps = []
        for sh in (1, 2, 4):
            keep = row8 >= sh
            steps.append((sh,
                          jnp.where(keep, jnp.broadcast_to(pwr[sh - 1:sh, :], (SUBLANES, col_w)), 0.0),
                          jnp.where(keep, jnp.broadcast_to(pwi[sh - 1:sh, :], (SUBLANES, col_w)), 0.0)))

        def row_body(g, carry, cs=cs, pwr=pwr, pwi=pwi, steps=steps):
            car, cai = carry
            rs = pl.ds(pl.multiple_of(g * SUBLANES, SUBLANES), SUBLANES)
            xr = hr_ref[rs, cs]
            xi = hi_ref[rs, cs]
            for sh, sr, si in steps:
                yr = pltpu.roll(xr, sh, 0)
                yi = pltpu.roll(xi, sh, 0)
                xr, xi = xr + sr * yr - si * yi, xi + sr * yi + si * yr
            xr = xr + pwr * car - pwi * cai
            xi = xi + pwr * cai + pwi * car
            hr_ref[rs, cs] = xr
            hi_ref[rs, cs] = xi
            return xr[SUBLANES - 1:SUBLANES, :], xi[SUBLANES - 1:SUBLANES, :]

        car, cai = lax.fori_loop(0, tile // SUBLANES, row_body, (cr_ref[:, cs], ci_ref[:, cs]))
        cr_ref[:, cs] = car
        ci_ref[:, cs] = cai

    outs = []
    for s in range(S5_SETS):
        hr = hr_ref[:, set_w * s:set_w * (s + 1)].astype(BF16)
        hi = hi_ref[:, set_w * s:set_w * (s + 1)].astype(BF16)
        outs.append(_dot(hr, cre_ref[s]) + _dot(hi, cmi_ref[s]))
    y = jnp.concatenate(outs, axis=1) + d_ref[...] * u
    y = 0.5 * y * (1.0 + jnp.tanh(0.7978845608028654 * (y + 0.044715 * (y * y * y))))
    y = y * jax.nn.sigmoid(_dot(y.astype(BF16), gluw_ref[...]) + glub_ref[...])
    yz_ref[...] = y * zg

    @pl.when(t == pl.num_programs(1) - 1)
    def _():
        sr_out_ref[...] = cr_ref[...]
        si_out_ref[...] = ci_ref[...]


def _s5_mixer(x, mod, g_pre, wu, wg, bbr, bbi, cre, cmi, pwr, pwi, d_skip, glu_w, glu_b, s0r, s0i, *, tile):
    bsz, tp, _ = x.shape
    assert tp % tile == 0 and tile % SUBLANES == 0
    nt = tp // tile
    row = lambda n: pl.BlockSpec((1, n), lambda b, t: (0, 0))
    full = lambda a: pl.BlockSpec(a.shape, lambda b, t: (0,) * a.ndim)
    state = pl.BlockSpec((None, 1, S5_STATE_W), lambda b, t: (b, 0, 0))
    kern = functools.partial(_s5_kernel, tile=tile, col_w=256)
    return pl.pallas_call(
        kern,
        grid=(bsz, nt),
        in_specs=[
            pl.BlockSpec((None, tile, D_MODEL), lambda b, t: (b, t, 0)),
            pl.BlockSpec((None, 3, D_MODEL), lambda b, t: (b, 0, 0)),
            row(D_MODEL), full(wu), full(wg), full(bbr), full(bbi), full(cre), full(cmi), full(pwr), full(pwi),
            row(D_MODEL), full(glu_w), row(D_MODEL), state, state,
        ],
        out_specs=[pl.BlockSpec((None, tile, D_MODEL), lambda b, t: (b, t, 0)), state, state],
        out_shape=[
            jax.ShapeDtypeStruct((bsz, tp, D_MODEL), F32),
            jax.ShapeDtypeStruct((bsz, 1, S5_STATE_W), F32),
            jax.ShapeDtypeStruct((bsz, 1, S5_STATE_W), F32),
        ],
        scratch_shapes=[
            pltpu.VMEM((tile, S5_STATE_W), F32),
            pltpu.VMEM((tile, S5_STATE_W), F32),
            pltpu.VMEM((1, S5_STATE_W), F32),
            pltpu.VMEM((1, S5_STATE_W), F32),
        ],
        compiler_params=pltpu.CompilerParams(
            dimension_semantics=("arbitrary", "arbitrary"), vmem_limit_bytes=VMEM_LIMIT),
        name="s5_mixer",
    )(x, mod, g_pre, wu, wg, bbr, bbi, cre, cmi, pwr, pwi, d_skip, glu_w, glu_b, s0r, s0i)


def _lru_kernel(x_ref, mod_ref, g_ref, wx_ref, wg_ref, cw_ref, cb_ref, wa4_ref, ba_ref, wx4_ref, bx_ref, lam_ref,
                conv0_ref, h0_ref,
                yz_ref, conv_out_ref, h_out_ref,
                xbuf_ref, a_ref, b_ref, hc_ref, *, tile):
    t = pl.program_id(1)
    pad = SUBLANES
    hist = CONV_W - 1

    @pl.when(t == 0)
    def _():
        xbuf_ref[pad - hist:pad, :] = conv0_ref[...]
        hc_ref[...] = h0_ref[...]

    h = _norm_mod(x_ref[...], g_ref[...], mod_ref[...]).astype(BF16)
    xl = _dot(h, wx_ref[...])
    zg = _silu(_dot(h, wg_ref[...]))
    xbuf_ref[pad:pad + tile, :] = xl
    xc = cb_ref[...] + xl * cw_ref[CONV_W - 1:CONV_W, :]
    for j in range(hist):
        xc = xc + xbuf_ref[pad - hist + j:pad - hist + j + tile, :] * cw_ref[j:j + 1, :]
    tail = xbuf_ref[pad + tile - hist:pad + tile, :]
    xbuf_ref[pad - hist:pad, :] = tail

    ga, gx = [], []
    blk = D_MODEL // 4
    for s in range(4):
        xb = xc[:, blk * s:blk * (s + 1)].astype(BF16)
        ga.append(_dot(xb, wa4_ref[s]))
        gx.append(_dot(xb, wx4_ref[s]))
    gate_r = jax.nn.sigmoid(jnp.concatenate(ga, axis=1) + ba_ref[...])
    gate_i = jax.nn.sigmoid(jnp.concatenate(gx, axis=1) + bx_ref[...])
    log_a = -LRU_C * gate_r * _softplus(-lam_ref[...])
    a = jnp.exp(log_a)
    a_ref[...] = a
    b_ref[...] = jnp.sqrt(-jnp.tanh(log_a) * (a * a + 1.0)) * (gate_i * xc)

    row8 = lax.broadcasted_iota(jnp.int32, (SUBLANES, 1), 0)

    def row_body(g, carry):
        rs = pl.ds(pl.multiple_of(g * SUBLANES, SUBLANES), SUBLANES)
        av = a_ref[rs, :]
        bv = b_ref[rs, :]
        for sh in (1, 2, 4):
            keep = row8 >= sh
            ash = jnp.where(keep, pltpu.roll(av, sh, 0), 1.0)
            bsh = jnp.where(keep, pltpu.roll(bv, sh, 0), 0.0)
            bv = bv + av * bsh
            av = av * ash
        hs = bv + av * carry
        b_ref[rs, :] = hs
        return hs[SUBLANES - 1:SUBLANES, :]

    hc_ref[...] = lax.fori_loop(0, tile // SUBLANES, row_body, hc_ref[...])
    yz_ref[...] = b_ref[...] * zg

    @pl.when(t == pl.num_programs(1) - 1)
    def _():
        conv_out_ref[...] = tail
        h_out_ref[...] = hc_ref[...]


def _lru_mixer(x, mod, g_pre, wx, wg, conv_w, conv_b, wa4, ba, wx4, bx, lam, conv0, h0, *, tile):
    bsz, tp, _ = x.shape
    assert tp % tile == 0 and tile % SUBLANES == 0 and tile >= SUBLANES
    nt = tp // tile
    row = lambda n: pl.BlockSpec((1, n), lambda b, t: (0, 0))
    full = lambda a: pl.BlockSpec(a.shape, lambda b, t: (0,) * a.ndim)
    kern = functools.partial(_lru_kernel, tile=tile)
    return pl.pallas_call(
        kern,
        grid=(bsz, nt),
        in_specs=[
            pl.BlockSpec((None, tile, D_MODEL), lambda b, t: (b, t, 0)),
            pl.BlockSpec((None, 3, D_MODEL), lambda b, t: (b, 0, 0)),
            row(D_MODEL), full(wx), full(wg), full(conv_w), row(D_MODEL), full(wa4), row(D_MODEL), full(wx4),
            row(D_MODEL), row(D_MODEL),
            pl.BlockSpec((None, CONV_W - 1, D_MODEL), lambda b, t: (b, 0, 0)),
            pl.BlockSpec((None, 1, D_MODEL), lambda b, t: (b, 0, 0)),
        ],
        out_specs=[
            pl.BlockSpec((None, tile, D_MODEL), lambda b, t: (b, t, 0)),
            pl.BlockSpec((None, CONV_W - 1, D_MODEL), lambda b, t: (b, 0, 0)),
            pl.BlockSpec((None, 1, D_MODEL), lambda b, t: (b, 0, 0)),
        ],
        out_shape=[
            jax.ShapeDtypeStruct((bsz, tp, D_MODEL), F32),
            jax.ShapeDtypeStruct((bsz, CONV_W - 1, D_MODEL), F32),
            jax.ShapeDtypeStruct((bsz, 1, D_MODEL), F32),
        ],
        scratch_shapes=[
            pltpu.VMEM((tile + SUBLANES, D_MODEL), F32),
            pltpu.VMEM((tile, D_MODEL), F32),
            pltpu.VMEM((tile, D_MODEL), F32),
            pltpu.VMEM((1, D_MODEL), F32),
        ],
        compiler_params=pltpu.CompilerParams(
            dimension_semantics=("arbitrary", "arbitrary"), vmem_limit_bytes=VMEM_LIMIT),
        name="rglru_mixer",
    )(x, mod, g_pre, wx, wg, conv_w, conv_b, wa4, ba, wx4, bx, lam, conv0, h0)


def _merge_kernel(x_ref, mod_ref, g_ref, gpost_ref, wm_ref, wo_ref, yr_ref, ys_ref, yl_ref, o_ref):
    x = x_ref[...]
    mod = mod_ref[...]
    h = _norm_mod(x, g_ref[...], mod).astype(BF16)
    m = jax.nn.sigmoid(_dot(h, wm_ref[...]))
    merged = (m[:, :D_MODEL] * yr_ref[...] + m[:, D_MODEL:2 * D_MODEL] * ys_ref[...]
              + m[:, 2 * D_MODEL:] * yl_ref[...])
    o = _dot(merged.astype(BF16), wo_ref[...])
    ms = jnp.mean(o * o, axis=-1, keepdims=True)
    o = (o * lax.rsqrt(ms + RMS_EPS)) * gpost_ref[...]
    o_ref[...] = x + mod[2:3, :] * o


def _merge_out(x, mod, g_pre, g_post, wm, wo, yr, ys, yl, *, tile):
    bsz, tp, _ = x.shape
    assert tp % tile == 0
    nt = tp // tile
    act = pl.BlockSpec((None, tile, D_MODEL), lambda b, t: (b, t, 0))
    row = lambda n: pl.BlockSpec((1, n), lambda b, t: (0, 0))
    full = lambda a: pl.BlockSpec(a.shape, lambda b, t: (0,) * a.ndim)
    return pl.pallas_call(
        _merge_kernel,
        grid=(bsz, nt),
        in_specs=[act, pl.BlockSpec((None, 3, D_MODEL), lambda b, t: (b, 0, 0)), row(D_MODEL), row(D_MODEL),
                  full(wm), full(wo), act, act, act],
        out_specs=act,
        out_shape=jax.ShapeDtypeStruct((bsz, tp, D_MODEL), F32),
        compiler_params=pltpu.CompilerParams(
            dimension_semantics=("arbitrary", "arbitrary"), vmem_limit_bytes=VMEM_LIMIT),
        name="merge_out",
    )(x, mod, g_pre, g_post, wm, wo, yr, ys, yl)


def _block_diag_expand(w, per):
    n, a, b = w.shape
    w = w.reshape(n // per, per, a, b)
    eye = jnp.eye(per, dtype=w.dtype)
    return jnp.einsum("sgab,gh->sgahb", w, eye).reshape(n // per, per * a, per * b)


def _wkv_to_pairs(wkv):
    bsz = wkv.shape[0]
    st = jnp.swapaxes(wkv, -1, -2).reshape(bsz * PAIRS, 2, HEAD, HEAD)
    return _block_diag_expand(st.reshape(bsz * PAIRS * 2, HEAD, HEAD), 2).reshape(bsz, PAIRS, LANES, LANES)


def _wkv_from_pairs(st):
    bsz = st.shape[0]
    st = st.reshape(bsz, PAIRS, 2, HEAD, 2, HEAD)
    diag = jnp.stack([st[:, :, 0, :, 0, :], st[:, :, 1, :, 1, :]], axis=2)
    return jnp.swapaxes(diag.reshape(bsz, HEADS, HEAD, HEAD), -1, -2)


def _layer(x, mod, state, prm, *, t_valid, tiles):
    shift_row, wkv, s_re, s_im, lru_h, lru_conv = state
    bsz = x.shape[0]
    g_pre = prm["g_pre"]

    xr = x
    if x.shape[1] % tiles["rwkv"] != 0:
        xr = jnp.pad(x, ((0, 0), (0, tiles["rwkv"] - x.shape[1] % tiles["rwkv"]), (0, 0)))
    yz_r, shift_new, wkv_new = _rwkv_mixer(
        xr, mod, g_pre, prm["w_rwkv"], prm["w_rwkv_g"], prm["rwkv_vecs"], prm["w2a"],
        shift_row.reshape(bsz, 1, SHIFT_W), _wkv_to_pairs(wkv), tile=tiles["rwkv"], t_valid=t_valid)
    yz_r = yz_r[:, :t_valid]
    yz_s, s_re_new, s_im_new = _s5_mixer(
        x, mod, g_pre, prm["w_s5"], prm["w_s5_g"], prm["bbr"], prm["bbi"], prm["cre"], prm["cmi"],
        prm["pwr"], prm["pwi"], prm["s5_d"], prm["glu_w"], prm["glu_b"],
        s_re.reshape(bsz, 1, S5_STATE_W), s_im.reshape(bsz, 1, S5_STATE_W), tile=tiles["s5"])
    yz_l, conv_new, h_new = _lru_mixer(
        x, mod, g_pre, prm["w_lru"], prm["w_lru_g"], prm["conv_w"], prm["conv_b"], prm["wa4"], prm["ba"],
        prm["wx4"], prm["bx"], prm["lam"], lru_conv, lru_h.reshape(bsz, 1, D_MODEL), tile=tiles["lru"])
    x_new = _merge_out(x, mod, g_pre, prm["g_post"], prm["w_merge"], prm["w_out"], yz_r, yz_s, yz_l,
                       tile=tiles["merge"])
    new_state = (shift_new.reshape(bsz, SHIFT_W), _wkv_from_pairs(wkv_new),
                 s_re_new.reshape(bsz, 64, 64), s_im_new.reshape(bsz, 64, 64),
                 h_new.reshape(bsz, D_MODEL), conv_new)
    return x_new, new_state


def _layer_params(l, w_in, w_out, norm_pre, norm_post, rwkv, s5, lru):
    d = D_MODEL
    wb = w_in[l].astype(BF16)
    o = SHIFT_W
    row = lambda a: a.reshape(1, -1)
    mu, w0, w2, a0, a2, k_k, k_a, r_k, ln_g, ln_b = [a[l] for a in rwkv]
    a_re, a_im, log_step, b_re, b_im, c_re, c_im, s5_d, glu_w, glu_b = [a[l] for a in s5]
    conv_w, conv_b, wa, ba, wx, bx, lam = [a[l] for a in lru]
    zeros = jnp.zeros((LORA, d), F32)
    w2a = jnp.concatenate([jnp.concatenate([w2, zeros], axis=1), jnp.concatenate([zeros, a2], axis=1)], axis=0)
    pwr, pwi, bbr, bbi = _s5_prep(a_re, a_im, log_step, b_re, b_im)
    per = 64 // S5_SETS
    return dict(
        g_pre=row(norm_pre[l]), g_post=row(norm_post[l]),
        w_rwkv=wb[:, :o], w_rwkv_g=wb[:, o:o + d],
        w_s5=wb[:, o + d:o + 2 * d], w_s5_g=wb[:, o + 2 * d:o + 3 * d],
        w_lru=wb[:, o + 3 * d:o + 4 * d], w_lru_g=wb[:, o + 4 * d:o + 5 * d],
        w_merge=wb[:, o + 5 * d:], w_out=w_out[l].astype(BF16),
        rwkv_vecs=(row(mu), row(w0), row(a0), row(k_k), row(k_a), row(r_k), row(ln_g), row(ln_b)),
        w2a=w2a.astype(BF16),
        bbr=_block_diag_expand(bbr, per).astype(BF16), bbi=_block_diag_expand(bbi, per).astype(BF16),
        cre=_block_diag_expand(jnp.swapaxes(c_re, 1, 2), per).astype(BF16),
        cmi=_block_diag_expand(-jnp.swapaxes(c_im, 1, 2), per).astype(BF16),
        pwr=pwr.reshape(SUBLANES, S5_STATE_W), pwi=pwi.reshape(SUBLANES, S5_STATE_W),
        s5_d=row(s5_d), glu_w=glu_w.astype(BF16), glu_b=row(glu_b),
        conv_w=conv_w, conv_b=row(conv_b),
        wa4=_block_diag_expand(wa, 4).astype(BF16), ba=row(ba),
        wx4=_block_diag_expand(wx, 4).astype(BF16), bx=row(bx), lam=row(lam),
    )


def _tiles(t):
    pick = lambda want: want if t % want == 0 else t
    return dict(rwkv=256 if t % 256 == 0 else RWKV_CHUNK, s5=pick(256), lru=pick(512), merge=pick(512))


def kernel(x_prompt, x_sample, state_rwkv_shift, state_rwkv_wkv, state_s5_re, state_s5_im, state_lru_h,
           state_lru_conv, c_prompt, c_sample, ada_w, ada_b, norm_pre, norm_post, w_in, w_out, rwkv_mu, rwkv_w0,
           rwkv_w2, rwkv_a0, rwkv_a2, rwkv_k_k, rwkv_k_a, rwkv_r_k, rwkv_ln_g, rwkv_ln_b, s5_a_re, s5_a_im,
           s5_log_step, s5_b_re, s5_b_im, s5_c_re, s5_c_im, s5_d, s5_glu_w, s5_glu_b, lru_conv_w, lru_conv_b,
           lru_wa, lru_ba, lru_wx, lru_bx, lru_lambda):
    depth = w_in.shape[0]
    bp, tp, _ = x_prompt.shape
    bs, ts, _ = x_sample.shape
    c_all = jnp.concatenate([c_prompt, c_sample], axis=0)
    pad = (-c_all.shape[0]) % SUBLANES
    c_all = jnp.pad(c_all, ((0, pad), (0, 0)))
    mod_all = _modulation(c_all, ada_w, ada_b)
    rwkv = (rwkv_mu, rwkv_w0, rwkv_w2, rwkv_a0, rwkv_a2, rwkv_k_k, rwkv_k_a,
            rwkv_r_k.reshape(depth, D_MODEL), rwkv_ln_g, rwkv_ln_b)
    s5 = (s5_a_re, s5_a_im, s5_log_step, s5_b_re, s5_b_im, s5_c_re, s5_c_im, s5_d, s5_glu_w, s5_glu_b)
    lru = (lru_conv_w, lru_conv_b, lru_wa, lru_ba, lru_wx, lru_bx, lru_lambda)

    xp, xs = x_prompt, x_sample
    new_p = [[] for _ in range(6)]
    new_s = [[] for _ in range(6)]
    for l in range(depth):
        prm = _layer_params(l, w_in, w_out, norm_pre, norm_post, rwkv, s5, lru)
        mod_p = mod_all[l, :bp].reshape(bp, 3, D_MODEL)
        mod_s = mod_all[l, bp:bp + bs].reshape(bs, 3, D_MODEL)
        zero = (jnp.zeros((bp, SHIFT_W), F32), jnp.zeros((bp, HEADS, HEAD, HEAD), F32),
                jnp.zeros((bp, 64, 64), F32), jnp.zeros((bp, 64, 64), F32),
                jnp.zeros((bp, D_MODEL), F32), jnp.zeros((bp, CONV_W - 1, D_MODEL), F32))
        xp, st_p = _layer(xp, mod_p, zero, prm, t_valid=tp, tiles=_tiles(tp))
        st_in = (state_rwkv_shift[l], state_rwkv_wkv[l], state_s5_re[l], state_s5_im[l],
                 state_lru_h[l], state_lru_conv[l])
        xs, st_s = _layer(xs, mod_s, st_in, prm, t_valid=ts, tiles=_tiles(ts))
        for i in range(6):
            new_p[i].append(st_p[i])
            new_s[i].append(st_s[i])
    sp = [jnp.stack(z, axis=0) for z in new_p]
    ss = [jnp.stack(z, axis=0) for z in new_s]
    return (xp, xs, sp[0], sp[1], sp[2], sp[3], sp[4], sp[5], ss[0], ss[1], ss[2], ss[3], ss[4], ss[5])
```

```python
import functools

import jax
import jax.numpy as jnp
from jax import lax
from jax.experimental import pallas as pl
from jax.experimental.pallas import tpu as pltpu

F32 = jnp.float32
BF16 = jnp.bfloat16
HIGHEST = lax.Precision.HIGHEST

D_MODEL = 1024
HEAD = 64
HEADS = D_MODEL // HEAD
PAIRS = HEADS // 2
LORA = 64
SHIFT_W = 3 * D_MODEL + 2 * LORA
S5_STATE_W = 64 * 64
S5_SETS = 4
CONV_W = 4
LRU_C = 8.0
RMS_EPS = 1e-6
GN_EPS = 64e-5
RWKV_CHUNK = 64
RWKV_INSTANCES = 16
SUBLANES = 8
LANES = 128
VMEM_LIMIT = 56 * 1024 * 1024


def _silu(x):
    return x * jax.nn.sigmoid(x)


def _softplus(x):
    return jnp.maximum(x, 0.0) + jnp.log1p(jnp.exp(-jnp.abs(x)))


def _norm_mod(x, g, mod):
    ms = jnp.mean(x * x, axis=-1, keepdims=True)
    y = (x * lax.rsqrt(ms + RMS_EPS)) * g
    return y * (1.0 + mod[1:2, :]) + mod[0:1, :]


def _dot(a, b):
    return jnp.dot(a, b, preferred_element_type=F32)


def _dot_hi(a, b):
    return jnp.dot(a, b, preferred_element_type=F32, precision=HIGHEST)


def _split2(x):
    hi = x.astype(BF16)
    lo = (x - hi.astype(F32)).astype(BF16)
    return hi, lo


_NN = (((1,), (0,)), ((), ()))
_NT = (((1,), (1,)), ((), ()))
_TN = (((0,), (0,)), ((), ()))


def _mm(a, b, dims=_NN):
    return lax.dot_general(a.astype(BF16), b.astype(BF16), dims, preferred_element_type=F32)


def _mm3(a_parts, b_parts):
    (a_hi, a_lo), (b_hi, b_lo) = a_parts, b_parts
    return _dot(a_hi, b_hi) + (_dot(a_hi, b_lo) + _dot(a_lo, b_hi))


def _head_sum(x, ones_blk):
    outs = []
    for s in range(D_MODEL // 256):
        hi, lo = _split2(x[:, 256 * s:256 * (s + 1)])
        outs.append(_dot(hi, ones_blk) + _dot(lo, ones_blk))
    return jnp.concatenate(outs, axis=1)


def _block_ones(n, blk):
    ri = lax.broadcasted_iota(jnp.int32, (n, n), 0)
    ci = lax.broadcasted_iota(jnp.int32, (n, n), 1)
    sh = blk.bit_length() - 1
    return ((ri >> sh) == (ci >> sh)).astype(F32).astype(BF16)


def _mod_kernel(c_ref, w_ref, b_ref, o_ref):
    s = _silu(c_ref[...])
    o_ref[...] = _dot_hi(s, w_ref[...]) + b_ref[...]


def _modulation(c_all, ada_w, ada_b):
    depth = ada_w.shape[0]
    rows = c_all.shape[0]
    return pl.pallas_call(
        _mod_kernel,
        grid=(depth, 3),
        in_specs=[
            pl.BlockSpec((rows, D_MODEL), lambda l, j: (0, 0)),
            pl.BlockSpec((None, D_MODEL, D_MODEL), lambda l, j: (l, 0, j)),
            pl.BlockSpec((None, 1, D_MODEL), lambda l, j: (l, 0, j)),
        ],
        out_specs=pl.BlockSpec((None, rows, D_MODEL), lambda l, j: (l, 0, j)),
        out_shape=jax.ShapeDtypeStruct((depth, rows, 3 * D_MODEL), F32),
        name="adaln_mod",
    )(c_all, ada_w, ada_b.reshape(depth, 1, 3 * D_MODEL))


def _s5_prep_kernel(are_ref, aim_ref, ls_ref, bre_ref, bim_ref, pwr_ref, pwi_ref, bbr_ref, bbi_ref):
    are = are_ref[...]
    aim = aim_ref[...]
    dt = jnp.exp(ls_ref[...])
    for j in range(SUBLANES):
        mag = jnp.exp(are * dt * (j + 1.0))
        th = aim * dt * (j + 1.0)
        pwr_ref[j] = mag * jnp.cos(th)
        pwi_ref[j] = mag * jnp.sin(th)
    nr = pwr_ref[0] - 1.0
    ni = pwi_ref[0]
    den = are * are + aim * aim
    cr = (nr * are + ni * aim) / den
    ci = (ni * are - nr * aim) / den
    bre = bre_ref[...]
    bim = bim_ref[...]
    bbr_ref[...] = cr[:, None, :] * bre - ci[:, None, :] * bim
    bbi_ref[...] = cr[:, None, :] * bim + ci[:, None, :] * bre


def _s5_prep(a_re, a_im, log_step, b_re, b_im):
    g, p = a_re.shape
    i = b_re.shape[-1]
    return pl.pallas_call(
        _s5_prep_kernel,
        out_shape=(
            jax.ShapeDtypeStruct((SUBLANES, g, p), F32),
            jax.ShapeDtypeStruct((SUBLANES, g, p), F32),
            jax.ShapeDtypeStruct((g, i, p), F32),
            jax.ShapeDtypeStruct((g, i, p), F32),
        ),
        name="s5_discretise",
    )(a_re, a_im, log_step.reshape(g, 1), jnp.swapaxes(b_re, 1, 2), jnp.swapaxes(b_im, 1, 2))


def _rwkv_kernel(x_ref, mod_ref, g_ref, wr_ref, wg_ref, mu_ref, w0_ref, w2a_ref, a0_ref, kk_ref, ka_ref,
                 rk_ref, lng_ref, lnb_ref, shift0_ref, wkv0_ref,
                 yz_ref, shift_out_ref, wkv_out_ref,
                 carry_ref, st_ref, at_ref, rt_ref, bt_ref, kt_ref, bh_ref, kh_ref, v_ref, pl_ref, y_ref,
                 nh_ref, t_ref, aak_ref, ark_ref, rhs_ref, gm_ref, qe_ref, hc_ref, yc_ref,
                 *, tile, chunk, t_valid, group):
    t = pl.program_id(1)
    n_chunks = tile // chunk
    two = 2 * chunk
    n_inst = group * n_chunks

    @pl.when(t == 0)
    def _():
        carry_ref[...] = shift0_ref[...]
        st_ref[...] = wkv0_ref[...]

    h = _norm_mod(x_ref[...], g_ref[...], mod_ref[...]).astype(BF16)
    p = _dot(h, wr_ref[...])
    zg = _silu(_dot(h, wg_ref[...]))

    rows = lax.broadcasted_iota(jnp.int32, (tile, 1), 0)
    prev = jnp.where(rows == 0, carry_ref[...], pltpu.roll(p, 1, 0))
    last = (t_valid - 1) % tile
    carry_ref[...] = p[last:last + 1, :]
    pm = p + (prev - p) * mu_ref[...]

    r = pm[:, 0:D_MODEL]
    k = pm[:, D_MODEL:2 * D_MODEL]
    v = pm[:, 2 * D_MODEL:3 * D_MODEL]
    wa = pm[:, 3 * D_MODEL:SHIFT_W]
    lane = lax.broadcasted_iota(jnp.int32, (1, LANES), 1)
    wa = jnp.where(lane < LORA, jnp.tanh(wa), wa)
    lora = _dot(wa.astype(BF16), w2a_ref[...])
    w_log = -_softplus(-(w0_ref[...] + lora[:, :D_MODEL])) - 0.5
    logd = -jnp.exp(w_log)
    a_sig = jax.nn.sigmoid(a0_ref[...] + lora[:, D_MODEL:])

    ones_blk = _block_ones(256, HEAD)
    kk = k * kk_ref[...]
    kk = kk / jnp.maximum(jnp.sqrt(_head_sum(kk * kk, ones_blk)), 1e-12)
    k2 = k * (1.0 + (a_sig - 1.0) * ka_ref[...])
    bvec = kk * a_sig
    if t_valid % tile != 0:
        ok = rows < t_valid
        logd = jnp.where(ok, logd, 0.0)
        bvec = jnp.where(ok, bvec, 0.0)
        k2 = jnp.where(ok, k2, 0.0)

    rmod = rows & (chunk - 1)
    cum = logd
    s = 1
    while s < chunk:
        cum = cum + jnp.where(rmod >= s, pltpu.roll(cum, s, 0), 0.0)
        s *= 2
    cum_last = jnp.concatenate(
        [jnp.broadcast_to(cum[(j + 1) * chunk - 1:(j + 1) * chunk, :], (chunk, D_MODEL)) for j in range(n_chunks)],
        axis=0)
    p_inc = jnp.exp(cum)
    p_inv = jnp.exp(-cum)
    p_exc = jnp.exp(cum - logd)
    p_end = jnp.exp(cum_last - cum)

    def to_pairs(ref, val):
        for q in range(PAIRS):
            ref[q] = val[:, LANES * q:LANES * (q + 1)]

    to_pairs(at_ref, -kk * p_exc)
    to_pairs(rt_ref, r * p_inc)
    to_pairs(bt_ref, bvec * p_inv)
    to_pairs(kt_ref, k2 * p_inv)
    to_pairs(bh_ref, bvec * p_end)
    to_pairs(kh_ref, k2 * p_end)
    to_pairs(v_ref, v)
    for j in range(n_chunks):
        row = jnp.exp(cum[(j + 1) * chunk - 1:(j + 1) * chunk, :])
        for q in range(PAIRS):
            pl_ref[q, j] = jnp.broadcast_to(row[:, LANES * q:LANES * (q + 1)], (SUBLANES, LANES))

    m_a = (lane < HEAD).astype(F32)
    m_b = 1.0 - m_a
    ri = lax.broadcasted_iota(jnp.int32, (two, two), 0)
    ci = lax.broadcasted_iota(jnp.int32, (two, two), 1)
    csh = chunk.bit_length() - 1
    same_head = (ri >> csh) == (ci >> csh)
    strict = jnp.logical_and(same_head, (ri & (chunk - 1)) > (ci & (chunk - 1)))
    incl = jnp.logical_and(same_head, (ri & (chunk - 1)) >= (ci & (chunk - 1)))
    eye = ri == ci
    off_masks = []
    for lg in range(csh):
        off_masks.append(jnp.logical_and(strict, jnp.logical_and((ri >> (lg + 1)) == (ci >> (lg + 1)),
                                                                 (ri >> lg) != (ci >> lg))))

    def stack(x):
        return jnp.concatenate([x * m_a, x * m_b], axis=0)

    gsh = group.bit_length() - 1
    unroll = min(16, n_inst)

    def group_body(g, carry):
        def where_is(i):
            q = g * group + (i & (group - 1))
            j = i >> gsh
            return q, j, pl.ds(pl.multiple_of(j * chunk, chunk), chunk)

        def gram(i, c):
            q, _, sl = where_is(i)
            atm = stack(at_ref[q, sl, :])
            rtm = stack(rt_ref[q, sl, :])
            bt = bt_ref[q, sl, :]
            kt = kt_ref[q, sl, :]
            o1 = _mm(jnp.concatenate([atm, rtm], axis=0), jnp.concatenate([bt, bt, kt, kt], axis=0), _NT)
            n_ab = jnp.where(strict, o1[:two, :two], 0.0)
            nh_ref[i] = n_ab.astype(BF16)
            t_ref[i] = jnp.where(eye, 1.0, jnp.where(off_masks[0], n_ab, 0.0)).astype(BF16)
            aak_ref[i] = jnp.where(strict, o1[:two, two:], 0.0).astype(BF16)
            ark_ref[i] = jnp.concatenate([jnp.where(incl, o1[two:, :two], 0.0),
                                          jnp.where(incl, o1[two:, two:], 0.0)], axis=1).astype(BF16)
            return c

        lax.fori_loop(0, n_inst, gram, 0, unroll=unroll)

        for off in off_masks[1:]:
            def level(i, c, off=off):
                t_inv = t_ref[i]
                x = _dot(jnp.where(off, nh_ref[i], 0.0), t_inv).astype(BF16)
                t_ref[i] = t_inv + _dot(t_inv, x).astype(BF16)
                return c

            lax.fori_loop(0, n_inst, level, 0, unroll=unroll)

        def apply(i, c):
            q, _, sl = where_is(i)
            atm = stack(at_ref[q, sl, :])
            vm = stack(v_ref[q, sl, :])
            t_inv = t_ref[i]
            z0_hi, z0_lo = _split2(jnp.concatenate([atm, _mm(aak_ref[i], vm)], axis=1))
            z = _dot(t_inv, z0_hi) + _dot(t_inv, z0_lo)
            rhs_ref[i] = jnp.concatenate([z, jnp.concatenate([jnp.zeros_like(vm), vm], axis=1)], axis=0).astype(BF16)
            return c

        lax.fori_loop(0, n_inst, apply, 0, unroll=unroll)

        def operators(i, c):
            q, j, sl = where_is(i)
            rhs2 = rhs_ref[i]
            top = _mm(jnp.concatenate([stack(bh_ref[q, sl, :]), stack(kh_ref[q, sl, :])], axis=0), rhs2, _TN)
            bot = _dot(ark_ref[i], rhs2)
            gm_ref[i] = (top[:, :two]
                         + jnp.where(eye, jnp.broadcast_to(pl_ref[q, j][0:1, :], (two, two)), 0.0)).astype(BF16)
            hc_ref[i] = top[:, two:]
            qe_ref[i] = (bot[:, :two] + stack(rt_ref[q, sl, :])).astype(BF16)
            yc_ref[i] = bot[:, two:]
            return c

        lax.fori_loop(0, n_inst, operators, 0, unroll=unroll)

        for j in range(n_chunks):
            def advance(ql, c, j=j):
                q = g * group + ql
                i = j * group + ql
                st = st_ref[q].astype(BF16)
                ym = _dot(qe_ref[i], st) + yc_ref[i]
                y_ref[q, j * chunk:(j + 1) * chunk, :] = ym[:chunk, :] + ym[chunk:, :]
                st_ref[q] = _dot(gm_ref[i], st) + hc_ref[i]
                return c

            lax.fori_loop(0, group, advance, 0, unroll=True)
        return carry

    lax.fori_loop(0, PAIRS // group, group_body, 0)

    ys = jnp.concatenate([y_ref[q] for q in range(PAIRS)], axis=1)
    mean = _head_sum(ys, ones_blk) * (1.0 / HEAD)
    yc = ys - mean
    var = _head_sum(yc * yc, ones_blk) * (1.0 / HEAD)
    yn = yc * lax.rsqrt(var + GN_EPS) * lng_ref[...] + lnb_ref[...]
    bonus = _head_sum(r * k2 * rk_ref[...], ones_blk) * v
    yz_ref[...] = (yn + bonus) * zg

    @pl.when(t == pl.num_programs(1) - 1)
    def _():
        shift_out_ref[...] = carry_ref[...]
        wkv_out_ref[...] = st_ref[...]


def _rwkv_mixer(x, mod, g_pre, wr, wg, vecs, w2a, shift0, wkv0, *, tile, t_valid):
    bsz, tp, _ = x.shape
    chunk = RWKV_CHUNK
    assert tp % tile == 0 and tile % chunk == 0
    assert t_valid == tp or tp == tile
    nt = tp // tile
    mu, w0, a0, k_k, k_a, r_k, ln_g, ln_b = vecs
    row = lambda n: pl.BlockSpec((1, n), lambda b, t: (0, 0))
    full = lambda a: pl.BlockSpec(a.shape, lambda b, t: (0,) * a.ndim)
    pair_buf = pltpu.VMEM((PAIRS, tile, LANES), F32)
    group = min(PAIRS, max(1, RWKV_INSTANCES // (tile // chunk)))
    n_inst = group * (tile // chunk)
    two = 2 * chunk
    inst = lambda cols, dtype: pltpu.VMEM((n_inst, two, cols), dtype)
    kern = functools.partial(_rwkv_kernel, tile=tile, chunk=chunk, t_valid=t_valid, group=group)
    return pl.pallas_call(
        kern,
        grid=(bsz, nt),
        in_specs=[
            pl.BlockSpec((None, tile, D_MODEL), lambda b, t: (b, t, 0)),
            pl.BlockSpec((None, 3, D_MODEL), lambda b, t: (b, 0, 0)),
            row(D_MODEL), full(wr), full(wg), row(SHIFT_W), row(D_MODEL), full(w2a), row(D_MODEL),
            row(D_MODEL), row(D_MODEL), row(D_MODEL), row(D_MODEL), row(D_MODEL),
            pl.BlockSpec((None, 1, SHIFT_W), lambda b, t: (b, 0, 0)),
            pl.BlockSpec((None, PAIRS, LANES, LANES), lambda b, t: (b, 0, 0, 0)),
        ],
        out_specs=[
            pl.BlockSpec((None, tile, D_MODEL), lambda b, t: (b, t, 0)),
            pl.BlockSpec((None, 1, SHIFT_W), lambda b, t: (b, 0, 0)),
            pl.BlockSpec((None, PAIRS, LANES, LANES), lambda b, t: (b, 0, 0, 0)),
        ],
        out_shape=[
            jax.ShapeDtypeStruct((bsz, tp, D_MODEL), F32),
            jax.ShapeDtypeStruct((bsz, 1, SHIFT_W), F32),
            jax.ShapeDtypeStruct((bsz, PAIRS, LANES, LANES), F32),
        ],
        scratch_shapes=[
            pltpu.VMEM((1, SHIFT_W), F32),
            pltpu.VMEM((PAIRS, LANES, LANES), F32),
            pair_buf, pair_buf, pair_buf, pair_buf, pair_buf, pair_buf, pair_buf,
            pltpu.VMEM((PAIRS, tile // chunk, SUBLANES, LANES), F32),
            pair_buf,
            inst(two, BF16), inst(two, BF16), inst(two, BF16), inst(2 * two, BF16),
            pltpu.VMEM((n_inst, 2 * two, 2 * two), BF16),
            inst(two, BF16), inst(two, BF16), inst(two, F32), inst(two, F32),
        ],
        compiler_params=pltpu.CompilerParams(
            dimension_semantics=("arbitrary", "arbitrary"), vmem_limit_bytes=VMEM_LIMIT),
        name="rwkv7_mixer",
    )(x, mod, g_pre, wr, wg, mu, w0, w2a, a0, k_k, k_a, r_k, ln_g, ln_b, shift0, wkv0)


def _s5_kernel(x_ref, mod_ref, g_ref, wu_ref, wg_ref, bbr_ref, bbi_ref, cre_ref, cmi_ref, pwr_ref, pwi_ref,
               d_ref, gluw_ref, glub_ref, s0r_ref, s0i_ref,
               yz_ref, sr_out_ref, si_out_ref,
               hr_ref, hi_ref, cr_ref, ci_ref, *, tile, col_w):
    t = pl.program_id(1)

    @pl.when(t == 0)
    def _():
        cr_ref[...] = s0r_ref[...]
        ci_ref[...] = s0i_ref[...]

    h = _norm_mod(x_ref[...], g_ref[...], mod_ref[...]).astype(BF16)
    u = _dot(h, wu_ref[...])
    zg = _silu(_dot(h, wg_ref[...]))
    set_w = S5_STATE_W // S5_SETS
    ch_w = D_MODEL // S5_SETS
    for s in range(S5_SETS):
        ub = u[:, ch_w * s:ch_w * (s + 1)].astype(BF16)
        hr_ref[:, set_w * s:set_w * (s + 1)] = _dot(ub, bbr_ref[s])
        hi_ref[:, set_w * s:set_w * (s + 1)] = _dot(ub, bbi_ref[s])

    row8 = lax.broadcasted_iota(jnp.int32, (SUBLANES, 1), 0)
    for c in range(S5_STATE_W // col_w):
        cs = slice(c * col_w, (c + 1) * col_w)
        pwr = pwr_ref[:, cs]
        pwi = pwi_ref[:, cs]
        steps = []
        for sh in (1, 2, 4):
            keep = row8 >= sh
            steps.append((sh,
                          jnp.where(keep, jnp.broadcast_to(pwr[sh - 1:sh, :], (SUBLANES, col_w)), 0.0),
                          jnp.where(keep, jnp.broadcast_to(pwi[sh - 1:sh, :], (SUBLANES, col_w)), 0.0)))

        def row_body(g, carry, cs=cs, pwr=pwr, pwi=pwi, steps=steps):
            car, cai = carry
            rs = pl.ds(pl.multiple_of(g * SUBLANES, SUBLANES), SUBLANES)
            xr = hr_ref[rs, cs]
            xi = hi_ref[rs, cs]
            for sh, sr, si in steps:
                yr = pltpu.roll(xr, sh, 0)
                yi = pltpu.roll(xi, sh, 0)
                xr, xi = xr + sr * yr - si * yi, xi + sr * yi + si * yr
            xr = xr + pwr * car - pwi * cai
            xi = xi + pwr * cai + pwi * car
            hr_ref[rs, cs] = xr
            hi_ref[rs, cs] = xi
            return xr[SUBLANES - 1:SUBLANES, :], xi[SUBLANES - 1:SUBLANES, :]

        car, cai = lax.fori_loop(0, tile // SUBLANES, row_body, (cr_ref[:, cs], ci_ref[:, cs]))
        cr_ref[:, cs] = car
        ci_ref[:, cs] = cai

    outs = []
    for s in range(S5_SETS):
        hr = hr_ref[:, set_w * s:set_w * (s + 1)].astype(BF16)
        hi = hi_ref[:, set_w * s:set_w * (s + 1)].astype(BF16)
        outs.append(_dot(hr, cre_ref[s]) + _dot(hi, cmi_ref[s]))
    y = jnp.concatenate(outs, axis=1) + d_ref[...] * u
    y = 0.5 * y * (1.0 + jnp.tanh(0.7978845608028654 * (y + 0.044715 * (y * y * y))))
    y = y * jax.nn.sigmoid(_dot(y.astype(BF16), gluw_ref[...]) + glub_ref[...])
    yz_ref[...] = y * zg

    @pl.when(t == pl.num_programs(1) - 1)
    def _():
        sr_out_ref[...] = cr_ref[...]
        si_out_ref[...] = ci_ref[...]


def _s5_mixer(x, mod, g_pre, wu, wg, bbr, bbi, cre, cmi, pwr, pwi, d_skip, glu_w, glu_b, s0r, s0i, *, tile):
    bsz, tp, _ = x.shape
    assert tp % tile == 0 and tile % SUBLANES == 0
    nt = tp // tile
    row = lambda n: pl.BlockSpec((1, n), lambda b, t: (0, 0))
    full = lambda a: pl.BlockSpec(a.shape, lambda b, t: (0,) * a.ndim)
    state = pl.BlockSpec((None, 1, S5_STATE_W), lambda b, t: (b, 0, 0))
    kern = functools.partial(_s5_kernel, tile=tile, col_w=256)
    return pl.pallas_call(
        kern,
        grid=(bsz, nt),
        in_specs=[
            pl.BlockSpec((None, tile, D_MODEL), lambda b, t: (b, t, 0)),
            pl.BlockSpec((None, 3, D_MODEL), lambda b, t: (b, 0, 0)),
            row(D_MODEL), full(wu), full(wg), full(bbr), full(bbi), full(cre), full(cmi), full(pwr), full(pwi),
            row(D_MODEL), full(glu_w), row(D_MODEL), state, state,
        ],
        out_specs=[pl.BlockSpec((None, tile, D_MODEL), lambda b, t: (b, t, 0)), state, state],
        out_shape=[
            jax.ShapeDtypeStruct((bsz, tp, D_MODEL), F32),
            jax.ShapeDtypeStruct((bsz, 1, S5_STATE_W), F32),
            jax.ShapeDtypeStruct((bsz, 1, S5_STATE_W), F32),
        ],
        scratch_shapes=[
            pltpu.VMEM((tile, S5_STATE_W), F32),
            pltpu.VMEM((tile, S5_STATE_W), F32),
            pltpu.VMEM((1, S5_STATE_W), F32),
            pltpu.VMEM((1, S5_STATE_W), F32),
        ],
        compiler_params=pltpu.CompilerParams(
            dimension_semantics=("arbitrary", "arbitrary"), vmem_limit_bytes=VMEM_LIMIT),
        name="s5_mixer",
    )(x, mod, g_pre, wu, wg, bbr, bbi, cre, cmi, pwr, pwi, d_skip, glu_w, glu_b, s0r, s0i)


def _lru_kernel(x_ref, mod_ref, g_ref, wx_ref, wg_ref, cw_ref, cb_ref, wa4_ref, ba_ref, wx4_ref, bx_ref, lam_ref,
                conv0_ref, h0_ref,
                yz_ref, conv_out_ref, h_out_ref,
                xbuf_ref, a_ref, b_ref, hc_ref, *, tile):
    t = pl.program_id(1)
    pad = SUBLANES
    hist = CONV_W - 1

    @pl.when(t == 0)
    def _():
        xbuf_ref[pad - hist:pad, :] = conv0_ref[...]
        hc_ref[...] = h0_ref[...]

    h = _norm_mod(x_ref[...], g_ref[...], mod_ref[...]).astype(BF16)
    xl = _dot(h, wx_ref[...])
    zg = _silu(_dot(h, wg_ref[...]))
    xbuf_ref[pad:pad + tile, :] = xl
    xc = cb_ref[...] + xl * cw_ref[CONV_W - 1:CONV_W, :]
    for j in range(hist):
        xc = xc + xbuf_ref[pad - hist + j:pad - hist + j + tile, :] * cw_ref[j:j + 1, :]
    tail = xbuf_ref[pad + tile - hist:pad + tile, :]
    xbuf_ref[pad - hist:pad, :] = tail

    ga, gx = [], []
    blk = D_MODEL // 4
    for s in range(4):
        xb = xc[:, blk * s:blk * (s + 1)].astype(BF16)
        ga.append(_dot(xb, wa4_ref[s]))
        gx.append(_dot(xb, wx4_ref[s]))
    gate_r = jax.nn.sigmoid(jnp.concatenate(ga, axis=1) + ba_ref[...])
    gate_i = jax.nn.sigmoid(jnp.concatenate(gx, axis=1) + bx_ref[...])
    log_a = -LRU_C * gate_r * _softplus(-lam_ref[...])
    a = jnp.exp(log_a)
    a_ref[...] = a
    b_ref[...] = jnp.sqrt(-jnp.tanh(log_a) * (a * a + 1.0)) * (gate_i * xc)

    row8 = lax.broadcasted_iota(jnp.int32, (SUBLANES, 1), 0)

    def row_body(g, carry):
        rs = pl.ds(pl.multiple_of(g * SUBLANES, SUBLANES), SUBLANES)
        av = a_ref[rs, :]
        bv = b_ref[rs, :]
        for sh in (1, 2, 4):
            keep = row8 >= sh
            ash = jnp.where(keep, pltpu.roll(av, sh, 0), 1.0)
            bsh = jnp.where(keep, pltpu.roll(bv, sh, 0), 0.0)
            bv = bv + av * bsh
            av = av * ash
        hs = bv + av * carry
        b_ref[rs, :] = hs
        return hs[SUBLANES - 1:SUBLANES, :]

    hc_ref[...] = lax.fori_loop(0, tile // SUBLANES, row_body, hc_ref[...])
    yz_ref[...] = b_ref[...] * zg

    @pl.when(t == pl.num_programs(1) - 1)
    def _():
        conv_out_ref[...] = tail
        h_out_ref[...] = hc_ref[...]


def _lru_mixer(x, mod, g_pre, wx, wg, conv_w, conv_b, wa4, ba, wx4, bx, lam, conv0, h0, *, tile):
    bsz, tp, _ = x.shape
    assert tp % tile == 0 and tile % SUBLANES == 0 and tile >= SUBLANES
    nt = tp // tile
    row = lambda n: pl.BlockSpec((1, n), lambda b, t: (0, 0))
    full = lambda a: pl.BlockSpec(a.shape, lambda b, t: (0,) * a.ndim)
    kern = functools.partial(_lru_kernel, tile=tile)
    return pl.pallas_call(
        kern,
        grid=(bsz, nt),
        in_specs=[
            pl.BlockSpec((None, tile, D_MODEL), lambda b, t: (b, t, 0)),
            pl.BlockSpec((None, 3, D_MODEL), lambda b, t: (b, 0, 0)),
            row(D_MODEL), full(wx), full(wg), full(conv_w), row(D_MODEL), full(wa4), row(D_MODEL), full(wx4),
            row(D_MODEL), row(D_MODEL),
            pl.BlockSpec((None, CONV_W - 1, D_MODEL), lambda b, t: (b, 0, 0)),
            pl.BlockSpec((None, 1, D_MODEL), lambda b, t: (b, 0, 0)),
        ],
        out_specs=[
            pl.BlockSpec((None, tile, D_MODEL), lambda b, t: (b, t, 0)),
            pl.BlockSpec((None, CONV_W - 1, D_MODEL), lambda b, t: (b, 0, 0)),
            pl.BlockSpec((None, 1, D_MODEL), lambda b, t: (b, 0, 0)),
        ],
        out_shape=[
            jax.ShapeDtypeStruct((bsz, tp, D_MODEL), F32),
            jax.ShapeDtypeStruct((bsz, CONV_W - 1, D_MODEL), F32),
            jax.ShapeDtypeStruct((bsz, 1, D_MODEL), F32),
        ],
        scratch_shapes=[
            pltpu.VMEM((tile + SUBLANES, D_MODEL), F32),
            pltpu.VMEM((tile, D_MODEL), F32),
            pltpu.VMEM((tile, D_MODEL), F32),
            pltpu.VMEM((1, D_MODEL), F32),
        ],
        compiler_params=pltpu.CompilerParams(
            dimension_semantics=("arbitrary", "arbitrary"), vmem_limit_bytes=VMEM_LIMIT),
        name="rglru_mixer",
    )(x, mod, g_pre, wx, wg, conv_w, conv_b, wa4, ba, wx4, bx, lam, conv0, h0)


def _merge_kernel(x_ref, mod_ref, g_ref, gpost_ref, wm_ref, wo_ref, yr_ref, ys_ref, yl_ref, o_ref):
    x = x_ref[...]
    mod = mod_ref[...]
    h = _norm_mod(x, g_ref[...], mod).astype(BF16)
    m = jax.nn.sigmoid(_dot(h, wm_ref[...]))
    merged = (m[:, :D_MODEL] * yr_ref[...] + m[:, D_MODEL:2 * D_MODEL] * ys_ref[...]
              + m[:, 2 * D_MODEL:] * yl_ref[...])
    o = _dot(merged.astype(BF16), wo_ref[...])
    ms = jnp.mean(o * o, axis=-1, keepdims=True)
    o = (o * lax.rsqrt(ms + RMS_EPS)) * gpost_ref[...]
    o_ref[...] = x + mod[2:3, :] * o


def _merge_out(x, mod, g_pre, g_post, wm, wo, yr, ys, yl, *, tile):
    bsz, tp, _ = x.shape
    assert tp % tile == 0
    nt = tp // tile
    act = pl.BlockSpec((None, tile, D_MODEL), lambda b, t: (b, t, 0))
    row = lambda n: pl.BlockSpec((1, n), lambda b, t: (0, 0))
    full = lambda a: pl.BlockSpec(a.shape, lambda b, t: (0,) * a.ndim)
    return pl.pallas_call(
        _merge_kernel,
        grid=(bsz, nt),
        in_specs=[act, pl.BlockSpec((None, 3, D_MODEL), lambda b, t: (b, 0, 0)), row(D_MODEL), row(D_MODEL),
                  full(wm), full(wo), act, act, act],
        out_specs=act,
        out_shape=jax.ShapeDtypeStruct((bsz, tp, D_MODEL), F32),
        compiler_params=pltpu.CompilerParams(
            dimension_semantics=("arbitrary", "arbitrary"), vmem_limit_bytes=VMEM_LIMIT),
        name="merge_out",
    )(x, mod, g_pre, g_post, wm, wo, yr, ys, yl)


def _block_diag_expand(w, per):
    n, a, b = w.shape
    w = w.reshape(n // per, per, a, b)
    eye = jnp.eye(per, dtype=w.dtype)
    return jnp.einsum("sgab,gh->sgahb", w, eye).reshape(n // per, per * a, per * b)


def _wkv_to_pairs(wkv):
    bsz = wkv.shape[0]
    st = jnp.swapaxes(wkv, -1, -2).reshape(bsz * PAIRS, 2, HEAD, HEAD)
    return _block_diag_expand(st.reshape(bsz * PAIRS * 2, HEAD, HEAD), 2).reshape(bsz, PAIRS, LANES, LANES)


def _wkv_from_pairs(st):
    bsz = st.shape[0]
    st = st.reshape(bsz, PAIRS, 2, HEAD, 2, HEAD)
    diag = jnp.stack([st[:, :, 0, :, 0, :], st[:, :, 1, :, 1, :]], axis=2)
    return jnp.swapaxes(diag.reshape(bsz, HEADS, HEAD, HEAD), -1, -2)


def _layer(x, mod, state, prm, *, t_valid, tiles):
    shift_row, wkv, s_re, s_im, lru_h, lru_conv = state
    bsz = x.shape[0]
    g_pre = prm["g_pre"]

    xr = x
    if x.shape[1] % tiles["rwkv"] != 0:
        xr = jnp.pad(x, ((0, 0), (0, tiles["rwkv"] - x.shape[1] % tiles["rwkv"]), (0, 0)))
    yz_r, shift_new, wkv_new = _rwkv_mixer(
        xr, mod, g_pre, prm["w_rwkv"], prm["w_rwkv_g"], prm["rwkv_vecs"], prm["w2a"],
        shift_row.reshape(bsz, 1, SHIFT_W), _wkv_to_pairs(wkv), tile=tiles["rwkv"], t_valid=t_valid)
    yz_r = yz_r[:, :t_valid]
    yz_s, s_re_new, s_im_new = _s5_mixer(
        x, mod, g_pre, prm["w_s5"], prm["w_s5_g"], prm["bbr"], prm["bbi"], prm["cre"], prm["cmi"],
        prm["pwr"], prm["pwi"], prm["s5_d"], prm["glu_w"], prm["glu_b"],
        s_re.reshape(bsz, 1, S5_STATE_W), s_im.reshape(bsz, 1, S5_STATE_W), tile=tiles["s5"])
    yz_l, conv_new, h_new = _lru_mixer(
        x, mod, g_pre, prm["w_lru"], prm["w_lru_g"], prm["conv_w"], prm["conv_b"], prm["wa4"], prm["ba"],
        prm["wx4"], prm["bx"], prm["lam"], lru_conv, lru_h.reshape(bsz, 1, D_MODEL), tile=tiles["lru"])
    x_new = _merge_out(x, mod, g_pre, prm["g_post"], prm["w_merge"], prm["w_out"], yz_r, yz_s, yz_l,
                       tile=tiles["merge"])
    new_state = (shift_new.reshape(bsz, SHIFT_W), _wkv_from_pairs(wkv_new),
                 s_re_new.reshape(bsz, 64, 64), s_im_new.reshape(bsz, 64, 64),
                 h_new.reshape(bsz, D_MODEL), conv_new)
    return x_new, new_state


def _layer_params(l, w_in, w_out, norm_pre, norm_post, rwkv, s5, lru):
    d = D_MODEL
    wb = w_in[l].astype(BF16)
    o = SHIFT_W
    row = lambda a: a.reshape(1, -1)
    mu, w0, w2, a0, a2, k_k, k_a, r_k, ln_g, ln_b = [a[l] for a in rwkv]
    a_re, a_im, log_step, b_re, b_im, c_re, c_im, s5_d, glu_w, glu_b = [a[l] for a in s5]
    conv_w, conv_b, wa, ba, wx, bx, lam = [a[l] for a in lru]
    zeros = jnp.zeros((LORA, d), F32)
    w2a = jnp.concatenate([jnp.concatenate([w2, zeros], axis=1), jnp.concatenate([zeros, a2], axis=1)], axis=0)
    pwr, pwi, bbr, bbi = _s5_prep(a_re, a_im, log_step, b_re, b_im)
    per = 64 // S5_SETS
    return dict(
        g_pre=row(norm_pre[l]), g_post=row(norm_post[l]),
        w_rwkv=wb[:, :o], w_rwkv_g=wb[:, o:o + d],
        w_s5=wb[:, o + d:o + 2 * d], w_s5_g=wb[:, o + 2 * d:o + 3 * d],
        w_lru=wb[:, o + 3 * d:o + 4 * d], w_lru_g=wb[:, o + 4 * d:o + 5 * d],
        w_merge=wb[:, o + 5 * d:], w_out=w_out[l].astype(BF16),
        rwkv_vecs=(row(mu), row(w0), row(a0), row(k_k), row(k_a), row(r_k), row(ln_g), row(ln_b)),
        w2a=w2a.astype(BF16),
        bbr=_block_diag_expand(bbr, per).astype(BF16), bbi=_block_diag_expand(bbi, per).astype(BF16),
        cre=_block_diag_expand(jnp.swapaxes(c_re, 1, 2), per).astype(BF16),
        cmi=_block_diag_expand(-jnp.swapaxes(c_im, 1, 2), per).astype(BF16),
        pwr=pwr.reshape(SUBLANES, S5_STATE_W), pwi=pwi.reshape(SUBLANES, S5_STATE_W),
        s5_d=row(s5_d), glu_w=glu_w.astype(BF16), glu_b=row(glu_b),
        conv_w=conv_w, conv_b=row(conv_b),
        wa4=_block_diag_expand(wa, 4).astype(BF16), ba=row(ba),
        wx4=_block_diag_expand(wx, 4).astype(BF16), bx=row(bx), lam=row(lam),
    )


def _tiles(t):
    pick = lambda want: want if t % want == 0 else t
    return dict(rwkv=256 if t % 256 == 0 else RWKV_CHUNK, s5=pick(256), lru=pick(512), merge=pick(512))


def kernel(x_prompt, x_sample, state_rwkv_shift, state_rwkv_wkv, state_s5_re, state_s5_im, state_lru_h,
           state_lru_conv, c_prompt, c_sample, ada_w, ada_b, norm_pre, norm_post, w_in, w_out, rwkv_mu, rwkv_w0,
           rwkv_w2, rwkv_a0, rwkv_a2, rwkv_k_k, rwkv_k_a, rwkv_r_k, rwkv_ln_g, rwkv_ln_b, s5_a_re, s5_a_im,
           s5_log_step, s5_b_re, s5_b_im, s5_c_re, s5_c_im, s5_d, s5_glu_w, s5_glu_b, lru_conv_w, lru_conv_b,
           lru_wa, lru_ba, lru_wx, lru_bx, lru_lambda):
    depth = w_in.shape[0]
    bp, tp, _ = x_prompt.shape
    bs, ts, _ = x_sample.shape
    c_all = jnp.concatenate([c_prompt, c_sample], axis=0)
    pad = (-c_all.shape[0]) % SUBLANES
    c_all = jnp.pad(c_all, ((0, pad), (0, 0)))
    mod_all = _modulation(c_all, ada_w, ada_b)
    rwkv = (rwkv_mu, rwkv_w0, rwkv_w2, rwkv_a0, rwkv_a2, rwkv_k_k, rwkv_k_a,
            rwkv_r_k.reshape(depth, D_MODEL), rwkv_ln_g, rwkv_ln_b)
    s5 = (s5_a_re, s5_a_im, s5_log_step, s5_b_re, s5_b_im, s5_c_re, s5_c_im, s5_d, s5_glu_w, s5_glu_b)
    lru = (lru_conv_w, lru_conv_b, lru_wa, lru_ba, lru_wx, lru_bx, lru_lambda)

    xp, xs = x_prompt, x_sample
    new_p = [[] for _ in range(6)]
    new_s = [[] for _ in range(6)]
    for l in range(depth):
        prm = _layer_params(l, w_in, w_out, norm_pre, norm_post, rwkv, s5, lru)
        mod_p = mod_all[l, :bp].reshape(bp, 3, D_MODEL)
        mod_s = mod_all[l, bp:bp + bs].reshape(bs, 3, D_MODEL)
        zero = (jnp.zeros((bp, SHIFT_W), F32), jnp.zeros((bp, HEADS, HEAD, HEAD), F32),
                jnp.zeros((bp, 64, 64), F32), jnp.zeros((bp, 64, 64), F32),
                jnp.zeros((bp, D_MODEL), F32), jnp.zeros((bp, CONV_W - 1, D_MODEL), F32))
        xp, st_p = _layer(xp, mod_p, zero, prm, t_valid=tp, tiles=_tiles(tp))
        st_in = (state_rwkv_shift[l], state_rwkv_wkv[l], state_s5_re[l], state_s5_im[l],
                 state_lru_h[l], state_lru_conv[l])
        xs, st_s = _layer(xs, mod_s, st_in, prm, t_valid=ts, tiles=_tiles(ts))
        for i in range(6):
            new_p[i].append(st_p[i])
            new_s[i].append(st_s[i])
    sp = [jnp.stack(z, axis=0) for z in new_p]
    ss = [jnp.stack(z, axis=0) for z in new_s]
    return (xp, xs, sp[0], sp[1], sp[2], sp[3], sp[4], sp[5], ss[0], ss[1], ss[2], ss[3], ss[4], ss[5])
```

```python
import functools

import jax
import jax.numpy as jnp
from jax import lax
from jax.experimental import pallas as pl
from jax.experimental.pallas import tpu as pltpu

F32 = jnp.float32
BF16 = jnp.bfloat16
HIGHEST = lax.Precision.HIGHEST

D_MODEL = 1024
HEAD = 64
HEADS = D_MODEL // HEAD
PAIRS = HEADS // 2
LORA = 64
SHIFT_W = 3 * D_MODEL + 2 * LORA
S5_STATE_W = 64 * 64
S5_SETS = 4
S5_TILES = S5_STATE_W // 128
S5_ROW_STRIDE = 40
CONV_W = 4
LRU_C = 8.0
RMS_EPS = 1e-6
GN_EPS = 64e-5
RWKV_CHUNK = 64
RWKV_INSTANCES = 16
SUBLANES = 8
LANES = 128
VMEM_LIMIT = 56 * 1024 * 1024


def _silu(x):
    return x * jax.nn.sigmoid(x)


def _softplus(x):
    return jnp.maximum(x, 0.0) + jnp.log1p(jnp.exp(-jnp.abs(x)))


def _norm_mod(x, g, mod):
    ms = jnp.mean(x * x, axis=-1, keepdims=True)
    y = (x * lax.rsqrt(ms + RMS_EPS)) * g
    return y * (1.0 + mod[1:2, :]) + mod[0:1, :]


def _dot(a, b):
    return jnp.dot(a, b, preferred_element_type=F32)


def _dot_hi(a, b):
    return jnp.dot(a, b, preferred_element_type=F32, precision=HIGHEST)


def _split2(x):
    hi = x.astype(BF16)
    lo = (x - hi.astype(F32)).astype(BF16)
    return hi, lo


_NN = (((1,), (0,)), ((), ()))
_NT = (((1,), (1,)), ((), ()))
_TN = (((0,), (0,)), ((), ()))


def _mm(a, b, dims=_NN):
    return lax.dot_general(a.astype(BF16), b.astype(BF16), dims, preferred_element_type=F32)


def _mm3(a_parts, b_parts):
    (a_hi, a_lo), (b_hi, b_lo) = a_parts, b_parts
    return _dot(a_hi, b_hi) + (_dot(a_hi, b_lo) + _dot(a_lo, b_hi))


def _head_sum(x, ones_blk):
    outs = []
    for s in range(D_MODEL // 256):
        hi, lo = _split2(x[:, 256 * s:256 * (s + 1)])
        outs.append(_dot(hi, ones_blk) + _dot(lo, ones_blk))
    return jnp.concatenate(outs, axis=1)


def _block_ones(n, blk):
    ri = lax.broadcasted_iota(jnp.int32, (n, n), 0)
    ci = lax.broadcasted_iota(jnp.int32, (n, n), 1)
    sh = blk.bit_length() - 1
    return ((ri >> sh) == (ci >> sh)).astype(F32).astype(BF16)


def _mod_kernel(c_ref, w_ref, b_ref, o_ref):
    s = _silu(c_ref[...])
    o_ref[...] = _dot_hi(s, w_ref[...]) + b_ref[...]


def _modulation(c_all, ada_w, ada_b):
    depth = ada_w.shape[0]
    rows = c_all.shape[0]
    return pl.pallas_call(
        _mod_kernel,
        grid=(depth, 3),
        in_specs=[
            pl.BlockSpec((rows, D_MODEL), lambda l, j: (0, 0)),
            pl.BlockSpec((None, D_MODEL, D_MODEL), lambda l, j: (l, 0, j)),
            pl.BlockSpec((None, 1, D_MODEL), lambda l, j: (l, 0, j)),
        ],
        out_specs=pl.BlockSpec((None, rows, D_MODEL), lambda l, j: (l, 0, j)),
        out_shape=jax.ShapeDtypeStruct((depth, rows, 3 * D_MODEL), F32),
        name="adaln_mod",
    )(c_all, ada_w, ada_b.reshape(depth, 1, 3 * D_MODEL))


def _s5_prep_kernel(are_ref, aim_ref, ls_ref, bre_ref, bim_ref, abr_ref, abi_ref, bbr_ref, bbi_ref):
    are = are_ref[...]
    aim = aim_ref[...]
    dt = jnp.exp(ls_ref[...])
    mag = jnp.exp(are * dt)
    abr = mag * jnp.cos(aim * dt)
    abi = mag * jnp.sin(aim * dt)
    abr_ref[...] = abr
    abi_ref[...] = abi
    nr = abr - 1.0
    ni = abi
    den = are * are + aim * aim
    cr = (nr * are + ni * aim) / den
    ci = (ni * are - nr * aim) / den
    bre = bre_ref[...]
    bim = bim_ref[...]
    bbr_ref[...] = cr[:, None, :] * bre - ci[:, None, :] * bim
    bbi_ref[...] = cr[:, None, :] * bim + ci[:, None, :] * bre


def _s5_prep(a_re, a_im, log_step, b_re, b_im):
    g, p = a_re.shape
    i = b_re.shape[-1]
    return pl.pallas_call(
        _s5_prep_kernel,
        out_shape=(
            jax.ShapeDtypeStruct((g, p), F32),
            jax.ShapeDtypeStruct((g, p), F32),
            jax.ShapeDtypeStruct((g, i, p), F32),
            jax.ShapeDtypeStruct((g, i, p), F32),
        ),
        name="s5_discretise",
    )(a_re, a_im, log_step.reshape(g, 1), jnp.swapaxes(b_re, 1, 2), jnp.swapaxes(b_im, 1, 2))


def _rwkv_kernel(x_ref, mod_ref, g_ref, wr_ref, wg_ref, mu_ref, w0_ref, w2a_ref, a0_ref, kk_ref, ka_ref,
                 rk_ref, lng_ref, lnb_ref, shift0_ref, wkv0_ref,
                 yz_ref, shift_out_ref, wkv_out_ref,
                 carry_ref, st_ref, at_ref, rt_ref, bt_ref, kt_ref, bh_ref, kh_ref, v_ref, pl_ref, y_ref,
                 nh_ref, t_ref, aak_ref, ark_ref, rhs_ref, gm_ref, qe_ref, hc_ref, yc_ref,
                 *, tile, chunk, t_valid, group):
    t = pl.program_id(1)
    n_chunks = tile // chunk
    two = 2 * chunk
    n_inst = group * n_chunks

    @pl.when(t == 0)
    def _():
        carry_ref[...] = shift0_ref[...]
        st_ref[...] = wkv0_ref[...]

    h = _norm_mod(x_ref[...], g_ref[...], mod_ref[...]).astype(BF16)
    p = _dot(h, wr_ref[...])
    zg = _silu(_dot(h, wg_ref[...]))

    rows = lax.broadcasted_iota(jnp.int32, (tile, 1), 0)
    prev = jnp.where(rows == 0, carry_ref[...], pltpu.roll(p, 1, 0))
    last = (t_valid - 1) % tile
    carry_ref[...] = p[last:last + 1, :]
    pm = p + (prev - p) * mu_ref[...]

    r = pm[:, 0:D_MODEL]
    k = pm[:, D_MODEL:2 * D_MODEL]
    v = pm[:, 2 * D_MODEL:3 * D_MODEL]
    wa = pm[:, 3 * D_MODEL:SHIFT_W]
    lane = lax.broadcasted_iota(jnp.int32, (1, LANES), 1)
    wa = jnp.where(lane < LORA, jnp.tanh(wa), wa)
    lora = _dot(wa.astype(BF16), w2a_ref[...])
    w_log = -_softplus(-(w0_ref[...] + lora[:, :D_MODEL])) - 0.5
    logd = -jnp.exp(w_log)
    a_sig = jax.nn.sigmoid(a0_ref[...] + lora[:, D_MODEL:])

    ones_blk = _block_ones(256, HEAD)
    kk = k * kk_ref[...]
    kk = kk / jnp.maximum(jnp.sqrt(_head_sum(kk * kk, ones_blk)), 1e-12)
    k2 = k * (1.0 + (a_sig - 1.0) * ka_ref[...])
    bvec = kk * a_sig
    if t_valid % tile != 0:
        ok = rows < t_valid
        logd = jnp.where(ok, logd, 0.0)
        bvec = jnp.where(ok, bvec, 0.0)
        k2 = jnp.where(ok, k2, 0.0)

    rmod = rows & (chunk - 1)
    cum = logd
    s = 1
    while s < chunk:
        cum = cum + jnp.where(rmod >= s, pltpu.roll(cum, s, 0), 0.0)
        s *= 2
    cum_last = jnp.concatenate(
        [jnp.broadcast_to(cum[(j + 1) * chunk - 1:(j + 1) * chunk, :], (chunk, D_MODEL)) for j in range(n_chunks)],
        axis=0)
    p_inc = jnp.exp(cum)
    p_inv = jnp.exp(-cum)
    p_exc = jnp.exp(cum - logd)
    p_end = jnp.exp(cum_last - cum)

    def to_pairs(ref, val):
        for q in range(PAIRS):
            ref[q] = val[:, LANES * q:LANES * (q + 1)]

    to_pairs(at_ref, -kk * p_exc)
    to_pairs(rt_ref, r * p_inc)
    to_pairs(bt_ref, bvec * p_inv)
    to_pairs(kt_ref, k2 * p_inv)
    to_pairs(bh_ref, bvec * p_end)
    to_pairs(kh_ref, k2 * p_end)
    to_pairs(v_ref, v)
    for j in range(n_chunks):
        row = jnp.exp(cum[(j + 1) * chunk - 1:(j + 1) * chunk, :])
        for q in range(PAIRS):
            pl_ref[q, j] = jnp.broadcast_to(row[:, LANES * q:LANES * (q + 1)], (SUBLANES, LANES))

    m_a = (lane < HEAD).astype(F32)
    m_b = 1.0 - m_a
    ri = lax.broadcasted_iota(jnp.int32, (two, two), 0)
    ci = lax.broadcasted_iota(jnp.int32, (two, two), 1)
    csh = chunk.bit_length() - 1
    same_head = (ri >> csh) == (ci >> csh)
    strict = jnp.logical_and(same_head, (ri & (chunk - 1)) > (ci & (chunk - 1)))
    incl = jnp.logical_and(same_head, (ri & (chunk - 1)) >= (ci & (chunk - 1)))
    eye = ri == ci
    off_masks = []
    for lg in range(csh):
        off_masks.append(jnp.logical_and(strict, jnp.logical_and((ri >> (lg + 1)) == (ci >> (lg + 1)),
                                                                 (ri >> lg) != (ci >> lg))))

    def stack(x):
        return jnp.concatenate([x * m_a, x * m_b], axis=0)

    gsh = group.bit_length() - 1
    unroll = min(16, n_inst)

    def group_body(g, carry):
        def where_is(i):
            q = g * group + (i & (group - 1))
            j = i >> gsh
            return q, j, pl.ds(pl.multiple_of(j * chunk, chunk), chunk)

        def gram(i, c):
            q, _, sl = where_is(i)
            atm = stack(at_ref[q, sl, :])
            rtm = stack(rt_ref[q, sl, :])
            bt = bt_ref[q, sl, :]
            kt = kt_ref[q, sl, :]
            o1 = _mm(jnp.concatenate([atm, rtm], axis=0), jnp.concatenate([bt, bt, kt, kt], axis=0), _NT)
            n_ab = jnp.where(strict, o1[:two, :two], 0.0)
            nh_ref[i] = n_ab.astype(BF16)
            t_ref[i] = jnp.where(eye, 1.0, jnp.where(off_masks[0], n_ab, 0.0)).astype(BF16)
            aak_ref[i] = jnp.where(strict, o1[:two, two:], 0.0).astype(BF16)
            ark_ref[i] = jnp.concatenate([jnp.where(incl, o1[two:, :two], 0.0),
                                          jnp.where(incl, o1[two:, two:], 0.0)], axis=1).astype(BF16)
            return c

        lax.fori_loop(0, n_inst, gram, 0, unroll=unroll)

        for off in off_masks[1:]:
            def level(i, c, off=off):
                t_inv = t_ref[i]
                x = _dot(jnp.where(off, nh_ref[i], 0.0), t_inv).astype(BF16)
                t_ref[i] = t_inv + _dot(t_inv, x).astype(BF16)
                return c

            lax.fori_loop(0, n_inst, level, 0, unroll=unroll)

        def apply(i, c):
            q, _, sl = where_is(i)
            atm = stack(at_ref[q, sl, :])
            vm = stack(v_ref[q, sl, :])
            t_inv = t_ref[i]
            z0_hi, z0_lo = _split2(jnp.concatenate([atm, _mm(aak_ref[i], vm)], axis=1))
            z = _dot(t_inv, z0_hi) + _dot(t_inv, z0_lo)
            rhs_ref[i] = jnp.concatenate([z, jnp.concatenate([jnp.zeros_like(vm), vm], axis=1)], axis=0).astype(BF16)
            return c

        lax.fori_loop(0, n_inst, apply, 0, unroll=unroll)

        def operators(i, c):
            q, j, sl = where_is(i)
            rhs2 = rhs_ref[i]
            top = _mm(jnp.concatenate([stack(bh_ref[q, sl, :]), stack(kh_ref[q, sl, :])], axis=0), rhs2, _TN)
            bot = _dot(ark_ref[i], rhs2)
            gm_ref[i] = (top[:, :two]
                         + jnp.where(eye, jnp.broadcast_to(pl_ref[q, j][0:1, :], (two, two)), 0.0)).astype(BF16)
            hc_ref[i] = top[:, two:]
            qe_ref[i] = (bot[:, :two] + stack(rt_ref[q, sl, :])).astype(BF16)
            yc_ref[i] = bot[:, two:]
            return c

        lax.fori_loop(0, n_inst, operators, 0, unroll=unroll)

        for j in range(n_chunks):
            def advance(ql, c, j=j):
                q = g * group + ql
                i = j * group + ql
                st = st_ref[q].astype(BF16)
                ym = _dot(qe_ref[i], st) + yc_ref[i]
                y_ref[q, j * chunk:(j + 1) * chunk, :] = ym[:chunk, :] + ym[chunk:, :]
                st_ref[q] = _dot(gm_ref[i], st) + hc_ref[i]
                return c

            lax.fori_loop(0, group, advance, 0, unroll=True)
        return carry

    lax.fori_loop(0, PAIRS // group, group_body, 0)

    ys = jnp.concatenate([y_ref[q] for q in range(PAIRS)], axis=1)
    mean = _head_sum(ys, ones_blk) * (1.0 / HEAD)
    yc = ys - mean
    var = _head_sum(yc * yc, ones_blk) * (1.0 / HEAD)
    yn = yc * lax.rsqrt(var + GN_EPS) * lng_ref[...] + lnb_ref[...]
    bonus = _head_sum(r * k2 * rk_ref[...], ones_blk) * v
    yz_ref[...] = (yn + bonus) * zg

    @pl.when(t == pl.num_programs(1) - 1)
    def _():
        shift_out_ref[...] = carry_ref[...]
        wkv_out_ref[...] = st_ref[...]


def _rwkv_mixer(x, mod, g_pre, wr, wg, vecs, w2a, shift0, wkv0, *, tile, t_valid):
    bsz, tp, _ = x.shape
    chunk = RWKV_CHUNK
    assert tp % tile == 0 and tile % chunk == 0
    assert t_valid == tp or tp == tile
    nt = tp // tile
    mu, w0, a0, k_k, k_a, r_k, ln_g, ln_b = vecs
    row = lambda n: pl.BlockSpec((1, n), lambda b, t: (0, 0))
    full = lambda a: pl.BlockSpec(a.shape, lambda b, t: (0,) * a.ndim)
    pair_buf = pltpu.VMEM((PAIRS, tile, LANES), F32)
    group = min(PAIRS, max(1, RWKV_INSTANCES // (tile // chunk)))
    n_inst = group * (tile // chunk)
    two = 2 * chunk
    inst = lambda cols, dtype: pltpu.VMEM((n_inst, two, cols), dtype)
    kern = functools.partial(_rwkv_kernel, tile=tile, chunk=chunk, t_valid=t_valid, group=group)
    return pl.pallas_call(
        kern,
        grid=(bsz, nt),
        in_specs=[
            pl.BlockSpec((None, tile, D_MODEL), lambda b, t: (b, t, 0)),
            pl.BlockSpec((None, 3, D_MODEL), lambda b, t: (b, 0, 0)),
            row(D_MODEL), full(wr), full(wg), row(SHIFT_W), row(D_MODEL), full(w2a), row(D_MODEL),
            row(D_MODEL), row(D_MODEL), row(D_MODEL), row(D_MODEL), row(D_MODEL),
            pl.BlockSpec((None, 1, SHIFT_W), lambda b, t: (b, 0, 0)),
            pl.BlockSpec((None, PAIRS, LANES, LANES), lambda b, t: (b, 0, 0, 0)),
        ],
        out_specs=[
            pl.BlockSpec((None, tile, D_MODEL), lambda b, t: (b, t, 0)),
            pl.BlockSpec((None, 1, SHIFT_W), lambda b, t: (b, 0, 0)),
            pl.BlockSpec((None, PAIRS, LANES, LANES), lambda b, t: (b, 0, 0, 0)),
        ],
        out_shape=[
            jax.ShapeDtypeStruct((bsz, tp, D_MODEL), F32),
            jax.ShapeDtypeStruct((bsz, 1, SHIFT_W), F32),
            jax.ShapeDtypeStruct((bsz, PAIRS, LANES, LANES), F32),
        ],
        scratch_shapes=[
            pltpu.VMEM((1, SHIFT_W), F32),
            pltpu.VMEM((PAIRS, LANES, LANES), F32),
            pair_buf, pair_buf, pair_buf, pair_buf, pair_buf, pair_buf, pair_buf,
            pltpu.VMEM((PAIRS, tile // chunk, SUBLANES, LANES), F32),
            pair_buf,
            inst(two, BF16), inst(two, BF16), inst(two, BF16), inst(2 * two, BF16),
            pltpu.VMEM((n_inst, 2 * two, 2 * two), BF16),
            inst(two, BF16), inst(two, BF16), inst(two, F32), inst(two, F32),
        ],
        compiler_params=pltpu.CompilerParams(
            dimension_semantics=("arbitrary", "arbitrary"), vmem_limit_bytes=VMEM_LIMIT),
        name="rwkv7_mixer",
    )(x, mod, g_pre, wr, wg, mu, w0, w2a, a0, k_k, k_a, r_k, ln_g, ln_b, shift0, wkv0)


def _s5_kernel(x_ref, mod_ref, g_ref, wu_ref, wg_ref, bbr_ref, bbi_ref, cre_ref, cmi_ref, abr_ref, abi_ref,
               d_ref, gluw_ref, glub_ref, s0r_ref, s0i_ref,
               yz_ref, sr_out_ref, si_out_ref,
               hr_ref, hi_ref, cr_ref, ci_ref, *, tile):
    t = pl.program_id(1)

    @pl.when(t == 0)
    def _():
        cr_ref[...] = s0r_ref[...]
        ci_ref[...] = s0i_ref[...]

    h = _norm_mod(x_ref[...], g_ref[...], mod_ref[...]).astype(BF16)
    u = _dot(h, wu_ref[...])
    zg = _silu(_dot(h, wg_ref[...]))
    ch_w = D_MODEL // S5_SETS
    set_tiles = S5_TILES // S5_SETS

    def frame_rows(c):
        return pl.ds(c, tile, stride=S5_ROW_STRIDE)

    for s in range(S5_SETS):
        ub = u[:, ch_w * s:ch_w * (s + 1)].astype(BF16)
        bur = _dot(ub, bbr_ref[s])
        bui = _dot(ub, bbi_ref[s])
        for c in range(set_tiles):
            hr_ref[frame_rows(s * set_tiles + c), :] = bur[:, LANES * c:LANES * (c + 1)]
            hi_ref[frame_rows(s * set_tiles + c), :] = bui[:, LANES * c:LANES * (c + 1)]

    abr = abr_ref[...]
    abi = abi_ref[...]

    def frame(f, carry):
        sr, si = carry
        rows = pl.ds(pl.multiple_of(f * S5_ROW_STRIDE, SUBLANES), S5_TILES)
        nr = abr * sr - abi * si + hr_ref[rows, :]
        ni = abr * si + abi * sr + hi_ref[rows, :]
        hr_ref[rows, :] = nr
        hi_ref[rows, :] = ni
        return nr, ni

    sr, si = lax.fori_loop(0, tile, frame, (cr_ref[...], ci_ref[...]), unroll=8)
    cr_ref[...] = sr
    ci_ref[...] = si

    outs = []
    for s in range(S5_SETS):
        hr = jnp.concatenate([hr_ref[frame_rows(s * set_tiles + c), :] for c in range(set_tiles)], axis=1)
        hi = jnp.concatenate([hi_ref[frame_rows(s * set_tiles + c), :] for c in range(set_tiles)], axis=1)
        outs.append(_dot(hr.astype(BF16), cre_ref[s]) + _dot(hi.astype(BF16), cmi_ref[s]))
    y = jnp.concatenate(outs, axis=1) + d_ref[...] * u
    y = 0.5 * y * (1.0 + jnp.tanh(0.7978845608028654 * (y + 0.044715 * (y * y * y))))
    y = y * jax.nn.sigmoid(_dot(y.astype(BF16), gluw_ref[...]) + glub_ref[...])
    yz_ref[...] = y * zg

    @pl.when(t == pl.num_programs(1) - 1)
    def _():
        sr_out_ref[...] = cr_ref[...]
        si_out_ref[...] = ci_ref[...]


def _s5_mixer(x, mod, g_pre, wu, wg, bbr, bbi, cre, cmi, abr, abi, d_skip, glu_w, glu_b, s0r, s0i, *, tile):
    bsz, tp, _ = x.shape
    assert tp % tile == 0 and tile % SUBLANES == 0
    nt = tp // tile
    row = lambda n: pl.BlockSpec((1, n), lambda b, t: (0, 0))
    full = lambda a: pl.BlockSpec(a.shape, lambda b, t: (0,) * a.ndim)
    state = pl.BlockSpec((None, S5_TILES, LANES), lambda b, t: (b, 0, 0))
    kern = functools.partial(_s5_kernel, tile=tile)
    return pl.pallas_call(
        kern,
        grid=(bsz, nt),
        in_specs=[
            pl.BlockSpec((None, tile, D_MODEL), lambda b, t: (b, t, 0)),
            pl.BlockSpec((None, 3, D_MODEL), lambda b, t: (b, 0, 0)),
            row(D_MODEL), full(wu), full(wg), full(bbr), full(bbi), full(cre), full(cmi), full(abr), full(abi),
            row(D_MODEL), full(glu_w), row(D_MODEL), state, state,
        ],
        out_specs=[pl.BlockSpec((None, tile, D_MODEL), lambda b, t: (b, t, 0)), state, state],
        out_shape=[
            jax.ShapeDtypeStruct((bsz, tp, D_MODEL), F32),
            jax.ShapeDtypeStruct((bsz, S5_TILES, LANES), F32),
            jax.ShapeDtypeStruct((bsz, S5_TILES, LANES), F32),
        ],
        scratch_shapes=[
            pltpu.VMEM((tile * S5_ROW_STRIDE, LANES), F32),
            pltpu.VMEM((tile * S5_ROW_STRIDE, LANES), F32),
            pltpu.VMEM((S5_TILES, LANES), F32),
            pltpu.VMEM((S5_TILES, LANES), F32),
        ],
        compiler_params=pltpu.CompilerParams(
            dimension_semantics=("arbitrary", "arbitrary"), vmem_limit_bytes=VMEM_LIMIT),
        name="s5_mixer",
    )(x, mod, g_pre, wu, wg, bbr, bbi, cre, cmi, abr, abi, d_skip, glu_w, glu_b, s0r, s0i)


def _lru_kernel(x_ref, mod_ref, g_ref, wx_ref, wg_ref, cw_ref, cb_ref, wa4_ref, ba_ref, wx4_ref, bx_ref, lam_ref,
                conv0_ref, h0_ref,
                yz_ref, conv_out_ref, h_out_ref,
                xbuf_ref, a_ref, b_ref, hc_ref, *, tile):
    t = pl.program_id(1)
    pad = SUBLANES
    hist = CONV_W - 1

    @pl.when(t == 0)
    def _():
        xbuf_ref[pad - hist:pad, :] = conv0_ref[...]
        hc_ref[...] = h0_ref[...]

    h = _norm_mod(x_ref[...], g_ref[...], mod_ref[...]).astype(BF16)
    xl = _dot(h, wx_ref[...])
    zg = _silu(_dot(h, wg_ref[...]))
    xbuf_ref[pad:pad + tile, :] = xl
    xc = cb_ref[...] + xl * cw_ref[CONV_W - 1:CONV_W, :]
    for j in range(hist):
        xc = xc + xbuf_ref[pad - hist + j:pad - hist + j + tile, :] * cw_ref[j:j + 1, :]
    tail = xbuf_ref[pad + tile - hist:pad + tile, :]
    xbuf_ref[pad - hist:pad, :] = tail

    ga, gx = [], []
    blk = D_MODEL // 4
    for s in range(4):
        xb = xc[:, blk * s:blk * (s + 1)].astype(BF16)
        ga.append(_dot(xb, wa4_ref[s]))
        gx.append(_dot(xb, wx4_ref[s]))
    gate_r = jax.nn.sigmoid(jnp.concatenate(ga, axis=1) + ba_ref[...])
    gate_i = jax.nn.sigmoid(jnp.concatenate(gx, axis=1) + bx_ref[...])
    log_a = -LRU_C * gate_r * _softplus(-lam_ref[...])
    a = jnp.exp(log_a)
    a_ref[...] = a
    b_ref[...] = jnp.sqrt(-jnp.tanh(log_a) * (a * a + 1.0)) * (gate_i * xc)

    row8 = lax.broadcasted_iota(jnp.int32, (SUBLANES, 1), 0)

    def row_body(g, carry):
        rs = pl.ds(pl.multiple_of(g * SUBLANES, SUBLANES), SUBLANES)
        av = a_ref[rs, :]
        bv = b_ref[rs, :]
        for sh in (1, 2, 4):
            keep = row8 >= sh
            ash = jnp.where(keep, pltpu.roll(av, sh, 0), 1.0)
            bsh = jnp.where(keep, pltpu.roll(bv, sh, 0), 0.0)
            bv = bv + av * bsh
            av = av * ash
        hs = bv + av * carry
        b_ref[rs, :] = hs
        return hs[SUBLANES - 1:SUBLANES, :]

    hc_ref[...] = lax.fori_loop(0, tile // SUBLANES, row_body, hc_ref[...])
    yz_ref[...] = b_ref[...] * zg

    @pl.when(t == pl.num_programs(1) - 1)
    def _():
        conv_out_ref[...] = tail
        h_out_ref[...] = hc_ref[...]


def _lru_mixer(x, mod, g_pre, wx, wg, conv_w, conv_b, wa4, ba, wx4, bx, lam, conv0, h0, *, tile):
    bsz, tp, _ = x.shape
    assert tp % tile == 0 and tile % SUBLANES == 0 and tile >= SUBLANES
    nt = tp // tile
    row = lambda n: pl.BlockSpec((1, n), lambda b, t: (0, 0))
    full = lambda a: pl.BlockSpec(a.shape, lambda b, t: (0,) * a.ndim)
    kern = functools.partial(_lru_kernel, tile=tile)
    return pl.pallas_call(
        kern,
        grid=(bsz, nt),
        in_specs=[
            pl.BlockSpec((None, tile, D_MODEL), lambda b, t: (b, t, 0)),
            pl.BlockSpec((None, 3, D_MODEL), lambda b, t: (b, 0, 0)),
            row(D_MODEL), full(wx), full(wg), full(conv_w), row(D_MODEL), full(wa4), row(D_MODEL), full(wx4),
            row(D_MODEL), row(D_MODEL),
            pl.BlockSpec((None, CONV_W - 1, D_MODEL), lambda b, t: (b, 0, 0)),
            pl.BlockSpec((None, 1, D_MODEL), lambda b, t: (b, 0, 0)),
        ],
        out_specs=[
            pl.BlockSpec((None, tile, D_MODEL), lambda b, t: (b, t, 0)),
            pl.BlockSpec((None, CONV_W - 1, D_MODEL), lambda b, t: (b, 0, 0)),
            pl.BlockSpec((None, 1, D_MODEL), lambda b, t: (b, 0, 0)),
        ],
        out_shape=[
            jax.ShapeDtypeStruct((bsz, tp, D_MODEL), F32),
            jax.ShapeDtypeStruct((bsz, CONV_W - 1, D_MODEL), F32),
            jax.ShapeDtypeStruct((bsz, 1, D_MODEL), F32),
        ],
        scratch_shapes=[
            pltpu.VMEM((tile + SUBLANES, D_MODEL), F32),
            pltpu.VMEM((tile, D_MODEL), F32),
            pltpu.VMEM((tile, D_MODEL), F32),
            pltpu.VMEM((1, D_MODEL), F32),
        ],
        compiler_params=pltpu.CompilerParams(
            dimension_semantics=("arbitrary", "arbitrary"), vmem_limit_bytes=VMEM_LIMIT),
        name="rglru_mixer",
    )(x, mod, g_pre, wx, wg, conv_w, conv_b, wa4, ba, wx4, bx, lam, conv0, h0)


def _merge_kernel(x_ref, mod_ref, g_ref, gpost_ref, wm_ref, wo_ref, yr_ref, ys_ref, yl_ref, o_ref):
    x = x_ref[...]
    mod = mod_ref[...]
    h = _norm_mod(x, g_ref[...], mod).astype(BF16)
    m = jax.nn.sigmoid(_dot(h, wm_ref[...]))
    merged = (m[:, :D_MODEL] * yr_ref[...] + m[:, D_MODEL:2 * D_MODEL] * ys_ref[...]
              + m[:, 2 * D_MODEL:] * yl_ref[...])
    o = _dot(merged.astype(BF16), wo_ref[...])
    ms = jnp.mean(o * o, axis=-1, keepdims=True)
    o = (o * lax.rsqrt(ms + RMS_EPS)) * gpost_ref[...]
    o_ref[...] = x + mod[2:3, :] * o


def _merge_out(x, mod, g_pre, g_post, wm, wo, yr, ys, yl, *, tile):
    bsz, tp, _ = x.shape
    assert tp % tile == 0
    nt = tp // tile
    act = pl.BlockSpec((None, tile, D_MODEL), lambda b, t: (b, t, 0))
    row = lambda n: pl.BlockSpec((1, n), lambda b, t: (0, 0))
    full = lambda a: pl.BlockSpec(a.shape, lambda b, t: (0,) * a.ndim)
    return pl.pallas_call(
        _merge_kernel,
        grid=(bsz, nt),
        in_specs=[act, pl.BlockSpec((None, 3, D_MODEL), lambda b, t: (b, 0, 0)), row(D_MODEL), row(D_MODEL),
                  full(wm), full(wo), act, act, act],
        out_specs=act,
        out_shape=jax.ShapeDtypeStruct((bsz, tp, D_MODEL), F32),
        compiler_params=pltpu.CompilerParams(
            dimension_semantics=("arbitrary", "arbitrary"), vmem_limit_bytes=VMEM_LIMIT),
        name="merge_out",
    )(x, mod, g_pre, g_post, wm, wo, yr, ys, yl)


def _block_diag_expand(w, per):
    n, a, b = w.shape
    w = w.reshape(n // per, per, a, b)
    eye = jnp.eye(per, dtype=w.dtype)
    return jnp.einsum("sgab,gh->sgahb", w, eye).reshape(n // per, per * a, per * b)


def _wkv_to_pairs(wkv):
    bsz = wkv.shape[0]
    st = jnp.swapaxes(wkv, -1, -2).reshape(bsz * PAIRS, 2, HEAD, HEAD)
    return _block_diag_expand(st.reshape(bsz * PAIRS * 2, HEAD, HEAD), 2).reshape(bsz, PAIRS, LANES, LANES)


def _wkv_from_pairs(st):
    bsz = st.shape[0]
    st = st.reshape(bsz, PAIRS, 2, HEAD, 2, HEAD)
    diag = jnp.stack([st[:, :, 0, :, 0, :], st[:, :, 1, :, 1, :]], axis=2)
    return jnp.swapaxes(diag.reshape(bsz, HEADS, HEAD, HEAD), -1, -2)


def _layer(x, mod, state, prm, *, t_valid, tiles):
    shift_row, wkv, s_re, s_im, lru_h, lru_conv = state
    bsz = x.shape[0]
    g_pre = prm["g_pre"]

    xr = x
    if x.shape[1] % tiles["rwkv"] != 0:
        xr = jnp.pad(x, ((0, 0), (0, tiles["rwkv"] - x.shape[1] % tiles["rwkv"]), (0, 0)))
    yz_r, shift_new, wkv_new = _rwkv_mixer(
        xr, mod, g_pre, prm["w_rwkv"], prm["w_rwkv_g"], prm["rwkv_vecs"], prm["w2a"],
        shift_row.reshape(bsz, 1, SHIFT_W), _wkv_to_pairs(wkv), tile=tiles["rwkv"], t_valid=t_valid)
    yz_r = yz_r[:, :t_valid]
    yz_s, s_re_new, s_im_new = _s5_mixer(
        x, mod, g_pre, prm["w_s5"], prm["w_s5_g"], prm["bbr"], prm["bbi"], prm["cre"], prm["cmi"],
        prm["abr"], prm["abi"], prm["s5_d"], prm["glu_w"], prm["glu_b"],
        s_re.reshape(bsz, S5_TILES, LANES), s_im.reshape(bsz, S5_TILES, LANES), tile=tiles["s5"])
    yz_l, conv_new, h_new = _lru_mixer(
        x, mod, g_pre, prm["w_lru"], prm["w_lru_g"], prm["conv_w"], prm["conv_b"], prm["wa4"], prm["ba"],
        prm["wx4"], prm["bx"], prm["lam"], lru_conv, lru_h.reshape(bsz, 1, D_MODEL), tile=tiles["lru"])
    x_new = _merge_out(x, mod, g_pre, prm["g_post"], prm["w_merge"], prm["w_out"], yz_r, yz_s, yz_l,
                       tile=tiles["merge"])
    new_state = (shift_new.reshape(bsz, SHIFT_W), _wkv_from_pairs(wkv_new),
                 s_re_new.reshape(bsz, 64, 64), s_im_new.reshape(bsz, 64, 64),
                 h_new.reshape(bsz, D_MODEL), conv_new)
    return x_new, new_state


def _layer_params(l, w_in, w_out, norm_pre, norm_post, rwkv, s5, lru):
    d = D_MODEL
    wb = w_in[l].astype(BF16)
    o = SHIFT_W
    row = lambda a: a.reshape(1, -1)
    mu, w0, w2, a0, a2, k_k, k_a, r_k, ln_g, ln_b = [a[l] for a in rwkv]
    a_re, a_im, log_step, b_re, b_im, c_re, c_im, s5_d, glu_w, glu_b = [a[l] for a in s5]
    conv_w, conv_b, wa, ba, wx, bx, lam = [a[l] for a in lru]
    zeros = jnp.zeros((LORA, d), F32)
    w2a = jnp.concatenate([jnp.concatenate([w2, zeros], axis=1), jnp.concatenate([zeros, a2], axis=1)], axis=0)
    abr, abi, bbr, bbi = _s5_prep(a_re, a_im, log_step, b_re, b_im)
    per = 64 // S5_SETS
    return dict(
        g_pre=row(norm_pre[l]), g_post=row(norm_post[l]),
        w_rwkv=wb[:, :o], w_rwkv_g=wb[:, o:o + d],
        w_s5=wb[:, o + d:o + 2 * d], w_s5_g=wb[:, o + 2 * d:o + 3 * d],
        w_lru=wb[:, o + 3 * d:o + 4 * d], w_lru_g=wb[:, o + 4 * d:o + 5 * d],
        w_merge=wb[:, o + 5 * d:], w_out=w_out[l].astype(BF16),
        rwkv_vecs=(row(mu), row(w0), row(a0), row(k_k), row(k_a), row(r_k), row(ln_g), row(ln_b)),
        w2a=w2a.astype(BF16),
        bbr=_block_diag_expand(bbr, per).astype(BF16), bbi=_block_diag_expand(bbi, per).astype(BF16),
        cre=_block_diag_expand(jnp.swapaxes(c_re, 1, 2), per).astype(BF16),
        cmi=_block_diag_expand(-jnp.swapaxes(c_im, 1, 2), per).astype(BF16),
        abr=abr.reshape(S5_TILES, LANES), abi=abi.reshape(S5_TILES, LANES),
        s5_d=row(s5_d), glu_w=glu_w.astype(BF16), glu_b=row(glu_b),
        conv_w=conv_w, conv_b=row(conv_b),
        wa4=_block_diag_expand(wa, 4).astype(BF16), ba=row(ba),
        wx4=_block_diag_expand(wx, 4).astype(BF16), bx=row(bx), lam=row(lam),
    )


def _tiles(t):
    pick = lambda want: want if t % want == 0 else t
    return dict(rwkv=256 if t % 256 == 0 else RWKV_CHUNK, s5=pick(256), lru=pick(512), merge=pick(512))


def kernel(x_prompt, x_sample, state_rwkv_shift, state_rwkv_wkv, state_s5_re, state_s5_im, state_lru_h,
           state_lru_conv, c_prompt, c_sample, ada_w, ada_b, norm_pre, norm_post, w_in, w_out, rwkv_mu, rwkv_w0,
           rwkv_w2, rwkv_a0, rwkv_a2, rwkv_k_k, rwkv_k_a, rwkv_r_k, rwkv_ln_g, rwkv_ln_b, s5_a_re, s5_a_im,
           s5_log_step, s5_b_re, s5_b_im, s5_c_re, s5_c_im, s5_d, s5_glu_w, s5_glu_b, lru_conv_w, lru_conv_b,
           lru_wa, lru_ba, lru_wx, lru_bx, lru_lambda):
    depth = w_in.shape[0]
    bp, tp, _ = x_prompt.shape
    bs, ts, _ = x_sample.shape
    c_all = jnp.concatenate([c_prompt, c_sample], axis=0)
    pad = (-c_all.shape[0]) % SUBLANES
    c_all = jnp.pad(c_all, ((0, pad), (0, 0)))
    mod_all = _modulation(c_all, ada_w, ada_b)
    rwkv = (rwkv_mu, rwkv_w0, rwkv_w2, rwkv_a0, rwkv_a2, rwkv_k_k, rwkv_k_a,
            rwkv_r_k.reshape(depth, D_MODEL), rwkv_ln_g, rwkv_ln_b)
    s5 = (s5_a_re, s5_a_im, s5_log_step, s5_b_re, s5_b_im, s5_c_re, s5_c_im, s5_d, s5_glu_w, s5_glu_b)
    lru = (lru_conv_w, lru_conv_b, lru_wa, lru_ba, lru_wx, lru_bx, lru_lambda)

    xp, xs = x_prompt, x_sample
    new_p = [[] for _ in range(6)]
    new_s = [[] for _ in range(6)]
    for l in range(depth):
        prm = _layer_params(l, w_in, w_out, norm_pre, norm_post, rwkv, s5, lru)
        mod_p = mod_all[l, :bp].reshape(bp, 3, D_MODEL)
        mod_s = mod_all[l, bp:bp + bs].reshape(bs, 3, D_MODEL)
        zero = (jnp.zeros((bp, SHIFT_W), F32), jnp.zeros((bp, HEADS, HEAD, HEAD), F32),
                jnp.zeros((bp, 64, 64), F32), jnp.zeros((bp, 64, 64), F32),
                jnp.zeros((bp, D_MODEL), F32), jnp.zeros((bp, CONV_W - 1, D_MODEL), F32))
        xp, st_p = _layer(xp, mod_p, zero, prm, t_valid=tp, tiles=_tiles(tp))
        st_in = (state_rwkv_shift[l], state_rwkv_wkv[l], state_s5_re[l], state_s5_im[l],
                 state_lru_h[l], state_lru_conv[l])
        xs, st_s = _layer(xs, mod_s, st_in, prm, t_valid=ts, tiles=_tiles(ts))
        for i in range(6):
            new_p[i].append(st_p[i])
            new_s[i].append(st_s[i])
    sp = [jnp.stack(z, axis=0) for z in new_p]
    ss = [jnp.stack(z, axis=0) for z in new_s]
    return (xp, xs, sp[0], sp[1], sp[2], sp[3], sp[4], sp[5], ss[0], ss[1], ss[2], ss[3], ss[4], ss[5])
```

```python
import functools

import jax
import jax.numpy as jnp
from jax import lax
from jax.experimental import pallas as pl
from jax.experimental.pallas import tpu as pltpu

F32 = jnp.float32
BF16 = jnp.bfloat16
HIGHEST = lax.Precision.HIGHEST

D_MODEL = 1024
HEAD = 64
HEADS = D_MODEL // HEAD
PAIRS = HEADS // 2
LORA = 64
SHIFT_W = 3 * D_MODEL + 2 * LORA
S5_STATE_W = 64 * 64
S5_SETS = 4
S5_TILES = S5_STATE_W // 128
S5_ROW_STRIDE = 40
CONV_W = 4
LRU_C = 8.0
RMS_EPS = 1e-6
GN_EPS = 64e-5
RWKV_CHUNK = 64
DECAY_SCALE = 0.6065306597126334
RWKV_INSTANCES = 16
SUBLANES = 8
LANES = 128
VMEM_LIMIT = 56 * 1024 * 1024


def _sigmoid(x):
    return 0.5 * jnp.tanh(0.5 * x) + 0.5


def _silu(x):
    return x * _sigmoid(x)


def _softplus(x):
    return jnp.maximum(x, 0.0) + jnp.log1p(jnp.exp(-jnp.abs(x)))


def _norm_mod(x, g, mod):
    ms = jnp.mean(x * x, axis=-1, keepdims=True)
    return (x * lax.rsqrt(ms + RMS_EPS)) * (g * (1.0 + mod[1:2, :])) + mod[0:1, :]


def _dot(a, b):
    return jnp.dot(a, b, preferred_element_type=F32)


def _dot_hi(a, b):
    return jnp.dot(a, b, preferred_element_type=F32, precision=HIGHEST)


def _split2(x):
    hi = x.astype(BF16)
    lo = (x - hi.astype(F32)).astype(BF16)
    return hi, lo


_NN = (((1,), (0,)), ((), ()))
_NT = (((1,), (1,)), ((), ()))
_TN = (((0,), (0,)), ((), ()))


def _mm(a, b, dims=_NN):
    return lax.dot_general(a.astype(BF16), b.astype(BF16), dims, preferred_element_type=F32)


def _mm3(a_parts, b_parts):
    (a_hi, a_lo), (b_hi, b_lo) = a_parts, b_parts
    return _dot(a_hi, b_hi) + (_dot(a_hi, b_lo) + _dot(a_lo, b_hi))


def _head_sum(x, ones_blk):
    outs = []
    for s in range(D_MODEL // 256):
        hi, lo = _split2(x[:, 256 * s:256 * (s + 1)])
        outs.append(_dot(hi, ones_blk) + _dot(lo, ones_blk))
    return jnp.concatenate(outs, axis=1)


def _block_ones(n, blk):
    ri = lax.broadcasted_iota(jnp.int32, (n, n), 0)
    ci = lax.broadcasted_iota(jnp.int32, (n, n), 1)
    sh = blk.bit_length() - 1
    return ((ri >> sh) == (ci >> sh)).astype(F32).astype(BF16)


def _mod_kernel(c_ref, w_ref, b_ref, o_ref):
    s = _silu(c_ref[...])
    o_ref[...] = _dot_hi(s, w_ref[...]) + b_ref[...]


def _modulation(c_all, ada_w, ada_b):
    depth = ada_w.shape[0]
    rows = c_all.shape[0]
    return pl.pallas_call(
        _mod_kernel,
        grid=(depth, 3),
        in_specs=[
            pl.BlockSpec((rows, D_MODEL), lambda l, j: (0, 0)),
            pl.BlockSpec((None, D_MODEL, D_MODEL), lambda l, j: (l, 0, j)),
            pl.BlockSpec((None, 1, D_MODEL), lambda l, j: (l, 0, j)),
        ],
        out_specs=pl.BlockSpec((None, rows, D_MODEL), lambda l, j: (l, 0, j)),
        out_shape=jax.ShapeDtypeStruct((depth, rows, 3 * D_MODEL), F32),
        name="adaln_mod",
    )(c_all, ada_w, ada_b.reshape(depth, 1, 3 * D_MODEL))


def _s5_prep_kernel(are_ref, aim_ref, ls_ref, bre_ref, bim_ref, abr_ref, abi_ref, bbr_ref, bbi_ref):
    are = are_ref[...]
    aim = aim_ref[...]
    dt = jnp.exp(ls_ref[...])
    mag = jnp.exp(are * dt)
    abr = mag * jnp.cos(aim * dt)
    abi = mag * jnp.sin(aim * dt)
    abr_ref[...] = abr
    abi_ref[...] = abi
    nr = abr - 1.0
    ni = abi
    den = are * are + aim * aim
    cr = (nr * are + ni * aim) / den
    ci = (ni * are - nr * aim) / den
    bre = bre_ref[...]
    bim = bim_ref[...]
    bbr_ref[...] = cr[:, None, :] * bre - ci[:, None, :] * bim
    bbi_ref[...] = cr[:, None, :] * bim + ci[:, None, :] * bre


def _s5_prep(a_re, a_im, log_step, b_re, b_im):
    g, p = a_re.shape
    i = b_re.shape[-1]
    return pl.pallas_call(
        _s5_prep_kernel,
        out_shape=(
            jax.ShapeDtypeStruct((g, p), F32),
            jax.ShapeDtypeStruct((g, p), F32),
            jax.ShapeDtypeStruct((g, i, p), F32),
            jax.ShapeDtypeStruct((g, i, p), F32),
        ),
        name="s5_discretise",
    )(a_re, a_im, log_step.reshape(g, 1), jnp.swapaxes(b_re, 1, 2), jnp.swapaxes(b_im, 1, 2))


def _rwkv_kernel(x_ref, mod_ref, g_ref, wr_ref, wg_ref, mu_ref, w0_ref, w2a_ref, a0_ref, kk_ref, ka_ref,
                 rk_ref, lng_ref, lnb_ref, shift0_ref, wkv0_ref,
                 yz_ref, shift_out_ref, wkv_out_ref,
                 carry_ref, st_ref, at_ref, rt_ref, bt_ref, kt_ref, bh_ref, kh_ref, v_ref, pl_ref, y_ref,
                 nh_ref, t_ref, aak_ref, ark_ref, rhs_ref, gm_ref, qe_ref, hc_ref, yc_ref,
                 *, tile, chunk, t_valid, group):
    t = pl.program_id(1)
    n_chunks = tile // chunk
    two = 2 * chunk
    n_inst = group * n_chunks

    @pl.when(t == 0)
    def _():
        carry_ref[...] = shift0_ref[...]
        rh = lax.broadcasted_iota(jnp.int32, (LANES, LANES), 0) >> (HEAD.bit_length() - 1)
        ch = lax.broadcasted_iota(jnp.int32, (LANES, LANES), 1) >> (HEAD.bit_length() - 1)
        for q in range(PAIRS):
            s0 = wkv0_ref[q]
            st_ref[q] = jnp.where(rh == ch, jnp.concatenate([s0, s0], axis=1), 0.0)

    h = _norm_mod(x_ref[...], g_ref[...], mod_ref[...]).astype(BF16)
    p = _dot(h, wr_ref[...])
    zg = _silu(_dot(h, wg_ref[...]))

    rows = lax.broadcasted_iota(jnp.int32, (tile, 1), 0)
    prev = jnp.where(rows == 0, carry_ref[...], pltpu.roll(p, 1, 0))
    last = (t_valid - 1) % tile
    carry_ref[...] = p[last:last + 1, :]
    pm = p + (prev - p) * mu_ref[...]

    r = pm[:, 0:D_MODEL]
    k = pm[:, D_MODEL:2 * D_MODEL]
    v = pm[:, 2 * D_MODEL:3 * D_MODEL]
    wa = pm[:, 3 * D_MODEL:SHIFT_W]
    lane = lax.broadcasted_iota(jnp.int32, (1, LANES), 1)
    wa = jnp.where(lane < LORA, jnp.tanh(wa), wa)
    lora = _dot(wa.astype(BF16), w2a_ref[...])
    logd = -DECAY_SCALE * _sigmoid(w0_ref[...] + lora[:, :D_MODEL])
    a_sig = _sigmoid(a0_ref[...] + lora[:, D_MODEL:])

    ones_blk = _block_ones(256, HEAD)
    kk = k * kk_ref[...]
    kk = kk * jnp.minimum(lax.rsqrt(_head_sum(kk * kk, ones_blk)), 1e12)
    k2 = k * (1.0 + (a_sig - 1.0) * ka_ref[...])
    bvec = kk * a_sig
    if t_valid % tile != 0:
        ok = rows < t_valid
        logd = jnp.where(ok, logd, 0.0)
        bvec = jnp.where(ok, bvec, 0.0)
        k2 = jnp.where(ok, k2, 0.0)

    rmod = rows & (chunk - 1)
    cum = logd
    s = 1
    while s < chunk:
        cum = cum + jnp.where(rmod >= s, pltpu.roll(cum, s, 0), 0.0)
        s *= 2
    cum_last = jnp.concatenate(
        [jnp.broadcast_to(cum[(j + 1) * chunk - 1:(j + 1) * chunk, :], (chunk, D_MODEL)) for j in range(n_chunks)],
        axis=0)
    p_inc = jnp.exp(cum)
    p_inv = jnp.exp(-cum)
    p_exc = jnp.exp(cum - logd)
    p_end = jnp.exp(cum_last - cum)

    def to_pairs(ref, val):
        for q in range(PAIRS):
            ref[q] = val[:, LANES * q:LANES * (q + 1)]

    to_pairs(at_ref, -kk * p_exc)
    to_pairs(rt_ref, r * p_inc)
    to_pairs(bt_ref, bvec * p_inv)
    to_pairs(kt_ref, k2 * p_inv)
    to_pairs(bh_ref, bvec * p_end)
    to_pairs(kh_ref, k2 * p_end)
    to_pairs(v_ref, v)
    for j in range(n_chunks):
        row = jnp.exp(cum[(j + 1) * chunk - 1:(j + 1) * chunk, :])
        for q in range(PAIRS):
            pl_ref[q, j] = jnp.broadcast_to(row[:, LANES * q:LANES * (q + 1)], (SUBLANES, LANES))

    m_a = (lane < HEAD).astype(F32)
    m_b = 1.0 - m_a
    ri = lax.broadcasted_iota(jnp.int32, (two, two), 0)
    ci = lax.broadcasted_iota(jnp.int32, (two, two), 1)
    csh = chunk.bit_length() - 1
    same_head = (ri >> csh) == (ci >> csh)
    strict = jnp.logical_and(same_head, (ri & (chunk - 1)) > (ci & (chunk - 1)))
    incl = jnp.logical_and(same_head, (ri & (chunk - 1)) >= (ci & (chunk - 1)))
    eye = ri == ci
    off_masks = []
    for lg in range(csh):
        off_masks.append(jnp.logical_and(strict, jnp.logical_and((ri >> (lg + 1)) == (ci >> (lg + 1)),
                                                                 (ri >> lg) != (ci >> lg))))

    def stack(x):
        return jnp.concatenate([x * m_a, x * m_b], axis=0)

    gsh = group.bit_length() - 1
    unroll = min(16, n_inst)

    def group_body(g, carry):
        def where_is(i):
            q = g * group + (i & (group - 1))
            j = i >> gsh
            return q, j, pl.ds(pl.multiple_of(j * chunk, chunk), chunk)

        def gram(i, c):
            q, _, sl = where_is(i)
            atm = stack(at_ref[q, sl, :])
            rtm = stack(rt_ref[q, sl, :])
            bt = bt_ref[q, sl, :]
            kt = kt_ref[q, sl, :]
            o1 = _mm(jnp.concatenate([atm, rtm], axis=0), jnp.concatenate([bt, bt, kt, kt], axis=0), _NT)
            n_ab = jnp.where(strict, o1[:two, :two], 0.0)
            nh_ref[i] = n_ab.astype(BF16)
            t_ref[i] = jnp.where(eye, 1.0, jnp.where(off_masks[0], n_ab, 0.0)).astype(BF16)
            aak_ref[i] = jnp.where(strict, o1[:two, two:], 0.0).astype(BF16)
            ark_ref[i] = jnp.concatenate([jnp.where(incl, o1[two:, :two], 0.0),
                                          jnp.where(incl, o1[two:, two:], 0.0)], axis=1).astype(BF16)
            return c

        lax.fori_loop(0, n_inst, gram, 0, unroll=unroll)

        for off in off_masks[1:]:
            def level(i, c, off=off):
                t_inv = t_ref[i]
                x = _dot(jnp.where(off, nh_ref[i], 0.0), t_inv).astype(BF16)
                t_ref[i] = t_inv + _dot(t_inv, x).astype(BF16)
                return c

            lax.fori_loop(0, n_inst, level, 0, unroll=unroll)

        def apply(i, c):
            q, _, sl = where_is(i)
            atm = stack(at_ref[q, sl, :])
            vm = stack(v_ref[q, sl, :])
            t_inv = t_ref[i]
            z0_hi, z0_lo = _split2(jnp.concatenate([atm, _mm(aak_ref[i], vm)], axis=1))
            z = _dot(t_inv, z0_hi) + _dot(t_inv, z0_lo)
            rhs_ref[i] = jnp.concatenate([z, jnp.concatenate([jnp.zeros_like(vm), vm], axis=1)], axis=0).astype(BF16)
            return c

        lax.fori_loop(0, n_inst, apply, 0, unroll=unroll)

        def operators(i, c):
            q, j, sl = where_is(i)
            rhs2 = rhs_ref[i]
            top = _mm(rhs2, jnp.concatenate([stack(bh_ref[q, sl, :]), stack(kh_ref[q, sl, :])], axis=0), _TN)
            bot = _dot(ark_ref[i], rhs2)
            gm_ref[i] = (top[:two, :]
                         + jnp.where(eye, jnp.broadcast_to(pl_ref[q, j][0:1, :], (two, two)), 0.0)).astype(BF16)
            hc_ref[i] = top[two:, :]
            qe_ref[i] = (bot[:, :two] + stack(rt_ref[q, sl, :])).astype(BF16)
            yc_ref[i] = bot[:, two:]
            return c

        lax.fori_loop(0, n_inst, operators, 0, unroll=unroll)

        for j in range(n_chunks):
            def advance(ql, c, j=j):
                q = g * group + ql
                i = j * group + ql
                st = st_ref[q].astype(BF16)
                ym = _mm(qe_ref[i], st, _NT) + yc_ref[i]
                y_ref[q, j * chunk:(j + 1) * chunk, :] = ym[:chunk, :] + ym[chunk:, :]
                st_ref[q] = _dot(st, gm_ref[i]) + hc_ref[i]
                return c

            lax.fori_loop(0, group, advance, 0, unroll=True)
        return carry

    lax.fori_loop(0, PAIRS // group, group_body, 0)

    ys = jnp.concatenate([y_ref[q] for q in range(PAIRS)], axis=1)
    mean = _head_sum(ys, ones_blk) * (1.0 / HEAD)
    yc = ys - mean
    var = _head_sum(yc * yc, ones_blk) * (1.0 / HEAD)
    yn = yc * lax.rsqrt(var + GN_EPS) * lng_ref[...] + lnb_ref[...]
    bonus = _head_sum(r * k2 * rk_ref[...], ones_blk) * v
    yz_ref[...] = (yn + bonus) * zg

    @pl.when(t == pl.num_programs(1) - 1)
    def _():
        shift_out_ref[...] = carry_ref[...]
        for q in range(PAIRS):
            st = st_ref[q]
            wkv_out_ref[q] = st[:, :HEAD] + st[:, HEAD:]


def _rwkv_mixer(x, mod, g_pre, wr, wg, vecs, w2a, shift0, wkv0, *, tile, t_valid):
    bsz, tp, _ = x.shape
    chunk = RWKV_CHUNK
    assert tp % tile == 0 and tile % chunk == 0
    assert t_valid == tp or tp == tile
    nt = tp // tile
    mu, w0, a0, k_k, k_a, r_k, ln_g, ln_b = vecs
    row = lambda n: pl.BlockSpec((1, n), lambda b, t: (0, 0))
    full = lambda a: pl.BlockSpec(a.shape, lambda b, t: (0,) * a.ndim)
    pair_buf = pltpu.VMEM((PAIRS, tile, LANES), F32)
    group = min(PAIRS, max(1, RWKV_INSTANCES // (tile // chunk)))
    n_inst = group * (tile // chunk)
    two = 2 * chunk
    inst = lambda cols, dtype: pltpu.VMEM((n_inst, two, cols), dtype)
    kern = functools.partial(_rwkv_kernel, tile=tile, chunk=chunk, t_valid=t_valid, group=group)
    return pl.pallas_call(
        kern,
        grid=(bsz, nt),
        in_specs=[
            pl.BlockSpec((None, tile, D_MODEL), lambda b, t: (b, t, 0)),
            pl.BlockSpec((None, 3, D_MODEL), lambda b, t: (b, 0, 0)),
            row(D_MODEL), full(wr), full(wg), row(SHIFT_W), row(D_MODEL), full(w2a), row(D_MODEL),
            row(D_MODEL), row(D_MODEL), row(D_MODEL), row(D_MODEL), row(D_MODEL),
            pl.BlockSpec((None, 1, SHIFT_W), lambda b, t: (b, 0, 0)),
            pl.BlockSpec((None, PAIRS, LANES, HEAD), lambda b, t: (b, 0, 0, 0)),
        ],
        out_specs=[
            pl.BlockSpec((None, tile, D_MODEL), lambda b, t: (b, t, 0)),
            pl.BlockSpec((None, 1, SHIFT_W), lambda b, t: (b, 0, 0)),
            pl.BlockSpec((None, PAIRS, LANES, HEAD), lambda b, t: (b, 0, 0, 0)),
        ],
        out_shape=[
            jax.ShapeDtypeStruct((bsz, tp, D_MODEL), F32),
            jax.ShapeDtypeStruct((bsz, 1, SHIFT_W), F32),
            jax.ShapeDtypeStruct((bsz, PAIRS, LANES, HEAD), F32),
        ],
        scratch_shapes=[
            pltpu.VMEM((1, SHIFT_W), F32),
            pltpu.VMEM((PAIRS, LANES, LANES), F32),
            pair_buf, pair_buf, pair_buf, pair_buf, pair_buf, pair_buf, pair_buf,
            pltpu.VMEM((PAIRS, tile // chunk, SUBLANES, LANES), F32),
            pair_buf,
            inst(two, BF16), inst(two, BF16), inst(two, BF16), inst(2 * two, BF16),
            pltpu.VMEM((n_inst, 2 * two, 2 * two), BF16),
            inst(two, BF16), inst(two, BF16), inst(two, F32), inst(two, F32),
        ],
        compiler_params=pltpu.CompilerParams(
            dimension_semantics=("arbitrary", "arbitrary"), vmem_limit_bytes=VMEM_LIMIT),
        name="rwkv7_mixer",
    )(x, mod, g_pre, wr, wg, mu, w0, w2a, a0, k_k, k_a, r_k, ln_g, ln_b, shift0, wkv0)


def _s5_kernel(x_ref, mod_ref, g_ref, wu_ref, wg_ref, bbr_ref, bbi_ref, cre_ref, cmi_ref, abr_ref, abi_ref,
               d_ref, gluw_ref, glub_ref, s0r_ref, s0i_ref,
               yz_ref, sr_out_ref, si_out_ref,
               hr_ref, hi_ref, cr_ref, ci_ref, *, tile):
    t = pl.program_id(1)

    @pl.when(t == 0)
    def _():
        cr_ref[...] = s0r_ref[...]
        ci_ref[...] = s0i_ref[...]

    h = _norm_mod(x_ref[...], g_ref[...], mod_ref[...]).astype(BF16)
    u = _dot(h, wu_ref[...])
    zg = _silu(_dot(h, wg_ref[...]))
    ch_w = D_MODEL // S5_SETS
    set_tiles = S5_TILES // S5_SETS

    def frame_rows(c):
        return pl.ds(c, tile, stride=S5_ROW_STRIDE)

    for s in range(S5_SETS):
        ub = u[:, ch_w * s:ch_w * (s + 1)].astype(BF16)
        bur = _dot(ub, bbr_ref[s])
        bui = _dot(ub, bbi_ref[s])
        for c in range(set_tiles):
            hr_ref[frame_rows(s * set_tiles + c), :] = bur[:, LANES * c:LANES * (c + 1)]
            hi_ref[frame_rows(s * set_tiles + c), :] = bui[:, LANES * c:LANES * (c + 1)]

    abr = abr_ref[...]
    abi = abi_ref[...]

    def frame(f, carry):
        sr, si = carry
        rows = pl.ds(pl.multiple_of(f * S5_ROW_STRIDE, SUBLANES), S5_TILES)
        nr = abr * sr - abi * si + hr_ref[rows, :]
        ni = abr * si + abi * sr + hi_ref[rows, :]
        hr_ref[rows, :] = nr
        hi_ref[rows, :] = ni
        return nr, ni

    sr, si = lax.fori_loop(0, tile, frame, (cr_ref[...], ci_ref[...]), unroll=8)
    cr_ref[...] = sr
    ci_ref[...] = si

    outs = []
    for s in range(S5_SETS):
        hr = jnp.concatenate([hr_ref[frame_rows(s * set_tiles + c), :] for c in range(set_tiles)], axis=1)
        hi = jnp.concatenate([hi_ref[frame_rows(s * set_tiles + c), :] for c in range(set_tiles)], axis=1)
        outs.append(_dot(hr.astype(BF16), cre_ref[s]) + _dot(hi.astype(BF16), cmi_ref[s]))
    y = jnp.concatenate(outs, axis=1) + d_ref[...] * u
    y = 0.5 * y * (1.0 + jnp.tanh(0.7978845608028654 * (y + 0.044715 * (y * y * y))))
    y = y * _sigmoid(_dot(y.astype(BF16), gluw_ref[...]) + glub_ref[...])
    yz_ref[...] = y * zg

    @pl.when(t == pl.num_programs(1) - 1)
    def _():
        sr_out_ref[...] = cr_ref[...]
        si_out_ref[...] = ci_ref[...]


def _s5_mixer(x, mod, g_pre, wu, wg, bbr, bbi, cre, cmi, abr, abi, d_skip, glu_w, glu_b, s0r, s0i, *, tile):
    bsz, tp, _ = x.shape
    assert tp % tile == 0 and tile % SUBLANES == 0
    nt = tp // tile
    row = lambda n: pl.BlockSpec((1, n), lambda b, t: (0, 0))
    full = lambda a: pl.BlockSpec(a.shape, lambda b, t: (0,) * a.ndim)
    state = pl.BlockSpec((None, S5_TILES, LANES), lambda b, t: (b, 0, 0))
    kern = functools.partial(_s5_kernel, tile=tile)
    return pl.pallas_call(
        kern,
        grid=(bsz, nt),
        in_specs=[
            pl.BlockSpec((None, tile, D_MODEL), lambda b, t: (b, t, 0)),
            pl.BlockSpec((None, 3, D_MODEL), lambda b, t: (b, 0, 0)),
            row(D_MODEL), full(wu), full(wg), full(bbr), full(bbi), full(cre), full(cmi), full(abr), full(abi),
            row(D_MODEL), full(glu_w), row(D_MODEL), state, state,
        ],
        out_specs=[pl.BlockSpec((None, tile, D_MODEL), lambda b, t: (b, t, 0)), state, state],
        out_shape=[
            jax.ShapeDtypeStruct((bsz, tp, D_MODEL), F32),
            jax.ShapeDtypeStruct((bsz, S5_TILES, LANES), F32),
            jax.ShapeDtypeStruct((bsz, S5_TILES, LANES), F32),
        ],
        scratch_shapes=[
            pltpu.VMEM((tile * S5_ROW_STRIDE, LANES), F32),
            pltpu.VMEM((tile * S5_ROW_STRIDE, LANES), F32),
            pltpu.VMEM((S5_TILES, LANES), F32),
            pltpu.VMEM((S5_TILES, LANES), F32),
        ],
        compiler_params=pltpu.CompilerParams(
            dimension_semantics=("arbitrary", "arbitrary"), vmem_limit_bytes=VMEM_LIMIT),
        name="s5_mixer",
    )(x, mod, g_pre, wu, wg, bbr, bbi, cre, cmi, abr, abi, d_skip, glu_w, glu_b, s0r, s0i)


def _lru_kernel(x_ref, mod_ref, g_ref, wx_ref, wg_ref, cw_ref, cb_ref, wa4_ref, ba_ref, wx4_ref, bx_ref, lam_ref,
                conv0_ref, h0_ref,
                yz_ref, conv_out_ref, h_out_ref,
                xbuf_ref, a_ref, b_ref, hc_ref, *, tile):
    t = pl.program_id(1)
    pad = SUBLANES
    hist = CONV_W - 1

    @pl.when(t == 0)
    def _():
        xbuf_ref[pad - hist:pad, :] = conv0_ref[...]
        hc_ref[...] = h0_ref[...]

    h = _norm_mod(x_ref[...], g_ref[...], mod_ref[...]).astype(BF16)
    xl = _dot(h, wx_ref[...])
    zg = _silu(_dot(h, wg_ref[...]))
    xbuf_ref[pad:pad + tile, :] = xl
    xc = cb_ref[...] + xl * cw_ref[CONV_W - 1:CONV_W, :]
    for j in range(hist):
        xc = xc + xbuf_ref[pad - hist + j:pad - hist + j + tile, :] * cw_ref[j:j + 1, :]
    tail = xbuf_ref[pad + tile - hist:pad + tile, :]
    xbuf_ref[pad - hist:pad, :] = tail

    ga, gx = [], []
    blk = D_MODEL // 4
    for s in range(4):
        xb = xc[:, blk * s:blk * (s + 1)].astype(BF16)
        ga.append(_dot(xb, wa4_ref[s]))
        gx.append(_dot(xb, wx4_ref[s]))
    gate_r = _sigmoid(jnp.concatenate(ga, axis=1) + ba_ref[...])
    gate_i = _sigmoid(jnp.concatenate(gx, axis=1) + bx_ref[...])
    log_a = -LRU_C * gate_r * _softplus(-lam_ref[...])
    a = jnp.exp(log_a)
    a_ref[...] = a
    b_ref[...] = jnp.sqrt(-jnp.tanh(log_a) * (a * a + 1.0)) * (gate_i * xc)

    row8 = lax.broadcasted_iota(jnp.int32, (SUBLANES, 1), 0)

    def row_body(g, carry):
        rs = pl.ds(pl.multiple_of(g * SUBLANES, SUBLANES), SUBLANES)
        av = a_ref[rs, :]
        bv = b_ref[rs, :]
        for sh in (1, 2, 4):
            keep = row8 >= sh
            ash = jnp.where(keep, pltpu.roll(av, sh, 0), 1.0)
            bsh = jnp.where(keep, pltpu.roll(bv, sh, 0), 0.0)
            bv = bv + av * bsh
            av = av * ash
        hs = bv + av * carry
        b_ref[rs, :] = hs
        return hs[SUBLANES - 1:SUBLANES, :]

    hc_ref[...] = lax.fori_loop(0, tile // SUBLANES, row_body, hc_ref[...])
    yz_ref[...] = b_ref[...] * zg

    @pl.when(t == pl.num_programs(1) - 1)
    def _():
        conv_out_ref[...] = tail
        h_out_ref[...] = hc_ref[...]


def _lru_mixer(x, mod, g_pre, wx, wg, conv_w, conv_b, wa4, ba, wx4, bx, lam, conv0, h0, *, tile):
    bsz, tp, _ = x.shape
    assert tp % tile == 0 and tile % SUBLANES == 0 and tile >= SUBLANES
    nt = tp // tile
    row = lambda n: pl.BlockSpec((1, n), lambda b, t: (0, 0))
    full = lambda a: pl.BlockSpec(a.shape, lambda b, t: (0,) * a.ndim)
    kern = functools.partial(_lru_kernel, tile=tile)
    return pl.pallas_call(
        kern,
        grid=(bsz, nt),
        in_specs=[
            pl.BlockSpec((None, tile, D_MODEL), lambda b, t: (b, t, 0)),
            pl.BlockSpec((None, 3, D_MODEL), lambda b, t: (b, 0, 0)),
            row(D_MODEL), full(wx), full(wg), full(conv_w), row(D_MODEL), full(wa4), row(D_MODEL), full(wx4),
            row(D_MODEL), row(D_MODEL),
            pl.BlockSpec((None, CONV_W - 1, D_MODEL), lambda b, t: (b, 0, 0)),
            pl.BlockSpec((None, 1, D_MODEL), lambda b, t: (b, 0, 0)),
        ],
        out_specs=[
            pl.BlockSpec((None, tile, D_MODEL), lambda b, t: (b, t, 0)),
            pl.BlockSpec((None, CONV_W - 1, D_MODEL), lambda b, t: (b, 0, 0)),
            pl.BlockSpec((None, 1, D_MODEL), lambda b, t: (b, 0, 0)),
        ],
        out_shape=[
            jax.ShapeDtypeStruct((bsz, tp, D_MODEL), F32),
            jax.ShapeDtypeStruct((bsz, CONV_W - 1, D_MODEL), F32),
            jax.ShapeDtypeStruct((bsz, 1, D_MODEL), F32),
        ],
        scratch_shapes=[
            pltpu.VMEM((tile + SUBLANES, D_MODEL), F32),
            pltpu.VMEM((tile, D_MODEL), F32),
            pltpu.VMEM((tile, D_MODEL), F32),
            pltpu.VMEM((1, D_MODEL), F32),
        ],
        compiler_params=pltpu.CompilerParams(
            dimension_semantics=("arbitrary", "arbitrary"), vmem_limit_bytes=VMEM_LIMIT),
        name="rglru_mixer",
    )(x, mod, g_pre, wx, wg, conv_w, conv_b, wa4, ba, wx4, bx, lam, conv0, h0)


def _merge_kernel(x_ref, mod_ref, g_ref, gpost_ref, wm_ref, wo_ref, yr_ref, ys_ref, yl_ref, o_ref):
    x = x_ref[...]
    mod = mod_ref[...]
    h = _norm_mod(x, g_ref[...], mod).astype(BF16)
    m = _sigmoid(_dot(h, wm_ref[...]))
    merged = (m[:, :D_MODEL] * yr_ref[...] + m[:, D_MODEL:2 * D_MODEL] * ys_ref[...]
              + m[:, 2 * D_MODEL:] * yl_ref[...])
    o = _dot(merged.astype(BF16), wo_ref[...])
    ms = jnp.mean(o * o, axis=-1, keepdims=True)
    o = (o * lax.rsqrt(ms + RMS_EPS)) * gpost_ref[...]
    o_ref[...] = x + mod[2:3, :] * o


def _merge_out(x, mod, g_pre, g_post, wm, wo, yr, ys, yl, *, tile):
    bsz, tp, _ = x.shape
    assert tp % tile == 0
    nt = tp // tile
    act = pl.BlockSpec((None, tile, D_MODEL), lambda b, t: (b, t, 0))
    row = lambda n: pl.BlockSpec((1, n), lambda b, t: (0, 0))
    full = lambda a: pl.BlockSpec(a.shape, lambda b, t: (0,) * a.ndim)
    return pl.pallas_call(
        _merge_kernel,
        grid=(bsz, nt),
        in_specs=[act, pl.BlockSpec((None, 3, D_MODEL), lambda b, t: (b, 0, 0)), row(D_MODEL), row(D_MODEL),
                  full(wm), full(wo), act, act, act],
        out_specs=act,
        out_shape=jax.ShapeDtypeStruct((bsz, tp, D_MODEL), F32),
        compiler_params=pltpu.CompilerParams(
            dimension_semantics=("arbitrary", "arbitrary"), vmem_limit_bytes=VMEM_LIMIT),
        name="merge_out",
    )(x, mod, g_pre, g_post, wm, wo, yr, ys, yl)


def _block_diag_expand(w, per):
    n, a, b = w.shape
    w = w.reshape(n // per, per, a, b)
    eye = jnp.eye(per, dtype=w.dtype)
    return jnp.einsum("sgab,gh->sgahb", w, eye).reshape(n // per, per * a, per * b)


def _layer(x, mod, state, prm, *, t_valid, tiles):
    shift_row, wkv, s_re, s_im, lru_h, lru_conv = state
    bsz = x.shape[0]
    g_pre = prm["g_pre"]

    xr = x
    if x.shape[1] % tiles["rwkv"] != 0:
        xr = jnp.pad(x, ((0, 0), (0, tiles["rwkv"] - x.shape[1] % tiles["rwkv"]), (0, 0)))
    yz_r, shift_new, wkv_new = _rwkv_mixer(
        xr, mod, g_pre, prm["w_rwkv"], prm["w_rwkv_g"], prm["rwkv_vecs"], prm["w2a"],
        shift_row.reshape(bsz, 1, SHIFT_W), wkv.reshape(bsz, PAIRS, LANES, HEAD), tile=tiles["rwkv"],
        t_valid=t_valid)
    yz_r = yz_r[:, :t_valid]
    yz_s, s_re_new, s_im_new = _s5_mixer(
        x, mod, g_pre, prm["w_s5"], prm["w_s5_g"], prm["bbr"], prm["bbi"], prm["cre"], prm["cmi"],
        prm["abr"], prm["abi"], prm["s5_d"], prm["glu_w"], prm["glu_b"],
        s_re.reshape(bsz, S5_TILES, LANES), s_im.reshape(bsz, S5_TILES, LANES), tile=tiles["s5"])
    yz_l, conv_new, h_new = _lru_mixer(
        x, mod, g_pre, prm["w_lru"], prm["w_lru_g"], prm["conv_w"], prm["conv_b"], prm["wa4"], prm["ba"],
        prm["wx4"], prm["bx"], prm["lam"], lru_conv, lru_h.reshape(bsz, 1, D_MODEL), tile=tiles["lru"])
    x_new = _merge_out(x, mod, g_pre, prm["g_post"], prm["w_merge"], prm["w_out"], yz_r, yz_s, yz_l,
                       tile=tiles["merge"])
    new_state = (shift_new.reshape(bsz, SHIFT_W), wkv_new.reshape(bsz, HEADS, HEAD, HEAD),
                 s_re_new.reshape(bsz, 64, 64), s_im_new.reshape(bsz, 64, 64),
                 h_new.reshape(bsz, D_MODEL), conv_new)
    return x_new, new_state


def _layer_params(l, w_in, w_out, norm_pre, norm_post, rwkv, s5, lru):
    d = D_MODEL
    wb = w_in[l].astype(BF16)
    o = SHIFT_W
    row = lambda a: a.reshape(1, -1)
    mu, w0, w2, a0, a2, k_k, k_a, r_k, ln_g, ln_b = [a[l] for a in rwkv]
    a_re, a_im, log_step, b_re, b_im, c_re, c_im, s5_d, glu_w, glu_b = [a[l] for a in s5]
    conv_w, conv_b, wa, ba, wx, bx, lam = [a[l] for a in lru]
    zeros = jnp.zeros((LORA, d), F32)
    w2a = jnp.concatenate([jnp.concatenate([w2, zeros], axis=1), jnp.concatenate([zeros, a2], axis=1)], axis=0)
    abr, abi, bbr, bbi = _s5_prep(a_re, a_im, log_step, b_re, b_im)
    per = 64 // S5_SETS
    return dict(
        g_pre=row(norm_pre[l]), g_post=row(norm_post[l]),
        w_rwkv=wb[:, :o], w_rwkv_g=wb[:, o:o + d],
        w_s5=wb[:, o + d:o + 2 * d], w_s5_g=wb[:, o + 2 * d:o + 3 * d],
        w_lru=wb[:, o + 3 * d:o + 4 * d], w_lru_g=wb[:, o + 4 * d:o + 5 * d],
        w_merge=wb[:, o + 5 * d:], w_out=w_out[l].astype(BF16),
        rwkv_vecs=(row(mu), row(w0), row(a0), row(k_k), row(k_a), row(r_k), row(ln_g), row(ln_b)),
        w2a=w2a.astype(BF16),
        bbr=_block_diag_expand(bbr, per).astype(BF16), bbi=_block_diag_expand(bbi, per).astype(BF16),
        cre=_block_diag_expand(jnp.swapaxes(c_re, 1, 2), per).astype(BF16),
        cmi=_block_diag_expand(-jnp.swapaxes(c_im, 1, 2), per).astype(BF16),
        abr=abr.reshape(S5_TILES, LANES), abi=abi.reshape(S5_TILES, LANES),
        s5_d=row(s5_d), glu_w=glu_w.astype(BF16), glu_b=row(glu_b),
        conv_w=conv_w, conv_b=row(conv_b),
        wa4=_block_diag_expand(wa, 4).astype(BF16), ba=row(ba),
        wx4=_block_diag_expand(wx, 4).astype(BF16), bx=row(bx), lam=row(lam),
    )


def _tiles(t):
    pick = lambda want: want if t % want == 0 else t
    return dict(rwkv=256 if t % 256 == 0 else RWKV_CHUNK, s5=pick(256), lru=pick(512), merge=pick(512))


def kernel(x_prompt, x_sample, state_rwkv_shift, state_rwkv_wkv, state_s5_re, state_s5_im, state_lru_h,
           state_lru_conv, c_prompt, c_sample, ada_w, ada_b, norm_pre, norm_post, w_in, w_out, rwkv_mu, rwkv_w0,
           rwkv_w2, rwkv_a0, rwkv_a2, rwkv_k_k, rwkv_k_a, rwkv_r_k, rwkv_ln_g, rwkv_ln_b, s5_a_re, s5_a_im,
           s5_log_step, s5_b_re, s5_b_im, s5_c_re, s5_c_im, s5_d, s5_glu_w, s5_glu_b, lru_conv_w, lru_conv_b,
           lru_wa, lru_ba, lru_wx, lru_bx, lru_lambda):
    depth = w_in.shape[0]
    bp, tp, _ = x_prompt.shape
    bs, ts, _ = x_sample.shape
    c_all = jnp.concatenate([c_prompt, c_sample], axis=0)
    pad = (-c_all.shape[0]) % SUBLANES
    c_all = jnp.pad(c_all, ((0, pad), (0, 0)))
    mod_all = _modulation(c_all, ada_w, ada_b)
    rwkv = (rwkv_mu, rwkv_w0, rwkv_w2, rwkv_a0, rwkv_a2, rwkv_k_k, rwkv_k_a,
            rwkv_r_k.reshape(depth, D_MODEL), rwkv_ln_g, rwkv_ln_b)
    s5 = (s5_a_re, s5_a_im, s5_log_step, s5_b_re, s5_b_im, s5_c_re, s5_c_im, s5_d, s5_glu_w, s5_glu_b)
    lru = (lru_conv_w, lru_conv_b, lru_wa, lru_ba, lru_wx, lru_bx, lru_lambda)

    xp, xs = x_prompt, x_sample
    new_p = [[] for _ in range(6)]
    new_s = [[] for _ in range(6)]
    for l in range(depth):
        prm = _layer_params(l, w_in, w_out, norm_pre, norm_post, rwkv, s5, lru)
        mod_p = mod_all[l, :bp].reshape(bp, 3, D_MODEL)
        mod_s = mod_all[l, bp:bp + bs].reshape(bs, 3, D_MODEL)
        zero = (jnp.zeros((bp, SHIFT_W), F32), jnp.zeros((bp, HEADS, HEAD, HEAD), F32),
                jnp.zeros((bp, 64, 64), F32), jnp.zeros((bp, 64, 64), F32),
                jnp.zeros((bp, D_MODEL), F32), jnp.zeros((bp, CONV_W - 1, D_MODEL), F32))
        xp, st_p = _layer(xp, mod_p, zero, prm, t_valid=tp, tiles=_tiles(tp))
        st_in = (state_rwkv_shift[l], state_rwkv_wkv[l], state_s5_re[l], state_s5_im[l],
                 state_lru_h[l], state_lru_conv[l])
        xs, st_s = _layer(xs, mod_s, st_in, prm, t_valid=ts, tiles=_tiles(ts))
        for i in range(6):
            new_p[i].append(st_p[i])
            new_s[i].append(st_s[i])
    sp = [jnp.stack(z, axis=0) for z in new_p]
    ss = [jnp.stack(z, axis=0) for z in new_s]
    return (xp, xs, sp[0], sp[1], sp[2], sp[3], sp[4], sp[5], ss[0], ss[1], ss[2], ss[3], ss[4], ss[5])
```

```python
import functools

import jax
import jax.numpy as jnp
from jax import lax
from jax.experimental import pallas as pl
from jax.experimental.pallas import tpu as pltpu

F32 = jnp.float32
BF16 = jnp.bfloat16
HIGHEST = lax.Precision.HIGHEST

D_MODEL = 1024
HEAD = 64
HEADS = D_MODEL // HEAD
PAIRS = HEADS // 2
LORA = 64
SHIFT_W = 3 * D_MODEL + 2 * LORA
S5_STATE_W = 64 * 64
S5_SETS = 4
S5_IN_GROUPS = 8
S5_TILES = S5_STATE_W // 128
S5_ROW_STRIDE = 40
CONV_W = 4
LRU_C = 8.0
RMS_EPS = 1e-6
GN_EPS = 64e-5
RWKV_CHUNK = 64
DECAY_SCALE = 0.6065306597126334
RWKV_INSTANCES = 16
SUBLANES = 8
LANES = 128
VMEM_LIMIT = 56 * 1024 * 1024


def _sigmoid(x):
    return 0.5 * jnp.tanh(0.5 * x) + 0.5


def _silu(x):
    return x * _sigmoid(x)


def _softplus(x):
    return jnp.maximum(x, 0.0) + jnp.log1p(jnp.exp(-jnp.abs(x)))


def _norm_mod(x, g, mod):
    ms = jnp.mean(x * x, axis=-1, keepdims=True)
    return (x * lax.rsqrt(ms + RMS_EPS)) * (g * (1.0 + mod[1:2, :])) + mod[0:1, :]


def _dot(a, b):
    return jnp.dot(a, b, preferred_element_type=F32)


def _dot_hi(a, b):
    return jnp.dot(a, b, preferred_element_type=F32, precision=HIGHEST)


def _split2(x):
    hi = x.astype(BF16)
    lo = (x - hi.astype(F32)).astype(BF16)
    return hi, lo


_NN = (((1,), (0,)), ((), ()))
_NT = (((1,), (1,)), ((), ()))
_TN = (((0,), (0,)), ((), ()))


def _mm(a, b, dims=_NN):
    return lax.dot_general(a.astype(BF16), b.astype(BF16), dims, preferred_element_type=F32)


def _mm3(a_parts, b_parts):
    (a_hi, a_lo), (b_hi, b_lo) = a_parts, b_parts
    return _dot(a_hi, b_hi) + (_dot(a_hi, b_lo) + _dot(a_lo, b_hi))


def _head_sum(x, ones_blk):
    outs = []
    for s in range(D_MODEL // 256):
        outs.append(_dot(x[:, 256 * s:256 * (s + 1)].astype(BF16), ones_blk))
    return jnp.concatenate(outs, axis=1)


def _block_ones(n, blk):
    ri = lax.broadcasted_iota(jnp.int32, (n, n), 0)
    ci = lax.broadcasted_iota(jnp.int32, (n, n), 1)
    sh = blk.bit_length() - 1
    return ((ri >> sh) == (ci >> sh)).astype(F32).astype(BF16)


def _mod_kernel(c_ref, w_ref, b_ref, o_ref):
    s = _silu(c_ref[...])
    o_ref[...] = _dot_hi(s, w_ref[...]) + b_ref[...]


def _modulation(c_all, ada_w, ada_b):
    depth = ada_w.shape[0]
    rows = c_all.shape[0]
    return pl.pallas_call(
        _mod_kernel,
        grid=(depth, 3),
        in_specs=[
            pl.BlockSpec((rows, D_MODEL), lambda l, j: (0, 0)),
            pl.BlockSpec((None, D_MODEL, D_MODEL), lambda l, j: (l, 0, j)),
            pl.BlockSpec((None, 1, D_MODEL), lambda l, j: (l, 0, j)),
        ],
        out_specs=pl.BlockSpec((None, rows, D_MODEL), lambda l, j: (l, 0, j)),
        out_shape=jax.ShapeDtypeStruct((depth, rows, 3 * D_MODEL), F32),
        name="adaln_mod",
    )(c_all, ada_w, ada_b.reshape(depth, 1, 3 * D_MODEL))


def _s5_prep_kernel(are_ref, aim_ref, ls_ref, bre_ref, bim_ref, abr_ref, abi_ref, bbr_ref, bbi_ref):
    are = are_ref[...]
    aim = aim_ref[...]
    dt = jnp.exp(ls_ref[...])
    mag = jnp.exp(are * dt)
    abr = mag * jnp.cos(aim * dt)
    abi = mag * jnp.sin(aim * dt)
    abr_ref[...] = abr
    abi_ref[...] = abi
    nr = abr - 1.0
    ni = abi
    den = are * are + aim * aim
    cr = (nr * are + ni * aim) / den
    ci = (ni * are - nr * aim) / den
    bre = bre_ref[...]
    bim = bim_ref[...]
    bbr_ref[...] = cr[:, None, :] * bre - ci[:, None, :] * bim
    bbi_ref[...] = cr[:, None, :] * bim + ci[:, None, :] * bre


def _s5_prep(a_re, a_im, log_step, b_re, b_im):
    g, p = a_re.shape
    i = b_re.shape[-1]
    return pl.pallas_call(
        _s5_prep_kernel,
        out_shape=(
            jax.ShapeDtypeStruct((g, p), F32),
            jax.ShapeDtypeStruct((g, p), F32),
            jax.ShapeDtypeStruct((g, i, p), F32),
            jax.ShapeDtypeStruct((g, i, p), F32),
        ),
        name="s5_discretise",
    )(a_re, a_im, log_step.reshape(g, 1), jnp.swapaxes(b_re, 1, 2), jnp.swapaxes(b_im, 1, 2))


def _rwkv_kernel(x_ref, mod_ref, g_ref, wr_ref, wg_ref, mu_ref, w0_ref, w2a_ref, a0_ref, kk_ref, ka_ref,
                 rk_ref, lng_ref, lnb_ref, shift0_ref, wkv0_ref,
                 yz_ref, shift_out_ref, wkv_out_ref,
                 carry_ref, st_ref, at_ref, rt_ref, bt_ref, kt_ref, bh_ref, kh_ref, v_ref, pl_ref, y_ref,
                 nh_ref, t_ref, aak_ref, ark_ref, rhs_ref, gm_ref, qe_ref, hc_ref, yc_ref,
                 *, tile, chunk, t_valid, group):
    t = pl.program_id(1)
    n_chunks = tile // chunk
    two = 2 * chunk
    n_inst = group * n_chunks

    @pl.when(t == 0)
    def _():
        carry_ref[...] = shift0_ref[...]
        rh = lax.broadcasted_iota(jnp.int32, (LANES, LANES), 0) >> (HEAD.bit_length() - 1)
        ch = lax.broadcasted_iota(jnp.int32, (LANES, LANES), 1) >> (HEAD.bit_length() - 1)
        for q in range(PAIRS):
            s0 = wkv0_ref[q]
            st_ref[q] = jnp.where(rh == ch, jnp.concatenate([s0, s0], axis=1), 0.0)

    h = _norm_mod(x_ref[...], g_ref[...], mod_ref[...]).astype(BF16)
    p = _dot(h, wr_ref[...])
    zg = _silu(_dot(h, wg_ref[...]))

    rows = lax.broadcasted_iota(jnp.int32, (tile, 1), 0)
    prev = jnp.where(rows == 0, carry_ref[...], pltpu.roll(p, 1, 0))
    last = (t_valid - 1) % tile
    carry_ref[...] = p[last:last + 1, :]
    pm = p + (prev - p) * mu_ref[...]

    r = pm[:, 0:D_MODEL]
    k = pm[:, D_MODEL:2 * D_MODEL]
    v = pm[:, 2 * D_MODEL:3 * D_MODEL]
    wa = pm[:, 3 * D_MODEL:SHIFT_W]
    lane = lax.broadcasted_iota(jnp.int32, (1, LANES), 1)
    wa = jnp.where(lane < LORA, jnp.tanh(wa), wa)
    lora = _dot(wa.astype(BF16), w2a_ref[...])
    logd = -DECAY_SCALE * _sigmoid(w0_ref[...] + lora[:, :D_MODEL])
    a_sig = _sigmoid(a0_ref[...] + lora[:, D_MODEL:])

    ones_blk = _block_ones(256, HEAD)
    kk = k * kk_ref[...]
    kk = kk * jnp.minimum(lax.rsqrt(_head_sum(kk * kk, ones_blk)), 1e12)
    k2 = k * (1.0 + (a_sig - 1.0) * ka_ref[...])
    bvec = kk * a_sig
    if t_valid % tile != 0:
        ok = rows < t_valid
        logd = jnp.where(ok, logd, 0.0)
        bvec = jnp.where(ok, bvec, 0.0)
        k2 = jnp.where(ok, k2, 0.0)

    rmod = rows & (chunk - 1)
    cum = logd
    s = 1
    while s < chunk:
        cum = cum + jnp.where(rmod >= s, pltpu.roll(cum, s, 0), 0.0)
        s *= 2
    cum_last = jnp.concatenate(
        [jnp.broadcast_to(cum[(j + 1) * chunk - 1:(j + 1) * chunk, :], (chunk, D_MODEL)) for j in range(n_chunks)],
        axis=0)
    p_inc = jnp.exp(cum)
    p_inv = jnp.exp(-cum)
    p_exc = jnp.exp(cum - logd)
    p_end = jnp.exp(cum_last - cum)

    def to_pairs(ref, val):
        for q in range(PAIRS):
            ref[q] = val[:, LANES * q:LANES * (q + 1)]

    to_pairs(at_ref, -kk * p_exc)
    to_pairs(rt_ref, r * p_inc)
    to_pairs(bt_ref, bvec * p_inv)
    to_pairs(kt_ref, k2 * p_inv)
    to_pairs(bh_ref, bvec * p_end)
    to_pairs(kh_ref, k2 * p_end)
    to_pairs(v_ref, v)
    for j in range(n_chunks):
        row = jnp.exp(cum[(j + 1) * chunk - 1:(j + 1) * chunk, :])
        for q in range(PAIRS):
            pl_ref[q, j] = jnp.broadcast_to(row[:, LANES * q:LANES * (q + 1)], (SUBLANES, LANES))

    m_a = (lane < HEAD).astype(F32)
    m_b = 1.0 - m_a
    ri = lax.broadcasted_iota(jnp.int32, (two, two), 0)
    ci = lax.broadcasted_iota(jnp.int32, (two, two), 1)
    csh = chunk.bit_length() - 1
    same_head = (ri >> csh) == (ci >> csh)
    strict = jnp.logical_and(same_head, (ri & (chunk - 1)) > (ci & (chunk - 1)))
    incl = jnp.logical_and(same_head, (ri & (chunk - 1)) >= (ci & (chunk - 1)))
    eye = ri == ci
    head_a = ri < chunk
    off_masks = []
    for lg in range(csh):
        off_masks.append(jnp.logical_and(strict, jnp.logical_and((ri >> (lg + 1)) == (ci >> (lg + 1)),
                                                                 (ri >> lg) != (ci >> lg))))

    def stack(x):
        return jnp.concatenate([x * m_a, x * m_b], axis=0)

    gsh = group.bit_length() - 1
    unroll = min(16, n_inst)

    def group_body(g, carry):
        def where_is(i):
            q = g * group + (i & (group - 1))
            j = i >> gsh
            return q, j, pl.ds(pl.multiple_of(j * chunk, chunk), chunk)

        def gram(i, c):
            q, _, sl = where_is(i)
            atm = stack(at_ref[q, sl, :])
            rtm = stack(rt_ref[q, sl, :])
            bt = bt_ref[q, sl, :]
            kt = kt_ref[q, sl, :]
            o1 = _mm(jnp.concatenate([atm, rtm], axis=0), jnp.concatenate([bt, kt], axis=0), _NT)
            o1_sw = pltpu.roll(o1, chunk, 1)
            n_ab = jnp.where(strict, jnp.where(head_a, o1[:two], o1_sw[:two]), 0.0)
            nh_ref[i] = n_ab.astype(BF16)
            t_ref[i] = jnp.where(eye, 1.0, jnp.where(off_masks[0], n_ab, 0.0)).astype(BF16)
            aak_ref[i] = jnp.where(strict, jnp.where(head_a, o1_sw[:two], o1[:two]), 0.0).astype(BF16)
            ark_ref[i] = jnp.concatenate([jnp.where(incl, jnp.where(head_a, o1[two:], o1_sw[two:]), 0.0),
                                          jnp.where(incl, jnp.where(head_a, o1_sw[two:], o1[two:]), 0.0)],
                                         axis=1).astype(BF16)
            return c

        lax.fori_loop(0, n_inst, gram, 0, unroll=unroll)

        for off in off_masks[1:]:
            def level(i, c, off=off):
                t_inv = t_ref[i]
                x = _dot(jnp.where(off, nh_ref[i], 0.0), t_inv).astype(BF16)
                t_ref[i] = t_inv + _dot(t_inv, x).astype(BF16)
                return c

            lax.fori_loop(0, n_inst, level, 0, unroll=unroll)

        def apply(i, c):
            q, _, sl = where_is(i)
            atm = stack(at_ref[q, sl, :])
            vm = stack(v_ref[q, sl, :])
            t_inv = t_ref[i]
            z0_hi, z0_lo = _split2(jnp.concatenate([atm, _mm(aak_ref[i], vm)], axis=1))
            z = _dot(t_inv, z0_hi) + _dot(t_inv, z0_lo)
            rhs_ref[i] = jnp.concatenate([z, jnp.concatenate([jnp.zeros_like(vm), vm], axis=1)], axis=0).astype(BF16)
            return c

        lax.fori_loop(0, n_inst, apply, 0, unroll=unroll)

        def operators(i, c):
            q, j, sl = where_is(i)
            rhs2 = rhs_ref[i]
            top = _mm(rhs2, jnp.concatenate([stack(bh_ref[q, sl, :]), stack(kh_ref[q, sl, :])], axis=0), _TN)
            bot = _dot(ark_ref[i], rhs2)
            gm_ref[i] = (top[:two, :]
                         + jnp.where(eye, jnp.broadcast_to(pl_ref[q, j][0:1, :], (two, two)), 0.0)).astype(BF16)
            hc_ref[i] = top[two:, :]
            qe_ref[i] = (bot[:, :two] + stack(rt_ref[q, sl, :])).astype(BF16)
            yc_ref[i] = bot[:, two:]
            return c

        lax.fori_loop(0, n_inst, operators, 0, unroll=unroll)

        for j in range(n_chunks):
            def advance(ql, c, j=j):
                q = g * group + ql
                i = j * group + ql
                st = st_ref[q].astype(BF16)
                ym = _mm(qe_ref[i], st, _NT) + yc_ref[i]
                y_ref[q, j * chunk:(j + 1) * chunk, :] = ym[:chunk, :] + ym[chunk:, :]
                st_ref[q] = _dot(st, gm_ref[i]) + hc_ref[i]
                return c

            lax.fori_loop(0, group, advance, 0, unroll=True)
        return carry

    lax.fori_loop(0, PAIRS // group, group_body, 0)

    ys = jnp.concatenate([y_ref[q] for q in range(PAIRS)], axis=1)
    mean = _head_sum(ys, ones_blk) * (1.0 / HEAD)
    yc = ys - mean
    var = _head_sum(yc * yc, ones_blk) * (1.0 / HEAD)
    yn = yc * lax.rsqrt(var + GN_EPS) * lng_ref[...] + lnb_ref[...]
    bonus = _head_sum(r * k2 * rk_ref[...], ones_blk) * v
    yz_ref[...] = (yn + bonus) * zg

    @pl.when(t == pl.num_programs(1) - 1)
    def _():
        shift_out_ref[...] = carry_ref[...]
        for q in range(PAIRS):
            st = st_ref[q]
            wkv_out_ref[q] = st[:, :HEAD] + st[:, HEAD:]


def _rwkv_mixer(x, mod, g_pre, wr, wg, vecs, w2a, shift0, wkv0, *, tile, t_valid):
    bsz, tp, _ = x.shape
    chunk = RWKV_CHUNK
    assert tp % tile == 0 and tile % chunk == 0
    assert t_valid == tp or tp == tile
    nt = tp // tile
    mu, w0, a0, k_k, k_a, r_k, ln_g, ln_b = vecs
    row = lambda n: pl.BlockSpec((1, n), lambda b, t: (0, 0))
    full = lambda a: pl.BlockSpec(a.shape, lambda b, t: (0,) * a.ndim)
    pair_buf = pltpu.VMEM((PAIRS, tile, LANES), F32)
    group = min(PAIRS, max(1, RWKV_INSTANCES // (tile // chunk)))
    n_inst = group * (tile // chunk)
    two = 2 * chunk
    inst = lambda cols, dtype: pltpu.VMEM((n_inst, two, cols), dtype)
    kern = functools.partial(_rwkv_kernel, tile=tile, chunk=chunk, t_valid=t_valid, group=group)
    return pl.pallas_call(
        kern,
        grid=(bsz, nt),
        in_specs=[
            pl.BlockSpec((None, tile, D_MODEL), lambda b, t: (b, t, 0)),
            pl.BlockSpec((None, 3, D_MODEL), lambda b, t: (b, 0, 0)),
            row(D_MODEL), full(wr), full(wg), row(SHIFT_W), row(D_MODEL), full(w2a), row(D_MODEL),
            row(D_MODEL), row(D_MODEL), row(D_MODEL), row(D_MODEL), row(D_MODEL),
            pl.BlockSpec((None, 1, SHIFT_W), lambda b, t: (b, 0, 0)),
            pl.BlockSpec((None, PAIRS, LANES, HEAD), lambda b, t: (b, 0, 0, 0)),
        ],
        out_specs=[
            pl.BlockSpec((None, tile, D_MODEL), lambda b, t: (b, t, 0)),
            pl.BlockSpec((None, 1, SHIFT_W), lambda b, t: (b, 0, 0)),
            pl.BlockSpec((None, PAIRS, LANES, HEAD), lambda b, t: (b, 0, 0, 0)),
        ],
        out_shape=[
            jax.ShapeDtypeStruct((bsz, tp, D_MODEL), F32),
            jax.ShapeDtypeStruct((bsz, 1, SHIFT_W), F32),
            jax.ShapeDtypeStruct((bsz, PAIRS, LANES, HEAD), F32),
        ],
        scratch_shapes=[
            pltpu.VMEM((1, SHIFT_W), F32),
            pltpu.VMEM((PAIRS, LANES, LANES), F32),
            pair_buf, pair_buf, pair_buf, pair_buf, pair_buf, pair_buf, pair_buf,
            pltpu.VMEM((PAIRS, tile // chunk, SUBLANES, LANES), F32),
            pair_buf,
            inst(two, BF16), inst(two, BF16), inst(two, BF16), inst(2 * two, BF16),
            pltpu.VMEM((n_inst, 2 * two, 2 * two), BF16),
            inst(two, BF16), inst(two, BF16), inst(two, F32), inst(two, F32),
        ],
        compiler_params=pltpu.CompilerParams(
            dimension_semantics=("arbitrary", "arbitrary"), vmem_limit_bytes=VMEM_LIMIT),
        name="rwkv7_mixer",
    )(x, mod, g_pre, wr, wg, mu, w0, w2a, a0, k_k, k_a, r_k, ln_g, ln_b, shift0, wkv0)


def _s5_kernel(x_ref, mod_ref, g_ref, wu_ref, wg_ref, bbr_ref, bbi_ref, cre_ref, cmi_ref, abr_ref, abi_ref,
               d_ref, gluw_ref, glub_ref, s0r_ref, s0i_ref,
               yz_ref, sr_out_ref, si_out_ref,
               hr_ref, hi_ref, cr_ref, ci_ref, *, tile):
    t = pl.program_id(1)

    @pl.when(t == 0)
    def _():
        cr_ref[...] = s0r_ref[...]
        ci_ref[...] = s0i_ref[...]

    h = _norm_mod(x_ref[...], g_ref[...], mod_ref[...]).astype(BF16)
    u = _dot(h, wu_ref[...])
    zg = _silu(_dot(h, wg_ref[...]))
    set_tiles = S5_TILES // S5_SETS

    def frame_rows(c):
        return pl.ds(c, tile, stride=S5_ROW_STRIDE)

    in_tiles = bbr_ref.shape[2] // LANES
    for s in range(D_MODEL // LANES):
        ub = u[:, LANES * s:LANES * (s + 1)].astype(BF16)
        bur = _dot(ub, bbr_ref[s])
        bui = _dot(ub, bbi_ref[s])
        for c in range(in_tiles):
            hr_ref[frame_rows(s * in_tiles + c), :] = bur[:, LANES * c:LANES * (c + 1)]
            hi_ref[frame_rows(s * in_tiles + c), :] = bui[:, LANES * c:LANES * (c + 1)]

    abr = abr_ref[...]
    abi = abi_ref[...]

    def frame(f, carry):
        sr, si = carry
        rows = pl.ds(pl.multiple_of(f * S5_ROW_STRIDE, SUBLANES), S5_TILES)
        nr = abr * sr - abi * si + hr_ref[rows, :]
        ni = abr * si + abi * sr + hi_ref[rows, :]
        hr_ref[rows, :] = nr
        hi_ref[rows, :] = ni
        return nr, ni

    sr, si = lax.fori_loop(0, tile, frame, (cr_ref[...], ci_ref[...]), unroll=8)
    cr_ref[...] = sr
    ci_ref[...] = si

    outs = []
    for s in range(S5_SETS):
        hr = jnp.concatenate([hr_ref[frame_rows(s * set_tiles + c), :] for c in range(set_tiles)], axis=1)
        hi = jnp.concatenate([hi_ref[frame_rows(s * set_tiles + c), :] for c in range(set_tiles)], axis=1)
        outs.append(_dot(hr.astype(BF16), cre_ref[s]) + _dot(hi.astype(BF16), cmi_ref[s]))
    y = jnp.concatenate(outs, axis=1) + d_ref[...] * u
    y = 0.5 * y * (1.0 + jnp.tanh(0.7978845608028654 * (y + 0.044715 * (y * y * y))))
    y = y * _sigmoid(_dot(y.astype(BF16), gluw_ref[...]) + glub_ref[...])
    yz_ref[...] = y * zg

    @pl.when(t == pl.num_programs(1) - 1)
    def _():
        sr_out_ref[...] = cr_ref[...]
        si_out_ref[...] = ci_ref[...]


def _s5_mixer(x, mod, g_pre, wu, wg, bbr, bbi, cre, cmi, abr, abi, d_skip, glu_w, glu_b, s0r, s0i, *, tile):
    bsz, tp, _ = x.shape
    assert tp % tile == 0 and tile % SUBLANES == 0
    nt = tp // tile
    row = lambda n: pl.BlockSpec((1, n), lambda b, t: (0, 0))
    full = lambda a: pl.BlockSpec(a.shape, lambda b, t: (0,) * a.ndim)
    state = pl.BlockSpec((None, S5_TILES, LANES), lambda b, t: (b, 0, 0))
    kern = functools.partial(_s5_kernel, tile=tile)
    return pl.pallas_call(
        kern,
        grid=(bsz, nt),
        in_specs=[
            pl.BlockSpec((None, tile, D_MODEL), lambda b, t: (b, t, 0)),
            pl.BlockSpec((None, 3, D_MODEL), lambda b, t: (b, 0, 0)),
            row(D_MODEL), full(wu), full(wg), full(bbr), full(bbi), full(cre), full(cmi), full(abr), full(abi),
            row(D_MODEL), full(glu_w), row(D_MODEL), state, state,
        ],
        out_specs=[pl.BlockSpec((None, tile, D_MODEL), lambda b, t: (b, t, 0)), state, state],
        out_shape=[
            jax.ShapeDtypeStruct((bsz, tp, D_MODEL), F32),
            jax.ShapeDtypeStruct((bsz, S5_TILES, LANES), F32),
            jax.ShapeDtypeStruct((bsz, S5_TILES, LANES), F32),
        ],
        scratch_shapes=[
            pltpu.VMEM((tile * S5_ROW_STRIDE, LANES), F32),
            pltpu.VMEM((tile * S5_ROW_STRIDE, LANES), F32),
            pltpu.VMEM((S5_TILES, LANES), F32),
            pltpu.VMEM((S5_TILES, LANES), F32),
        ],
        compiler_params=pltpu.CompilerParams(
            dimension_semantics=("arbitrary", "arbitrary"), vmem_limit_bytes=VMEM_LIMIT),
        name="s5_mixer",
    )(x, mod, g_pre, wu, wg, bbr, bbi, cre, cmi, abr, abi, d_skip, glu_w, glu_b, s0r, s0i)


def _lru_kernel(x_ref, mod_ref, g_ref, wx_ref, wg_ref, cw_ref, cb_ref, wa4_ref, ba_ref, wx4_ref, bx_ref, lam_ref,
                conv0_ref, h0_ref,
                yz_ref, conv_out_ref, h_out_ref,
                xbuf_ref, a_ref, b_ref, hc_ref, *, tile):
    t = pl.program_id(1)
    pad = SUBLANES
    hist = CONV_W - 1

    @pl.when(t == 0)
    def _():
        xbuf_ref[pad - hist:pad, :] = conv0_ref[...]
        hc_ref[...] = h0_ref[...]

    h = _norm_mod(x_ref[...], g_ref[...], mod_ref[...]).astype(BF16)
    xl = _dot(h, wx_ref[...])
    zg = _silu(_dot(h, wg_ref[...]))
    xbuf_ref[pad:pad + tile, :] = xl
    xc = cb_ref[...] + xl * cw_ref[CONV_W - 1:CONV_W, :]
    for j in range(hist):
        xc = xc + xbuf_ref[pad - hist + j:pad - hist + j + tile, :] * cw_ref[j:j + 1, :]
    tail = xbuf_ref[pad + tile - hist:pad + tile, :]
    xbuf_ref[pad - hist:pad, :] = tail

    ga, gx = [], []
    blk = D_MODEL // 4
    for s in range(4):
        xb = xc[:, blk * s:blk * (s + 1)].astype(BF16)
        ga.append(_dot(xb, wa4_ref[s]))
        gx.append(_dot(xb, wx4_ref[s]))
    gate_r = _sigmoid(jnp.concatenate(ga, axis=1) + ba_ref[...])
    gate_i = _sigmoid(jnp.concatenate(gx, axis=1) + bx_ref[...])
    log_a = -LRU_C * gate_r * _softplus(-lam_ref[...])
    a = jnp.exp(log_a)
    a_ref[...] = a
    b_ref[...] = jnp.sqrt(-jnp.tanh(log_a) * (a * a + 1.0)) * (gate_i * xc)

    row8 = lax.broadcasted_iota(jnp.int32, (SUBLANES, 1), 0)

    def row_body(g, carry):
        rs = pl.ds(pl.multiple_of(g * SUBLANES, SUBLANES), SUBLANES)
        av = a_ref[rs, :]
        bv = b_ref[rs, :]
        for sh in (1, 2, 4):
            keep = row8 >= sh
            ash = jnp.where(keep, pltpu.roll(av, sh, 0), 1.0)
            bsh = jnp.where(keep, pltpu.roll(bv, sh, 0), 0.0)
            bv = bv + av * bsh
            av = av * ash
        hs = bv + av * carry
        b_ref[rs, :] = hs
        return hs[SUBLANES - 1:SUBLANES, :]

    hc_ref[...] = lax.fori_loop(0, tile // SUBLANES, row_body, hc_ref[...])
    yz_ref[...] = b_ref[...] * zg

    @pl.when(t == pl.num_programs(1) - 1)
    def _():
        conv_out_ref[...] = tail
        h_out_ref[...] = hc_ref[...]


def _lru_mixer(x, mod, g_pre, wx, wg, conv_w, conv_b, wa4, ba, wx4, bx, lam, conv0, h0, *, tile):
    bsz, tp, _ = x.shape
    assert tp % tile == 0 and tile % SUBLANES == 0 and tile >= SUBLANES
    nt = tp // tile
    row = lambda n: pl.BlockSpec((1, n), lambda b, t: (0, 0))
    full = lambda a: pl.BlockSpec(a.shape, lambda b, t: (0,) * a.ndim)
    kern = functools.partial(_lru_kernel, tile=tile)
    return pl.pallas_call(
        kern,
        grid=(bsz, nt),
        in_specs=[
            pl.BlockSpec((None, tile, D_MODEL), lambda b, t: (b, t, 0)),
            pl.BlockSpec((None, 3, D_MODEL), lambda b, t: (b, 0, 0)),
            row(D_MODEL), full(wx), full(wg), full(conv_w), row(D_MODEL), full(wa4), row(D_MODEL), full(wx4),
            row(D_MODEL), row(D_MODEL),
            pl.BlockSpec((None, CONV_W - 1, D_MODEL), lambda b, t: (b, 0, 0)),
            pl.BlockSpec((None, 1, D_MODEL), lambda b, t: (b, 0, 0)),
        ],
        out_specs=[
            pl.BlockSpec((None, tile, D_MODEL), lambda b, t: (b, t, 0)),
            pl.BlockSpec((None, CONV_W - 1, D_MODEL), lambda b, t: (b, 0, 0)),
            pl.BlockSpec((None, 1, D_MODEL), lambda b, t: (b, 0, 0)),
        ],
        out_shape=[
            jax.ShapeDtypeStruct((bsz, tp, D_MODEL), F32),
            jax.ShapeDtypeStruct((bsz, CONV_W - 1, D_MODEL), F32),
            jax.ShapeDtypeStruct((bsz, 1, D_MODEL), F32),
        ],
        scratch_shapes=[
            pltpu.VMEM((tile + SUBLANES, D_MODEL), F32),
            pltpu.VMEM((tile, D_MODEL), F32),
            pltpu.VMEM((tile, D_MODEL), F32),
            pltpu.VMEM((1, D_MODEL), F32),
        ],
        compiler_params=pltpu.CompilerParams(
            dimension_semantics=("arbitrary", "arbitrary"), vmem_limit_bytes=VMEM_LIMIT),
        name="rglru_mixer",
    )(x, mod, g_pre, wx, wg, conv_w, conv_b, wa4, ba, wx4, bx, lam, conv0, h0)


def _merge_kernel(x_ref, mod_ref, g_ref, gpost_ref, wm_ref, wo_ref, yr_ref, ys_ref, yl_ref, o_ref):
    x = x_ref[...]
    mod = mod_ref[...]
    h = _norm_mod(x, g_ref[...], mod).astype(BF16)
    m = _sigmoid(_dot(h, wm_ref[...]))
    merged = (m[:, :D_MODEL] * yr_ref[...] + m[:, D_MODEL:2 * D_MODEL] * ys_ref[...]
              + m[:, 2 * D_MODEL:] * yl_ref[...])
    o = _dot(merged.astype(BF16), wo_ref[...])
    ms = jnp.mean(o * o, axis=-1, keepdims=True)
    o = (o * lax.rsqrt(ms + RMS_EPS)) * gpost_ref[...]
    o_ref[...] = x + mod[2:3, :] * o


def _merge_out(x, mod, g_pre, g_post, wm, wo, yr, ys, yl, *, tile):
    bsz, tp, _ = x.shape
    assert tp % tile == 0
    nt = tp // tile
    act = pl.BlockSpec((None, tile, D_MODEL), lambda b, t: (b, t, 0))
    row = lambda n: pl.BlockSpec((1, n), lambda b, t: (0, 0))
    full = lambda a: pl.BlockSpec(a.shape, lambda b, t: (0,) * a.ndim)
    return pl.pallas_call(
        _merge_kernel,
        grid=(bsz, nt),
        in_specs=[act, pl.BlockSpec((None, 3, D_MODEL), lambda b, t: (b, 0, 0)), row(D_MODEL), row(D_MODEL),
                  full(wm), full(wo), act, act, act],
        out_specs=act,
        out_shape=jax.ShapeDtypeStruct((bsz, tp, D_MODEL), F32),
        compiler_params=pltpu.CompilerParams(
            dimension_semantics=("arbitrary", "arbitrary"), vmem_limit_bytes=VMEM_LIMIT),
        name="merge_out",
    )(x, mod, g_pre, g_post, wm, wo, yr, ys, yl)


def _block_diag_expand(w, per):
    n, a, b = w.shape
    w = w.reshape(n // per, per, a, b)
    eye = jnp.eye(per, dtype=w.dtype)
    return jnp.einsum("sgab,gh->sgahb", w, eye).reshape(n // per, per * a, per * b)


def _layer(x, mod, state, prm, *, t_valid, tiles):
    shift_row, wkv, s_re, s_im, lru_h, lru_conv = state
    bsz = x.shape[0]
    g_pre = prm["g_pre"]

    xr = x
    if x.shape[1] % tiles["rwkv"] != 0:
        xr = jnp.pad(x, ((0, 0), (0, tiles["rwkv"] - x.shape[1] % tiles["rwkv"]), (0, 0)))
    yz_r, shift_new, wkv_new = _rwkv_mixer(
        xr, mod, g_pre, prm["w_rwkv"], prm["w_rwkv_g"], prm["rwkv_vecs"], prm["w2a"],
        shift_row.reshape(bsz, 1, SHIFT_W), wkv.reshape(bsz, PAIRS, LANES, HEAD), tile=tiles["rwkv"],
        t_valid=t_valid)
    yz_r = yz_r[:, :t_valid]
    yz_s, s_re_new, s_im_new = _s5_mixer(
        x, mod, g_pre, prm["w_s5"], prm["w_s5_g"], prm["bbr"], prm["bbi"], prm["cre"], prm["cmi"],
        prm["abr"], prm["abi"], prm["s5_d"], prm["glu_w"], prm["glu_b"],
        s_re.reshape(bsz, S5_TILES, LANES), s_im.reshape(bsz, S5_TILES, LANES), tile=tiles["s5"])
    yz_l, conv_new, h_new = _lru_mixer(
        x, mod, g_pre, prm["w_lru"], prm["w_lru_g"], prm["conv_w"], prm["conv_b"], prm["wa4"], prm["ba"],
        prm["wx4"], prm["bx"], prm["lam"], lru_conv, lru_h.reshape(bsz, 1, D_MODEL), tile=tiles["lru"])
    x_new = _merge_out(x, mod, g_pre, prm["g_post"], prm["w_merge"], prm["w_out"], yz_r, yz_s, yz_l,
                       tile=tiles["merge"])
    new_state = (shift_new.reshape(bsz, SHIFT_W), wkv_new.reshape(bsz, HEADS, HEAD, HEAD),
                 s_re_new.reshape(bsz, 64, 64), s_im_new.reshape(bsz, 64, 64),
                 h_new.reshape(bsz, D_MODEL), conv_new)
    return x_new, new_state


def _layer_params(l, w_in, w_out, norm_pre, norm_post, rwkv, s5, lru):
    d = D_MODEL
    wb = w_in[l].astype(BF16)
    o = SHIFT_W
    row = lambda a: a.reshape(1, -1)
    mu, w0, w2, a0, a2, k_k, k_a, r_k, ln_g, ln_b = [a[l] for a in rwkv]
    a_re, a_im, log_step, b_re, b_im, c_re, c_im, s5_d, glu_w, glu_b = [a[l] for a in s5]
    conv_w, conv_b, wa, ba, wx, bx, lam = [a[l] for a in lru]
    zeros = jnp.zeros((LORA, d), F32)
    w2a = jnp.concatenate([jnp.concatenate([w2, zeros], axis=1), jnp.concatenate([zeros, a2], axis=1)], axis=0)
    abr, abi, bbr, bbi = _s5_prep(a_re, a_im, log_step, b_re, b_im)
    per = 64 // S5_SETS
    return dict(
        g_pre=row(norm_pre[l]), g_post=row(norm_post[l]),
        w_rwkv=wb[:, :o], w_rwkv_g=wb[:, o:o + d],
        w_s5=wb[:, o + d:o + 2 * d], w_s5_g=wb[:, o + 2 * d:o + 3 * d],
        w_lru=wb[:, o + 3 * d:o + 4 * d], w_lru_g=wb[:, o + 4 * d:o + 5 * d],
        w_merge=wb[:, o + 5 * d:], w_out=w_out[l].astype(BF16),
        rwkv_vecs=(row(mu), row(w0), row(a0), row(k_k), row(k_a), row(r_k), row(ln_g), row(ln_b)),
        w2a=w2a.astype(BF16),
        bbr=_block_diag_expand(bbr, S5_IN_GROUPS).astype(BF16),
        bbi=_block_diag_expand(bbi, S5_IN_GROUPS).astype(BF16),
        cre=_block_diag_expand(jnp.swapaxes(c_re, 1, 2), per).astype(BF16),
        cmi=_block_diag_expand(-jnp.swapaxes(c_im, 1, 2), per).astype(BF16),
        abr=abr.reshape(S5_TILES, LANES), abi=abi.reshape(S5_TILES, LANES),
        s5_d=row(s5_d), glu_w=glu_w.astype(BF16), glu_b=row(glu_b),
        conv_w=conv_w, conv_b=row(conv_b),
        wa4=_block_diag_expand(wa, 4).astype(BF16), ba=row(ba),
        wx4=_block_diag_expand(wx, 4).astype(BF16), bx=row(bx), lam=row(lam),
    )


def _tiles(t):
    pick = lambda want: want if t % want == 0 else t
    return dict(rwkv=256 if t % 256 == 0 else RWKV_CHUNK, s5=pick(256), lru=pick(512), merge=pick(512))


def kernel(x_prompt, x_sample, state_rwkv_shift, state_rwkv_wkv, state_s5_re, state_s5_im, state_lru_h,
           state_lru_conv, c_prompt, c_sample, ada_w, ada_b, norm_pre, norm_post, w_in, w_out, rwkv_mu, rwkv_w0,
           rwkv_w2, rwkv_a0, rwkv_a2, rwkv_k_k, rwkv_k_a, rwkv_r_k, rwkv_ln_g, rwkv_ln_b, s5_a_re, s5_a_im,
           s5_log_step, s5_b_re, s5_b_im, s5_c_re, s5_c_im, s5_d, s5_glu_w, s5_glu_b, lru_conv_w, lru_conv_b,
           lru_wa, lru_ba, lru_wx, lru_bx, lru_lambda):
    depth = w_in.shape[0]
    bp, tp, _ = x_prompt.shape
    bs, ts, _ = x_sample.shape
    c_all = jnp.concatenate([c_prompt, c_sample], axis=0)
    pad = (-c_all.shape[0]) % SUBLANES
    c_all = jnp.pad(c_all, ((0, pad), (0, 0)))
    mod_all = _modulation(c_all, ada_w, ada_b)
    rwkv = (rwkv_mu, rwkv_w0, rwkv_w2, rwkv_a0, rwkv_a2, rwkv_k_k, rwkv_k_a,
            rwkv_r_k.reshape(depth, D_MODEL), rwkv_ln_g, rwkv_ln_b)
    s5 = (s5_a_re, s5_a_im, s5_log_step, s5_b_re, s5_b_im, s5_c_re, s5_c_im, s5_d, s5_glu_w, s5_glu_b)
    lru = (lru_conv_w, lru_conv_b, lru_wa, lru_ba, lru_wx, lru_bx, lru_lambda)

    xp, xs = x_prompt, x_sample
    new_p = [[] for _ in range(6)]
    new_s = [[] for _ in range(6)]
    for l in range(depth):
        prm = _layer_params(l, w_in, w_out, norm_pre, norm_post, rwkv, s5, lru)
        mod_p = mod_all[l, :bp].reshape(bp, 3, D_MODEL)
        mod_s = mod_all[l, bp:bp + bs].reshape(bs, 3, D_MODEL)
        zero = (jnp.zeros((bp, SHIFT_W), F32), jnp.zeros((bp, HEADS, HEAD, HEAD), F32),
                jnp.zeros((bp, 64, 64), F32), jnp.zeros((bp, 64, 64), F32),
                jnp.zeros((bp, D_MODEL), F32), jnp.zeros((bp, CONV_W - 1, D_MODEL), F32))
        xp, st_p = _layer(xp, mod_p, zero, prm, t_valid=tp, tiles=_tiles(tp))
        st_in = (state_rwkv_shift[l], state_rwkv_wkv[l], state_s5_re[l], state_s5_im[l],
                 state_lru_h[l], state_lru_conv[l])
        xs, st_s = _layer(xs, mod_s, st_in, prm, t_valid=ts, tiles=_tiles(ts))
        for i in range(6):
            new_p[i].append(st_p[i])
            new_s[i].append(st_s[i])
    sp = [jnp.stack(z, axis=0) for z in new_p]
    ss = [jnp.stack(z, axis=0) for z in new_s]
    return (xp, xs, sp[0], sp[1], sp[2], sp[3], sp[4], sp[5], ss[0], ss[1], ss[2], ss[3], ss[4], ss[5])
```

```python
import functools

import jax
import jax.numpy as jnp
from jax import lax
from jax.experimental import pallas as pl
from jax.experimental.pallas import tpu as pltpu

F32 = jnp.float32
BF16 = jnp.bfloat16
HIGHEST = lax.Precision.HIGHEST

D_MODEL = 1024
HEAD = 64
HEADS = D_MODEL // HEAD
PAIRS = HEADS // 2
LORA = 64
SHIFT_W = 3 * D_MODEL + 2 * LORA
S5_STATE_W = 64 * 64
S5_SETS = 4
S5_IN_GROUPS = 8
S5_TILES = S5_STATE_W // 128
S5_ROW_STRIDE = 40
CONV_W = 4
LRU_C = 8.0
RMS_EPS = 1e-6
GN_EPS = 64e-5
RWKV_CHUNK = 64
DECAY_SCALE = 0.6065306597126334
RWKV_INSTANCES = 16
SUBLANES = 8
LANES = 128
VMEM_LIMIT = 56 * 1024 * 1024


def _sigmoid(x):
    return 0.5 * jnp.tanh(0.5 * x) + 0.5


def _silu(x):
    return x * _sigmoid(x)


def _softplus(x):
    return jnp.maximum(x, 0.0) + jnp.log1p(jnp.exp(-jnp.abs(x)))


def _norm_mod(x, g, mod):
    ms = jnp.mean(x * x, axis=-1, keepdims=True)
    return (x * lax.rsqrt(ms + RMS_EPS)) * (g * (1.0 + mod[1:2, :])) + mod[0:1, :]


def _dot(a, b):
    return jnp.dot(a, b, preferred_element_type=F32)


def _dot_hi(a, b):
    return jnp.dot(a, b, preferred_element_type=F32, precision=HIGHEST)


def _split2(x):
    hi = x.astype(BF16)
    lo = (x - hi.astype(F32)).astype(BF16)
    return hi, lo


_NN = (((1,), (0,)), ((), ()))
_NT = (((1,), (1,)), ((), ()))
_TN = (((0,), (0,)), ((), ()))


def _mm(a, b, dims=_NN):
    return lax.dot_general(a.astype(BF16), b.astype(BF16), dims, preferred_element_type=F32)


def _mm3(a_parts, b_parts):
    (a_hi, a_lo), (b_hi, b_lo) = a_parts, b_parts
    return _dot(a_hi, b_hi) + (_dot(a_hi, b_lo) + _dot(a_lo, b_hi))


def _head_sum(x, ones_blk):
    outs = []
    for s in range(D_MODEL // 256):
        outs.append(_dot(x[:, 256 * s:256 * (s + 1)].astype(BF16), ones_blk))
    return jnp.concatenate(outs, axis=1)


def _block_ones(n, blk):
    ri = lax.broadcasted_iota(jnp.int32, (n, n), 0)
    ci = lax.broadcasted_iota(jnp.int32, (n, n), 1)
    sh = blk.bit_length() - 1
    return ((ri >> sh) == (ci >> sh)).astype(F32).astype(BF16)


def _mod_kernel(c_ref, w_ref, b_ref, o_ref):
    s = _silu(c_ref[...])
    o_ref[...] = _dot_hi(s, w_ref[...]) + b_ref[...]


def _modulation(c_all, ada_w, ada_b):
    depth = ada_w.shape[0]
    rows = c_all.shape[0]
    return pl.pallas_call(
        _mod_kernel,
        grid=(depth, 3),
        in_specs=[
            pl.BlockSpec((rows, D_MODEL), lambda l, j: (0, 0)),
            pl.BlockSpec((None, D_MODEL, D_MODEL), lambda l, j: (l, 0, j)),
            pl.BlockSpec((None, 1, D_MODEL), lambda l, j: (l, 0, j)),
        ],
        out_specs=pl.BlockSpec((None, rows, D_MODEL), lambda l, j: (l, 0, j)),
        out_shape=jax.ShapeDtypeStruct((depth, rows, 3 * D_MODEL), F32),
        name="adaln_mod",
    )(c_all, ada_w, ada_b.reshape(depth, 1, 3 * D_MODEL))


def _s5_prep_kernel(are_ref, aim_ref, ls_ref, bre_ref, bim_ref, abr_ref, abi_ref, bbr_ref, bbi_ref):
    are = are_ref[...]
    aim = aim_ref[...]
    dt = jnp.exp(ls_ref[...])
    mag = jnp.exp(are * dt)
    abr = mag * jnp.cos(aim * dt)
    abi = mag * jnp.sin(aim * dt)
    abr_ref[...] = abr
    abi_ref[...] = abi
    nr = abr - 1.0
    ni = abi
    den = are * are + aim * aim
    cr = (nr * are + ni * aim) / den
    ci = (ni * are - nr * aim) / den
    bre = bre_ref[...]
    bim = bim_ref[...]
    bbr_ref[...] = cr[:, None, :] * bre - ci[:, None, :] * bim
    bbi_ref[...] = cr[:, None, :] * bim + ci[:, None, :] * bre


def _s5_prep(a_re, a_im, log_step, b_re, b_im):
    g, p = a_re.shape
    i = b_re.shape[-1]
    return pl.pallas_call(
        _s5_prep_kernel,
        out_shape=(
            jax.ShapeDtypeStruct((g, p), F32),
            jax.ShapeDtypeStruct((g, p), F32),
            jax.ShapeDtypeStruct((g, i, p), F32),
            jax.ShapeDtypeStruct((g, i, p), F32),
        ),
        name="s5_discretise",
    )(a_re, a_im, log_step.reshape(g, 1), jnp.swapaxes(b_re, 1, 2), jnp.swapaxes(b_im, 1, 2))


def _rwkv_kernel(x_ref, mod_ref, g_ref, wr_ref, wg_ref, mu_ref, w0_ref, w2a_ref, a0_ref, kk_ref, ka_ref,
                 rk_ref, lng_ref, lnb_ref, shift0_ref, wkv0_ref,
                 yz_ref, shift_out_ref, wkv_out_ref,
                 carry_ref, st_ref, at_ref, rt_ref, bt_ref, kt_ref, bh_ref, kh_ref, v_ref, pl_ref, y_ref,
                 nh_ref, t_ref, aak_ref, ark_ref, rhs_ref, gm_ref, qe_ref, hc_ref, yc_ref,
                 *, tile, chunk, t_valid, group):
    t = pl.program_id(1)
    n_chunks = tile // chunk
    two = 2 * chunk
    n_inst = group * n_chunks

    @pl.when(t == 0)
    def _():
        carry_ref[...] = shift0_ref[...]
        rh = lax.broadcasted_iota(jnp.int32, (LANES, LANES), 0) >> (HEAD.bit_length() - 1)
        ch = lax.broadcasted_iota(jnp.int32, (LANES, LANES), 1) >> (HEAD.bit_length() - 1)
        for q in range(PAIRS):
            s0 = wkv0_ref[q]
            st_ref[q] = jnp.where(rh == ch, jnp.concatenate([s0, s0], axis=1), 0.0)

    h = _norm_mod(x_ref[...], g_ref[...], mod_ref[...]).astype(BF16)
    p = _dot(h, wr_ref[...])
    zg = _silu(_dot(h, wg_ref[...]))

    rows = lax.broadcasted_iota(jnp.int32, (tile, 1), 0)
    prev = jnp.where(rows == 0, carry_ref[...], pltpu.roll(p, 1, 0))
    last = (t_valid - 1) % tile
    carry_ref[...] = p[last:last + 1, :]
    pm = p + (prev - p) * mu_ref[...]

    r = pm[:, 0:D_MODEL]
    k = pm[:, D_MODEL:2 * D_MODEL]
    v = pm[:, 2 * D_MODEL:3 * D_MODEL]
    wa = pm[:, 3 * D_MODEL:SHIFT_W]
    lane = lax.broadcasted_iota(jnp.int32, (1, LANES), 1)
    wa = jnp.where(lane < LORA, jnp.tanh(wa), wa)
    lora = _dot(wa.astype(BF16), w2a_ref[...])
    logd = -DECAY_SCALE * _sigmoid(w0_ref[...] + lora[:, :D_MODEL])
    a_sig = _sigmoid(a0_ref[...] + lora[:, D_MODEL:])

    ones_blk = _block_ones(256, HEAD)
    kk = k * kk_ref[...]
    kk = kk * jnp.minimum(lax.rsqrt(_head_sum(kk * kk, ones_blk)), 1e12)
    k2 = k * (1.0 + (a_sig - 1.0) * ka_ref[...])
    bvec = kk * a_sig
    if t_valid % tile != 0:
        ok = rows < t_valid
        logd = jnp.where(ok, logd, 0.0)
        bvec = jnp.where(ok, bvec, 0.0)
        k2 = jnp.where(ok, k2, 0.0)

    rmod = rows & (chunk - 1)
    cum = logd
    s = 1
    while s < chunk:
        cum = cum + jnp.where(rmod >= s, pltpu.roll(cum, s, 0), 0.0)
        s *= 2
    cum_last = jnp.concatenate(
        [jnp.broadcast_to(cum[(j + 1) * chunk - 1:(j + 1) * chunk, :], (chunk, D_MODEL)) for j in range(n_chunks)],
        axis=0)
    p_inc = jnp.exp(cum)
    p_inv = jnp.exp(-cum)
    p_exc = jnp.exp(cum - logd)
    p_end = jnp.exp(cum_last - cum)

    def to_pairs(ref, val):
        for q in range(PAIRS):
            ref[q] = val[:, LANES * q:LANES * (q + 1)]

    to_pairs(at_ref, -kk * p_exc)
    to_pairs(rt_ref, r * p_inc)
    to_pairs(bt_ref, bvec * p_inv)
    to_pairs(kt_ref, k2 * p_inv)
    to_pairs(bh_ref, bvec * p_end)
    to_pairs(kh_ref, k2 * p_end)
    to_pairs(v_ref, v)
    for j in range(n_chunks):
        row = jnp.exp(cum[(j + 1) * chunk - 1:(j + 1) * chunk, :])
        for q in range(PAIRS):
            pl_ref[q, j] = jnp.broadcast_to(row[:, LANES * q:LANES * (q + 1)], (SUBLANES, LANES))

    m_a = (lane < HEAD).astype(F32)
    m_b = 1.0 - m_a
    ri = lax.broadcasted_iota(jnp.int32, (two, two), 0)
    ci = lax.broadcasted_iota(jnp.int32, (two, two), 1)
    csh = chunk.bit_length() - 1
    same_head = (ri >> csh) == (ci >> csh)
    strict = jnp.logical_and(same_head, (ri & (chunk - 1)) > (ci & (chunk - 1)))
    incl = jnp.logical_and(same_head, (ri & (chunk - 1)) >= (ci & (chunk - 1)))
    eye = ri == ci
    head_a = ri < chunk
    off_masks = []
    for lg in range(csh):
        off_masks.append(jnp.logical_and(strict, jnp.logical_and((ri >> (lg + 1)) == (ci >> (lg + 1)),
                                                                 (ri >> lg) != (ci >> lg))))

    def stack(x):
        return jnp.concatenate([x * m_a, x * m_b], axis=0)

    gsh = group.bit_length() - 1
    unroll = min(16, n_inst)

    def group_body(g, carry):
        def where_is(i):
            q = g * group + (i & (group - 1))
            j = i >> gsh
            return q, j, pl.ds(pl.multiple_of(j * chunk, chunk), chunk)

        def gram(i, c):
            q, _, sl = where_is(i)
            atm = stack(at_ref[q, sl, :])
            rtm = stack(rt_ref[q, sl, :])
            bt = bt_ref[q, sl, :]
            kt = kt_ref[q, sl, :]
            o1 = _mm(jnp.concatenate([atm, rtm], axis=0), jnp.concatenate([bt, kt], axis=0), _NT)
            o1_sw = pltpu.roll(o1, chunk, 1)
            n_ab = jnp.where(strict, jnp.where(head_a, o1[:two], o1_sw[:two]), 0.0)
            nh_ref[i] = n_ab.astype(BF16)
            t_ref[i] = jnp.where(eye, 1.0, jnp.where(off_masks[0], n_ab, 0.0)).astype(BF16)
            aak_ref[i] = jnp.where(strict, jnp.where(head_a, o1_sw[:two], o1[:two]), 0.0).astype(BF16)
            ark_ref[i] = jnp.concatenate([jnp.where(incl, jnp.where(head_a, o1[two:], o1_sw[two:]), 0.0),
                                          jnp.where(incl, jnp.where(head_a, o1_sw[two:], o1[two:]), 0.0)],
                                         axis=1).astype(BF16)
            return c

        lax.fori_loop(0, n_inst, gram, 0, unroll=unroll)

        for off in off_masks[1:]:
            def level(i, c, off=off):
                t_inv = t_ref[i]
                x = _dot(jnp.where(off, nh_ref[i], 0.0), t_inv).astype(BF16)
                t_ref[i] = t_inv + _dot(t_inv, x).astype(BF16)
                return c

            lax.fori_loop(0, n_inst, level, 0, unroll=unroll)

        def right_side(i, c):
            q, _, sl = where_is(i)
            atm = stack(at_ref[q, sl, :])
            vm = stack(v_ref[q, sl, :])
            z0_hi, z0_lo = _split2(jnp.concatenate([atm, _mm(aak_ref[i], vm)], axis=1))
            rhs_ref[i] = jnp.concatenate([z0_hi, z0_lo], axis=0)
            return c

        lax.fori_loop(0, n_inst, right_side, 0, unroll=unroll)

        def apply(i, c):
            q, _, sl = where_is(i)
            vm = stack(v_ref[q, sl, :])
            t_inv = t_ref[i]
            z = _dot(t_inv, rhs_ref[i, :two, :]) + _dot(t_inv, rhs_ref[i, two:, :])
            rhs_ref[i] = jnp.concatenate([z, jnp.concatenate([jnp.zeros_like(vm), vm], axis=1)], axis=0).astype(BF16)
            return c

        lax.fori_loop(0, n_inst, apply, 0, unroll=unroll)

        def operators(i, c):
            q, j, sl = where_is(i)
            rhs2 = rhs_ref[i]
            k_all = g * n_inst + i
            top = _mm(rhs2, jnp.concatenate([stack(bh_ref[q, sl, :]), stack(kh_ref[q, sl, :])], axis=0), _TN)
            bot = _dot(ark_ref[i], rhs2)
            gm_ref[k_all] = (top[:two, :]
                             + jnp.where(eye, jnp.broadcast_to(pl_ref[q, j][0:1, :], (two, two)), 0.0)).astype(BF16)
            hc_ref[k_all] = top[two:, :]
            qe_ref[k_all] = (bot[:, :two] + stack(rt_ref[q, sl, :])).astype(BF16)
            yc_ref[k_all] = bot[:, two:]
            return c

        lax.fori_loop(0, n_inst, operators, 0, unroll=unroll)
        return carry

    lax.fori_loop(0, PAIRS // group, group_body, 0)

    for j in range(n_chunks):
        def advance(q, c, j=j):
            k_all = (q >> gsh) * n_inst + j * group + (q & (group - 1))
            st = st_ref[q].astype(BF16)
            ym = _mm(qe_ref[k_all], st, _NT) + yc_ref[k_all]
            y_ref[q, j * chunk:(j + 1) * chunk, :] = ym[:chunk, :] + ym[chunk:, :]
            st_ref[q] = _dot(st, gm_ref[k_all]) + hc_ref[k_all]
            return c

        lax.fori_loop(0, PAIRS, advance, 0, unroll=True)

    ys = jnp.concatenate([y_ref[q] for q in range(PAIRS)], axis=1)
    mean = _head_sum(ys, ones_blk) * (1.0 / HEAD)
    yc = ys - mean
    var = _head_sum(yc * yc, ones_blk) * (1.0 / HEAD)
    yn = yc * lax.rsqrt(var + GN_EPS) * lng_ref[...] + lnb_ref[...]
    bonus = _head_sum(r * k2 * rk_ref[...], ones_blk) * v
    yz_ref[...] = (yn + bonus) * zg

    @pl.when(t == pl.num_programs(1) - 1)
    def _():
        shift_out_ref[...] = carry_ref[...]
        for q in range(PAIRS):
            st = st_ref[q]
            wkv_out_ref[q] = st[:, :HEAD] + st[:, HEAD:]


def _rwkv_mixer(x, mod, g_pre, wr, wg, vecs, w2a, shift0, wkv0, *, tile, t_valid):
    bsz, tp, _ = x.shape
    chunk = RWKV_CHUNK
    assert tp % tile == 0 and tile % chunk == 0
    assert t_valid == tp or tp == tile
    nt = tp // tile
    mu, w0, a0, k_k, k_a, r_k, ln_g, ln_b = vecs
    row = lambda n: pl.BlockSpec((1, n), lambda b, t: (0, 0))
    full = lambda a: pl.BlockSpec(a.shape, lambda b, t: (0,) * a.ndim)
    pair_buf = pltpu.VMEM((PAIRS, tile, LANES), F32)
    group = min(PAIRS, max(1, RWKV_INSTANCES // (tile // chunk)))
    n_inst = group * (tile // chunk)
    two = 2 * chunk
    inst = lambda cols, dtype: pltpu.VMEM((n_inst, two, cols), dtype)
    every = lambda dtype: pltpu.VMEM((PAIRS * (tile // chunk), two, two), dtype)
    kern = functools.partial(_rwkv_kernel, tile=tile, chunk=chunk, t_valid=t_valid, group=group)
    return pl.pallas_call(
        kern,
        grid=(bsz, nt),
        in_specs=[
            pl.BlockSpec((None, tile, D_MODEL), lambda b, t: (b, t, 0)),
            pl.BlockSpec((None, 3, D_MODEL), lambda b, t: (b, 0, 0)),
            row(D_MODEL), full(wr), full(wg), row(SHIFT_W), row(D_MODEL), full(w2a), row(D_MODEL),
            row(D_MODEL), row(D_MODEL), row(D_MODEL), row(D_MODEL), row(D_MODEL),
            pl.BlockSpec((None, 1, SHIFT_W), lambda b, t: (b, 0, 0)),
            pl.BlockSpec((None, PAIRS, LANES, HEAD), lambda b, t: (b, 0, 0, 0)),
        ],
        out_specs=[
            pl.BlockSpec((None, tile, D_MODEL), lambda b, t: (b, t, 0)),
            pl.BlockSpec((None, 1, SHIFT_W), lambda b, t: (b, 0, 0)),
            pl.BlockSpec((None, PAIRS, LANES, HEAD), lambda b, t: (b, 0, 0, 0)),
        ],
        out_shape=[
            jax.ShapeDtypeStruct((bsz, tp, D_MODEL), F32),
            jax.ShapeDtypeStruct((bsz, 1, SHIFT_W), F32),
            jax.ShapeDtypeStruct((bsz, PAIRS, LANES, HEAD), F32),
        ],
        scratch_shapes=[
            pltpu.VMEM((1, SHIFT_W), F32),
            pltpu.VMEM((PAIRS, LANES, LANES), F32),
            pair_buf, pair_buf, pair_buf, pair_buf, pair_buf, pair_buf, pair_buf,
            pltpu.VMEM((PAIRS, tile // chunk, SUBLANES, LANES), F32),
            pair_buf,
            inst(two, BF16), inst(two, BF16), inst(two, BF16), inst(2 * two, BF16),
            pltpu.VMEM((n_inst, 2 * two, 2 * two), BF16),
            every(BF16), every(BF16), every(F32), every(F32),
        ],
        compiler_params=pltpu.CompilerParams(
            dimension_semantics=("arbitrary", "arbitrary"), vmem_limit_bytes=VMEM_LIMIT),
        name="rwkv7_mixer",
    )(x, mod, g_pre, wr, wg, mu, w0, w2a, a0, k_k, k_a, r_k, ln_g, ln_b, shift0, wkv0)


def _s5_kernel(x_ref, mod_ref, g_ref, wu_ref, wg_ref, bbr_ref, bbi_ref, cre_ref, cmi_ref, abr_ref, abi_ref,
               d_ref, gluw_ref, glub_ref, s0r_ref, s0i_ref,
               yz_ref, sr_out_ref, si_out_ref,
               hr_ref, hi_ref, cr_ref, ci_ref, *, tile):
    t = pl.program_id(1)

    @pl.when(t == 0)
    def _():
        cr_ref[...] = s0r_ref[...]
        ci_ref[...] = s0i_ref[...]

    h = _norm_mod(x_ref[...], g_ref[...], mod_ref[...]).astype(BF16)
    u = _dot(h, wu_ref[...])
    zg = _silu(_dot(h, wg_ref[...]))
    set_tiles = S5_TILES // S5_SETS

    def frame_rows(c):
        return pl.ds(c, tile, stride=S5_ROW_STRIDE)

    in_tiles = bbr_ref.shape[2] // LANES
    for s in range(D_MODEL // LANES):
        ub = u[:, LANES * s:LANES * (s + 1)].astype(BF16)
        bur = _dot(ub, bbr_ref[s])
        bui = _dot(ub, bbi_ref[s])
        for c in range(in_tiles):
            hr_ref[frame_rows(s * in_tiles + c), :] = bur[:, LANES * c:LANES * (c + 1)]
            hi_ref[frame_rows(s * in_tiles + c), :] = bui[:, LANES * c:LANES * (c + 1)]

    abr = abr_ref[...]
    abi = abi_ref[...]

    def frame(f, carry):
        sr, si = carry
        rows = pl.ds(pl.multiple_of(f * S5_ROW_STRIDE, SUBLANES), S5_TILES)
        nr = abr * sr - abi * si + hr_ref[rows, :]
        ni = abr * si + abi * sr + hi_ref[rows, :]
        hr_ref[rows, :] = nr
        hi_ref[rows, :] = ni
        return nr, ni

    sr, si = lax.fori_loop(0, tile, frame, (cr_ref[...], ci_ref[...]), unroll=8)
    cr_ref[...] = sr
    ci_ref[...] = si

    outs = []
    for s in range(S5_SETS):
        hr = jnp.concatenate([hr_ref[frame_rows(s * set_tiles + c), :] for c in range(set_tiles)], axis=1)
        hi = jnp.concatenate([hi_ref[frame_rows(s * set_tiles + c), :] for c in range(set_tiles)], axis=1)
        outs.append(_dot(hr.astype(BF16), cre_ref[s]) + _dot(hi.astype(BF16), cmi_ref[s]))
    y = jnp.concatenate(outs, axis=1) + d_ref[...] * u
    y = 0.5 * y * (1.0 + jnp.tanh(0.7978845608028654 * (y + 0.044715 * (y * y * y))))
    y = y * _sigmoid(_dot(y.astype(BF16), gluw_ref[...]) + glub_ref[...])
    yz_ref[...] = y * zg

    @pl.when(t == pl.num_programs(1) - 1)
    def _():
        sr_out_ref[...] = cr_ref[...]
        si_out_ref[...] = ci_ref[...]


def _s5_mixer(x, mod, g_pre, wu, wg, bbr, bbi, cre, cmi, abr, abi, d_skip, glu_w, glu_b, s0r, s0i, *, tile):
    bsz, tp, _ = x.shape
    assert tp % tile == 0 and tile % SUBLANES == 0
    nt = tp // tile
    row = lambda n: pl.BlockSpec((1, n), lambda b, t: (0, 0))
    full = lambda a: pl.BlockSpec(a.shape, lambda b, t: (0,) * a.ndim)
    state = pl.BlockSpec((None, S5_TILES, LANES), lambda b, t: (b, 0, 0))
    kern = functools.partial(_s5_kernel, tile=tile)
    return pl.pallas_call(
        kern,
        grid=(bsz, nt),
        in_specs=[
            pl.BlockSpec((None, tile, D_MODEL), lambda b, t: (b, t, 0)),
            pl.BlockSpec((None, 3, D_MODEL), lambda b, t: (b, 0, 0)),
            row(D_MODEL), full(wu), full(wg), full(bbr), full(bbi), full(cre), full(cmi), full(abr), full(abi),
            row(D_MODEL), full(glu_w), row(D_MODEL), state, state,
        ],
        out_specs=[pl.BlockSpec((None, tile, D_MODEL), lambda b, t: (b, t, 0)), state, state],
        out_shape=[
            jax.ShapeDtypeStruct((bsz, tp, D_MODEL), F32),
            jax.ShapeDtypeStruct((bsz, S5_TILES, LANES), F32),
            jax.ShapeDtypeStruct((bsz, S5_TILES, LANES), F32),
        ],
        scratch_shapes=[
            pltpu.VMEM((tile * S5_ROW_STRIDE, LANES), F32),
            pltpu.VMEM((tile * S5_ROW_STRIDE, LANES), F32),
            pltpu.VMEM((S5_TILES, LANES), F32),
            pltpu.VMEM((S5_TILES, LANES), F32),
        ],
        compiler_params=pltpu.CompilerParams(
            dimension_semantics=("arbitrary", "arbitrary"), vmem_limit_bytes=VMEM_LIMIT),
        name="s5_mixer",
    )(x, mod, g_pre, wu, wg, bbr, bbi, cre, cmi, abr, abi, d_skip, glu_w, glu_b, s0r, s0i)


def _lru_kernel(x_ref, mod_ref, g_ref, wx_ref, wg_ref, cw_ref, cb_ref, wa4_ref, ba_ref, wx4_ref, bx_ref, lam_ref,
                conv0_ref, h0_ref,
                yz_ref, conv_out_ref, h_out_ref,
                xbuf_ref, a_ref, b_ref, hc_ref, *, tile):
    t = pl.program_id(1)
    pad = SUBLANES
    hist = CONV_W - 1

    @pl.when(t == 0)
    def _():
        xbuf_ref[pad - hist:pad, :] = conv0_ref[...]
        hc_ref[...] = h0_ref[...]

    h = _norm_mod(x_ref[...], g_ref[...], mod_ref[...]).astype(BF16)
    xl = _dot(h, wx_ref[...])
    zg = _silu(_dot(h, wg_ref[...]))
    xbuf_ref[pad:pad + tile, :] = xl
    xc = cb_ref[...] + xl * cw_ref[CONV_W - 1:CONV_W, :]
    for j in range(hist):
        xc = xc + xbuf_ref[pad - hist + j:pad - hist + j + tile, :] * cw_ref[j:j + 1, :]
    tail = xbuf_ref[pad + tile - hist:pad + tile, :]
    xbuf_ref[pad - hist:pad, :] = tail

    ga, gx = [], []
    blk = D_MODEL // 4
    for s in range(4):
        xb = xc[:, blk * s:blk * (s + 1)].astype(BF16)
        ga.append(_dot(xb, wa4_ref[s]))
        gx.append(_dot(xb, wx4_ref[s]))
    gate_r = _sigmoid(jnp.concatenate(ga, axis=1) + ba_ref[...])
    gate_i = _sigmoid(jnp.concatenate(gx, axis=1) + bx_ref[...])
    log_a = -LRU_C * gate_r * _softplus(-lam_ref[...])
    a = jnp.exp(log_a)
    a_ref[...] = a
    b_ref[...] = jnp.sqrt(-jnp.tanh(log_a) * (a * a + 1.0)) * (gate_i * xc)

    row8 = lax.broadcasted_iota(jnp.int32, (SUBLANES, 1), 0)

    def row_body(g, carry):
        rs = pl.ds(pl.multiple_of(g * SUBLANES, SUBLANES), SUBLANES)
        av = a_ref[rs, :]
        bv = b_ref[rs, :]
        for sh in (1, 2, 4):
            keep = row8 >= sh
            ash = jnp.where(keep, pltpu.roll(av, sh, 0), 1.0)
            bsh = jnp.where(keep, pltpu.roll(bv, sh, 0), 0.0)
            bv = bv + av * bsh
            av = av * ash
        hs = bv + av * carry
        b_ref[rs, :] = hs
        return hs[SUBLANES - 1:SUBLANES, :]

    hc_ref[...] = lax.fori_loop(0, tile // SUBLANES, row_body, hc_ref[...])
    yz_ref[...] = b_ref[...] * zg

    @pl.when(t == pl.num_programs(1) - 1)
    def _():
        conv_out_ref[...] = tail
        h_out_ref[...] = hc_ref[...]


def _lru_mixer(x, mod, g_pre, wx, wg, conv_w, conv_b, wa4, ba, wx4, bx, lam, conv0, h0, *, tile):
    bsz, tp, _ = x.shape
    assert tp % tile == 0 and tile % SUBLANES == 0 and tile >= SUBLANES
    nt = tp // tile
    row = lambda n: pl.BlockSpec((1, n), lambda b, t: (0, 0))
    full = lambda a: pl.BlockSpec(a.shape, lambda b, t: (0,) * a.ndim)
    kern = functools.partial(_lru_kernel, tile=tile)
    return pl.pallas_call(
        kern,
        grid=(bsz, nt),
        in_specs=[
            pl.BlockSpec((None, tile, D_MODEL), lambda b, t: (b, t, 0)),
            pl.BlockSpec((None, 3, D_MODEL), lambda b, t: (b, 0, 0)),
            row(D_MODEL), full(wx), full(wg), full(conv_w), row(D_MODEL), full(wa4), row(D_MODEL), full(wx4),
            row(D_MODEL), row(D_MODEL),
            pl.BlockSpec((None, CONV_W - 1, D_MODEL), lambda b, t: (b, 0, 0)),
            pl.BlockSpec((None, 1, D_MODEL), lambda b, t: (b, 0, 0)),
        ],
        out_specs=[
            pl.BlockSpec((None, tile, D_MODEL), lambda b, t: (b, t, 0)),
            pl.BlockSpec((None, CONV_W - 1, D_MODEL), lambda b, t: (b, 0, 0)),
            pl.BlockSpec((None, 1, D_MODEL), lambda b, t: (b, 0, 0)),
        ],
        out_shape=[
            jax.ShapeDtypeStruct((bsz, tp, D_MODEL), F32),
            jax.ShapeDtypeStruct((bsz, CONV_W - 1, D_MODEL), F32),
            jax.ShapeDtypeStruct((bsz, 1, D_MODEL), F32),
        ],
        scratch_shapes=[
            pltpu.VMEM((tile + SUBLANES, D_MODEL), F32),
            pltpu.VMEM((tile, D_MODEL), F32),
            pltpu.VMEM((tile, D_MODEL), F32),
            pltpu.VMEM((1, D_MODEL), F32),
        ],
        compiler_params=pltpu.CompilerParams(
            dimension_semantics=("arbitrary", "arbitrary"), vmem_limit_bytes=VMEM_LIMIT),
        name="rglru_mixer",
    )(x, mod, g_pre, wx, wg, conv_w, conv_b, wa4, ba, wx4, bx, lam, conv0, h0)


def _merge_kernel(x_ref, mod_ref, g_ref, gpost_ref, wm_ref, wo_ref, yr_ref, ys_ref, yl_ref, o_ref):
    x = x_ref[...]
    mod = mod_ref[...]
    h = _norm_mod(x, g_ref[...], mod).astype(BF16)
    m = _sigmoid(_dot(h, wm_ref[...]))
    merged = (m[:, :D_MODEL] * yr_ref[...] + m[:, D_MODEL:2 * D_MODEL] * ys_ref[...]
              + m[:, 2 * D_MODEL:] * yl_ref[...])
    o = _dot(merged.astype(BF16), wo_ref[...])
    ms = jnp.mean(o * o, axis=-1, keepdims=True)
    o = (o * lax.rsqrt(ms + RMS_EPS)) * gpost_ref[...]
    o_ref[...] = x + mod[2:3, :] * o


def _merge_out(x, mod, g_pre, g_post, wm, wo, yr, ys, yl, *, tile):
    bsz, tp, _ = x.shape
    assert tp % tile == 0
    nt = tp // tile
    act = pl.BlockSpec((None, tile, D_MODEL), lambda b, t: (b, t, 0))
    row = lambda n: pl.BlockSpec((1, n), lambda b, t: (0, 0))
    full = lambda a: pl.BlockSpec(a.shape, lambda b, t: (0,) * a.ndim)
    return pl.pallas_call(
        _merge_kernel,
        grid=(bsz, nt),
        in_specs=[act, pl.BlockSpec((None, 3, D_MODEL), lambda b, t: (b, 0, 0)), row(D_MODEL), row(D_MODEL),
                  full(wm), full(wo), act, act, act],
        out_specs=act,
        out_shape=jax.ShapeDtypeStruct((bsz, tp, D_MODEL), F32),
        compiler_params=pltpu.CompilerParams(
            dimension_semantics=("arbitrary", "arbitrary"), vmem_limit_bytes=VMEM_LIMIT),
        name="merge_out",
    )(x, mod, g_pre, g_post, wm, wo, yr, ys, yl)


def _block_diag_expand(w, per):
    n, a, b = w.shape
    w = w.reshape(n // per, per, a, b)
    eye = jnp.eye(per, dtype=w.dtype)
    return jnp.einsum("sgab,gh->sgahb", w, eye).reshape(n // per, per * a, per * b)


def _layer(x, mod, state, prm, *, t_valid, tiles):
    shift_row, wkv, s_re, s_im, lru_h, lru_conv = state
    bsz = x.shape[0]
    g_pre = prm["g_pre"]

    xr = x
    if x.shape[1] % tiles["rwkv"] != 0:
        xr = jnp.pad(x, ((0, 0), (0, tiles["rwkv"] - x.shape[1] % tiles["rwkv"]), (0, 0)))
    yz_r, shift_new, wkv_new = _rwkv_mixer(
        xr, mod, g_pre, prm["w_rwkv"], prm["w_rwkv_g"], prm["rwkv_vecs"], prm["w2a"],
        shift_row.reshape(bsz, 1, SHIFT_W), wkv.reshape(bsz, PAIRS, LANES, HEAD), tile=tiles["rwkv"],
        t_valid=t_valid)
    yz_r = yz_r[:, :t_valid]
    yz_s, s_re_new, s_im_new = _s5_mixer(
        x, mod, g_pre, prm["w_s5"], prm["w_s5_g"], prm["bbr"], prm["bbi"], prm["cre"], prm["cmi"],
        prm["abr"], prm["abi"], prm["s5_d"], prm["glu_w"], prm["glu_b"],
        s_re.reshape(bsz, S5_TILES, LANES), s_im.reshape(bsz, S5_TILES, LANES), tile=tiles["s5"])
    yz_l, conv_new, h_new = _lru_mixer(
        x, mod, g_pre, prm["w_lru"], prm["w_lru_g"], prm["conv_w"], prm["conv_b"], prm["wa4"], prm["ba"],
        prm["wx4"], prm["bx"], prm["lam"], lru_conv, lru_h.reshape(bsz, 1, D_MODEL), tile=tiles["lru"])
    x_new = _merge_out(x, mod, g_pre, prm["g_post"], prm["w_merge"], prm["w_out"], yz_r, yz_s, yz_l,
                       tile=tiles["merge"])
    new_state = (shift_new.reshape(bsz, SHIFT_W), wkv_new.reshape(bsz, HEADS, HEAD, HEAD),
                 s_re_new.reshape(bsz, 64, 64), s_im_new.reshape(bsz, 64, 64),
                 h_new.reshape(bsz, D_MODEL), conv_new)
    return x_new, new_state


def _layer_params(l, w_in, w_out, norm_pre, norm_post, rwkv, s5, lru):
    d = D_MODEL
    wb = w_in[l].astype(BF16)
    o = SHIFT_W
    row = lambda a: a.reshape(1, -1)
    mu, w0, w2, a0, a2, k_k, k_a, r_k, ln_g, ln_b = [a[l] for a in rwkv]
    a_re, a_im, log_step, b_re, b_im, c_re, c_im, s5_d, glu_w, glu_b = [a[l] for a in s5]
    conv_w, conv_b, wa, ba, wx, bx, lam = [a[l] for a in lru]
    zeros = jnp.zeros((LORA, d), F32)
    w2a = jnp.concatenate([jnp.concatenate([w2, zeros], axis=1), jnp.concatenate([zeros, a2], axis=1)], axis=0)
    abr, abi, bbr, bbi = _s5_prep(a_re, a_im, log_step, b_re, b_im)
    per = 64 // S5_SETS
    return dict(
        g_pre=row(norm_pre[l]), g_post=row(norm_post[l]),
        w_rwkv=wb[:, :o], w_rwkv_g=wb[:, o:o + d],
        w_s5=wb[:, o + d:o + 2 * d], w_s5_g=wb[:, o + 2 * d:o + 3 * d],
        w_lru=wb[:, o + 3 * d:o + 4 * d], w_lru_g=wb[:, o + 4 * d:o + 5 * d],
        w_merge=wb[:, o + 5 * d:], w_out=w_out[l].astype(BF16),
        rwkv_vecs=(row(mu), row(w0), row(a0), row(k_k), row(k_a), row(r_k), row(ln_g), row(ln_b)),
        w2a=w2a.astype(BF16),
        bbr=_block_diag_expand(bbr, S5_IN_GROUPS).astype(BF16),
        bbi=_block_diag_expand(bbi, S5_IN_GROUPS).astype(BF16),
        cre=_block_diag_expand(jnp.swapaxes(c_re, 1, 2), per).astype(BF16),
        cmi=_block_diag_expand(-jnp.swapaxes(c_im, 1, 2), per).astype(BF16),
        abr=abr.reshape(S5_TILES, LANES), abi=abi.reshape(S5_TILES, LANES),
        s5_d=row(s5_d), glu_w=glu_w.astype(BF16), glu_b=row(glu_b),
        conv_w=conv_w, conv_b=row(conv_b),
        wa4=_block_diag_expand(wa, 4).astype(BF16), ba=row(ba),
        wx4=_block_diag_expand(wx, 4).astype(BF16), bx=row(bx), lam=row(lam),
    )


def _tiles(t):
    pick = lambda want: want if t % want == 0 else t
    return dict(rwkv=256 if t % 256 == 0 else RWKV_CHUNK, s5=pick(256), lru=pick(512), merge=pick(512))


def kernel(x_prompt, x_sample, state_rwkv_shift, state_rwkv_wkv, state_s5_re, state_s5_im, state_lru_h,
           state_lru_conv, c_prompt, c_sample, ada_w, ada_b, norm_pre, norm_post, w_in, w_out, rwkv_mu, rwkv_w0,
           rwkv_w2, rwkv_a0, rwkv_a2, rwkv_k_k, rwkv_k_a, rwkv_r_k, rwkv_ln_g, rwkv_ln_b, s5_a_re, s5_a_im,
           s5_log_step, s5_b_re, s5_b_im, s5_c_re, s5_c_im, s5_d, s5_glu_w, s5_glu_b, lru_conv_w, lru_conv_b,
           lru_wa, lru_ba, lru_wx, lru_bx, lru_lambda):
    depth = w_in.shape[0]
    bp, tp, _ = x_prompt.shape
    bs, ts, _ = x_sample.shape
    c_all = jnp.concatenate([c_prompt, c_sample], axis=0)
    pad = (-c_all.shape[0]) % SUBLANES
    c_all = jnp.pad(c_all, ((0, pad), (0, 0)))
    mod_all = _modulation(c_all, ada_w, ada_b)
    rwkv = (rwkv_mu, rwkv_w0, rwkv_w2, rwkv_a0, rwkv_a2, rwkv_k_k, rwkv_k_a,
            rwkv_r_k.reshape(depth, D_MODEL), rwkv_ln_g, rwkv_ln_b)
    s5 = (s5_a_re, s5_a_im, s5_log_step, s5_b_re, s5_b_im, s5_c_re, s5_c_im, s5_d, s5_glu_w, s5_glu_b)
    lru = (lru_conv_w, lru_conv_b, lru_wa, lru_ba, lru_wx, lru_bx, lru_lambda)

    xp, xs = x_prompt, x_sample
    new_p = [[] for _ in range(6)]
    new_s = [[] for _ in range(6)]
    for l in range(depth):
        prm = _layer_params(l, w_in, w_out, norm_pre, norm_post, rwkv, s5, lru)
        mod_p = mod_all[l, :bp].reshape(bp, 3, D_MODEL)
        mod_s = mod_all[l, bp:bp + bs].reshape(bs, 3, D_MODEL)
        zero = (jnp.zeros((bp, SHIFT_W), F32), jnp.zeros((bp, HEADS, HEAD, HEAD), F32),
                jnp.zeros((bp, 64, 64), F32), jnp.zeros((bp, 64, 64), F32),
                jnp.zeros((bp, D_MODEL), F32), jnp.zeros((bp, CONV_W - 1, D_MODEL), F32))
        xp, st_p = _layer(xp, mod_p, zero, prm, t_valid=tp, tiles=_tiles(tp))
        st_in = (state_rwkv_shift[l], state_rwkv_wkv[l], state_s5_re[l], state_s5_im[l],
                 state_lru_h[l], state_lru_conv[l])
        xs, st_s = _layer(xs, mod_s, st_in, prm, t_valid=ts, tiles=_tiles(ts))
        for i in range(6):
            new_p[i].append(st_p[i])
            new_s[i].append(st_s[i])
    sp = [jnp.stack(z, axis=0) for z in new_p]
    ss = [jnp.stack(z, axis=0) for z in new_s]
    return (xp, xs, sp[0], sp[1], sp[2], sp[3], sp[4], sp[5], ss[0], ss[1], ss[2], ss[3], ss[4], ss[5])
```

```python
import functools

import jax
import jax.numpy as jnp
from jax import lax
from jax.experimental import pallas as pl
from jax.experimental.pallas import tpu as pltpu

F32 = jnp.float32
BF16 = jnp.bfloat16
HIGHEST = lax.Precision.HIGHEST

D_MODEL = 1024
HEAD = 64
HEADS = D_MODEL // HEAD
PAIRS = HEADS // 2
LORA = 64
SHIFT_W = 3 * D_MODEL + 2 * LORA
S5_STATE_W = 64 * 64
S5_SETS = 4
S5_IN_GROUPS = 8
S5_TILES = S5_STATE_W // 128
S5_ROW_STRIDE = 40
CONV_W = 4
LRU_C = 8.0
RMS_EPS = 1e-6
GN_EPS = 64e-5
RWKV_CHUNK = 64
DECAY_SCALE = 0.6065306597126334
RWKV_INSTANCES = 16
SUBLANES = 8
LANES = 128
VMEM_LIMIT = 56 * 1024 * 1024


def _sigmoid(x):
    return 0.5 * jnp.tanh(0.5 * x) + 0.5


def _silu(x):
    return x * _sigmoid(x)


def _softplus(x):
    return jnp.maximum(x, 0.0) + jnp.log1p(jnp.exp(-jnp.abs(x)))


def _norm_mod(x, g, mod):
    ms = jnp.mean(x * x, axis=-1, keepdims=True)
    return (x * lax.rsqrt(ms + RMS_EPS)) * (g * (1.0 + mod[1:2, :])) + mod[0:1, :]


def _dot(a, b):
    return jnp.dot(a, b, preferred_element_type=F32)


def _dot_hi(a, b):
    return jnp.dot(a, b, preferred_element_type=F32, precision=HIGHEST)


def _split2(x):
    hi = x.astype(BF16)
    lo = (x - hi.astype(F32)).astype(BF16)
    return hi, lo


_NN = (((1,), (0,)), ((), ()))
_NT = (((1,), (1,)), ((), ()))
_TN = (((0,), (0,)), ((), ()))


def _mm(a, b, dims=_NN):
    return lax.dot_general(a.astype(BF16), b.astype(BF16), dims, preferred_element_type=F32)


def _mm3(a_parts, b_parts):
    (a_hi, a_lo), (b_hi, b_lo) = a_parts, b_parts
    return _dot(a_hi, b_hi) + (_dot(a_hi, b_lo) + _dot(a_lo, b_hi))


def _head_sum(x, ones_blk):
    outs = []
    for s in range(D_MODEL // 256):
        outs.append(_dot(x[:, 256 * s:256 * (s + 1)].astype(BF16), ones_blk))
    return jnp.concatenate(outs, axis=1)


def _block_ones(n, blk):
    ri = lax.broadcasted_iota(jnp.int32, (n, n), 0)
    ci = lax.broadcasted_iota(jnp.int32, (n, n), 1)
    sh = blk.bit_length() - 1
    return ((ri >> sh) == (ci >> sh)).astype(F32).astype(BF16)


def _mod_kernel(c_ref, w_ref, b_ref, o_ref):
    s = _silu(c_ref[...])
    o_ref[...] = _dot_hi(s, w_ref[...]) + b_ref[...]


def _modulation(c_all, ada_w, ada_b):
    depth = ada_w.shape[0]
    rows = c_all.shape[0]
    return pl.pallas_call(
        _mod_kernel,
        grid=(depth, 3),
        in_specs=[
            pl.BlockSpec((rows, D_MODEL), lambda l, j: (0, 0)),
            pl.BlockSpec((None, D_MODEL, D_MODEL), lambda l, j: (l, 0, j)),
            pl.BlockSpec((None, 1, D_MODEL), lambda l, j: (l, 0, j)),
        ],
        out_specs=pl.BlockSpec((None, rows, D_MODEL), lambda l, j: (l, 0, j)),
        out_shape=jax.ShapeDtypeStruct((depth, rows, 3 * D_MODEL), F32),
        name="adaln_mod",
    )(c_all, ada_w, ada_b.reshape(depth, 1, 3 * D_MODEL))


def _s5_prep_kernel(are_ref, aim_ref, ls_ref, bre_ref, bim_ref, abr_ref, abi_ref, bbr_ref, bbi_ref):
    are = are_ref[...]
    aim = aim_ref[...]
    dt = jnp.exp(ls_ref[...])
    mag = jnp.exp(are * dt)
    abr = mag * jnp.cos(aim * dt)
    abi = mag * jnp.sin(aim * dt)
    abr_ref[...] = abr
    abi_ref[...] = abi
    nr = abr - 1.0
    ni = abi
    den = are * are + aim * aim
    cr = (nr * are + ni * aim) / den
    ci = (ni * are - nr * aim) / den
    bre = bre_ref[...]
    bim = bim_ref[...]
    bbr_ref[...] = cr[:, None, :] * bre - ci[:, None, :] * bim
    bbi_ref[...] = cr[:, None, :] * bim + ci[:, None, :] * bre


def _s5_prep(a_re, a_im, log_step, b_re, b_im):
    depth, g, p = a_re.shape
    i = b_re.shape[-1]
    layer = lambda *dims: pl.BlockSpec((None,) + dims, lambda l: (l,) + (0,) * len(dims))
    return pl.pallas_call(
        _s5_prep_kernel,
        grid=(depth,),
        in_specs=[layer(g, p), layer(g, p), layer(g, 1), layer(g, i, p), layer(g, i, p)],
        out_specs=(layer(g, p), layer(g, p), layer(g, i, p), layer(g, i, p)),
        out_shape=(
            jax.ShapeDtypeStruct((depth, g, p), F32),
            jax.ShapeDtypeStruct((depth, g, p), F32),
            jax.ShapeDtypeStruct((depth, g, i, p), F32),
            jax.ShapeDtypeStruct((depth, g, i, p), F32),
        ),
        name="s5_discretise",
    )(a_re, a_im, log_step.reshape(depth, g, 1), jnp.swapaxes(b_re, 2, 3), jnp.swapaxes(b_im, 2, 3))


def _rwkv_kernel(x_ref, mod_ref, g_ref, wr_ref, wg_ref, mu_ref, w0_ref, w2a_ref, a0_ref, kk_ref, ka_ref,
                 rk_ref, lng_ref, lnb_ref, shift0_ref, wkv0_ref,
                 yz_ref, shift_out_ref, wkv_out_ref,
                 carry_ref, st_ref, at_ref, rt_ref, bt_ref, kt_ref, bh_ref, kh_ref, v_ref, pl_ref, y_ref,
                 nh_ref, t_ref, aak_ref, ark_ref, rhs_ref, gm_ref, qe_ref, hc_ref, yc_ref,
                 *, tile, chunk, t_valid, group):
    t = pl.program_id(1)
    n_chunks = tile // chunk
    two = 2 * chunk
    n_inst = group * n_chunks

    @pl.when(t == 0)
    def _():
        carry_ref[...] = shift0_ref[...]
        rh = lax.broadcasted_iota(jnp.int32, (LANES, LANES), 0) >> (HEAD.bit_length() - 1)
        ch = lax.broadcasted_iota(jnp.int32, (LANES, LANES), 1) >> (HEAD.bit_length() - 1)
        for q in range(PAIRS):
            s0 = wkv0_ref[q]
            st_ref[q] = jnp.where(rh == ch, jnp.concatenate([s0, s0], axis=1), 0.0)

    h = _norm_mod(x_ref[...], g_ref[...], mod_ref[...]).astype(BF16)
    p = _dot(h, wr_ref[...])
    zg = _silu(_dot(h, wg_ref[...]))

    rows = lax.broadcasted_iota(jnp.int32, (tile, 1), 0)
    prev = jnp.where(rows == 0, carry_ref[...], pltpu.roll(p, 1, 0))
    last = (t_valid - 1) % tile
    carry_ref[...] = p[last:last + 1, :]
    pm = p + (prev - p) * mu_ref[...]

    r = pm[:, 0:D_MODEL]
    k = pm[:, D_MODEL:2 * D_MODEL]
    v = pm[:, 2 * D_MODEL:3 * D_MODEL]
    wa = pm[:, 3 * D_MODEL:SHIFT_W]
    lane = lax.broadcasted_iota(jnp.int32, (1, LANES), 1)
    wa = jnp.where(lane < LORA, jnp.tanh(wa), wa)
    lora = _dot(wa.astype(BF16), w2a_ref[...])
    logd = -DECAY_SCALE * _sigmoid(w0_ref[...] + lora[:, :D_MODEL])
    a_sig = _sigmoid(a0_ref[...] + lora[:, D_MODEL:])

    ones_blk = _block_ones(256, HEAD)
    kk = k * kk_ref[...]
    kk = kk * jnp.minimum(lax.rsqrt(_head_sum(kk * kk, ones_blk)), 1e12)
    k2 = k * (1.0 + (a_sig - 1.0) * ka_ref[...])
    bvec = kk * a_sig
    if t_valid % tile != 0:
        ok = rows < t_valid
        logd = jnp.where(ok, logd, 0.0)
        bvec = jnp.where(ok, bvec, 0.0)
        k2 = jnp.where(ok, k2, 0.0)

    rmod = rows & (chunk - 1)
    cum = logd
    s = 1
    while s < chunk:
        cum = cum + jnp.where(rmod >= s, pltpu.roll(cum, s, 0), 0.0)
        s *= 2
    cum_last = jnp.concatenate(
        [jnp.broadcast_to(cum[(j + 1) * chunk - 1:(j + 1) * chunk, :], (chunk, D_MODEL)) for j in range(n_chunks)],
        axis=0)
    p_inc = jnp.exp(cum)
    p_inv = jnp.exp(-cum)
    p_exc = jnp.exp(cum - logd)
    p_end = jnp.exp(cum_last - cum)

    def to_pairs(ref, val):
        for q in range(PAIRS):
            ref[q] = val[:, LANES * q:LANES * (q + 1)]

    to_pairs(at_ref, -kk * p_exc)
    to_pairs(rt_ref, r * p_inc)
    to_pairs(bt_ref, bvec * p_inv)
    to_pairs(kt_ref, k2 * p_inv)
    to_pairs(bh_ref, bvec * p_end)
    to_pairs(kh_ref, k2 * p_end)
    to_pairs(v_ref, v)
    for j in range(n_chunks):
        row = jnp.exp(cum[(j + 1) * chunk - 1:(j + 1) * chunk, :])
        for q in range(PAIRS):
            pl_ref[q, j] = jnp.broadcast_to(row[:, LANES * q:LANES * (q + 1)], (SUBLANES, LANES))

    m_a = (lane < HEAD).astype(F32)
    m_b = 1.0 - m_a
    ri = lax.broadcasted_iota(jnp.int32, (two, two), 0)
    ci = lax.broadcasted_iota(jnp.int32, (two, two), 1)
    csh = chunk.bit_length() - 1
    same_head = (ri >> csh) == (ci >> csh)
    strict = jnp.logical_and(same_head, (ri & (chunk - 1)) > (ci & (chunk - 1)))
    incl = jnp.logical_and(same_head, (ri & (chunk - 1)) >= (ci & (chunk - 1)))
    eye = ri == ci
    head_a = ri < chunk
    off_masks = []
    for lg in range(csh):
        off_masks.append(jnp.logical_and(strict, jnp.logical_and((ri >> (lg + 1)) == (ci >> (lg + 1)),
                                                                 (ri >> lg) != (ci >> lg))))

    def stack(x):
        return jnp.concatenate([x * m_a, x * m_b], axis=0)

    gsh = group.bit_length() - 1
    unroll = min(16, n_inst)

    def group_body(g, carry):
        def where_is(i):
            q = g * group + (i & (group - 1))
            j = i >> gsh
            return q, j, pl.ds(pl.multiple_of(j * chunk, chunk), chunk)

        def gram(i, c):
            q, _, sl = where_is(i)
            atm = stack(at_ref[q, sl, :])
            rtm = stack(rt_ref[q, sl, :])
            bt = bt_ref[q, sl, :]
            kt = kt_ref[q, sl, :]
            o1 = _mm(jnp.concatenate([atm, rtm], axis=0), jnp.concatenate([bt, kt], axis=0), _NT)
            o1_sw = pltpu.roll(o1, chunk, 1)
            n_ab = jnp.where(strict, jnp.where(head_a, o1[:two], o1_sw[:two]), 0.0)
            nh_ref[i] = n_ab.astype(BF16)
            t_ref[i] = jnp.where(eye, 1.0, jnp.where(off_masks[0], n_ab, 0.0)).astype(BF16)
            aak_ref[i] = jnp.where(strict, jnp.where(head_a, o1_sw[:two], o1[:two]), 0.0).astype(BF16)
            ark_ref[i] = jnp.concatenate([jnp.where(incl, jnp.where(head_a, o1[two:], o1_sw[two:]), 0.0),
                                          jnp.where(incl, jnp.where(head_a, o1_sw[two:], o1[two:]), 0.0)],
                                         axis=1).astype(BF16)
            return c

        lax.fori_loop(0, n_inst, gram, 0, unroll=unroll)

        for off in off_masks[1:]:
            def level(i, c, off=off):
                t_inv = t_ref[i]
                x = _dot(jnp.where(off, nh_ref[i], 0.0), t_inv).astype(BF16)
                t_ref[i] = t_inv + _dot(t_inv, x).astype(BF16)
                return c

            lax.fori_loop(0, n_inst, level, 0, unroll=unroll)

        def right_side(i, c):
            q, _, sl = where_is(i)
            atm = stack(at_ref[q, sl, :])
            vm = stack(v_ref[q, sl, :])
            z0_hi, z0_lo = _split2(jnp.concatenate([atm, _mm(aak_ref[i], vm)], axis=1))
            rhs_ref[i] = jnp.concatenate([z0_hi, z0_lo], axis=0)
            return c

        lax.fori_loop(0, n_inst, right_side, 0, unroll=unroll)

        def apply(i, c):
            q, _, sl = where_is(i)
            vm = stack(v_ref[q, sl, :])
            t_inv = t_ref[i]
            z = _dot(t_inv, rhs_ref[i, :two, :]) + _dot(t_inv, rhs_ref[i, two:, :])
            rhs_ref[i] = jnp.concatenate([z, jnp.concatenate([jnp.zeros_like(vm), vm], axis=1)], axis=0).astype(BF16)
            return c

        lax.fori_loop(0, n_inst, apply, 0, unroll=unroll)

        def operators(i, c):
            q, j, sl = where_is(i)
            rhs2 = rhs_ref[i]
            k_all = g * n_inst + i
            top = _mm(rhs2, jnp.concatenate([stack(bh_ref[q, sl, :]), stack(kh_ref[q, sl, :])], axis=0), _TN)
            bot = _dot(ark_ref[i], rhs2)
            gm_ref[k_all] = (top[:two, :]
                             + jnp.where(eye, jnp.broadcast_to(pl_ref[q, j][0:1, :], (two, two)), 0.0)).astype(BF16)
            hc_ref[k_all] = top[two:, :]
            qe_ref[k_all] = (bot[:, :two] + stack(rt_ref[q, sl, :])).astype(BF16)
            yc_ref[k_all] = bot[:, two:]
            return c

        lax.fori_loop(0, n_inst, operators, 0, unroll=unroll)
        return carry

    lax.fori_loop(0, PAIRS // group, group_body, 0)

    for j in range(n_chunks):
        def advance(q, c, j=j):
            k_all = (q >> gsh) * n_inst + j * group + (q & (group - 1))
            st = st_ref[q].astype(BF16)
            ym = _mm(qe_ref[k_all], st, _NT) + yc_ref[k_all]
            y_ref[q, j * chunk:(j + 1) * chunk, :] = ym[:chunk, :] + ym[chunk:, :]
            st_ref[q] = _dot(st, gm_ref[k_all]) + hc_ref[k_all]
            return c

        lax.fori_loop(0, PAIRS, advance, 0, unroll=True)

    ys = jnp.concatenate([y_ref[q] for q in range(PAIRS)], axis=1)
    mean = _head_sum(ys, ones_blk) * (1.0 / HEAD)
    yc = ys - mean
    var = _head_sum(yc * yc, ones_blk) * (1.0 / HEAD)
    yn = yc * lax.rsqrt(var + GN_EPS) * lng_ref[...] + lnb_ref[...]
    bonus = _head_sum(r * k2 * rk_ref[...], ones_blk) * v
    yz_ref[...] = (yn + bonus) * zg

    @pl.when(t == pl.num_programs(1) - 1)
    def _():
        shift_out_ref[...] = carry_ref[...]
        for q in range(PAIRS):
            st = st_ref[q]
            wkv_out_ref[q] = st[:, :HEAD] + st[:, HEAD:]


def _specs(layer, state_layer, mod_row):
    def weight(a):
        return pl.BlockSpec((None,) + a.shape[1:], lambda b, t: (layer,) + (0,) * (a.ndim - 1))

    def state(a):
        return pl.BlockSpec((None, None) + a.shape[2:], lambda b, t: (state_layer, b) + (0,) * (a.ndim - 2))

    def mod(a):
        return pl.BlockSpec((None, None) + a.shape[2:], lambda b, t: (layer, mod_row + b, 0, 0))

    return weight, state, mod


def _rwkv_mixer(x, mod, g_pre, wr, wg, vecs, w2a, shift0, wkv0, *, sel, tile, t_valid):
    bsz, tp, _ = x.shape
    chunk = RWKV_CHUNK
    assert tp % tile == 0 and tile % chunk == 0
    assert t_valid == tp or tp == tile
    nt = tp // tile
    mu, w0, a0, k_k, k_a, r_k, ln_g, ln_b = vecs
    weight, state, mod_spec = _specs(*sel)
    pair_buf = pltpu.VMEM((PAIRS, tile, LANES), F32)
    group = min(PAIRS, max(1, RWKV_INSTANCES // (tile // chunk)))
    n_inst = group * (tile // chunk)
    two = 2 * chunk
    inst = lambda cols, dtype: pltpu.VMEM((n_inst, two, cols), dtype)
    every = lambda dtype: pltpu.VMEM((PAIRS * (tile // chunk), two, two), dtype)
    kern = functools.partial(_rwkv_kernel, tile=tile, chunk=chunk, t_valid=t_valid, group=group)
    return pl.pallas_call(
        kern,
        grid=(bsz, nt),
        in_specs=[
            pl.BlockSpec((None, tile, D_MODEL), lambda b, t: (b, t, 0)),
            mod_spec(mod),
            weight(g_pre), weight(wr), weight(wg), weight(mu), weight(w0), weight(w2a), weight(a0),
            weight(k_k), weight(k_a), weight(r_k), weight(ln_g), weight(ln_b),
            state(shift0), state(wkv0),
        ],
        out_specs=[
            pl.BlockSpec((None, tile, D_MODEL), lambda b, t: (b, t, 0)),
            pl.BlockSpec((None, 1, SHIFT_W), lambda b, t: (b, 0, 0)),
            pl.BlockSpec((None, PAIRS, LANES, HEAD), lambda b, t: (b, 0, 0, 0)),
        ],
        out_shape=[
            jax.ShapeDtypeStruct((bsz, tp, D_MODEL), F32),
            jax.ShapeDtypeStruct((bsz, 1, SHIFT_W), F32),
            jax.ShapeDtypeStruct((bsz, PAIRS, LANES, HEAD), F32),
        ],
        scratch_shapes=[
            pltpu.VMEM((1, SHIFT_W), F32),
            pltpu.VMEM((PAIRS, LANES, LANES), F32),
            pair_buf, pair_buf, pair_buf, pair_buf, pair_buf, pair_buf, pair_buf,
            pltpu.VMEM((PAIRS, tile // chunk, SUBLANES, LANES), F32),
            pair_buf,
            inst(two, BF16), inst(two, BF16), inst(two, BF16), inst(2 * two, BF16),
            pltpu.VMEM((n_inst, 2 * two, 2 * two), BF16),
            every(BF16), every(BF16), every(F32), every(F32),
        ],
        compiler_params=pltpu.CompilerParams(
            dimension_semantics=("arbitrary", "arbitrary"), vmem_limit_bytes=VMEM_LIMIT),
        name="rwkv7_mixer",
    )(x, mod, g_pre, wr, wg, mu, w0, w2a, a0, k_k, k_a, r_k, ln_g, ln_b, shift0, wkv0)


def _s5_kernel(x_ref, mod_ref, g_ref, wu_ref, wg_ref, bbr_ref, bbi_ref, cre_ref, cmi_ref, abr_ref, abi_ref,
               d_ref, gluw_ref, glub_ref, s0r_ref, s0i_ref,
               yz_ref, sr_out_ref, si_out_ref,
               hr_ref, hi_ref, cr_ref, ci_ref, *, tile):
    t = pl.program_id(1)

    @pl.when(t == 0)
    def _():
        cr_ref[...] = s0r_ref[...]
        ci_ref[...] = s0i_ref[...]

    h = _norm_mod(x_ref[...], g_ref[...], mod_ref[...]).astype(BF16)
    u = _dot(h, wu_ref[...])
    zg = _silu(_dot(h, wg_ref[...]))
    set_tiles = S5_TILES // S5_SETS

    def frame_rows(c):
        return pl.ds(c, tile, stride=S5_ROW_STRIDE)

    in_tiles = bbr_ref.shape[2] // LANES
    for s in range(D_MODEL // LANES):
        ub = u[:, LANES * s:LANES * (s + 1)].astype(BF16)
        bur = _dot(ub, bbr_ref[s])
        bui = _dot(ub, bbi_ref[s])
        for c in range(in_tiles):
            hr_ref[frame_rows(s * in_tiles + c), :] = bur[:, LANES * c:LANES * (c + 1)]
            hi_ref[frame_rows(s * in_tiles + c), :] = bui[:, LANES * c:LANES * (c + 1)]

    abr = abr_ref[...]
    abi = abi_ref[...]

    def frame(f, carry):
        sr, si = carry
        rows = pl.ds(pl.multiple_of(f * S5_ROW_STRIDE, SUBLANES), S5_TILES)
        nr = abr * sr - abi * si + hr_ref[rows, :]
        ni = abr * si + abi * sr + hi_ref[rows, :]
        hr_ref[rows, :] = nr
        hi_ref[rows, :] = ni
        return nr, ni

    sr, si = lax.fori_loop(0, tile, frame, (cr_ref[...], ci_ref[...]), unroll=8)
    cr_ref[...] = sr
    ci_ref[...] = si

    outs = []
    for s in range(S5_SETS):
        hr = jnp.concatenate([hr_ref[frame_rows(s * set_tiles + c), :] for c in range(set_tiles)], axis=1)
        hi = jnp.concatenate([hi_ref[frame_rows(s * set_tiles + c), :] for c in range(set_tiles)], axis=1)
        outs.append(_dot(hr.astype(BF16), cre_ref[s]) + _dot(hi.astype(BF16), cmi_ref[s]))
    y = jnp.concatenate(outs, axis=1) + d_ref[...] * u
    y = 0.5 * y * (1.0 + jnp.tanh(0.7978845608028654 * (y + 0.044715 * (y * y * y))))
    y = y * _sigmoid(_dot(y.astype(BF16), gluw_ref[...]) + glub_ref[...])
    yz_ref[...] = y * zg

    @pl.when(t == pl.num_programs(1) - 1)
    def _():
        sr_out_ref[...] = cr_ref[...]
        si_out_ref[...] = ci_ref[...]


def _s5_mixer(x, mod, g_pre, wu, wg, bbr, bbi, cre, cmi, abr, abi, d_skip, glu_w, glu_b, s0r, s0i, *, sel, tile):
    bsz, tp, _ = x.shape
    assert tp % tile == 0 and tile % SUBLANES == 0
    nt = tp // tile
    weight, state, mod_spec = _specs(*sel)
    state_out = pl.BlockSpec((None, S5_TILES, LANES), lambda b, t: (b, 0, 0))
    kern = functools.partial(_s5_kernel, tile=tile)
    return pl.pallas_call(
        kern,
        grid=(bsz, nt),
        in_specs=[
            pl.BlockSpec((None, tile, D_MODEL), lambda b, t: (b, t, 0)),
            mod_spec(mod),
            weight(g_pre), weight(wu), weight(wg), weight(bbr), weight(bbi), weight(cre), weight(cmi),
            weight(abr), weight(abi), weight(d_skip), weight(glu_w), weight(glu_b), state(s0r), state(s0i),
        ],
        out_specs=[pl.BlockSpec((None, tile, D_MODEL), lambda b, t: (b, t, 0)), state_out, state_out],
        out_shape=[
            jax.ShapeDtypeStruct((bsz, tp, D_MODEL), F32),
            jax.ShapeDtypeStruct((bsz, S5_TILES, LANES), F32),
            jax.ShapeDtypeStruct((bsz, S5_TILES, LANES), F32),
        ],
        scratch_shapes=[
            pltpu.VMEM((tile * S5_ROW_STRIDE, LANES), F32),
            pltpu.VMEM((tile * S5_ROW_STRIDE, LANES), F32),
            pltpu.VMEM((S5_TILES, LANES), F32),
            pltpu.VMEM((S5_TILES, LANES), F32),
        ],
        compiler_params=pltpu.CompilerParams(
            dimension_semantics=("arbitrary", "arbitrary"), vmem_limit_bytes=VMEM_LIMIT),
        name="s5_mixer",
    )(x, mod, g_pre, wu, wg, bbr, bbi, cre, cmi, abr, abi, d_skip, glu_w, glu_b, s0r, s0i)


def _lru_kernel(x_ref, mod_ref, g_ref, wx_ref, wg_ref, cw_ref, cb_ref, wa4_ref, ba_ref, wx4_ref, bx_ref, lam_ref,
                conv0_ref, h0_ref,
                yz_ref, conv_out_ref, h_out_ref,
                xbuf_ref, a_ref, b_ref, hc_ref, *, tile):
    t = pl.program_id(1)
    pad = SUBLANES
    hist = CONV_W - 1

    @pl.when(t == 0)
    def _():
        xbuf_ref[pad - hist:pad, :] = conv0_ref[...]
        hc_ref[...] = h0_ref[...]

    h = _norm_mod(x_ref[...], g_ref[...], mod_ref[...]).astype(BF16)
    xl = _dot(h, wx_ref[...])
    zg = _silu(_dot(h, wg_ref[...]))
    xbuf_ref[pad:pad + tile, :] = xl
    xc = cb_ref[...] + xl * cw_ref[CONV_W - 1:CONV_W, :]
    for j in range(hist):
        xc = xc + xbuf_ref[pad - hist + j:pad - hist + j + tile, :] * cw_ref[j:j + 1, :]
    tail = xbuf_ref[pad + tile - hist:pad + tile, :]
    xbuf_ref[pad - hist:pad, :] = tail

    ga, gx = [], []
    blk = D_MODEL // 4
    for s in range(4):
        xb = xc[:, blk * s:blk * (s + 1)].astype(BF16)
        ga.append(_dot(xb, wa4_ref[s]))
        gx.append(_dot(xb, wx4_ref[s]))
    gate_r = _sigmoid(jnp.concatenate(ga, axis=1) + ba_ref[...])
    gate_i = _sigmoid(jnp.concatenate(gx, axis=1) + bx_ref[...])
    log_a = -LRU_C * gate_r * _softplus(-lam_ref[...])
    a = jnp.exp(log_a)
    a_ref[...] = a
    b_ref[...] = jnp.sqrt(-jnp.tanh(log_a) * (a * a + 1.0)) * (gate_i * xc)

    row8 = lax.broadcasted_iota(jnp.int32, (SUBLANES, 1), 0)

    def row_body(g, carry):
        rs = pl.ds(pl.multiple_of(g * SUBLANES, SUBLANES), SUBLANES)
        av = a_ref[rs, :]
        bv = b_ref[rs, :]
        for sh in (1, 2, 4):
            keep = row8 >= sh
            ash = jnp.where(keep, pltpu.roll(av, sh, 0), 1.0)
            bsh = jnp.where(keep, pltpu.roll(bv, sh, 0), 0.0)
            bv = bv + av * bsh
            av = av * ash
        hs = bv + av * carry
        b_ref[rs, :] = hs
        return hs[SUBLANES - 1:SUBLANES, :]

    hc_ref[...] = lax.fori_loop(0, tile // SUBLANES, row_body, hc_ref[...])
    yz_ref[...] = b_ref[...] * zg

    @pl.when(t == pl.num_programs(1) - 1)
    def _():
        conv_out_ref[...] = tail
        h_out_ref[...] = hc_ref[...]


def _lru_mixer(x, mod, g_pre, wx, wg, conv_w, conv_b, wa4, ba, wx4, bx, lam, conv0, h0, *, sel, tile):
    bsz, tp, _ = x.shape
    assert tp % tile == 0 and tile % SUBLANES == 0 and tile >= SUBLANES
    nt = tp // tile
    weight, state, mod_spec = _specs(*sel)
    kern = functools.partial(_lru_kernel, tile=tile)
    return pl.pallas_call(
        kern,
        grid=(bsz, nt),
        in_specs=[
            pl.BlockSpec((None, tile, D_MODEL), lambda b, t: (b, t, 0)),
            mod_spec(mod),
            weight(g_pre), weight(wx), weight(wg), weight(conv_w), weight(conv_b), weight(wa4), weight(ba),
            weight(wx4), weight(bx), weight(lam), state(conv0), state(h0),
        ],
        out_specs=[
            pl.BlockSpec((None, tile, D_MODEL), lambda b, t: (b, t, 0)),
            pl.BlockSpec((None, CONV_W - 1, D_MODEL), lambda b, t: (b, 0, 0)),
            pl.BlockSpec((None, 1, D_MODEL), lambda b, t: (b, 0, 0)),
        ],
        out_shape=[
            jax.ShapeDtypeStruct((bsz, tp, D_MODEL), F32),
            jax.ShapeDtypeStruct((bsz, CONV_W - 1, D_MODEL), F32),
            jax.ShapeDtypeStruct((bsz, 1, D_MODEL), F32),
        ],
        scratch_shapes=[
            pltpu.VMEM((tile + SUBLANES, D_MODEL), F32),
            pltpu.VMEM((tile, D_MODEL), F32),
            pltpu.VMEM((tile, D_MODEL), F32),
            pltpu.VMEM((1, D_MODEL), F32),
        ],
        compiler_params=pltpu.CompilerParams(
            dimension_semantics=("arbitrary", "arbitrary"), vmem_limit_bytes=VMEM_LIMIT),
        name="rglru_mixer",
    )(x, mod, g_pre, wx, wg, conv_w, conv_b, wa4, ba, wx4, bx, lam, conv0, h0)


def _merge_kernel(x_ref, mod_ref, g_ref, gpost_ref, wm_ref, wo_ref, yr_ref, ys_ref, yl_ref, o_ref):
    x = x_ref[...]
    mod = mod_ref[...]
    h = _norm_mod(x, g_ref[...], mod).astype(BF16)
    m = _sigmoid(_dot(h, wm_ref[...]))
    merged = (m[:, :D_MODEL] * yr_ref[...] + m[:, D_MODEL:2 * D_MODEL] * ys_ref[...]
              + m[:, 2 * D_MODEL:] * yl_ref[...])
    o = _dot(merged.astype(BF16), wo_ref[...])
    ms = jnp.mean(o * o, axis=-1, keepdims=True)
    o = (o * lax.rsqrt(ms + RMS_EPS)) * gpost_ref[...]
    o_ref[...] = x + mod[2:3, :] * o


def _merge_out(x, mod, g_pre, g_post, wm, wo, yr, ys, yl, *, sel, tile):
    bsz, tp, _ = x.shape
    assert tp % tile == 0
    nt = tp // tile
    act = pl.BlockSpec((None, tile, D_MODEL), lambda b, t: (b, t, 0))
    weight, _, mod_spec = _specs(*sel)
    return pl.pallas_call(
        _merge_kernel,
        grid=(bsz, nt),
        in_specs=[act, mod_spec(mod), weight(g_pre), weight(g_post), weight(wm), weight(wo), act, act, act],
        out_specs=act,
        out_shape=jax.ShapeDtypeStruct((bsz, tp, D_MODEL), F32),
        compiler_params=pltpu.CompilerParams(
            dimension_semantics=("arbitrary", "arbitrary"), vmem_limit_bytes=VMEM_LIMIT),
        name="merge_out",
    )(x, mod, g_pre, g_post, wm, wo, yr, ys, yl)


def _block_diag_expand(w, per):
    depth, n, a, b = w.shape
    w = w.reshape(depth, n // per, per, a, b)
    eye = jnp.eye(per, dtype=w.dtype)
    return jnp.einsum("lsgab,gh->lsgahb", w, eye).reshape(depth, n // per, per * a, per * b)


def _layer(x, mod, state, prm, *, sel, t_valid, tiles):
    shift_row, wkv, s_re, s_im, lru_h, lru_conv = state
    bsz = x.shape[0]
    g_pre = prm["g_pre"]

    xr = x
    if x.shape[1] % tiles["rwkv"] != 0:
        xr = jnp.pad(x, ((0, 0), (0, tiles["rwkv"] - x.shape[1] % tiles["rwkv"]), (0, 0)))
    yz_r, shift_new, wkv_new = _rwkv_mixer(
        xr, mod, g_pre, prm["w_rwkv"], prm["w_rwkv_g"], prm["rwkv_vecs"], prm["w2a"], shift_row, wkv,
        sel=sel, tile=tiles["rwkv"], t_valid=t_valid)
    yz_r = yz_r[:, :t_valid]
    yz_s, s_re_new, s_im_new = _s5_mixer(
        x, mod, g_pre, prm["w_s5"], prm["w_s5_g"], prm["bbr"], prm["bbi"], prm["cre"], prm["cmi"],
        prm["abr"], prm["abi"], prm["s5_d"], prm["glu_w"], prm["glu_b"], s_re, s_im, sel=sel, tile=tiles["s5"])
    yz_l, conv_new, h_new = _lru_mixer(
        x, mod, g_pre, prm["w_lru"], prm["w_lru_g"], prm["conv_w"], prm["conv_b"], prm["wa4"], prm["ba"],
        prm["wx4"], prm["bx"], prm["lam"], lru_conv, lru_h, sel=sel, tile=tiles["lru"])
    x_new = _merge_out(x, mod, g_pre, prm["g_post"], prm["w_merge"], prm["w_out"], yz_r, yz_s, yz_l,
                       sel=sel, tile=tiles["merge"])
    new_state = (shift_new.reshape(bsz, SHIFT_W), wkv_new.reshape(bsz, HEADS, HEAD, HEAD),
                 s_re_new.reshape(bsz, 64, 64), s_im_new.reshape(bsz, 64, 64),
                 h_new.reshape(bsz, D_MODEL), conv_new)
    return x_new, new_state


def _stacked_params(w_in, w_out, norm_pre, norm_post, rwkv, s5, lru):
    depth = w_in.shape[0]
    d = D_MODEL
    wb = w_in.astype(BF16)
    o = SHIFT_W
    row = lambda a: a.reshape(depth, 1, -1)
    mu, w0, w2, a0, a2, k_k, k_a, r_k, ln_g, ln_b = rwkv
    a_re, a_im, log_step, b_re, b_im, c_re, c_im, s5_d, glu_w, glu_b = s5
    conv_w, conv_b, wa, ba, wx, bx, lam = lru
    zeros = jnp.zeros((depth, LORA, d), F32)
    w2a = jnp.concatenate([jnp.concatenate([w2, zeros], axis=2), jnp.concatenate([zeros, a2], axis=2)], axis=1)
    abr, abi, bbr, bbi = _s5_prep(a_re, a_im, log_step, b_re, b_im)
    per = 64 // S5_SETS
    return dict(
        g_pre=row(norm_pre), g_post=row(norm_post),
        w_rwkv=wb[:, :, :o], w_rwkv_g=wb[:, :, o:o + d],
        w_s5=wb[:, :, o + d:o + 2 * d], w_s5_g=wb[:, :, o + 2 * d:o + 3 * d],
        w_lru=wb[:, :, o + 3 * d:o + 4 * d], w_lru_g=wb[:, :, o + 4 * d:o + 5 * d],
        w_merge=wb[:, :, o + 5 * d:], w_out=w_out.astype(BF16),
        rwkv_vecs=(row(mu), row(w0), row(a0), row(k_k), row(k_a), row(r_k), row(ln_g), row(ln_b)),
        w2a=w2a.astype(BF16),
        bbr=_block_diag_expand(bbr, S5_IN_GROUPS).astype(BF16),
        bbi=_block_diag_expand(bbi, S5_IN_GROUPS).astype(BF16),
        cre=_block_diag_expand(jnp.swapaxes(c_re, 2, 3), per).astype(BF16),
        cmi=_block_diag_expand(-jnp.swapaxes(c_im, 2, 3), per).astype(BF16),
        abr=abr.reshape(depth, S5_TILES, LANES), abi=abi.reshape(depth, S5_TILES, LANES),
        s5_d=row(s5_d), glu_w=glu_w.astype(BF16), glu_b=row(glu_b),
        conv_w=conv_w, conv_b=row(conv_b),
        wa4=_block_diag_expand(wa, 4).astype(BF16), ba=row(ba),
        wx4=_block_diag_expand(wx, 4).astype(BF16), bx=row(bx), lam=row(lam),
    )


def _state_layout(shift, wkv, s_re, s_im, lru_h, lru_conv):
    n, bsz = shift.shape[:2]
    return (shift.reshape(n, bsz, 1, SHIFT_W), wkv.reshape(n, bsz, PAIRS, LANES, HEAD),
            s_re.reshape(n, bsz, S5_TILES, LANES), s_im.reshape(n, bsz, S5_TILES, LANES),
            lru_h.reshape(n, bsz, 1, D_MODEL), lru_conv)


def _tiles(t):
    pick = lambda want: want if t % want == 0 else t
    return dict(rwkv=256 if t % 256 == 0 else RWKV_CHUNK, s5=pick(256), lru=pick(512), merge=pick(512))


def kernel(x_prompt, x_sample, state_rwkv_shift, state_rwkv_wkv, state_s5_re, state_s5_im, state_lru_h,
           state_lru_conv, c_prompt, c_sample, ada_w, ada_b, norm_pre, norm_post, w_in, w_out, rwkv_mu, rwkv_w0,
           rwkv_w2, rwkv_a0, rwkv_a2, rwkv_k_k, rwkv_k_a, rwkv_r_k, rwkv_ln_g, rwkv_ln_b, s5_a_re, s5_a_im,
           s5_log_step, s5_b_re, s5_b_im, s5_c_re, s5_c_im, s5_d, s5_glu_w, s5_glu_b, lru_conv_w, lru_conv_b,
           lru_wa, lru_ba, lru_wx, lru_bx, lru_lambda):
    depth = w_in.shape[0]
    bp, tp, _ = x_prompt.shape
    bs, ts, _ = x_sample.shape
    c_all = jnp.concatenate([c_prompt, c_sample], axis=0)
    pad = (-c_all.shape[0]) % SUBLANES
    c_all = jnp.pad(c_all, ((0, pad), (0, 0)))
    mod_all = _modulation(c_all, ada_w, ada_b)
    mod_all = mod_all.reshape(depth, c_all.shape[0], 3, D_MODEL)
    rwkv = (rwkv_mu, rwkv_w0, rwkv_w2, rwkv_a0, rwkv_a2, rwkv_k_k, rwkv_k_a,
            rwkv_r_k.reshape(depth, D_MODEL), rwkv_ln_g, rwkv_ln_b)
    s5 = (s5_a_re, s5_a_im, s5_log_step, s5_b_re, s5_b_im, s5_c_re, s5_c_im, s5_d, s5_glu_w, s5_glu_b)
    lru = (lru_conv_w, lru_conv_b, lru_wa, lru_ba, lru_wx, lru_bx, lru_lambda)
    prm = _stacked_params(w_in, w_out, norm_pre, norm_post, rwkv, s5, lru)
    zero = _state_layout(jnp.zeros((1, bp, SHIFT_W), F32), jnp.zeros((1, bp, HEADS, HEAD, HEAD), F32),
                         jnp.zeros((1, bp, 64, 64), F32), jnp.zeros((1, bp, 64, 64), F32),
                         jnp.zeros((1, bp, D_MODEL), F32), jnp.zeros((1, bp, CONV_W - 1, D_MODEL), F32))
    st_in = _state_layout(state_rwkv_shift, state_rwkv_wkv, state_s5_re, state_s5_im, state_lru_h,
                          state_lru_conv)

    xp, xs = x_prompt, x_sample
    new_p = [[] for _ in range(6)]
    new_s = [[] for _ in range(6)]
    for l in range(depth):
        xp, st_p = _layer(xp, mod_all, zero, prm, sel=(l, 0, 0), t_valid=tp, tiles=_tiles(tp))
        xs, st_s = _layer(xs, mod_all, st_in, prm, sel=(l, l, bp), t_valid=ts, tiles=_tiles(ts))
        for i in range(6):
            new_p[i].append(st_p[i])
            new_s[i].append(st_s[i])
    sp = [jnp.stack(z, axis=0) for z in new_p]
    ss = [jnp.stack(z, axis=0) for z in new_s]
    return (xp, xs, sp[0], sp[1], sp[2], sp[3], sp[4], sp[5], ss[0], ss[1], ss[2], ss[3], ss[4], ss[5])
```

```python
import functools

import jax
import jax.numpy as jnp
from jax import lax
from jax.experimental import pallas as pl
from jax.experimental.pallas import tpu as pltpu

F32 = jnp.float32
BF16 = jnp.bfloat16
HIGHEST = lax.Precision.HIGHEST

D_MODEL = 1024
HEAD = 64
HEADS = D_MODEL // HEAD
PAIRS = HEADS // 2
LORA = 64
SHIFT_W = 3 * D_MODEL + 2 * LORA
S5_STATE_W = 64 * 64
S5_SETS = 4
S5_IN_GROUPS = 8
S5_TILES = S5_STATE_W // 128
S5_ROW_STRIDE = 40
CONV_W = 4
LRU_C = 8.0
RMS_EPS = 1e-6
GN_EPS = 64e-5
RWKV_CHUNK = 64
RWKV_TILE = 256
SHORT_ROWS = 512
DECAY_SCALE = 0.6065306597126334
RWKV_INSTANCES = 16
SUBLANES = 8
LANES = 128
VMEM_LIMIT = 56 * 1024 * 1024


def _sigmoid(x):
    return 0.5 * jnp.tanh(0.5 * x) + 0.5


def _silu(x):
    return x * _sigmoid(x)


def _softplus(x):
    return jnp.maximum(x, 0.0) + jnp.log1p(jnp.exp(-jnp.abs(x)))


def _norm_mod(x, g, mod):
    ms = jnp.mean(x * x, axis=-1, keepdims=True)
    return (x * lax.rsqrt(ms + RMS_EPS)) * (g * (1.0 + mod[1:2, :])) + mod[0:1, :]


def _dot(a, b):
    return jnp.dot(a, b, preferred_element_type=F32)


def _dot_hi(a, b):
    return jnp.dot(a, b, preferred_element_type=F32, precision=HIGHEST)


def _split2(x):
    hi = x.astype(BF16)
    lo = (x - hi.astype(F32)).astype(BF16)
    return hi, lo


_NN = (((1,), (0,)), ((), ()))
_NT = (((1,), (1,)), ((), ()))
_TN = (((0,), (0,)), ((), ()))


def _mm(a, b, dims=_NN):
    return lax.dot_general(a.astype(BF16), b.astype(BF16), dims, preferred_element_type=F32)


def _mm3(a_parts, b_parts):
    (a_hi, a_lo), (b_hi, b_lo) = a_parts, b_parts
    return _dot(a_hi, b_hi) + (_dot(a_hi, b_lo) + _dot(a_lo, b_hi))


def _head_sum(x, ones_blk):
    outs = []
    for s in range(D_MODEL // 256):
        outs.append(_dot(x[:, 256 * s:256 * (s + 1)].astype(BF16), ones_blk))
    return jnp.concatenate(outs, axis=1)


def _block_ones(n, blk):
    ri = lax.broadcasted_iota(jnp.int32, (n, n), 0)
    ci = lax.broadcasted_iota(jnp.int32, (n, n), 1)
    sh = blk.bit_length() - 1
    return ((ri >> sh) == (ci >> sh)).astype(F32).astype(BF16)


def _mod_kernel(c_ref, w_ref, b_ref, o_ref):
    s = _silu(c_ref[...])
    o_ref[...] = _dot_hi(s, w_ref[...]) + b_ref[...]


def _modulation(c_all, ada_w, ada_b):
    depth = ada_w.shape[0]
    rows = c_all.shape[0]
    return pl.pallas_call(
        _mod_kernel,
        grid=(depth, 3),
        in_specs=[
            pl.BlockSpec((rows, D_MODEL), lambda l, j: (0, 0)),
            pl.BlockSpec((None, D_MODEL, D_MODEL), lambda l, j: (l, 0, j)),
            pl.BlockSpec((None, 1, D_MODEL), lambda l, j: (l, 0, j)),
        ],
        out_specs=pl.BlockSpec((None, rows, D_MODEL), lambda l, j: (l, 0, j)),
        out_shape=jax.ShapeDtypeStruct((depth, rows, 3 * D_MODEL), F32),
        name="adaln_mod",
    )(c_all, ada_w, ada_b.reshape(depth, 1, 3 * D_MODEL))


def _s5_prep_kernel(are_ref, aim_ref, ls_ref, bre_ref, bim_ref, abr_ref, abi_ref, bbr_ref, bbi_ref):
    are = are_ref[...]
    aim = aim_ref[...]
    dt = jnp.exp(ls_ref[...])
    mag = jnp.exp(are * dt)
    abr = mag * jnp.cos(aim * dt)
    abi = mag * jnp.sin(aim * dt)
    abr_ref[...] = abr
    abi_ref[...] = abi
    nr = abr - 1.0
    ni = abi
    den = are * are + aim * aim
    cr = (nr * are + ni * aim) / den
    ci = (ni * are - nr * aim) / den
    bre = bre_ref[...]
    bim = bim_ref[...]
    bbr_ref[...] = cr[:, None, :] * bre - ci[:, None, :] * bim
    bbi_ref[...] = cr[:, None, :] * bim + ci[:, None, :] * bre


def _s5_prep(a_re, a_im, log_step, b_re, b_im):
    depth, g, p = a_re.shape
    i = b_re.shape[-1]
    layer = lambda *dims: pl.BlockSpec((None,) + dims, lambda l: (l,) + (0,) * len(dims))
    return pl.pallas_call(
        _s5_prep_kernel,
        grid=(depth,),
        in_specs=[layer(g, p), layer(g, p), layer(g, 1), layer(g, i, p), layer(g, i, p)],
        out_specs=(layer(g, p), layer(g, p), layer(g, i, p), layer(g, i, p)),
        out_shape=(
            jax.ShapeDtypeStruct((depth, g, p), F32),
            jax.ShapeDtypeStruct((depth, g, p), F32),
            jax.ShapeDtypeStruct((depth, g, i, p), F32),
            jax.ShapeDtypeStruct((depth, g, i, p), F32),
        ),
        name="s5_discretise",
    )(a_re, a_im, log_step.reshape(depth, g, 1), jnp.swapaxes(b_re, 2, 3), jnp.swapaxes(b_im, 2, 3))


def _rwkv_kernel(x_ref, mod_ref, g_ref, wr_ref, wg_ref, mu_ref, w0_ref, w2a_ref, a0_ref, kk_ref, ka_ref,
                 rk_ref, lng_ref, lnb_ref, shift0_ref, wkv0_ref,
                 yz_ref, shift_out_ref, wkv_out_ref,
                 carry_ref, st_ref, at_ref, rt_ref, bt_ref, kt_ref, bh_ref, kh_ref, v_ref, pl_ref, y_ref,
                 nh_ref, t_ref, aak_ref, ark_ref, rhs_ref, gm_ref, qe_ref, hc_ref, yc_ref,
                 *, tile, chunk, t_valid, group):
    t = pl.program_id(1)
    n_chunks = tile // chunk
    two = 2 * chunk
    n_inst = group * n_chunks
    chained = x_ref.shape[0] == 1

    def pair_state(s0):
        rh = lax.broadcasted_iota(jnp.int32, (LANES, LANES), 0) >> (HEAD.bit_length() - 1)
        ch = lax.broadcasted_iota(jnp.int32, (LANES, LANES), 1) >> (HEAD.bit_length() - 1)
        return jnp.where(rh == ch, jnp.concatenate([s0, s0], axis=1), 0.0)

    def head_states(st):
        return st[:, :HEAD] + st[:, HEAD:]

    if chained:
        @pl.when(t == 0)
        def _():
            carry_ref[...] = shift0_ref[0]
            for q in range(PAIRS):
                st_ref[q] = pair_state(wkv0_ref[0, q])

    h = _pre_norm(x_ref, g_ref, mod_ref)
    p = _dot(h, wr_ref[...])
    zg = _silu(_dot(h, wg_ref[...]))

    rows = lax.broadcasted_iota(jnp.int32, (tile, 1), 0)
    rmod = rows & (chunk - 1)
    last = (t_valid - 1) % chunk
    if chained:
        prev = jnp.where(rows == 0, carry_ref[...], pltpu.roll(p, 1, 0))
        carry_ref[...] = p[tile - chunk + last:tile - chunk + last + 1, :]
    else:
        first = jnp.concatenate([jnp.broadcast_to(shift0_ref[s], (chunk, SHIFT_W)) for s in range(n_chunks)], axis=0)
        prev = jnp.where(rmod == 0, first, pltpu.roll(p, 1, 0))
        for s in range(n_chunks):
            shift_out_ref[s] = p[s * chunk + last:s * chunk + last + 1, :]
    pm = p + (prev - p) * mu_ref[...]

    r = pm[:, 0:D_MODEL]
    k = pm[:, D_MODEL:2 * D_MODEL]
    v = pm[:, 2 * D_MODEL:3 * D_MODEL]
    wa = pm[:, 3 * D_MODEL:SHIFT_W]
    lane = lax.broadcasted_iota(jnp.int32, (1, LANES), 1)
    wa = jnp.where(lane < LORA, jnp.tanh(wa), wa)
    lora = _dot(wa.astype(BF16), w2a_ref[...])
    logd = -DECAY_SCALE * _sigmoid(w0_ref[...] + lora[:, :D_MODEL])
    a_sig = _sigmoid(a0_ref[...] + lora[:, D_MODEL:])

    ones_blk = _block_ones(256, HEAD)
    kk = k * kk_ref[...]
    kk = kk * jnp.minimum(lax.rsqrt(_head_sum(kk * kk, ones_blk)), 1e12)
    k2 = k * (1.0 + (a_sig - 1.0) * ka_ref[...])
    bvec = kk * a_sig
    if t_valid % chunk != 0:
        ok = rmod < t_valid % chunk
        logd = jnp.where(ok, logd, 0.0)
        bvec = jnp.where(ok, bvec, 0.0)
        k2 = jnp.where(ok, k2, 0.0)

    cum = logd
    s = 1
    while s < chunk:
        cum = cum + jnp.where(rmod >= s, pltpu.roll(cum, s, 0), 0.0)
        s *= 2
    cum_last = jnp.concatenate(
        [jnp.broadcast_to(cum[(j + 1) * chunk - 1:(j + 1) * chunk, :], (chunk, D_MODEL)) for j in range(n_chunks)],
        axis=0)
    p_inc = jnp.exp(cum)
    p_inv = jnp.exp(-cum)
    p_exc = jnp.exp(cum - logd)
    p_end = jnp.exp(cum_last - cum)

    def to_pairs(ref, val):
        for q in range(PAIRS):
            ref[q] = val[:, LANES * q:LANES * (q + 1)]

    to_pairs(at_ref, -kk * p_exc)
    to_pairs(rt_ref, r * p_inc)
    to_pairs(bt_ref, bvec * p_inv)
    to_pairs(kt_ref, k2 * p_inv)
    to_pairs(bh_ref, bvec * p_end)
    to_pairs(kh_ref, k2 * p_end)
    to_pairs(v_ref, v)
    for j in range(n_chunks):
        row = jnp.exp(cum[(j + 1) * chunk - 1:(j + 1) * chunk, :])
        for q in range(PAIRS):
            pl_ref[q, j] = jnp.broadcast_to(row[:, LANES * q:LANES * (q + 1)], (SUBLANES, LANES))

    m_a = (lane < HEAD).astype(F32)
    m_b = 1.0 - m_a
    ri = lax.broadcasted_iota(jnp.int32, (two, two), 0)
    ci = lax.broadcasted_iota(jnp.int32, (two, two), 1)
    csh = chunk.bit_length() - 1
    same_head = (ri >> csh) == (ci >> csh)
    strict = jnp.logical_and(same_head, (ri & (chunk - 1)) > (ci & (chunk - 1)))
    incl = jnp.logical_and(same_head, (ri & (chunk - 1)) >= (ci & (chunk - 1)))
    eye = ri == ci
    head_a = ri < chunk
    off_masks = []
    for lg in range(csh):
        off_masks.append(jnp.logical_and(strict, jnp.logical_and((ri >> (lg + 1)) == (ci >> (lg + 1)),
                                                                 (ri >> lg) != (ci >> lg))))

    def stack(x):
        return jnp.concatenate([x * m_a, x * m_b], axis=0)

    gsh = group.bit_length() - 1
    unroll = min(16, n_inst)

    def group_body(g, carry):
        def where_is(i):
            q = g * group + (i & (group - 1))
            j = i >> gsh
            return q, j, pl.ds(pl.multiple_of(j * chunk, chunk), chunk)

        def gram(i, c):
            q, _, sl = where_is(i)
            atm = stack(at_ref[q, sl, :])
            rtm = stack(rt_ref[q, sl, :])
            bt = bt_ref[q, sl, :]
            kt = kt_ref[q, sl, :]
            o1 = _mm(jnp.concatenate([atm, rtm], axis=0), jnp.concatenate([bt, kt], axis=0), _NT)
            o1_sw = pltpu.roll(o1, chunk, 1)
            n_ab = jnp.where(strict, jnp.where(head_a, o1[:two], o1_sw[:two]), 0.0)
            nh_ref[i] = n_ab.astype(BF16)
            t_ref[i] = jnp.where(eye, 1.0, jnp.where(off_masks[0], n_ab, 0.0)).astype(BF16)
            aak_ref[i] = jnp.where(strict, jnp.where(head_a, o1_sw[:two], o1[:two]), 0.0).astype(BF16)
            ark_ref[i] = jnp.concatenate([jnp.where(incl, jnp.where(head_a, o1[two:], o1_sw[two:]), 0.0),
                                          jnp.where(incl, jnp.where(head_a, o1_sw[two:], o1[two:]), 0.0)],
                                         axis=1).astype(BF16)
            return c

        lax.fori_loop(0, n_inst, gram, 0, unroll=unroll)

        for off in off_masks[1:]:
            def level(i, c, off=off):
                t_inv = t_ref[i]
                x = _dot(jnp.where(off, nh_ref[i], 0.0), t_inv).astype(BF16)
                t_ref[i] = t_inv + _dot(t_inv, x).astype(BF16)
                return c

            lax.fori_loop(0, n_inst, level, 0, unroll=unroll)

        def right_side(i, c):
            q, _, sl = where_is(i)
            atm = stack(at_ref[q, sl, :])
            vm = stack(v_ref[q, sl, :])
            z0_hi, z0_lo = _split2(jnp.concatenate([atm, _mm(aak_ref[i], vm)], axis=1))
            rhs_ref[i] = jnp.concatenate([z0_hi, z0_lo], axis=0)
            return c

        lax.fori_loop(0, n_inst, right_side, 0, unroll=unroll)

        def apply(i, c):
            q, _, sl = where_is(i)
            vm = stack(v_ref[q, sl, :])
            t_inv = t_ref[i]
            z = _dot(t_inv, rhs_ref[i, :two, :]) + _dot(t_inv, rhs_ref[i, two:, :])
            rhs_ref[i] = jnp.concatenate([z, jnp.concatenate([jnp.zeros_like(vm), vm], axis=1)], axis=0).astype(BF16)
            return c

        lax.fori_loop(0, n_inst, apply, 0, unroll=unroll)

        def operators(i, c):
            q, j, sl = where_is(i)
            rhs2 = rhs_ref[i]
            k_all = g * n_inst + i
            top = _mm(rhs2, jnp.concatenate([stack(bh_ref[q, sl, :]), stack(kh_ref[q, sl, :])], axis=0), _TN)
            bot = _dot(ark_ref[i], rhs2)
            gm_ref[k_all] = (top[:two, :]
                             + jnp.where(eye, jnp.broadcast_to(pl_ref[q, j][0:1, :], (two, two)), 0.0)).astype(BF16)
            hc_ref[k_all] = top[two:, :]
            qe_ref[k_all] = (bot[:, :two] + stack(rt_ref[q, sl, :])).astype(BF16)
            yc_ref[k_all] = bot[:, two:]
            return c

        lax.fori_loop(0, n_inst, operators, 0, unroll=unroll)
        return carry

    lax.fori_loop(0, PAIRS // group, group_body, 0)

    if chained:
        states = [st_ref[q] for q in range(PAIRS)]
    for j in range(n_chunks):
        for q in range(PAIRS):
            k_all = (q // group) * n_inst + j * group + q % group
            st = (states[q] if chained else pair_state(wkv0_ref[j, q])).astype(BF16)
            ym = _mm(qe_ref[k_all], st, _NT) + yc_ref[k_all]
            y_ref[q, j * chunk:(j + 1) * chunk, :] = ym[:chunk, :] + ym[chunk:, :]
            st_new = _dot(st, gm_ref[k_all]) + hc_ref[k_all]
            if chained:
                states[q] = st_new
            else:
                wkv_out_ref[j, q] = head_states(st_new)
    if chained:
        for q in range(PAIRS):
            st_ref[q] = states[q]

    ys = jnp.concatenate([y_ref[q] for q in range(PAIRS)], axis=1)
    mean = _head_sum(ys, ones_blk) * (1.0 / HEAD)
    yc = ys - mean
    var = _head_sum(yc * yc, ones_blk) * (1.0 / HEAD)
    yn = yc * lax.rsqrt(var + GN_EPS) * lng_ref[...] + lnb_ref[...]
    bonus = _head_sum(r * k2 * rk_ref[...], ones_blk) * v
    yz_ref[...] = ((yn + bonus) * zg).reshape(yz_ref.shape)

    if chained:
        @pl.when(t == pl.num_programs(1) - 1)
        def _():
            shift_out_ref[0] = carry_ref[...]
            for q in range(PAIRS):
                wkv_out_ref[0, q] = head_states(st_ref[q])


def _specs(layer, state_layer, mod_row, seqs=None):
    assert seqs is None or mod_row % seqs == 0
    per_step = 1 if seqs is None else seqs

    def weight(a):
        return pl.BlockSpec((None,) + a.shape[1:], lambda b, t: (layer,) + (0,) * (a.ndim - 1))

    def state(a):
        return pl.BlockSpec((None, seqs) + a.shape[2:], lambda b, t: (state_layer, b) + (0,) * (a.ndim - 2))

    def mod(a):
        return pl.BlockSpec((None, seqs) + a.shape[2:], lambda b, t: (layer, mod_row // per_step + b, 0, 0))

    return weight, state, mod


def _pre_norm(x_ref, g_ref, mod_ref):
    g = g_ref[...]
    rows = [_norm_mod(x_ref[s], g, mod_ref[s]) for s in range(x_ref.shape[0])]
    return jnp.concatenate(rows, axis=0).astype(BF16)


def _rwkv_mixer(x, mod, g_pre, wr, wg, vecs, w2a, shift0, wkv0, *, sel, tile, t_valid, seqs):
    bsz, tp, _ = x.shape
    chunk = RWKV_CHUNK
    assert tp % tile == 0 and tile % chunk == 0 and bsz % seqs == 0
    assert t_valid == tp or tp == tile
    assert seqs == 1 or tp == chunk
    nt = tp // tile
    block_rows = tile
    tile = seqs * tile
    mu, w0, a0, k_k, k_a, r_k, ln_g, ln_b = vecs
    weight, state, mod_spec = _specs(*sel, seqs=seqs)
    pair_buf = pltpu.VMEM((PAIRS, tile, LANES), F32)
    group = min(PAIRS, max(1, RWKV_INSTANCES // (tile // chunk)))
    n_inst = group * (tile // chunk)
    two = 2 * chunk
    inst = lambda cols, dtype: pltpu.VMEM((n_inst, two, cols), dtype)
    every = lambda dtype: pltpu.VMEM((PAIRS * (tile // chunk), two, two), dtype)
    kern = functools.partial(_rwkv_kernel, tile=tile, chunk=chunk, t_valid=t_valid, group=group)
    return pl.pallas_call(
        kern,
        grid=(bsz // seqs, nt),
        in_specs=[
            pl.BlockSpec((seqs, block_rows, D_MODEL), lambda b, t: (b, t, 0)),
            mod_spec(mod),
            weight(g_pre), weight(wr), weight(wg), weight(mu), weight(w0), weight(w2a), weight(a0),
            weight(k_k), weight(k_a), weight(r_k), weight(ln_g), weight(ln_b),
            state(shift0), state(wkv0),
        ],
        out_specs=[
            pl.BlockSpec((seqs, block_rows, D_MODEL), lambda b, t: (b, t, 0)),
            pl.BlockSpec((seqs, 1, SHIFT_W), lambda b, t: (b, 0, 0)),
            pl.BlockSpec((seqs, PAIRS, LANES, HEAD), lambda b, t: (b, 0, 0, 0)),
        ],
        out_shape=[
            jax.ShapeDtypeStruct((bsz, tp, D_MODEL), F32),
            jax.ShapeDtypeStruct((bsz, 1, SHIFT_W), F32),
            jax.ShapeDtypeStruct((bsz, PAIRS, LANES, HEAD), F32),
        ],
        scratch_shapes=[
            pltpu.VMEM((1, SHIFT_W), F32),
            pltpu.VMEM((PAIRS, LANES, LANES), F32),
            pair_buf, pair_buf, pair_buf, pair_buf, pair_buf, pair_buf, pair_buf,
            pltpu.VMEM((PAIRS, tile // chunk, SUBLANES, LANES), F32),
            pair_buf,
            inst(two, BF16), inst(two, BF16), inst(two, BF16), inst(2 * two, BF16),
            pltpu.VMEM((n_inst, 2 * two, 2 * two), BF16),
            every(BF16), every(BF16), every(F32), every(F32),
        ],
        compiler_params=pltpu.CompilerParams(
            dimension_semantics=("arbitrary", "arbitrary"), vmem_limit_bytes=VMEM_LIMIT),
        name="rwkv7_mixer",
    )(x, mod, g_pre, wr, wg, mu, w0, w2a, a0, k_k, k_a, r_k, ln_g, ln_b, shift0, wkv0)


def _s5_kernel(x_ref, mod_ref, g_ref, wu_ref, wg_ref, bbr_ref, bbi_ref, cre_ref, cmi_ref, abr_ref, abi_ref,
               d_ref, gluw_ref, glub_ref, s0r_ref, s0i_ref,
               yz_ref, sr_out_ref, si_out_ref,
               hr_ref, hi_ref, cr_ref, ci_ref, *, tile):
    t = pl.program_id(1)

    @pl.when(t == 0)
    def _():
        cr_ref[...] = s0r_ref[...]
        ci_ref[...] = s0i_ref[...]

    seqs = x_ref.shape[0]
    frames = seqs * tile
    h = _pre_norm(x_ref, g_ref, mod_ref)
    u = _dot(h, wu_ref[...])
    zg = _silu(_dot(h, wg_ref[...]))
    set_tiles = S5_TILES // S5_SETS

    def frame_rows(c):
        return pl.ds(c, frames, stride=S5_ROW_STRIDE)

    in_tiles = bbr_ref.shape[2] // LANES
    for s in range(D_MODEL // LANES):
        ub = u[:, LANES * s:LANES * (s + 1)].astype(BF16)
        bur = _dot(ub, bbr_ref[s])
        bui = _dot(ub, bbi_ref[s])
        for c in range(in_tiles):
            hr_ref[frame_rows(s * in_tiles + c), :] = bur[:, LANES * c:LANES * (c + 1)]
            hi_ref[frame_rows(s * in_tiles + c), :] = bui[:, LANES * c:LANES * (c + 1)]

    abr = abr_ref[...]
    abi = abi_ref[...]

    def frame(f, carry):
        sr, si = carry
        rows = pl.ds(pl.multiple_of(f * S5_ROW_STRIDE, SUBLANES), S5_TILES)
        nr = abr * sr - abi * si + hr_ref[rows, :]
        ni = abr * si + abi * sr + hi_ref[rows, :]
        hr_ref[rows, :] = nr
        hi_ref[rows, :] = ni
        return nr, ni

    for s in range(seqs):
        cr_ref[s], ci_ref[s] = lax.fori_loop(s * tile, (s + 1) * tile, frame, (cr_ref[s], ci_ref[s]), unroll=8)

    outs = []
    for s in range(S5_SETS):
        hr = jnp.concatenate([hr_ref[frame_rows(s * set_tiles + c), :] for c in range(set_tiles)], axis=1)
        hi = jnp.concatenate([hi_ref[frame_rows(s * set_tiles + c), :] for c in range(set_tiles)], axis=1)
        outs.append(_dot(hr.astype(BF16), cre_ref[s]) + _dot(hi.astype(BF16), cmi_ref[s]))
    y = jnp.concatenate(outs, axis=1) + d_ref[...] * u
    y = 0.5 * y * (1.0 + jnp.tanh(0.7978845608028654 * (y + 0.044715 * (y * y * y))))
    y = y * _sigmoid(_dot(y.astype(BF16), gluw_ref[...]) + glub_ref[...])
    yz_ref[...] = (y * zg).reshape(seqs, tile, D_MODEL)

    @pl.when(t == pl.num_programs(1) - 1)
    def _():
        sr_out_ref[...] = cr_ref[...]
        si_out_ref[...] = ci_ref[...]


def _s5_mixer(x, mod, g_pre, wu, wg, bbr, bbi, cre, cmi, abr, abi, d_skip, glu_w, glu_b, s0r, s0i, *, sel, tile,
              seqs):
    bsz, tp, _ = x.shape
    assert tp % tile == 0 and tile % SUBLANES == 0 and bsz % seqs == 0
    nt = tp // tile
    weight, state, mod_spec = _specs(*sel, seqs=seqs)
    state_out = pl.BlockSpec((seqs, S5_TILES, LANES), lambda b, t: (b, 0, 0))
    kern = functools.partial(_s5_kernel, tile=tile)
    return pl.pallas_call(
        kern,
        grid=(bsz // seqs, nt),
        in_specs=[
            pl.BlockSpec((seqs, tile, D_MODEL), lambda b, t: (b, t, 0)),
            mod_spec(mod),
            weight(g_pre), weight(wu), weight(wg), weight(bbr), weight(bbi), weight(cre), weight(cmi),
            weight(abr), weight(abi), weight(d_skip), weight(glu_w), weight(glu_b), state(s0r), state(s0i),
        ],
        out_specs=[pl.BlockSpec((seqs, tile, D_MODEL), lambda b, t: (b, t, 0)), state_out, state_out],
        out_shape=[
            jax.ShapeDtypeStruct((bsz, tp, D_MODEL), F32),
            jax.ShapeDtypeStruct((bsz, S5_TILES, LANES), F32),
            jax.ShapeDtypeStruct((bsz, S5_TILES, LANES), F32),
        ],
        scratch_shapes=[
            pltpu.VMEM((seqs * tile * S5_ROW_STRIDE, LANES), F32),
            pltpu.VMEM((seqs * tile * S5_ROW_STRIDE, LANES), F32),
            pltpu.VMEM((seqs, S5_TILES, LANES), F32),
            pltpu.VMEM((seqs, S5_TILES, LANES), F32),
        ],
        compiler_params=pltpu.CompilerParams(
            dimension_semantics=("arbitrary", "arbitrary"), vmem_limit_bytes=VMEM_LIMIT),
        name="s5_mixer",
    )(x, mod, g_pre, wu, wg, bbr, bbi, cre, cmi, abr, abi, d_skip, glu_w, glu_b, s0r, s0i)


def _lru_kernel(x_ref, mod_ref, g_ref, wx_ref, wg_ref, cw_ref, cb_ref, wa4_ref, ba_ref, wx4_ref, bx_ref, lam_ref,
                conv0_ref, h0_ref,
                yz_ref, conv_out_ref, h_out_ref,
                xbuf_ref, a_ref, b_ref, hc_ref, *, tile):
    t = pl.program_id(1)
    pad = SUBLANES
    hist = CONV_W - 1
    seqs = x_ref.shape[0]

    @pl.when(t == 0)
    def _():
        xbuf_ref[:, pad - hist:pad, :] = conv0_ref[...]
        hc_ref[...] = h0_ref[...]

    h = _pre_norm(x_ref, g_ref, mod_ref)
    xl = _dot(h, wx_ref[...])
    zg = _silu(_dot(h, wg_ref[...]))
    xcs, tails = [], []
    for s in range(seqs):
        xs = xl[s * tile:(s + 1) * tile, :]
        xbuf_ref[s, pad:pad + tile, :] = xs
        acc = cb_ref[...] + xs * cw_ref[CONV_W - 1:CONV_W, :]
        for j in range(hist):
            acc = acc + xbuf_ref[s, pad - hist + j:pad - hist + j + tile, :] * cw_ref[j:j + 1, :]
        tails.append(xbuf_ref[s, pad + tile - hist:pad + tile, :])
        xbuf_ref[s, pad - hist:pad, :] = tails[s]
        xcs.append(acc)
    xc = jnp.concatenate(xcs, axis=0)

    ga, gx = [], []
    blk = D_MODEL // 4
    for s in range(4):
        xb = xc[:, blk * s:blk * (s + 1)].astype(BF16)
        ga.append(_dot(xb, wa4_ref[s]))
        gx.append(_dot(xb, wx4_ref[s]))
    gate_r = _sigmoid(jnp.concatenate(ga, axis=1) + ba_ref[...])
    gate_i = _sigmoid(jnp.concatenate(gx, axis=1) + bx_ref[...])
    log_a = -LRU_C * gate_r * _softplus(-lam_ref[...])
    a = jnp.exp(log_a)
    a_ref[...] = a
    b_ref[...] = jnp.sqrt(-jnp.tanh(log_a) * (a * a + 1.0)) * (gate_i * xc)

    row8 = lax.broadcasted_iota(jnp.int32, (SUBLANES, 1), 0)

    def row_body(g, carry):
        rs = pl.ds(pl.multiple_of(g * SUBLANES, SUBLANES), SUBLANES)
        av = a_ref[rs, :]
        bv = b_ref[rs, :]
        for sh in (1, 2, 4):
            keep = row8 >= sh
            ash = jnp.where(keep, pltpu.roll(av, sh, 0), 1.0)
            bsh = jnp.where(keep, pltpu.roll(bv, sh, 0), 0.0)
            bv = bv + av * bsh
            av = av * ash
        hs = bv + av * carry
        b_ref[rs, :] = hs
        return hs[SUBLANES - 1:SUBLANES, :]

    groups = tile // SUBLANES
    for s in range(seqs):
        hc_ref[s] = lax.fori_loop(s * groups, (s + 1) * groups, row_body, hc_ref[s])
    yz_ref[...] = (b_ref[...] * zg).reshape(seqs, tile, D_MODEL)

    @pl.when(t == pl.num_programs(1) - 1)
    def _():
        for s in range(seqs):
            conv_out_ref[s] = tails[s]
        h_out_ref[...] = hc_ref[...]


def _lru_mixer(x, mod, g_pre, wx, wg, conv_w, conv_b, wa4, ba, wx4, bx, lam, conv0, h0, *, sel, tile, seqs):
    bsz, tp, _ = x.shape
    assert tp % tile == 0 and tile % SUBLANES == 0 and tile >= SUBLANES and bsz % seqs == 0
    nt = tp // tile
    weight, state, mod_spec = _specs(*sel, seqs=seqs)
    kern = functools.partial(_lru_kernel, tile=tile)
    return pl.pallas_call(
        kern,
        grid=(bsz // seqs, nt),
        in_specs=[
            pl.BlockSpec((seqs, tile, D_MODEL), lambda b, t: (b, t, 0)),
            mod_spec(mod),
            weight(g_pre), weight(wx), weight(wg), weight(conv_w), weight(conv_b), weight(wa4), weight(ba),
            weight(wx4), weight(bx), weight(lam), state(conv0), state(h0),
        ],
        out_specs=[
            pl.BlockSpec((seqs, tile, D_MODEL), lambda b, t: (b, t, 0)),
            pl.BlockSpec((seqs, CONV_W - 1, D_MODEL), lambda b, t: (b, 0, 0)),
            pl.BlockSpec((seqs, 1, D_MODEL), lambda b, t: (b, 0, 0)),
        ],
        out_shape=[
            jax.ShapeDtypeStruct((bsz, tp, D_MODEL), F32),
            jax.ShapeDtypeStruct((bsz, CONV_W - 1, D_MODEL), F32),
            jax.ShapeDtypeStruct((bsz, 1, D_MODEL), F32),
        ],
        scratch_shapes=[
            pltpu.VMEM((seqs, tile + SUBLANES, D_MODEL), F32),
            pltpu.VMEM((seqs * tile, D_MODEL), F32),
            pltpu.VMEM((seqs * tile, D_MODEL), F32),
            pltpu.VMEM((seqs, 1, D_MODEL), F32),
        ],
        compiler_params=pltpu.CompilerParams(
            dimension_semantics=("arbitrary", "arbitrary"), vmem_limit_bytes=VMEM_LIMIT),
        name="rglru_mixer",
    )(x, mod, g_pre, wx, wg, conv_w, conv_b, wa4, ba, wx4, bx, lam, conv0, h0)


def _merge_kernel(x_ref, mod_ref, g_ref, gpost_ref, wm_ref, wo_ref, yr_ref, ys_ref, yl_ref, o_ref):
    seqs, tile, _ = x_ref.shape
    rows = lambda ref: ref[...].reshape(seqs * tile, D_MODEL)
    m = _sigmoid(_dot(_pre_norm(x_ref, g_ref, mod_ref), wm_ref[...]))
    merged = (m[:, :D_MODEL] * rows(yr_ref) + m[:, D_MODEL:2 * D_MODEL] * rows(ys_ref)
              + m[:, 2 * D_MODEL:] * rows(yl_ref))
    o = _dot(merged.astype(BF16), wo_ref[...])
    ms = jnp.mean(o * o, axis=-1, keepdims=True)
    o = (o * lax.rsqrt(ms + RMS_EPS)) * gpost_ref[...]
    for s in range(seqs):
        o_ref[s] = x_ref[s] + mod_ref[s][2:3, :] * o[s * tile:(s + 1) * tile, :]


def _merge_out(x, mod, g_pre, g_post, wm, wo, yr, ys, yl, *, sel, tile, seqs):
    bsz, tp, _ = x.shape
    assert tp % tile == 0 and bsz % seqs == 0
    nt = tp // tile
    act = pl.BlockSpec((seqs, tile, D_MODEL), lambda b, t: (b, t, 0))
    weight, _, mod_spec = _specs(*sel, seqs=seqs)
    return pl.pallas_call(
        _merge_kernel,
        grid=(bsz // seqs, nt),
        in_specs=[act, mod_spec(mod), weight(g_pre), weight(g_post), weight(wm), weight(wo), act, act, act],
        out_specs=act,
        out_shape=jax.ShapeDtypeStruct((bsz, tp, D_MODEL), F32),
        compiler_params=pltpu.CompilerParams(
            dimension_semantics=("arbitrary", "arbitrary"), vmem_limit_bytes=VMEM_LIMIT),
        name="merge_out",
    )(x, mod, g_pre, g_post, wm, wo, yr, ys, yl)


def _block_diag_expand(w, per):
    depth, n, a, b = w.shape
    w = w.reshape(depth, n // per, per, a, b)
    eye = jnp.eye(per, dtype=w.dtype)
    return jnp.einsum("lsgab,gh->lsgahb", w, eye).reshape(depth, n // per, per * a, per * b)


def _layer(x, mod, state, prm, *, sel, t_valid, tiles):
    shift_row, wkv, s_re, s_im, lru_h, lru_conv = state
    bsz = x.shape[0]
    g_pre = prm["g_pre"]

    xr = x
    if x.shape[1] % tiles["rwkv"] != 0:
        xr = jnp.pad(x, ((0, 0), (0, tiles["rwkv"] - x.shape[1] % tiles["rwkv"]), (0, 0)))
    yz_r, shift_new, wkv_new = _rwkv_mixer(
        xr, mod, g_pre, prm["w_rwkv"], prm["w_rwkv_g"], prm["rwkv_vecs"], prm["w2a"], shift_row, wkv,
        sel=sel, tile=tiles["rwkv"], t_valid=t_valid, seqs=tiles["rwkv_seqs"])
    yz_r = yz_r[:, :t_valid]
    yz_s, s_re_new, s_im_new = _s5_mixer(
        x, mod, g_pre, prm["w_s5"], prm["w_s5_g"], prm["bbr"], prm["bbi"], prm["cre"], prm["cmi"],
        prm["abr"], prm["abi"], prm["s5_d"], prm["glu_w"], prm["glu_b"], s_re, s_im, sel=sel, tile=tiles["s5"],
        seqs=tiles["seqs"])
    yz_l, conv_new, h_new = _lru_mixer(
        x, mod, g_pre, prm["w_lru"], prm["w_lru_g"], prm["conv_w"], prm["conv_b"], prm["wa4"], prm["ba"],
        prm["wx4"], prm["bx"], prm["lam"], lru_conv, lru_h, sel=sel, tile=tiles["lru"], seqs=tiles["seqs"])
    x_new = _merge_out(x, mod, g_pre, prm["g_post"], prm["w_merge"], prm["w_out"], yz_r, yz_s, yz_l,
                       sel=sel, tile=tiles["merge"], seqs=tiles["seqs"])
    new_state = (shift_new.reshape(bsz, SHIFT_W), wkv_new.reshape(bsz, HEADS, HEAD, HEAD),
                 s_re_new.reshape(bsz, 64, 64), s_im_new.reshape(bsz, 64, 64),
                 h_new.reshape(bsz, D_MODEL), conv_new)
    return x_new, new_state


def _stacked_params(w_in, w_out, norm_pre, norm_post, rwkv, s5, lru):
    depth = w_in.shape[0]
    d = D_MODEL
    wb = w_in.astype(BF16)
    o = SHIFT_W
    row = lambda a: a.reshape(depth, 1, -1)
    mu, w0, w2, a0, a2, k_k, k_a, r_k, ln_g, ln_b = rwkv
    a_re, a_im, log_step, b_re, b_im, c_re, c_im, s5_d, glu_w, glu_b = s5
    conv_w, conv_b, wa, ba, wx, bx, lam = lru
    zeros = jnp.zeros((depth, LORA, d), F32)
    w2a = jnp.concatenate([jnp.concatenate([w2, zeros], axis=2), jnp.concatenate([zeros, a2], axis=2)], axis=1)
    abr, abi, bbr, bbi = _s5_prep(a_re, a_im, log_step, b_re, b_im)
    per = 64 // S5_SETS
    return dict(
        g_pre=row(norm_pre), g_post=row(norm_post),
        w_rwkv=wb[:, :, :o], w_rwkv_g=wb[:, :, o:o + d],
        w_s5=wb[:, :, o + d:o + 2 * d], w_s5_g=wb[:, :, o + 2 * d:o + 3 * d],
        w_lru=wb[:, :, o + 3 * d:o + 4 * d], w_lru_g=wb[:, :, o + 4 * d:o + 5 * d],
        w_merge=wb[:, :, o + 5 * d:], w_out=w_out.astype(BF16),
        rwkv_vecs=(row(mu), row(w0), row(a0), row(k_k), row(k_a), row(r_k), row(ln_g), row(ln_b)),
        w2a=w2a.astype(BF16),
        bbr=_block_diag_expand(bbr, S5_IN_GROUPS).astype(BF16),
        bbi=_block_diag_expand(bbi, S5_IN_GROUPS).astype(BF16),
        cre=_block_diag_expand(jnp.swapaxes(c_re, 2, 3), per).astype(BF16),
        cmi=_block_diag_expand(-jnp.swapaxes(c_im, 2, 3), per).astype(BF16),
        abr=abr.reshape(depth, S5_TILES, LANES), abi=abi.reshape(depth, S5_TILES, LANES),
        s5_d=row(s5_d), glu_w=glu_w.astype(BF16), glu_b=row(glu_b),
        conv_w=conv_w, conv_b=row(conv_b),
        wa4=_block_diag_expand(wa, 4).astype(BF16), ba=row(ba),
        wx4=_block_diag_expand(wx, 4).astype(BF16), bx=row(bx), lam=row(lam),
    )


def _state_layout(shift, wkv, s_re, s_im, lru_h, lru_conv):
    n, bsz = shift.shape[:2]
    return (shift.reshape(n, bsz, 1, SHIFT_W), wkv.reshape(n, bsz, PAIRS, LANES, HEAD),
            s_re.reshape(n, bsz, S5_TILES, LANES), s_im.reshape(n, bsz, S5_TILES, LANES),
            lru_h.reshape(n, bsz, 1, D_MODEL), lru_conv)


def _tiles(t, bsz):
    pick = lambda want: want if t % want == 0 else t
    share = lambda rows, limit: max(n for n in range(1, bsz + 1) if bsz % n == 0 and n * rows <= max(limit, rows))
    seqs = share(t, SHORT_ROWS) if t < SHORT_ROWS else 1
    rwkv = RWKV_TILE if t % RWKV_TILE == 0 else RWKV_CHUNK
    rwkv_seqs = share(RWKV_CHUNK, RWKV_TILE) if t <= RWKV_CHUNK else 1
    return dict(rwkv=rwkv, rwkv_seqs=rwkv_seqs, s5=pick(256), lru=pick(512), merge=pick(512), seqs=seqs)


def kernel(x_prompt, x_sample, state_rwkv_shift, state_rwkv_wkv, state_s5_re, state_s5_im, state_lru_h,
           state_lru_conv, c_prompt, c_sample, ada_w, ada_b, norm_pre, norm_post, w_in, w_out, rwkv_mu, rwkv_w0,
           rwkv_w2, rwkv_a0, rwkv_a2, rwkv_k_k, rwkv_k_a, rwkv_r_k, rwkv_ln_g, rwkv_ln_b, s5_a_re, s5_a_im,
           s5_log_step, s5_b_re, s5_b_im, s5_c_re, s5_c_im, s5_d, s5_glu_w, s5_glu_b, lru_conv_w, lru_conv_b,
           lru_wa, lru_ba, lru_wx, lru_bx, lru_lambda):
    depth = w_in.shape[0]
    bp, tp, _ = x_prompt.shape
    bs, ts, _ = x_sample.shape
    c_all = jnp.concatenate([c_sample, c_prompt], axis=0)
    pad = (-c_all.shape[0]) % SUBLANES
    c_all = jnp.pad(c_all, ((0, pad), (0, 0)))
    mod_all = _modulation(c_all, ada_w, ada_b)
    mod_all = mod_all.reshape(depth, c_all.shape[0], 3, D_MODEL)
    rwkv = (rwkv_mu, rwkv_w0, rwkv_w2, rwkv_a0, rwkv_a2, rwkv_k_k, rwkv_k_a,
            rwkv_r_k.reshape(depth, D_MODEL), rwkv_ln_g, rwkv_ln_b)
    s5 = (s5_a_re, s5_a_im, s5_log_step, s5_b_re, s5_b_im, s5_c_re, s5_c_im, s5_d, s5_glu_w, s5_glu_b)
    lru = (lru_conv_w, lru_conv_b, lru_wa, lru_ba, lru_wx, lru_bx, lru_lambda)
    prm = _stacked_params(w_in, w_out, norm_pre, norm_post, rwkv, s5, lru)
    zero = _state_layout(jnp.zeros((1, bp, SHIFT_W), F32), jnp.zeros((1, bp, HEADS, HEAD, HEAD), F32),
                         jnp.zeros((1, bp, 64, 64), F32), jnp.zeros((1, bp, 64, 64), F32),
                         jnp.zeros((1, bp, D_MODEL), F32), jnp.zeros((1, bp, CONV_W - 1, D_MODEL), F32))
    st_in = _state_layout(state_rwkv_shift, state_rwkv_wkv, state_s5_re, state_s5_im, state_lru_h,
                          state_lru_conv)

    xp, xs = x_prompt, x_sample
    new_p = [[] for _ in range(6)]
    new_s = [[] for _ in range(6)]
    for l in range(depth):
        xp, st_p = _layer(xp, mod_all, zero, prm, sel=(l, 0, bs), t_valid=tp, tiles=_tiles(tp, bp))
        xs, st_s = _layer(xs, mod_all, st_in, prm, sel=(l, l, 0), t_valid=ts, tiles=_tiles(ts, bs))
        for i in range(6):
            new_p[i].append(st_p[i])
            new_s[i].append(st_s[i])
    sp = [jnp.stack(z, axis=0) for z in new_p]
    ss = [jnp.stack(z, axis=0) for z in new_s]
    return (xp, xs, sp[0], sp[1], sp[2], sp[3], sp[4], sp[5], ss[0], ss[1], ss[2], ss[3], ss[4], ss[5])
```

```python
import functools

import jax
import jax.numpy as jnp
from jax import lax
from jax.experimental import pallas as pl
from jax.experimental.pallas import tpu as pltpu

F32 = jnp.float32
BF16 = jnp.bfloat16
HIGHEST = lax.Precision.HIGHEST

SUBLANES = 8
LANES = 128
MXU_TILE = 256
VMEM_LIMIT = 56 * 1024 * 1024

D_MODEL = 1024
HEAD = 64
HEADS = D_MODEL // HEAD
PAIRS = HEADS // 2
LORA = 64
SHIFT_W = 3 * D_MODEL + 2 * LORA
S5_GROUPS = 64
S5_STATES = 64
S5_STATE_W = S5_GROUPS * S5_STATES
LRU_BLOCK = 64
S5_SETS = 4
S5_IN_GROUPS = 8
S5_TILES = S5_STATE_W // LANES
S5_ROW_STRIDE = 40
CONV_W = 4
LRU_C = 8.0
RMS_EPS = 1e-6
GN_EPS = 64e-5
RWKV_CHUNK = 64
RWKV_TILE = 256
S5_TILE = 256
WIDE_TILE = 512
SHORT_ROWS = 512
DECAY_SCALE = 0.6065306597126334
RWKV_INSTANCES = 16


def _sigmoid(x):
    return 0.5 * jnp.tanh(0.5 * x) + 0.5


def _silu(x):
    return x * _sigmoid(x)


def _softplus(x):
    return jnp.maximum(x, 0.0) + jnp.log1p(jnp.exp(-jnp.abs(x)))


def _norm_mod(x, g, mod):
    ms = jnp.mean(x * x, axis=-1, keepdims=True)
    return (x * lax.rsqrt(ms + RMS_EPS)) * (g * (1.0 + mod[1:2, :])) + mod[0:1, :]


def _dot(a, b):
    return jnp.dot(a, b, preferred_element_type=F32)


def _dot_hi(a, b):
    return jnp.dot(a, b, preferred_element_type=F32, precision=HIGHEST)


def _split2(x):
    hi = x.astype(BF16)
    lo = (x - hi.astype(F32)).astype(BF16)
    return hi, lo


_NN = (((1,), (0,)), ((), ()))
_NT = (((1,), (1,)), ((), ()))
_TN = (((0,), (0,)), ((), ()))


def _mm(a, b, dims=_NN):
    return lax.dot_general(a.astype(BF16), b.astype(BF16), dims, preferred_element_type=F32)


def _head_sum(x, ones_blk):
    outs = []
    for s in range(D_MODEL // MXU_TILE):
        outs.append(_dot(x[:, MXU_TILE * s:MXU_TILE * (s + 1)].astype(BF16), ones_blk))
    return jnp.concatenate(outs, axis=1)


def _block_ones(n, blk):
    ri = lax.broadcasted_iota(jnp.int32, (n, n), 0)
    ci = lax.broadcasted_iota(jnp.int32, (n, n), 1)
    sh = blk.bit_length() - 1
    return ((ri >> sh) == (ci >> sh)).astype(F32).astype(BF16)


def _mod_kernel(c_ref, w_ref, b_ref, o_ref):
    s = _silu(c_ref[...])
    o_ref[...] = _dot_hi(s, w_ref[...]) + b_ref[...]


def _modulation(c_all, ada_w, ada_b):
    depth = ada_w.shape[0]
    rows = c_all.shape[0]
    return pl.pallas_call(
        _mod_kernel,
        grid=(depth, 3),
        in_specs=[
            pl.BlockSpec((rows, D_MODEL), lambda l, j: (0, 0)),
            pl.BlockSpec((None, D_MODEL, D_MODEL), lambda l, j: (l, 0, j)),
            pl.BlockSpec((None, 1, D_MODEL), lambda l, j: (l, 0, j)),
        ],
        out_specs=pl.BlockSpec((None, rows, D_MODEL), lambda l, j: (l, 0, j)),
        out_shape=jax.ShapeDtypeStruct((depth, rows, 3 * D_MODEL), F32),
        name="adaln_mod",
    )(c_all, ada_w, ada_b.reshape(depth, 1, 3 * D_MODEL))


def _s5_prep_kernel(are_ref, aim_ref, ls_ref, bre_ref, bim_ref, abr_ref, abi_ref, bbr_ref, bbi_ref):
    are = are_ref[...]
    aim = aim_ref[...]
    dt = jnp.exp(ls_ref[...])
    mag = jnp.exp(are * dt)
    abr = mag * jnp.cos(aim * dt)
    abi = mag * jnp.sin(aim * dt)
    abr_ref[...] = abr
    abi_ref[...] = abi
    nr = abr - 1.0
    ni = abi
    den = are * are + aim * aim
    cr = (nr * are + ni * aim) / den
    ci = (ni * are - nr * aim) / den
    bre = bre_ref[...]
    bim = bim_ref[...]
    bbr_ref[...] = cr[:, None, :] * bre - ci[:, None, :] * bim
    bbi_ref[...] = cr[:, None, :] * bim + ci[:, None, :] * bre


def _s5_prep(a_re, a_im, log_step, b_re, b_im):
    depth, g, p = a_re.shape
    i = b_re.shape[-1]
    layer = lambda *dims: pl.BlockSpec((None,) + dims, lambda l: (l,) + (0,) * len(dims))
    return pl.pallas_call(
        _s5_prep_kernel,
        grid=(depth,),
        in_specs=[layer(g, p), layer(g, p), layer(g, 1), layer(g, i, p), layer(g, i, p)],
        out_specs=(layer(g, p), layer(g, p), layer(g, i, p), layer(g, i, p)),
        out_shape=(
            jax.ShapeDtypeStruct((depth, g, p), F32),
            jax.ShapeDtypeStruct((depth, g, p), F32),
            jax.ShapeDtypeStruct((depth, g, i, p), F32),
            jax.ShapeDtypeStruct((depth, g, i, p), F32),
        ),
        name="s5_discretise",
    )(a_re, a_im, log_step.reshape(depth, g, 1), jnp.swapaxes(b_re, 2, 3), jnp.swapaxes(b_im, 2, 3))


def _rwkv_kernel(x_ref, mod_ref, g_ref, wr_ref, wg_ref, mu_ref, w0_ref, w2a_ref, a0_ref, kk_ref, ka_ref,
                 rk_ref, lng_ref, lnb_ref, shift0_ref, wkv0_ref,
                 yz_ref, shift_out_ref, wkv_out_ref,
                 carry_ref, st_ref, at_ref, rt_ref, bt_ref, kt_ref, bh_ref, kh_ref, v_ref, pl_ref, y_ref,
                 nh_ref, t_ref, aak_ref, ark_ref, rhs_ref, gm_ref, qe_ref, hc_ref, yc_ref, ones_ref,
                 *, tile, chunk, t_valid, group):
    t = pl.program_id(1)
    n_chunks = tile // chunk
    two = 2 * chunk
    n_inst = group * n_chunks
    chained = x_ref.shape[0] == 1

    def pair_state(s0):
        rh = lax.broadcasted_iota(jnp.int32, (LANES, LANES), 0) >> (HEAD.bit_length() - 1)
        ch = lax.broadcasted_iota(jnp.int32, (LANES, LANES), 1) >> (HEAD.bit_length() - 1)
        return jnp.where(rh == ch, jnp.concatenate([s0, s0], axis=1), 0.0)

    def head_states(st):
        return st[:, :HEAD] + st[:, HEAD:]

    if chained:
        @pl.when(t == 0)
        def _():
            carry_ref[...] = shift0_ref[0]
            for q in range(PAIRS):
                st_ref[q] = pair_state(wkv0_ref[0, q])

    h = _pre_norm(x_ref, g_ref, mod_ref)
    p = _dot(h, wr_ref[...])
    zg = _silu(_dot(h, wg_ref[...]))

    rows = lax.broadcasted_iota(jnp.int32, (tile, 1), 0)
    rmod = rows & (chunk - 1)
    last = (t_valid - 1) % chunk
    if chained:
        prev = jnp.where(rows == 0, carry_ref[...], pltpu.roll(p, 1, 0))
        carry_ref[...] = p[tile - chunk + last:tile - chunk + last + 1, :]
    else:
        first = jnp.concatenate([jnp.broadcast_to(shift0_ref[s], (chunk, SHIFT_W)) for s in range(n_chunks)], axis=0)
        prev = jnp.where(rmod == 0, first, pltpu.roll(p, 1, 0))
        for s in range(n_chunks):
            shift_out_ref[s] = p[s * chunk + last:s * chunk + last + 1, :]
    pm = p + (prev - p) * mu_ref[...]

    r = pm[:, 0:D_MODEL]
    k = pm[:, D_MODEL:2 * D_MODEL]
    v = pm[:, 2 * D_MODEL:3 * D_MODEL]
    wa = pm[:, 3 * D_MODEL:SHIFT_W]
    lane = lax.broadcasted_iota(jnp.int32, (1, LANES), 1)
    wa = jnp.where(lane < LORA, jnp.tanh(wa), wa)
    lora = _dot(wa.astype(BF16), w2a_ref[...])
    logd = -DECAY_SCALE * _sigmoid(w0_ref[...] + lora[:, :D_MODEL])
    a_sig = _sigmoid(a0_ref[...] + lora[:, D_MODEL:])

    @pl.when(t == 0)
    def _():
        ones_ref[...] = _block_ones(MXU_TILE, HEAD)

    ones_blk = ones_ref[...]
    kk = k * kk_ref[...]
    kk = kk * jnp.minimum(lax.rsqrt(_head_sum(kk * kk, ones_blk)), 1e12)
    k2 = k * (1.0 + (a_sig - 1.0) * ka_ref[...])
    bvec = kk * a_sig
    if t_valid % chunk != 0:
        ok = rmod < t_valid % chunk
        logd = jnp.where(ok, logd, 0.0)
        bvec = jnp.where(ok, bvec, 0.0)
        k2 = jnp.where(ok, k2, 0.0)

    cum = logd
    s = 1
    while s < chunk:
        cum = cum + jnp.where(rmod >= s, pltpu.roll(cum, s, 0), 0.0)
        s *= 2
    cum_last = jnp.concatenate(
        [jnp.broadcast_to(cum[(j + 1) * chunk - 1:(j + 1) * chunk, :], (chunk, D_MODEL)) for j in range(n_chunks)],
        axis=0)
    p_inc = jnp.exp(cum)
    p_inv = jnp.exp(-cum)
    p_exc = jnp.exp(cum - logd)
    p_end = jnp.exp(cum_last - cum)

    def to_pairs(ref, val):
        for q in range(PAIRS):
            ref[q] = val[:, LANES * q:LANES * (q + 1)]

    to_pairs(at_ref, -kk * p_exc)
    to_pairs(rt_ref, r * p_inc)
    to_pairs(bt_ref, bvec * p_inv)
    to_pairs(kt_ref, k2 * p_inv)
    to_pairs(bh_ref, bvec * p_end)
    to_pairs(kh_ref, k2 * p_end)
    to_pairs(v_ref, v)
    for j in range(n_chunks):
        row = jnp.exp(cum[(j + 1) * chunk - 1:(j + 1) * chunk, :])
        for q in range(PAIRS):
            pl_ref[q, j] = jnp.broadcast_to(row[:, LANES * q:LANES * (q + 1)], (SUBLANES, LANES))

    m_a = (lane < HEAD).astype(F32)
    m_b = 1.0 - m_a
    ri = lax.broadcasted_iota(jnp.int32, (two, two), 0)
    ci = lax.broadcasted_iota(jnp.int32, (two, two), 1)
    csh = chunk.bit_length() - 1
    same_head = (ri >> csh) == (ci >> csh)
    strict = jnp.logical_and(same_head, (ri & (chunk - 1)) > (ci & (chunk - 1)))
    incl = jnp.logical_and(same_head, (ri & (chunk - 1)) >= (ci & (chunk - 1)))
    eye = ri == ci
    head_a = ri < chunk
    off_masks = []
    for lg in range(csh):
        off_masks.append(jnp.logical_and(strict, jnp.logical_and((ri >> (lg + 1)) == (ci >> (lg + 1)),
                                                                 (ri >> lg) != (ci >> lg))))

    def stack(x):
        return jnp.concatenate([x * m_a, x * m_b], axis=0)

    gsh = group.bit_length() - 1
    unroll = min(16, n_inst)

    def group_body(g, carry):
        def where_is(i):
            q = g * group + (i & (group - 1))
            j = i >> gsh
            return q, j, pl.ds(pl.multiple_of(j * chunk, chunk), chunk)

        def gram(i, c):
            q, _, sl = where_is(i)
            atm = stack(at_ref[q, sl, :])
            rtm = stack(rt_ref[q, sl, :])
            bt = bt_ref[q, sl, :]
            kt = kt_ref[q, sl, :]
            o1 = _mm(jnp.concatenate([atm, rtm], axis=0), jnp.concatenate([bt, kt], axis=0), _NT)
            o1_sw = pltpu.roll(o1, chunk, 1)
            n_ab = jnp.where(strict, jnp.where(head_a, o1[:two], o1_sw[:two]), 0.0)
            nh_ref[i] = n_ab.astype(BF16)
            t_ref[i] = jnp.where(eye, 1.0, jnp.where(off_masks[0], n_ab, 0.0)).astype(BF16)
            aak_ref[i] = jnp.where(strict, jnp.where(head_a, o1_sw[:two], o1[:two]), 0.0).astype(BF16)
            ark_ref[i] = jnp.concatenate([jnp.where(incl, jnp.where(head_a, o1[two:], o1_sw[two:]), 0.0),
                                          jnp.where(incl, jnp.where(head_a, o1_sw[two:], o1[two:]), 0.0)],
                                         axis=1).astype(BF16)
            return c

        lax.fori_loop(0, n_inst, gram, 0, unroll=unroll)

        for off in off_masks[1:]:
            def level(i, c, off=off):
                t_inv = t_ref[i]
                x = _dot(jnp.where(off, nh_ref[i], 0.0), t_inv).astype(BF16)
                t_ref[i] = t_inv + _dot(t_inv, x).astype(BF16)
                return c

            lax.fori_loop(0, n_inst, level, 0, unroll=unroll)

        def right_side(i, c):
            q, _, sl = where_is(i)
            atm = stack(at_ref[q, sl, :])
            vm = stack(v_ref[q, sl, :])
            z0_hi, z0_lo = _split2(jnp.concatenate([atm, _mm(aak_ref[i], vm)], axis=1))
            rhs_ref[i] = jnp.concatenate([z0_hi, z0_lo], axis=0)
            return c

        lax.fori_loop(0, n_inst, right_side, 0, unroll=unroll)

        def apply(i, c):
            q, _, sl = where_is(i)
            vm = stack(v_ref[q, sl, :])
            t_inv = t_ref[i]
            z = _dot(t_inv, rhs_ref[i, :two, :]) + _dot(t_inv, rhs_ref[i, two:, :])
            rhs_ref[i] = jnp.concatenate([z, jnp.concatenate([jnp.zeros_like(vm), vm], axis=1)], axis=0).astype(BF16)
            return c

        lax.fori_loop(0, n_inst, apply, 0, unroll=unroll)

        def operators(i, c):
            q, j, sl = where_is(i)
            rhs2 = rhs_ref[i]
            k_all = g * n_inst + i
            top = _mm(rhs2, jnp.concatenate([stack(bh_ref[q, sl, :]), stack(kh_ref[q, sl, :])], axis=0), _TN)
            bot = _dot(ark_ref[i], rhs2)
            gm_ref[k_all] = (top[:two, :]
                             + jnp.where(eye, jnp.broadcast_to(pl_ref[q, j][0:1, :], (two, two)), 0.0)).astype(BF16)
            hc_ref[k_all] = top[two:, :]
            qe_ref[k_all] = (bot[:, :two] + stack(rt_ref[q, sl, :])).astype(BF16)
            yc_ref[k_all] = bot[:, two:]
            return c

        lax.fori_loop(0, n_inst, operators, 0, unroll=unroll)
        return carry

    lax.fori_loop(0, PAIRS // group, group_body, 0)

    if chained:
        states = [st_ref[q] for q in range(PAIRS)]
    for j in range(n_chunks):
        for q in range(PAIRS):
            k_all = (q // group) * n_inst + j * group + q % group
            st = (states[q] if chained else pair_state(wkv0_ref[j, q])).astype(BF16)
            ym = _mm(qe_ref[k_all], st, _NT) + yc_ref[k_all]
            y_ref[q, j * chunk:(j + 1) * chunk, :] = ym[:chunk, :] + ym[chunk:, :]
            st_new = _dot(st, gm_ref[k_all]) + hc_ref[k_all]
            if chained:
                states[q] = st_new
            else:
                wkv_out_ref[j, q] = head_states(st_new)
    if chained:
        for q in range(PAIRS):
            st_ref[q] = states[q]

    ys = jnp.concatenate([y_ref[q] for q in range(PAIRS)], axis=1)
    mean = _head_sum(ys, ones_blk) * (1.0 / HEAD)
    yc = ys - mean
    var = _head_sum(yc * yc, ones_blk) * (1.0 / HEAD)
    yn = yc * lax.rsqrt(var + GN_EPS) * lng_ref[...] + lnb_ref[...]
    bonus = _head_sum(r * k2 * rk_ref[...], ones_blk) * v
    yz_ref[...] = ((yn + bonus) * zg).reshape(yz_ref.shape)

    if chained:
        @pl.when(t == pl.num_programs(1) - 1)
        def _():
            shift_out_ref[0] = carry_ref[...]
            for q in range(PAIRS):
                wkv_out_ref[0, q] = head_states(st_ref[q])


def _specs(layer, state_layer, mod_row, seqs=None):
    assert seqs is None or mod_row % seqs == 0
    per_step = 1 if seqs is None else seqs

    def weight(a):
        return pl.BlockSpec((None,) + a.shape[1:], lambda b, t: (layer,) + (0,) * (a.ndim - 1))

    def state(a):
        return pl.BlockSpec((None, seqs) + a.shape[2:], lambda b, t: (state_layer, b) + (0,) * (a.ndim - 2))

    def mod(a):
        return pl.BlockSpec((None, seqs) + a.shape[2:], lambda b, t: (layer, mod_row // per_step + b, 0, 0))

    return weight, state, mod


def _pre_norm(x_ref, g_ref, mod_ref):
    g = g_ref[...]
    rows = [_norm_mod(x_ref[s], g, mod_ref[s]) for s in range(x_ref.shape[0])]
    return jnp.concatenate(rows, axis=0).astype(BF16)


def _rwkv_mixer(x, mod, g_pre, wr, wg, vecs, w2a, shift0, wkv0, *, sel, tile, t_valid, seqs):
    bsz, tp, _ = x.shape
    chunk = RWKV_CHUNK
    assert tp % tile == 0 and tile % chunk == 0 and bsz % seqs == 0
    assert t_valid == tp or tp == tile
    assert seqs == 1 or tp == chunk
    nt = tp // tile
    block_rows = tile
    tile = seqs * tile
    mu, w0, a0, k_k, k_a, r_k, ln_g, ln_b = vecs
    weight, state, mod_spec = _specs(*sel, seqs=seqs)
    pair_buf = pltpu.VMEM((PAIRS, tile, LANES), F32)
    group = min(PAIRS, max(1, RWKV_INSTANCES // (tile // chunk)))
    n_inst = group * (tile // chunk)
    two = 2 * chunk
    inst = lambda cols, dtype: pltpu.VMEM((n_inst, two, cols), dtype)
    every = lambda dtype: pltpu.VMEM((PAIRS * (tile // chunk), two, two), dtype)
    kern = functools.partial(_rwkv_kernel, tile=tile, chunk=chunk, t_valid=t_valid, group=group)
    return pl.pallas_call(
        kern,
        grid=(bsz // seqs, nt),
        in_specs=[
            pl.BlockSpec((seqs, block_rows, D_MODEL), lambda b, t: (b, t, 0)),
            mod_spec(mod),
            weight(g_pre), weight(wr), weight(wg), weight(mu), weight(w0), weight(w2a), weight(a0),
            weight(k_k), weight(k_a), weight(r_k), weight(ln_g), weight(ln_b),
            state(shift0), state(wkv0),
        ],
        out_specs=[
            pl.BlockSpec((seqs, block_rows, D_MODEL), lambda b, t: (b, t, 0)),
            pl.BlockSpec((seqs, 1, SHIFT_W), lambda b, t: (b, 0, 0)),
            pl.BlockSpec((seqs, PAIRS, LANES, HEAD), lambda b, t: (b, 0, 0, 0)),
        ],
        out_shape=[
            jax.ShapeDtypeStruct((bsz, tp, D_MODEL), F32),
            jax.ShapeDtypeStruct((bsz, 1, SHIFT_W), F32),
            jax.ShapeDtypeStruct((bsz, PAIRS, LANES, HEAD), F32),
        ],
        scratch_shapes=[
            pltpu.VMEM((1, SHIFT_W), F32),
            pltpu.VMEM((PAIRS, LANES, LANES), F32),
            pair_buf, pair_buf, pair_buf, pair_buf, pair_buf, pair_buf, pair_buf,
            pltpu.VMEM((PAIRS, tile // chunk, SUBLANES, LANES), F32),
            pair_buf,
            inst(two, BF16), inst(two, BF16), inst(two, BF16), inst(2 * two, BF16),
            pltpu.VMEM((n_inst, 2 * two, 2 * two), BF16),
            every(BF16), every(BF16), every(F32), every(F32),
            pltpu.VMEM((MXU_TILE, MXU_TILE), BF16),
        ],
        compiler_params=pltpu.CompilerParams(
            dimension_semantics=("arbitrary", "arbitrary"), vmem_limit_bytes=VMEM_LIMIT),
        name="rwkv7_mixer",
    )(x, mod, g_pre, wr, wg, mu, w0, w2a, a0, k_k, k_a, r_k, ln_g, ln_b, shift0, wkv0)


def _s5_kernel(x_ref, mod_ref, g_ref, wu_ref, wg_ref, bbr_ref, bbi_ref, cre_ref, cmi_ref, abr_ref, abi_ref,
               d_ref, gluw_ref, glub_ref, s0r_ref, s0i_ref,
               yz_ref, sr_out_ref, si_out_ref,
               hr_ref, hi_ref, cr_ref, ci_ref, *, tile):
    t = pl.program_id(1)

    @pl.when(t == 0)
    def _():
        cr_ref[...] = s0r_ref[...]
        ci_ref[...] = s0i_ref[...]

    seqs = x_ref.shape[0]
    frames = seqs * tile
    h = _pre_norm(x_ref, g_ref, mod_ref)
    u = _dot(h, wu_ref[...])
    zg = _silu(_dot(h, wg_ref[...]))
    set_tiles = S5_TILES // S5_SETS

    def frame_rows(c):
        return pl.ds(c, frames, stride=S5_ROW_STRIDE)

    in_tiles = bbr_ref.shape[2] // LANES
    for s in range(D_MODEL // LANES):
        ub = u[:, LANES * s:LANES * (s + 1)].astype(BF16)
        bur = _dot(ub, bbr_ref[s])
        bui = _dot(ub, bbi_ref[s])
        for c in range(in_tiles):
            hr_ref[frame_rows(s * in_tiles + c), :] = bur[:, LANES * c:LANES * (c + 1)]
            hi_ref[frame_rows(s * in_tiles + c), :] = bui[:, LANES * c:LANES * (c + 1)]

    abr = abr_ref[...]
    abi = abi_ref[...]

    def frame(f, carry):
        sr, si = carry
        rows = pl.ds(pl.multiple_of(f * S5_ROW_STRIDE, SUBLANES), S5_TILES)
        nr = abr * sr - abi * si + hr_ref[rows, :]
        ni = abr * si + abi * sr + hi_ref[rows, :]
        hr_ref[rows, :] = nr
        hi_ref[rows, :] = ni
        return nr, ni

    for s in range(seqs):
        cr_ref[s], ci_ref[s] = lax.fori_loop(s * tile, (s + 1) * tile, frame, (cr_ref[s], ci_ref[s]), unroll=16)

    outs = []
    for s in range(S5_SETS):
        hr = jnp.concatenate([hr_ref[frame_rows(s * set_tiles + c), :] for c in range(set_tiles)], axis=1)
        hi = jnp.concatenate([hi_ref[frame_rows(s * set_tiles + c), :] for c in range(set_tiles)], axis=1)
        outs.append(_dot(hr.astype(BF16), cre_ref[s]) + _dot(hi.astype(BF16), cmi_ref[s]))
    y = jnp.concatenate(outs, axis=1) + d_ref[...] * u
    y = 0.5 * y * (1.0 + jnp.tanh(0.7978845608028654 * (y + 0.044715 * (y * y * y))))
    y = y * _sigmoid(_dot(y.astype(BF16), gluw_ref[...]) + glub_ref[...])
    yz_ref[...] = (y * zg).reshape(seqs, tile, D_MODEL)

    @pl.when(t == pl.num_programs(1) - 1)
    def _():
        sr_out_ref[...] = cr_ref[...]
        si_out_ref[...] = ci_ref[...]


def _s5_mixer(x, mod, g_pre, wu, wg, bbr, bbi, cre, cmi, abr, abi, d_skip, glu_w, glu_b, s0r, s0i, *, sel, tile,
              seqs):
    bsz, tp, _ = x.shape
    assert tp % tile == 0 and tile % SUBLANES == 0 and bsz % seqs == 0
    nt = tp // tile
    weight, state, mod_spec = _specs(*sel, seqs=seqs)
    state_out = pl.BlockSpec((seqs, S5_TILES, LANES), lambda b, t: (b, 0, 0))
    kern = functools.partial(_s5_kernel, tile=tile)
    return pl.pallas_call(
        kern,
        grid=(bsz // seqs, nt),
        in_specs=[
            pl.BlockSpec((seqs, tile, D_MODEL), lambda b, t: (b, t, 0)),
            mod_spec(mod),
            weight(g_pre), weight(wu), weight(wg), weight(bbr), weight(bbi), weight(cre), weight(cmi),
            weight(abr), weight(abi), weight(d_skip), weight(glu_w), weight(glu_b), state(s0r), state(s0i),
        ],
        out_specs=[pl.BlockSpec((seqs, tile, D_MODEL), lambda b, t: (b, t, 0)), state_out, state_out],
        out_shape=[
            jax.ShapeDtypeStruct((bsz, tp, D_MODEL), F32),
            jax.ShapeDtypeStruct((bsz, S5_TILES, LANES), F32),
            jax.ShapeDtypeStruct((bsz, S5_TILES, LANES), F32),
        ],
        scratch_shapes=[
            pltpu.VMEM((seqs * tile * S5_ROW_STRIDE, LANES), F32),
            pltpu.VMEM((seqs * tile * S5_ROW_STRIDE, LANES), F32),
            pltpu.VMEM((seqs, S5_TILES, LANES), F32),
            pltpu.VMEM((seqs, S5_TILES, LANES), F32),
        ],
        compiler_params=pltpu.CompilerParams(
            dimension_semantics=("arbitrary", "arbitrary"), vmem_limit_bytes=VMEM_LIMIT),
        name="s5_mixer",
    )(x, mod, g_pre, wu, wg, bbr, bbi, cre, cmi, abr, abi, d_skip, glu_w, glu_b, s0r, s0i)


def _lru_kernel(x_ref, mod_ref, g_ref, wx_ref, wg_ref, cw_ref, cb_ref, wa4_ref, ba_ref, wx4_ref, bx_ref, lam_ref,
                conv0_ref, h0_ref,
                yz_ref, conv_out_ref, h_out_ref,
                xbuf_ref, a_ref, b_ref, hc_ref, *, tile):
    t = pl.program_id(1)
    pad = SUBLANES
    hist = CONV_W - 1
    seqs = x_ref.shape[0]

    @pl.when(t == 0)
    def _():
        xbuf_ref[:, pad - hist:pad, :] = conv0_ref[...]
        hc_ref[...] = h0_ref[...]

    h = _pre_norm(x_ref, g_ref, mod_ref)
    xl = _dot(h, wx_ref[...])
    zg = _silu(_dot(h, wg_ref[...]))
    xcs, tails = [], []
    for s in range(seqs):
        xs = xl[s * tile:(s + 1) * tile, :]
        xbuf_ref[s, pad:pad + tile, :] = xs
        acc = cb_ref[...] + xs * cw_ref[CONV_W - 1:CONV_W, :]
        for j in range(hist):
            acc = acc + xbuf_ref[s, pad - hist + j:pad - hist + j + tile, :] * cw_ref[j:j + 1, :]
        tails.append(xbuf_ref[s, pad + tile - hist:pad + tile, :])
        xbuf_ref[s, pad - hist:pad, :] = tails[s]
        xcs.append(acc)
    xc = jnp.concatenate(xcs, axis=0)

    ga, gx = [], []
    blk = MXU_TILE
    for s in range(D_MODEL // blk):
        xb = xc[:, blk * s:blk * (s + 1)].astype(BF16)
        ga.append(_dot(xb, wa4_ref[s]))
        gx.append(_dot(xb, wx4_ref[s]))
    gate_r = _sigmoid(jnp.concatenate(ga, axis=1) + ba_ref[...])
    gate_i = _sigmoid(jnp.concatenate(gx, axis=1) + bx_ref[...])
    log_a = -LRU_C * gate_r * _softplus(-lam_ref[...])
    a = jnp.exp(log_a)
    a_ref[...] = a
    b_ref[...] = jnp.sqrt(-jnp.tanh(log_a) * (a * a + 1.0)) * (gate_i * xc)

    row8 = lax.broadcasted_iota(jnp.int32, (SUBLANES, 1), 0)

    def row_body(g, carry):
        rs = pl.ds(pl.multiple_of(g * SUBLANES, SUBLANES), SUBLANES)
        av = a_ref[rs, :]
        bv = b_ref[rs, :]
        for sh in (1, 2, 4):
            keep = row8 >= sh
            ash = jnp.where(keep, pltpu.roll(av, sh, 0), 1.0)
            bsh = jnp.where(keep, pltpu.roll(bv, sh, 0), 0.0)
            bv = bv + av * bsh
            av = av * ash
        hs = bv + av * carry
        b_ref[rs, :] = hs
        return hs[SUBLANES - 1:SUBLANES, :]

    groups = tile // SUBLANES
    for s in range(seqs):
        hc_ref[s] = lax.fori_loop(s * groups, (s + 1) * groups, row_body, hc_ref[s])
    yz_ref[...] = (b_ref[...] * zg).reshape(seqs, tile, D_MODEL)

    @pl.when(t == pl.num_programs(1) - 1)
    def _():
        for s in range(seqs):
            conv_out_ref[s] = tails[s]
        h_out_ref[...] = hc_ref[...]


def _lru_mixer(x, mod, g_pre, wx, wg, conv_w, conv_b, wa4, ba, wx4, bx, lam, conv0, h0, *, sel, tile, seqs):
    bsz, tp, _ = x.shape
    assert tp % tile == 0 and tile % SUBLANES == 0 and tile >= SUBLANES and bsz % seqs == 0
    nt = tp // tile
    weight, state, mod_spec = _specs(*sel, seqs=seqs)
    kern = functools.partial(_lru_kernel, tile=tile)
    return pl.pallas_call(
        kern,
        grid=(bsz // seqs, nt),
        in_specs=[
            pl.BlockSpec((seqs, tile, D_MODEL), lambda b, t: (b, t, 0)),
            mod_spec(mod),
            weight(g_pre), weight(wx), weight(wg), weight(conv_w), weight(conv_b), weight(wa4), weight(ba),
            weight(wx4), weight(bx), weight(lam), state(conv0), state(h0),
        ],
        out_specs=[
            pl.BlockSpec((seqs, tile, D_MODEL), lambda b, t: (b, t, 0)),
            pl.BlockSpec((seqs, CONV_W - 1, D_MODEL), lambda b, t: (b, 0, 0)),
            pl.BlockSpec((seqs, 1, D_MODEL), lambda b, t: (b, 0, 0)),
        ],
        out_shape=[
            jax.ShapeDtypeStruct((bsz, tp, D_MODEL), F32),
            jax.ShapeDtypeStruct((bsz, CONV_W - 1, D_MODEL), F32),
            jax.ShapeDtypeStruct((bsz, 1, D_MODEL), F32),
        ],
        scratch_shapes=[
            pltpu.VMEM((seqs, tile + SUBLANES, D_MODEL), F32),
            pltpu.VMEM((seqs * tile, D_MODEL), F32),
            pltpu.VMEM((seqs * tile, D_MODEL), F32),
            pltpu.VMEM((seqs, 1, D_MODEL), F32),
        ],
        compiler_params=pltpu.CompilerParams(
            dimension_semantics=("arbitrary", "arbitrary"), vmem_limit_bytes=VMEM_LIMIT),
        name="rglru_mixer",
    )(x, mod, g_pre, wx, wg, conv_w, conv_b, wa4, ba, wx4, bx, lam, conv0, h0)


def _merge_kernel(x_ref, mod_ref, g_ref, gpost_ref, wm_ref, wo_ref, yr_ref, ys_ref, yl_ref, o_ref):
    seqs, tile, _ = x_ref.shape
    rows = lambda ref: ref[...].reshape(seqs * tile, D_MODEL)
    m = _sigmoid(_dot(_pre_norm(x_ref, g_ref, mod_ref), wm_ref[...]))
    merged = (m[:, :D_MODEL] * rows(yr_ref) + m[:, D_MODEL:2 * D_MODEL] * rows(ys_ref)
              + m[:, 2 * D_MODEL:] * rows(yl_ref))
    o = _dot(merged.astype(BF16), wo_ref[...])
    ms = jnp.mean(o * o, axis=-1, keepdims=True)
    o = (o * lax.rsqrt(ms + RMS_EPS)) * gpost_ref[...]
    for s in range(seqs):
        o_ref[s] = x_ref[s] + mod_ref[s][2:3, :] * o[s * tile:(s + 1) * tile, :]


def _merge_out(x, mod, g_pre, g_post, wm, wo, yr, ys, yl, *, sel, tile, seqs):
    bsz, tp, _ = x.shape
    assert tp % tile == 0 and bsz % seqs == 0
    nt = tp // tile
    act = pl.BlockSpec((seqs, tile, D_MODEL), lambda b, t: (b, t, 0))
    weight, _, mod_spec = _specs(*sel, seqs=seqs)
    return pl.pallas_call(
        _merge_kernel,
        grid=(bsz // seqs, nt),
        in_specs=[act, mod_spec(mod), weight(g_pre), weight(g_post), weight(wm), weight(wo), act, act, act],
        out_specs=act,
        out_shape=jax.ShapeDtypeStruct((bsz, tp, D_MODEL), F32),
        compiler_params=pltpu.CompilerParams(
            dimension_semantics=("arbitrary", "arbitrary"), vmem_limit_bytes=VMEM_LIMIT),
        name="merge_out",
    )(x, mod, g_pre, g_post, wm, wo, yr, ys, yl)


def _block_diag_expand(w, per):
    depth, n, a, b = w.shape
    w = w.reshape(depth, n // per, per, a, b)
    eye = jnp.eye(per, dtype=w.dtype)
    return jnp.einsum("lsgab,gh->lsgahb", w, eye).reshape(depth, n // per, per * a, per * b)


def _layer(x, mod, state, prm, *, sel, t_valid, tiles):
    shift_row, wkv, s_re, s_im, lru_h, lru_conv = state
    bsz = x.shape[0]
    g_pre = prm["g_pre"]

    xr = x
    if x.shape[1] % tiles["rwkv"] != 0:
        xr = jnp.pad(x, ((0, 0), (0, tiles["rwkv"] - x.shape[1] % tiles["rwkv"]), (0, 0)))
    yz_r, shift_new, wkv_new = _rwkv_mixer(
        xr, mod, g_pre, prm["w_rwkv"], prm["w_rwkv_g"], prm["rwkv_vecs"], prm["w2a"], shift_row, wkv,
        sel=sel, tile=tiles["rwkv"], t_valid=t_valid, seqs=tiles["rwkv_seqs"])
    yz_r = yz_r[:, :t_valid]
    yz_s, s_re_new, s_im_new = _s5_mixer(
        x, mod, g_pre, prm["w_s5"], prm["w_s5_g"], prm["bbr"], prm["bbi"], prm["cre"], prm["cmi"],
        prm["abr"], prm["abi"], prm["s5_d"], prm["glu_w"], prm["glu_b"], s_re, s_im, sel=sel, tile=tiles["s5"],
        seqs=tiles["seqs"])
    yz_l, conv_new, h_new = _lru_mixer(
        x, mod, g_pre, prm["w_lru"], prm["w_lru_g"], prm["conv_w"], prm["conv_b"], prm["wa4"], prm["ba"],
        prm["wx4"], prm["bx"], prm["lam"], lru_conv, lru_h, sel=sel, tile=tiles["lru"], seqs=tiles["seqs"])
    x_new = _merge_out(x, mod, g_pre, prm["g_post"], prm["w_merge"], prm["w_out"], yz_r, yz_s, yz_l,
                       sel=sel, tile=tiles["merge"], seqs=tiles["seqs"])
    new_state = (shift_new.reshape(bsz, SHIFT_W), wkv_new.reshape(bsz, HEADS, HEAD, HEAD),
                 s_re_new.reshape(bsz, S5_GROUPS, S5_STATES), s_im_new.reshape(bsz, S5_GROUPS, S5_STATES),
                 h_new.reshape(bsz, D_MODEL), conv_new)
    return x_new, new_state


def _stacked_params(w_in, w_out, norm_pre, norm_post, rwkv, s5, lru):
    depth = w_in.shape[0]
    d = D_MODEL
    wb = w_in.astype(BF16)
    o = SHIFT_W
    row = lambda a: a.reshape(depth, 1, -1)
    mu, w0, w2, a0, a2, k_k, k_a, r_k, ln_g, ln_b = rwkv
    a_re, a_im, log_step, b_re, b_im, c_re, c_im, s5_d, glu_w, glu_b = s5
    conv_w, conv_b, wa, ba, wx, bx, lam = lru
    zeros = jnp.zeros((depth, LORA, d), F32)
    w2a = jnp.concatenate([jnp.concatenate([w2, zeros], axis=2), jnp.concatenate([zeros, a2], axis=2)], axis=1)
    abr, abi, bbr, bbi = _s5_prep(a_re, a_im, log_step, b_re, b_im)
    per = S5_GROUPS // S5_SETS
    gate_per = MXU_TILE // LRU_BLOCK
    return dict(
        g_pre=row(norm_pre), g_post=row(norm_post),
        w_rwkv=wb[:, :, :o], w_rwkv_g=wb[:, :, o:o + d],
        w_s5=wb[:, :, o + d:o + 2 * d], w_s5_g=wb[:, :, o + 2 * d:o + 3 * d],
        w_lru=wb[:, :, o + 3 * d:o + 4 * d], w_lru_g=wb[:, :, o + 4 * d:o + 5 * d],
        w_merge=wb[:, :, o + 5 * d:], w_out=w_out.astype(BF16),
        rwkv_vecs=(row(mu), row(w0), row(a0), row(k_k), row(k_a), row(r_k), row(ln_g), row(ln_b)),
        w2a=w2a.astype(BF16),
        bbr=_block_diag_expand(bbr, S5_IN_GROUPS).astype(BF16),
        bbi=_block_diag_expand(bbi, S5_IN_GROUPS).astype(BF16),
        cre=_block_diag_expand(jnp.swapaxes(c_re, 2, 3), per).astype(BF16),
        cmi=_block_diag_expand(-jnp.swapaxes(c_im, 2, 3), per).astype(BF16),
        abr=abr.reshape(depth, S5_TILES, LANES), abi=abi.reshape(depth, S5_TILES, LANES),
        s5_d=row(s5_d), glu_w=glu_w.astype(BF16), glu_b=row(glu_b),
        conv_w=conv_w, conv_b=row(conv_b),
        wa4=_block_diag_expand(wa, gate_per).astype(BF16), ba=row(ba),
        wx4=_block_diag_expand(wx, gate_per).astype(BF16), bx=row(bx), lam=row(lam),
    )


def _state_layout(shift, wkv, s_re, s_im, lru_h, lru_conv):
    n, bsz = shift.shape[:2]
    return (shift.reshape(n, bsz, 1, SHIFT_W), wkv.reshape(n, bsz, PAIRS, LANES, HEAD),
            s_re.reshape(n, bsz, S5_TILES, LANES), s_im.reshape(n, bsz, S5_TILES, LANES),
            lru_h.reshape(n, bsz, 1, D_MODEL), lru_conv)


def _tiles(t, bsz):
    pick = lambda want: want if t % want == 0 else t
    share = lambda rows, limit: max(n for n in range(1, bsz + 1) if bsz % n == 0 and n * rows <= max(limit, rows))
    seqs = share(t, SHORT_ROWS) if t < SHORT_ROWS else 1
    rwkv = RWKV_TILE if t % RWKV_TILE == 0 else RWKV_CHUNK
    rwkv_seqs = share(RWKV_CHUNK, RWKV_TILE) if t <= RWKV_CHUNK else 1
    return dict(rwkv=rwkv, rwkv_seqs=rwkv_seqs, s5=pick(S5_TILE), lru=pick(WIDE_TILE), merge=pick(WIDE_TILE),
                seqs=seqs)


def kernel(x_prompt, x_sample, state_rwkv_shift, state_rwkv_wkv, state_s5_re, state_s5_im, state_lru_h,
           state_lru_conv, c_prompt, c_sample, ada_w, ada_b, norm_pre, norm_post, w_in, w_out, rwkv_mu, rwkv_w0,
           rwkv_w2, rwkv_a0, rwkv_a2, rwkv_k_k, rwkv_k_a, rwkv_r_k, rwkv_ln_g, rwkv_ln_b, s5_a_re, s5_a_im,
           s5_log_step, s5_b_re, s5_b_im, s5_c_re, s5_c_im, s5_d, s5_glu_w, s5_glu_b, lru_conv_w, lru_conv_b,
           lru_wa, lru_ba, lru_wx, lru_bx, lru_lambda):
    depth = w_in.shape[0]
    bp, tp, _ = x_prompt.shape
    bs, ts, _ = x_sample.shape
    c_all = jnp.concatenate([c_sample, c_prompt], axis=0)
    pad = (-c_all.shape[0]) % SUBLANES
    c_all = jnp.pad(c_all, ((0, pad), (0, 0)))
    mod_all = _modulation(c_all, ada_w, ada_b)
    mod_all = mod_all.reshape(depth, c_all.shape[0], 3, D_MODEL)
    rwkv = (rwkv_mu, rwkv_w0, rwkv_w2, rwkv_a0, rwkv_a2, rwkv_k_k, rwkv_k_a,
            rwkv_r_k.reshape(depth, D_MODEL), rwkv_ln_g, rwkv_ln_b)
    s5 = (s5_a_re, s5_a_im, s5_log_step, s5_b_re, s5_b_im, s5_c_re, s5_c_im, s5_d, s5_glu_w, s5_glu_b)
    lru = (lru_conv_w, lru_conv_b, lru_wa, lru_ba, lru_wx, lru_bx, lru_lambda)
    prm = _stacked_params(w_in, w_out, norm_pre, norm_post, rwkv, s5, lru)
    zero = _state_layout(jnp.zeros((1, bp, SHIFT_W), F32), jnp.zeros((1, bp, HEADS, HEAD, HEAD), F32),
                         jnp.zeros((1, bp, S5_GROUPS, S5_STATES), F32),
                         jnp.zeros((1, bp, S5_GROUPS, S5_STATES), F32),
                         jnp.zeros((1, bp, D_MODEL), F32), jnp.zeros((1, bp, CONV_W - 1, D_MODEL), F32))
    st_in = _state_layout(state_rwkv_shift, state_rwkv_wkv, state_s5_re, state_s5_im, state_lru_h,
                          state_lru_conv)

    xp, xs = x_prompt, x_sample
    new_p = [[] for _ in range(6)]
    new_s = [[] for _ in range(6)]
    for l in range(depth):
        xp, st_p = _layer(xp, mod_all, zero, prm, sel=(l, 0, bs), t_valid=tp, tiles=_tiles(tp, bp))
        xs, st_s = _layer(xs, mod_all, st_in, prm, sel=(l, l, 0), t_valid=ts, tiles=_tiles(ts, bs))
        for i in range(6):
            new_p[i].append(st_p[i])
            new_s[i].append(st_s[i])
    sp = [jnp.stack(z, axis=0) for z in new_p]
    ss = [jnp.stack(z, axis=0) for z in new_s]
    return (xp, xs, sp[0], sp[1], sp[2], sp[3], sp[4], sp[5], ss[0], ss[1], ss[2], ss[3], ss[4], ss[5])
```

```python
import functools

import jax
import jax.numpy as jnp
from jax import lax
from jax.experimental import pallas as pl
from jax.experimental.pallas import tpu as pltpu

F32 = jnp.float32
BF16 = jnp.bfloat16
HIGHEST = lax.Precision.HIGHEST

SUBLANES = 8
LANES = 128
MXU_TILE = 256
VMEM_LIMIT = 56 * 1024 * 1024

D_MODEL = 1024
HEAD = 64
HEADS = D_MODEL // HEAD
PAIRS = HEADS // 2
LORA = 64
SHIFT_W = 3 * D_MODEL + 2 * LORA
S5_GROUPS = 64
S5_STATES = 64
S5_STATE_W = S5_GROUPS * S5_STATES
LRU_BLOCK = 64
S5_SETS = 4
S5_IN_GROUPS = 8
S5_TILES = S5_STATE_W // LANES
S5_ROW_STRIDE = 40
CONV_W = 4
LRU_C = 8.0
RMS_EPS = 1e-6
GN_EPS = 64e-5
RWKV_CHUNK = 64
RWKV_TILE = 256
S5_TILE = 256
WIDE_TILE = 512
SHORT_ROWS = 512
DECAY_SCALE = 0.6065306597126334
RWKV_INSTANCES = 16


def _sigmoid(x):
    return 0.5 * jnp.tanh(0.5 * x) + 0.5


def _silu(x):
    return x * _sigmoid(x)


def _softplus(x):
    return jnp.maximum(x, 0.0) + jnp.log1p(jnp.exp(-jnp.abs(x)))


def _norm_mod(x, g, mod):
    ms = jnp.mean(x * x, axis=-1, keepdims=True)
    return (x * lax.rsqrt(ms + RMS_EPS)) * (g * (1.0 + mod[1:2, :])) + mod[0:1, :]


def _dot(a, b):
    return jnp.dot(a, b, preferred_element_type=F32)


def _dot_hi(a, b):
    return jnp.dot(a, b, preferred_element_type=F32, precision=HIGHEST)


def _split2(x):
    hi = x.astype(BF16)
    lo = (x - hi.astype(F32)).astype(BF16)
    return hi, lo


_NN = (((1,), (0,)), ((), ()))
_NT = (((1,), (1,)), ((), ()))
_TN = (((0,), (0,)), ((), ()))


def _mm(a, b, dims=_NN):
    return lax.dot_general(a.astype(BF16), b.astype(BF16), dims, preferred_element_type=F32)


def _head_sum(x, ones_blk):
    outs = []
    for s in range(D_MODEL // MXU_TILE):
        outs.append(_dot(x[:, MXU_TILE * s:MXU_TILE * (s + 1)].astype(BF16), ones_blk))
    return jnp.concatenate(outs, axis=1)


def _block_ones(n, blk):
    ri = lax.broadcasted_iota(jnp.int32, (n, n), 0)
    ci = lax.broadcasted_iota(jnp.int32, (n, n), 1)
    sh = blk.bit_length() - 1
    return ((ri >> sh) == (ci >> sh)).astype(F32).astype(BF16)


def _mod_kernel(c_ref, w_ref, b_ref, o_ref):
    s = _silu(c_ref[...])
    o_ref[...] = _dot_hi(s, w_ref[...]) + b_ref[...]


def _modulation(c_all, ada_w, ada_b):
    depth = ada_w.shape[0]
    rows = c_all.shape[0]
    return pl.pallas_call(
        _mod_kernel,
        grid=(depth, 3),
        in_specs=[
            pl.BlockSpec((rows, D_MODEL), lambda l, j: (0, 0)),
            pl.BlockSpec((None, D_MODEL, D_MODEL), lambda l, j: (l, 0, j)),
            pl.BlockSpec((None, 1, D_MODEL), lambda l, j: (l, 0, j)),
        ],
        out_specs=pl.BlockSpec((None, rows, D_MODEL), lambda l, j: (l, 0, j)),
        out_shape=jax.ShapeDtypeStruct((depth, rows, 3 * D_MODEL), F32),
        name="adaln_mod",
    )(c_all, ada_w, ada_b.reshape(depth, 1, 3 * D_MODEL))


def _s5_prep_kernel(are_ref, aim_ref, ls_ref, bre_ref, bim_ref, abr_ref, abi_ref, bbr_ref, bbi_ref):
    are = are_ref[...]
    aim = aim_ref[...]
    dt = jnp.exp(ls_ref[...])
    mag = jnp.exp(are * dt)
    abr = mag * jnp.cos(aim * dt)
    abi = mag * jnp.sin(aim * dt)
    abr_ref[...] = abr
    abi_ref[...] = abi
    nr = abr - 1.0
    ni = abi
    den = are * are + aim * aim
    cr = (nr * are + ni * aim) / den
    ci = (ni * are - nr * aim) / den
    bre = bre_ref[...]
    bim = bim_ref[...]
    bbr_ref[...] = cr[:, None, :] * bre - ci[:, None, :] * bim
    bbi_ref[...] = cr[:, None, :] * bim + ci[:, None, :] * bre


def _s5_prep(a_re, a_im, log_step, b_re, b_im):
    depth, g, p = a_re.shape
    i = b_re.shape[-1]
    layer = lambda *dims: pl.BlockSpec((None,) + dims, lambda l: (l,) + (0,) * len(dims))
    return pl.pallas_call(
        _s5_prep_kernel,
        grid=(depth,),
        in_specs=[layer(g, p), layer(g, p), layer(g, 1), layer(g, i, p), layer(g, i, p)],
        out_specs=(layer(g, p), layer(g, p), layer(g, i, p), layer(g, i, p)),
        out_shape=(
            jax.ShapeDtypeStruct((depth, g, p), F32),
            jax.ShapeDtypeStruct((depth, g, p), F32),
            jax.ShapeDtypeStruct((depth, g, i, p), F32),
            jax.ShapeDtypeStruct((depth, g, i, p), F32),
        ),
        name="s5_discretise",
    )(a_re, a_im, log_step.reshape(depth, g, 1), jnp.swapaxes(b_re, 2, 3), jnp.swapaxes(b_im, 2, 3))


def _rwkv_kernel(x_ref, mod_ref, g_ref, wr_ref, wg_ref, mu_ref, w0_ref, w2a_ref, a0_ref, kk_ref, ka_ref,
                 rk_ref, lng_ref, lnb_ref, shift0_ref, wkv0_ref,
                 yz_ref, shift_out_ref, wkv_out_ref,
                 carry_ref, st_ref, at_ref, rt_ref, bt_ref, kt_ref, bh_ref, kh_ref, v_ref, pl_ref, y_ref,
                 nh_ref, t_ref, aak_ref, ark_ref, rhs_ref, qg_ref, hc_ref, yc_ref, ones_ref,
                 *, tile, chunk, t_valid, group):
    t = pl.program_id(1)
    n_chunks = tile // chunk
    two = 2 * chunk
    n_inst = group * n_chunks
    chained = x_ref.shape[0] == 1

    def pair_state(s0):
        rh = lax.broadcasted_iota(jnp.int32, (LANES, LANES), 0) >> (HEAD.bit_length() - 1)
        ch = lax.broadcasted_iota(jnp.int32, (LANES, LANES), 1) >> (HEAD.bit_length() - 1)
        return jnp.where(rh == ch, jnp.concatenate([s0, s0], axis=1), 0.0).T

    def head_states(st):
        sv = st.T
        return sv[:, :HEAD] + sv[:, HEAD:]

    if chained:
        @pl.when(t == 0)
        def _():
            carry_ref[...] = shift0_ref[0]
            for q in range(PAIRS):
                st_ref[q] = pair_state(wkv0_ref[0, q])

    h = _pre_norm(x_ref, g_ref, mod_ref)
    p = _dot(h, wr_ref[...])
    zg = _silu(_dot(h, wg_ref[...]))

    rows = lax.broadcasted_iota(jnp.int32, (tile, 1), 0)
    rmod = rows & (chunk - 1)
    last = (t_valid - 1) % chunk
    if chained:
        prev = jnp.where(rows == 0, carry_ref[...], pltpu.roll(p, 1, 0))
        carry_ref[...] = p[tile - chunk + last:tile - chunk + last + 1, :]
    else:
        first = jnp.concatenate([jnp.broadcast_to(shift0_ref[s], (chunk, SHIFT_W)) for s in range(n_chunks)], axis=0)
        prev = jnp.where(rmod == 0, first, pltpu.roll(p, 1, 0))
        for s in range(n_chunks):
            shift_out_ref[s] = p[s * chunk + last:s * chunk + last + 1, :]
    pm = p + (prev - p) * mu_ref[...]

    r = pm[:, 0:D_MODEL]
    k = pm[:, D_MODEL:2 * D_MODEL]
    v = pm[:, 2 * D_MODEL:3 * D_MODEL]
    wa = pm[:, 3 * D_MODEL:SHIFT_W]
    lane = lax.broadcasted_iota(jnp.int32, (1, LANES), 1)
    wa = jnp.where(lane < LORA, jnp.tanh(wa), wa)
    lora = _dot(wa.astype(BF16), w2a_ref[...])
    logd = -DECAY_SCALE * _sigmoid(w0_ref[...] + lora[:, :D_MODEL])
    a_sig = _sigmoid(a0_ref[...] + lora[:, D_MODEL:])

    @pl.when(t == 0)
    def _():
        ones_ref[...] = _block_ones(MXU_TILE, HEAD)

    ones_blk = ones_ref[...]
    kk = k * kk_ref[...]
    kk = kk * jnp.minimum(lax.rsqrt(_head_sum(kk * kk, ones_blk)), 1e12)
    k2 = k * (1.0 + (a_sig - 1.0) * ka_ref[...])
    bvec = kk * a_sig
    if t_valid % chunk != 0:
        ok = rmod < t_valid % chunk
        logd = jnp.where(ok, logd, 0.0)
        bvec = jnp.where(ok, bvec, 0.0)
        k2 = jnp.where(ok, k2, 0.0)

    cum = logd
    s = 1
    while s < chunk:
        cum = cum + jnp.where(rmod >= s, pltpu.roll(cum, s, 0), 0.0)
        s *= 2
    cum_last = jnp.concatenate(
        [jnp.broadcast_to(cum[(j + 1) * chunk - 1:(j + 1) * chunk, :], (chunk, D_MODEL)) for j in range(n_chunks)],
        axis=0)
    p_inc = jnp.exp(cum)
    p_inv = jnp.exp(-cum)
    p_exc = jnp.exp(cum - logd)
    p_end = jnp.exp(cum_last - cum)

    def to_pairs(ref, val):
        for q in range(PAIRS):
            ref[q] = val[:, LANES * q:LANES * (q + 1)]

    to_pairs(at_ref, -kk * p_exc)
    to_pairs(rt_ref, r * p_inc)
    to_pairs(bt_ref, bvec * p_inv)
    to_pairs(kt_ref, k2 * p_inv)
    to_pairs(bh_ref, bvec * p_end)
    to_pairs(kh_ref, k2 * p_end)
    to_pairs(v_ref, v)
    for j in range(n_chunks):
        row = jnp.exp(cum[(j + 1) * chunk - 1:(j + 1) * chunk, :])
        for q in range(PAIRS):
            pl_ref[q, j] = jnp.broadcast_to(row[:, LANES * q:LANES * (q + 1)], (SUBLANES, LANES))

    m_a = (lane < HEAD).astype(F32)
    m_b = 1.0 - m_a
    ri = lax.broadcasted_iota(jnp.int32, (two, two), 0)
    ci = lax.broadcasted_iota(jnp.int32, (two, two), 1)
    csh = chunk.bit_length() - 1
    same_head = (ri >> csh) == (ci >> csh)
    strict = jnp.logical_and(same_head, (ri & (chunk - 1)) > (ci & (chunk - 1)))
    incl = jnp.logical_and(same_head, (ri & (chunk - 1)) >= (ci & (chunk - 1)))
    eye = ri == ci
    head_a = ri < chunk
    off_masks = []
    for lg in range(csh):
        off_masks.append(jnp.logical_and(strict, jnp.logical_and((ri >> (lg + 1)) == (ci >> (lg + 1)),
                                                                 (ri >> lg) != (ci >> lg))))

    def stack(x):
        return jnp.concatenate([x * m_a, x * m_b], axis=0)

    gsh = group.bit_length() - 1
    unroll = min(16, n_inst)

    def group_body(g, carry):
        def where_is(i):
            q = g * group + (i & (group - 1))
            j = i >> gsh
            return q, j, pl.ds(pl.multiple_of(j * chunk, chunk), chunk)

        def gram(i, c):
            q, _, sl = where_is(i)
            atm = stack(at_ref[q, sl, :])
            rtm = stack(rt_ref[q, sl, :])
            bt = bt_ref[q, sl, :]
            kt = kt_ref[q, sl, :]
            o1 = _mm(jnp.concatenate([atm, rtm], axis=0), jnp.concatenate([bt, kt], axis=0), _NT)
            o1_sw = pltpu.roll(o1, chunk, 1)
            n_ab = jnp.where(strict, jnp.where(head_a, o1[:two], o1_sw[:two]), 0.0)
            nh_ref[i] = n_ab.astype(BF16)
            t_ref[i] = jnp.where(eye, 1.0, jnp.where(off_masks[0], n_ab, 0.0)).astype(BF16)
            aak_ref[i] = jnp.where(strict, jnp.where(head_a, o1_sw[:two], o1[:two]), 0.0).astype(BF16)
            ark_ref[i] = jnp.concatenate([jnp.where(incl, jnp.where(head_a, o1[two:], o1_sw[two:]), 0.0),
                                          jnp.where(incl, jnp.where(head_a, o1_sw[two:], o1[two:]), 0.0)],
                                         axis=1).astype(BF16)
            return c

        lax.fori_loop(0, n_inst, gram, 0, unroll=unroll)

        for off in off_masks[1:]:
            def level(i, c, off=off):
                t_inv = t_ref[i]
                x = _dot(jnp.where(off, nh_ref[i], 0.0), t_inv).astype(BF16)
                t_ref[i] = t_inv + _dot(t_inv, x).astype(BF16)
                return c

            lax.fori_loop(0, n_inst, level, 0, unroll=unroll)

        def right_side(i, c):
            q, _, sl = where_is(i)
            atm = stack(at_ref[q, sl, :])
            vm = stack(v_ref[q, sl, :])
            z0_hi, z0_lo = _split2(jnp.concatenate([atm, _mm(aak_ref[i], vm)], axis=1))
            rhs_ref[i] = jnp.concatenate([z0_hi, z0_lo], axis=0)
            return c

        lax.fori_loop(0, n_inst, right_side, 0, unroll=unroll)

        def apply(i, c):
            q, _, sl = where_is(i)
            vm = stack(v_ref[q, sl, :])
            t_inv = t_ref[i]
            z = _dot(t_inv, rhs_ref[i, :two, :]) + _dot(t_inv, rhs_ref[i, two:, :])
            rhs_ref[i] = jnp.concatenate([z, jnp.concatenate([jnp.zeros_like(vm), vm], axis=1)], axis=0).astype(BF16)
            return c

        lax.fori_loop(0, n_inst, apply, 0, unroll=unroll)

        def operators(i, c):
            q, j, sl = where_is(i)
            rhs2 = rhs_ref[i]
            k_all = g * n_inst + i
            top = _mm(jnp.concatenate([stack(bh_ref[q, sl, :]), stack(kh_ref[q, sl, :])], axis=0), rhs2, _TN)
            bot = _dot(ark_ref[i], rhs2)
            g_mat = top[:, :two] + jnp.where(eye, jnp.broadcast_to(pl_ref[q, j][0:1, :], (two, two)), 0.0)
            q_eff = bot[:, :two] + stack(rt_ref[q, sl, :])
            qg_ref[k_all] = jnp.concatenate([q_eff, g_mat], axis=0).astype(BF16)
            hc_ref[k_all] = top[:, two:]
            yc_ref[k_all] = bot[:, two:]
            return c

        lax.fori_loop(0, n_inst, operators, 0, unroll=unroll)
        return carry

    lax.fori_loop(0, PAIRS // group, group_body, 0)

    if chained:
        states = [st_ref[q] for q in range(PAIRS)]
    for j in range(n_chunks):
        for q in range(PAIRS):
            k_all = (q // group) * n_inst + j * group + q % group
            st = (states[q] if chained else pair_state(wkv0_ref[j, q])).astype(BF16)
            both = _dot(qg_ref[k_all], st)
            ym = both[:two, :] + yc_ref[k_all]
            y_ref[q, j * chunk:(j + 1) * chunk, :] = ym[:chunk, :] + ym[chunk:, :]
            st_new = both[two:, :] + hc_ref[k_all]
            if chained:
                states[q] = st_new
            else:
                wkv_out_ref[j, q] = head_states(st_new)
    if chained:
        for q in range(PAIRS):
            st_ref[q] = states[q]

    ys = jnp.concatenate([y_ref[q] for q in range(PAIRS)], axis=1)
    mean = _head_sum(ys, ones_blk) * (1.0 / HEAD)
    yc = ys - mean
    var = _head_sum(yc * yc, ones_blk) * (1.0 / HEAD)
    yn = yc * lax.rsqrt(var + GN_EPS) * lng_ref[...] + lnb_ref[...]
    bonus = _head_sum(r * k2 * rk_ref[...], ones_blk) * v
    yz_ref[...] = ((yn + bonus) * zg).reshape(yz_ref.shape)

    if chained:
        @pl.when(t == pl.num_programs(1) - 1)
        def _():
            shift_out_ref[0] = carry_ref[...]
            for q in range(PAIRS):
                wkv_out_ref[0, q] = head_states(st_ref[q])


def _specs(layer, state_layer, mod_row, seqs=None):
    assert seqs is None or mod_row % seqs == 0
    per_step = 1 if seqs is None else seqs

    def weight(a):
        return pl.BlockSpec((None,) + a.shape[1:], lambda b, t: (layer,) + (0,) * (a.ndim - 1))

    def state(a):
        return pl.BlockSpec((None, seqs) + a.shape[2:], lambda b, t: (state_layer, b) + (0,) * (a.ndim - 2))

    def mod(a):
        return pl.BlockSpec((None, seqs) + a.shape[2:], lambda b, t: (layer, mod_row // per_step + b, 0, 0))

    return weight, state, mod


def _pre_norm(x_ref, g_ref, mod_ref):
    g = g_ref[...]
    rows = [_norm_mod(x_ref[s], g, mod_ref[s]) for s in range(x_ref.shape[0])]
    return jnp.concatenate(rows, axis=0).astype(BF16)


def _rwkv_mixer(x, mod, g_pre, wr, wg, vecs, w2a, shift0, wkv0, *, sel, tile, t_valid, seqs):
    bsz, tp, _ = x.shape
    chunk = RWKV_CHUNK
    assert tp % tile == 0 and tile % chunk == 0 and bsz % seqs == 0
    assert t_valid == tp or tp == tile
    assert seqs == 1 or tp == chunk
    nt = tp // tile
    block_rows = tile
    tile = seqs * tile
    mu, w0, a0, k_k, k_a, r_k, ln_g, ln_b = vecs
    weight, state, mod_spec = _specs(*sel, seqs=seqs)
    pair_buf = pltpu.VMEM((PAIRS, tile, LANES), F32)
    group = min(PAIRS, max(1, RWKV_INSTANCES // (tile // chunk)))
    n_inst = group * (tile // chunk)
    two = 2 * chunk
    inst = lambda cols, dtype: pltpu.VMEM((n_inst, two, cols), dtype)
    every = lambda dtype: pltpu.VMEM((PAIRS * (tile // chunk), two, two), dtype)
    kern = functools.partial(_rwkv_kernel, tile=tile, chunk=chunk, t_valid=t_valid, group=group)
    return pl.pallas_call(
        kern,
        grid=(bsz // seqs, nt),
        in_specs=[
            pl.BlockSpec((seqs, block_rows, D_MODEL), lambda b, t: (b, t, 0)),
            mod_spec(mod),
            weight(g_pre), weight(wr), weight(wg), weight(mu), weight(w0), weight(w2a), weight(a0),
            weight(k_k), weight(k_a), weight(r_k), weight(ln_g), weight(ln_b),
            state(shift0), state(wkv0),
        ],
        out_specs=[
            pl.BlockSpec((seqs, block_rows, D_MODEL), lambda b, t: (b, t, 0)),
            pl.BlockSpec((seqs, 1, SHIFT_W), lambda b, t: (b, 0, 0)),
            pl.BlockSpec((seqs, PAIRS, LANES, HEAD), lambda b, t: (b, 0, 0, 0)),
        ],
        out_shape=[
            jax.ShapeDtypeStruct((bsz, tp, D_MODEL), F32),
            jax.ShapeDtypeStruct((bsz, 1, SHIFT_W), F32),
            jax.ShapeDtypeStruct((bsz, PAIRS, LANES, HEAD), F32),
        ],
        scratch_shapes=[
            pltpu.VMEM((1, SHIFT_W), F32),
            pltpu.VMEM((PAIRS, LANES, LANES), F32),
            pair_buf, pair_buf, pair_buf, pair_buf, pair_buf, pair_buf, pair_buf,
            pltpu.VMEM((PAIRS, tile // chunk, SUBLANES, LANES), F32),
            pair_buf,
            inst(two, BF16), inst(two, BF16), inst(two, BF16), inst(2 * two, BF16),
            pltpu.VMEM((n_inst, 2 * two, 2 * two), BF16),
            pltpu.VMEM((PAIRS * (tile // chunk), 2 * two, two), BF16), every(F32), every(F32),
            pltpu.VMEM((MXU_TILE, MXU_TILE), BF16),
        ],
        compiler_params=pltpu.CompilerParams(
            dimension_semantics=("arbitrary", "arbitrary"), vmem_limit_bytes=VMEM_LIMIT),
        name="rwkv7_mixer",
    )(x, mod, g_pre, wr, wg, mu, w0, w2a, a0, k_k, k_a, r_k, ln_g, ln_b, shift0, wkv0)


def _s5_kernel(x_ref, mod_ref, g_ref, wu_ref, wg_ref, bbr_ref, bbi_ref, cre_ref, cmi_ref, abr_ref, abi_ref,
               d_ref, gluw_ref, glub_ref, s0r_ref, s0i_ref,
               yz_ref, sr_out_ref, si_out_ref,
               hr_ref, hi_ref, cr_ref, ci_ref, *, tile):
    t = pl.program_id(1)

    @pl.when(t == 0)
    def _():
        cr_ref[...] = s0r_ref[...]
        ci_ref[...] = s0i_ref[...]

    seqs = x_ref.shape[0]
    frames = seqs * tile
    h = _pre_norm(x_ref, g_ref, mod_ref)
    u = _dot(h, wu_ref[...])
    zg = _silu(_dot(h, wg_ref[...]))
    set_tiles = S5_TILES // S5_SETS

    def frame_rows(c):
        return pl.ds(c, frames, stride=S5_ROW_STRIDE)

    in_tiles = bbr_ref.shape[2] // LANES
    for s in range(D_MODEL // LANES):
        ub = u[:, LANES * s:LANES * (s + 1)].astype(BF16)
        bur = _dot(ub, bbr_ref[s])
        bui = _dot(ub, bbi_ref[s])
        for c in range(in_tiles):
            hr_ref[frame_rows(s * in_tiles + c), :] = bur[:, LANES * c:LANES * (c + 1)]
            hi_ref[frame_rows(s * in_tiles + c), :] = bui[:, LANES * c:LANES * (c + 1)]

    abr = abr_ref[...]
    abi = abi_ref[...]

    def frame(f, carry):
        sr, si = carry
        rows = pl.ds(pl.multiple_of(f * S5_ROW_STRIDE, SUBLANES), S5_TILES)
        nr = abr * sr - abi * si + hr_ref[rows, :]
        ni = abr * si + abi * sr + hi_ref[rows, :]
        hr_ref[rows, :] = nr
        hi_ref[rows, :] = ni
        return nr, ni

    for s in range(seqs):
        cr_ref[s], ci_ref[s] = lax.fori_loop(s * tile, (s + 1) * tile, frame, (cr_ref[s], ci_ref[s]), unroll=16)

    outs = []
    for s in range(S5_SETS):
        hr = jnp.concatenate([hr_ref[frame_rows(s * set_tiles + c), :] for c in range(set_tiles)], axis=1)
        hi = jnp.concatenate([hi_ref[frame_rows(s * set_tiles + c), :] for c in range(set_tiles)], axis=1)
        outs.append(_dot(hr.astype(BF16), cre_ref[s]) + _dot(hi.astype(BF16), cmi_ref[s]))
    y = jnp.concatenate(outs, axis=1) + d_ref[...] * u
    y = 0.5 * y * (1.0 + jnp.tanh(0.7978845608028654 * (y + 0.044715 * (y * y * y))))
    y = y * _sigmoid(_dot(y.astype(BF16), gluw_ref[...]) + glub_ref[...])
    yz_ref[...] = (y * zg).reshape(seqs, tile, D_MODEL)

    @pl.when(t == pl.num_programs(1) - 1)
    def _():
        sr_out_ref[...] = cr_ref[...]
        si_out_ref[...] = ci_ref[...]


def _s5_mixer(x, mod, g_pre, wu, wg, bbr, bbi, cre, cmi, abr, abi, d_skip, glu_w, glu_b, s0r, s0i, *, sel, tile,
              seqs):
    bsz, tp, _ = x.shape
    assert tp % tile == 0 and tile % SUBLANES == 0 and bsz % seqs == 0
    nt = tp // tile
    weight, state, mod_spec = _specs(*sel, seqs=seqs)
    state_out = pl.BlockSpec((seqs, S5_TILES, LANES), lambda b, t: (b, 0, 0))
    kern = functools.partial(_s5_kernel, tile=tile)
    return pl.pallas_call(
        kern,
        grid=(bsz // seqs, nt),
        in_specs=[
            pl.BlockSpec((seqs, tile, D_MODEL), lambda b, t: (b, t, 0)),
            mod_spec(mod),
            weight(g_pre), weight(wu), weight(wg), weight(bbr), weight(bbi), weight(cre), weight(cmi),
            weight(abr), weight(abi), weight(d_skip), weight(glu_w), weight(glu_b), state(s0r), state(s0i),
        ],
        out_specs=[pl.BlockSpec((seqs, tile, D_MODEL), lambda b, t: (b, t, 0)), state_out, state_out],
        out_shape=[
            jax.ShapeDtypeStruct((bsz, tp, D_MODEL), F32),
            jax.ShapeDtypeStruct((bsz, S5_TILES, LANES), F32),
            jax.ShapeDtypeStruct((bsz, S5_TILES, LANES), F32),
        ],
        scratch_shapes=[
            pltpu.VMEM((seqs * tile * S5_ROW_STRIDE, LANES), F32),
            pltpu.VMEM((seqs * tile * S5_ROW_STRIDE, LANES), F32),
            pltpu.VMEM((seqs, S5_TILES, LANES), F32),
            pltpu.VMEM((seqs, S5_TILES, LANES), F32),
        ],
        compiler_params=pltpu.CompilerParams(
            dimension_semantics=("arbitrary", "arbitrary"), vmem_limit_bytes=VMEM_LIMIT),
        name="s5_mixer",
    )(x, mod, g_pre, wu, wg, bbr, bbi, cre, cmi, abr, abi, d_skip, glu_w, glu_b, s0r, s0i)


def _lru_kernel(x_ref, mod_ref, g_ref, wx_ref, wg_ref, cw_ref, cb_ref, wa4_ref, ba_ref, wx4_ref, bx_ref, lam_ref,
                conv0_ref, h0_ref,
                yz_ref, conv_out_ref, h_out_ref,
                xbuf_ref, a_ref, b_ref, hc_ref, *, tile):
    t = pl.program_id(1)
    pad = SUBLANES
    hist = CONV_W - 1
    seqs = x_ref.shape[0]

    @pl.when(t == 0)
    def _():
        xbuf_ref[:, pad - hist:pad, :] = conv0_ref[...]
        hc_ref[...] = h0_ref[...]

    h = _pre_norm(x_ref, g_ref, mod_ref)
    xl = _dot(h, wx_ref[...])
    zg = _silu(_dot(h, wg_ref[...]))
    xcs, tails = [], []
    for s in range(seqs):
        xs = xl[s * tile:(s + 1) * tile, :]
        xbuf_ref[s, pad:pad + tile, :] = xs
        acc = cb_ref[...] + xs * cw_ref[CONV_W - 1:CONV_W, :]
        for j in range(hist):
            acc = acc + xbuf_ref[s, pad - hist + j:pad - hist + j + tile, :] * cw_ref[j:j + 1, :]
        tails.append(xbuf_ref[s, pad + tile - hist:pad + tile, :])
        xbuf_ref[s, pad - hist:pad, :] = tails[s]
        xcs.append(acc)
    xc = jnp.concatenate(xcs, axis=0)

    ga, gx = [], []
    blk = MXU_TILE
    for s in range(D_MODEL // blk):
        xb = xc[:, blk * s:blk * (s + 1)].astype(BF16)
        ga.append(_dot(xb, wa4_ref[s]))
        gx.append(_dot(xb, wx4_ref[s]))
    gate_r = _sigmoid(jnp.concatenate(ga, axis=1) + ba_ref[...])
    gate_i = _sigmoid(jnp.concatenate(gx, axis=1) + bx_ref[...])
    log_a = -LRU_C * gate_r * _softplus(-lam_ref[...])
    a = jnp.exp(log_a)
    a_ref[...] = a
    b_ref[...] = jnp.sqrt(-jnp.tanh(log_a) * (a * a + 1.0)) * (gate_i * xc)

    row8 = lax.broadcasted_iota(jnp.int32, (SUBLANES, 1), 0)

    def row_body(g, carry):
        rs = pl.ds(pl.multiple_of(g * SUBLANES, SUBLANES), SUBLANES)
        av = a_ref[rs, :]
        bv = b_ref[rs, :]
        for sh in (1, 2, 4):
            keep = row8 >= sh
            ash = jnp.where(keep, pltpu.roll(av, sh, 0), 1.0)
            bsh = jnp.where(keep, pltpu.roll(bv, sh, 0), 0.0)
            bv = bv + av * bsh
            av = av * ash
        hs = bv + av * carry
        b_ref[rs, :] = hs
        return hs[SUBLANES - 1:SUBLANES, :]

    groups = tile // SUBLANES
    for s in range(seqs):
        hc_ref[s] = lax.fori_loop(s * groups, (s + 1) * groups, row_body, hc_ref[s])
    yz_ref[...] = (b_ref[...] * zg).reshape(seqs, tile, D_MODEL)

    @pl.when(t == pl.num_programs(1) - 1)
    def _():
        for s in range(seqs):
            conv_out_ref[s] = tails[s]
        h_out_ref[...] = hc_ref[...]


def _lru_mixer(x, mod, g_pre, wx, wg, conv_w, conv_b, wa4, ba, wx4, bx, lam, conv0, h0, *, sel, tile, seqs):
    bsz, tp, _ = x.shape
    assert tp % tile == 0 and tile % SUBLANES == 0 and tile >= SUBLANES and bsz % seqs == 0
    nt = tp // tile
    weight, state, mod_spec = _specs(*sel, seqs=seqs)
    kern = functools.partial(_lru_kernel, tile=tile)
    return pl.pallas_call(
        kern,
        grid=(bsz // seqs, nt),
        in_specs=[
            pl.BlockSpec((seqs, tile, D_MODEL), lambda b, t: (b, t, 0)),
            mod_spec(mod),
            weight(g_pre), weight(wx), weight(wg), weight(conv_w), weight(conv_b), weight(wa4), weight(ba),
            weight(wx4), weight(bx), weight(lam), state(conv0), state(h0),
        ],
        out_specs=[
            pl.BlockSpec((seqs, tile, D_MODEL), lambda b, t: (b, t, 0)),
            pl.BlockSpec((seqs, CONV_W - 1, D_MODEL), lambda b, t: (b, 0, 0)),
            pl.BlockSpec((seqs, 1, D_MODEL), lambda b, t: (b, 0, 0)),
        ],
        out_shape=[
            jax.ShapeDtypeStruct((bsz, tp, D_MODEL), F32),
            jax.ShapeDtypeStruct((bsz, CONV_W - 1, D_MODEL), F32),
            jax.ShapeDtypeStruct((bsz, 1, D_MODEL), F32),
        ],
        scratch_shapes=[
            pltpu.VMEM((seqs, tile + SUBLANES, D_MODEL), F32),
            pltpu.VMEM((seqs * tile, D_MODEL), F32),
            pltpu.VMEM((seqs * tile, D_MODEL), F32),
            pltpu.VMEM((seqs, 1, D_MODEL), F32),
        ],
        compiler_params=pltpu.CompilerParams(
            dimension_semantics=("arbitrary", "arbitrary"), vmem_limit_bytes=VMEM_LIMIT),
        name="rglru_mixer",
    )(x, mod, g_pre, wx, wg, conv_w, conv_b, wa4, ba, wx4, bx, lam, conv0, h0)


def _merge_kernel(x_ref, mod_ref, g_ref, gpost_ref, wm_ref, wo_ref, yr_ref, ys_ref, yl_ref, o_ref):
    seqs, tile, _ = x_ref.shape
    rows = lambda ref: ref[...].reshape(seqs * tile, D_MODEL)
    m = _sigmoid(_dot(_pre_norm(x_ref, g_ref, mod_ref), wm_ref[...]))
    merged = (m[:, :D_MODEL] * rows(yr_ref) + m[:, D_MODEL:2 * D_MODEL] * rows(ys_ref)
              + m[:, 2 * D_MODEL:] * rows(yl_ref))
    o = _dot(merged.astype(BF16), wo_ref[...])
    ms = jnp.mean(o * o, axis=-1, keepdims=True)
    o = (o * lax.rsqrt(ms + RMS_EPS)) * gpost_ref[...]
    for s in range(seqs):
        o_ref[s] = x_ref[s] + mod_ref[s][2:3, :] * o[s * tile:(s + 1) * tile, :]


def _merge_out(x, mod, g_pre, g_post, wm, wo, yr, ys, yl, *, sel, tile, seqs):
    bsz, tp, _ = x.shape
    assert tp % tile == 0 and bsz % seqs == 0
    nt = tp // tile
    act = pl.BlockSpec((seqs, tile, D_MODEL), lambda b, t: (b, t, 0))
    weight, _, mod_spec = _specs(*sel, seqs=seqs)
    return pl.pallas_call(
        _merge_kernel,
        grid=(bsz // seqs, nt),
        in_specs=[act, mod_spec(mod), weight(g_pre), weight(g_post), weight(wm), weight(wo), act, act, act],
        out_specs=act,
        out_shape=jax.ShapeDtypeStruct((bsz, tp, D_MODEL), F32),
        compiler_params=pltpu.CompilerParams(
            dimension_semantics=("arbitrary", "arbitrary"), vmem_limit_bytes=VMEM_LIMIT),
        name="merge_out",
    )(x, mod, g_pre, g_post, wm, wo, yr, ys, yl)


def _block_diag_expand(w, per):
    depth, n, a, b = w.shape
    w = w.reshape(depth, n // per, per, a, b)
    eye = jnp.eye(per, dtype=w.dtype)
    return jnp.einsum("lsgab,gh->lsgahb", w, eye).reshape(depth, n // per, per * a, per * b)


def _layer(x, mod, state, prm, *, sel, t_valid, tiles):
    shift_row, wkv, s_re, s_im, lru_h, lru_conv = state
    bsz = x.shape[0]
    g_pre = prm["g_pre"]

    xr = x
    if x.shape[1] % tiles["rwkv"] != 0:
        xr = jnp.pad(x, ((0, 0), (0, tiles["rwkv"] - x.shape[1] % tiles["rwkv"]), (0, 0)))
    yz_r, shift_new, wkv_new = _rwkv_mixer(
        xr, mod, g_pre, prm["w_rwkv"], prm["w_rwkv_g"], prm["rwkv_vecs"], prm["w2a"], shift_row, wkv,
        sel=sel, tile=tiles["rwkv"], t_valid=t_valid, seqs=tiles["rwkv_seqs"])
    yz_r = yz_r[:, :t_valid]
    yz_s, s_re_new, s_im_new = _s5_mixer(
        x, mod, g_pre, prm["w_s5"], prm["w_s5_g"], prm["bbr"], prm["bbi"], prm["cre"], prm["cmi"],
        prm["abr"], prm["abi"], prm["s5_d"], prm["glu_w"], prm["glu_b"], s_re, s_im, sel=sel, tile=tiles["s5"],
        seqs=tiles["seqs"])
    yz_l, conv_new, h_new = _lru_mixer(
        x, mod, g_pre, prm["w_lru"], prm["w_lru_g"], prm["conv_w"], prm["conv_b"], prm["wa4"], prm["ba"],
        prm["wx4"], prm["bx"], prm["lam"], lru_conv, lru_h, sel=sel, tile=tiles["lru"], seqs=tiles["seqs"])
    x_new = _merge_out(x, mod, g_pre, prm["g_post"], prm["w_merge"], prm["w_out"], yz_r, yz_s, yz_l,
                       sel=sel, tile=tiles["merge"], seqs=tiles["seqs"])
    new_state = (shift_new.reshape(bsz, SHIFT_W), wkv_new.reshape(bsz, HEADS, HEAD, HEAD),
                 s_re_new.reshape(bsz, S5_GROUPS, S5_STATES), s_im_new.reshape(bsz, S5_GROUPS, S5_STATES),
                 h_new.reshape(bsz, D_MODEL), conv_new)
    return x_new, new_state


def _stacked_params(w_in, w_out, norm_pre, norm_post, rwkv, s5, lru):
    depth = w_in.shape[0]
    d = D_MODEL
    wb = w_in.astype(BF16)
    o = SHIFT_W
    row = lambda a: a.reshape(depth, 1, -1)
    mu, w0, w2, a0, a2, k_k, k_a, r_k, ln_g, ln_b = rwkv
    a_re, a_im, log_step, b_re, b_im, c_re, c_im, s5_d, glu_w, glu_b = s5
    conv_w, conv_b, wa, ba, wx, bx, lam = lru
    zeros = jnp.zeros((depth, LORA, d), F32)
    w2a = jnp.concatenate([jnp.concatenate([w2, zeros], axis=2), jnp.concatenate([zeros, a2], axis=2)], axis=1)
    abr, abi, bbr, bbi = _s5_prep(a_re, a_im, log_step, b_re, b_im)
    per = S5_GROUPS // S5_SETS
    gate_per = MXU_TILE // LRU_BLOCK
    return dict(
        g_pre=row(norm_pre), g_post=row(norm_post),
        w_rwkv=wb[:, :, :o], w_rwkv_g=wb[:, :, o:o + d],
        w_s5=wb[:, :, o + d:o + 2 * d], w_s5_g=wb[:, :, o + 2 * d:o + 3 * d],
        w_lru=wb[:, :, o + 3 * d:o + 4 * d], w_lru_g=wb[:, :, o + 4 * d:o + 5 * d],
        w_merge=wb[:, :, o + 5 * d:], w_out=w_out.astype(BF16),
        rwkv_vecs=(row(mu), row(w0), row(a0), row(k_k), row(k_a), row(r_k), row(ln_g), row(ln_b)),
        w2a=w2a.astype(BF16),
        bbr=_block_diag_expand(bbr, S5_IN_GROUPS).astype(BF16),
        bbi=_block_diag_expand(bbi, S5_IN_GROUPS).astype(BF16),
        cre=_block_diag_expand(jnp.swapaxes(c_re, 2, 3), per).astype(BF16),
        cmi=_block_diag_expand(-jnp.swapaxes(c_im, 2, 3), per).astype(BF16),
        abr=abr.reshape(depth, S5_TILES, LANES), abi=abi.reshape(depth, S5_TILES, LANES),
        s5_d=row(s5_d), glu_w=glu_w.astype(BF16), glu_b=row(glu_b),
        conv_w=conv_w, conv_b=row(conv_b),
        wa4=_block_diag_expand(wa, gate_per).astype(BF16), ba=row(ba),
        wx4=_block_diag_expand(wx, gate_per).astype(BF16), bx=row(bx), lam=row(lam),
    )


def _state_layout(shift, wkv, s_re, s_im, lru_h, lru_conv):
    n, bsz = shift.shape[:2]
    return (shift.reshape(n, bsz, 1, SHIFT_W), wkv.reshape(n, bsz, PAIRS, LANES, HEAD),
            s_re.reshape(n, bsz, S5_TILES, LANES), s_im.reshape(n, bsz, S5_TILES, LANES),
            lru_h.reshape(n, bsz, 1, D_MODEL), lru_conv)


def _tiles(t, bsz):
    pick = lambda want: want if t % want == 0 else t
    share = lambda rows, limit: max(n for n in range(1, bsz + 1) if bsz % n == 0 and n * rows <= max(limit, rows))
    seqs = share(t, SHORT_ROWS) if t < SHORT_ROWS else 1
    rwkv = RWKV_TILE if t % RWKV_TILE == 0 else RWKV_CHUNK
    rwkv_seqs = share(RWKV_CHUNK, RWKV_TILE) if t <= RWKV_CHUNK else 1
    return dict(rwkv=rwkv, rwkv_seqs=rwkv_seqs, s5=pick(S5_TILE), lru=pick(WIDE_TILE), merge=pick(WIDE_TILE),
                seqs=seqs)


def kernel(x_prompt, x_sample, state_rwkv_shift, state_rwkv_wkv, state_s5_re, state_s5_im, state_lru_h,
           state_lru_conv, c_prompt, c_sample, ada_w, ada_b, norm_pre, norm_post, w_in, w_out, rwkv_mu, rwkv_w0,
           rwkv_w2, rwkv_a0, rwkv_a2, rwkv_k_k, rwkv_k_a, rwkv_r_k, rwkv_ln_g, rwkv_ln_b, s5_a_re, s5_a_im,
           s5_log_step, s5_b_re, s5_b_im, s5_c_re, s5_c_im, s5_d, s5_glu_w, s5_glu_b, lru_conv_w, lru_conv_b,
           lru_wa, lru_ba, lru_wx, lru_bx, lru_lambda):
    depth = w_in.shape[0]
    bp, tp, _ = x_prompt.shape
    bs, ts, _ = x_sample.shape
    c_all = jnp.concatenate([c_sample, c_prompt], axis=0)
    pad = (-c_all.shape[0]) % SUBLANES
    c_all = jnp.pad(c_all, ((0, pad), (0, 0)))
    mod_all = _modulation(c_all, ada_w, ada_b)
    mod_all = mod_all.reshape(depth, c_all.shape[0], 3, D_MODEL)
    rwkv = (rwkv_mu, rwkv_w0, rwkv_w2, rwkv_a0, rwkv_a2, rwkv_k_k, rwkv_k_a,
            rwkv_r_k.reshape(depth, D_MODEL), rwkv_ln_g, rwkv_ln_b)
    s5 = (s5_a_re, s5_a_im, s5_log_step, s5_b_re, s5_b_im, s5_c_re, s5_c_im, s5_d, s5_glu_w, s5_glu_b)
    lru = (lru_conv_w, lru_conv_b, lru_wa, lru_ba, lru_wx, lru_bx, lru_lambda)
    prm = _stacked_params(w_in, w_out, norm_pre, norm_post, rwkv, s5, lru)
    zero = _state_layout(jnp.zeros((1, bp, SHIFT_W), F32), jnp.zeros((1, bp, HEADS, HEAD, HEAD), F32),
                         jnp.zeros((1, bp, S5_GROUPS, S5_STATES), F32),
                         jnp.zeros((1, bp, S5_GROUPS, S5_STATES), F32),
                         jnp.zeros((1, bp, D_MODEL), F32), jnp.zeros((1, bp, CONV_W - 1, D_MODEL), F32))
    st_in = _state_layout(state_rwkv_shift, state_rwkv_wkv, state_s5_re, state_s5_im, state_lru_h,
                          state_lru_conv)

    xp, xs = x_prompt, x_sample
    new_p = [[] for _ in range(6)]
    new_s = [[] for _ in range(6)]
    for l in range(depth):
        xp, st_p = _layer(xp, mod_all, zero, prm, sel=(l, 0, bs), t_valid=tp, tiles=_tiles(tp, bp))
        xs, st_s = _layer(xs, mod_all, st_in, prm, sel=(l, l, 0), t_valid=ts, tiles=_tiles(ts, bs))
        for i in range(6):
            new_p[i].append(st_p[i])
            new_s[i].append(st_s[i])
    sp = [jnp.stack(z, axis=0) for z in new_p]
    ss = [jnp.stack(z, axis=0) for z in new_s]
    return (xp, xs, sp[0], sp[1], sp[2], sp[3], sp[4], sp[5], ss[0], ss[1], ss[2], ss[3], ss[4], ss[5])
```

```python
import functools

import jax
import jax.numpy as jnp
from jax import lax
from jax.experimental import pallas as pl
from jax.experimental.pallas import tpu as pltpu

F32 = jnp.float32
BF16 = jnp.bfloat16
HIGHEST = lax.Precision.HIGHEST

SUBLANES = 8
LANES = 128
MXU_TILE = 256
VMEM_LIMIT = 56 * 1024 * 1024

D_MODEL = 1024
HEAD = 64
HEADS = D_MODEL // HEAD
PAIRS = HEADS // 2
LORA = 64
SHIFT_W = 3 * D_MODEL + 2 * LORA
S5_GROUPS = 64
S5_STATES = 64
S5_STATE_W = S5_GROUPS * S5_STATES
LRU_BLOCK = 64
S5_SETS = 4
S5_IN_GROUPS = 8
S5_TILES = S5_STATE_W // LANES
S5_ROW_STRIDE = 40
CONV_W = 4
LRU_C = 8.0
RMS_EPS = 1e-6
GN_EPS = 64e-5
RWKV_CHUNK = 64
RWKV_TILE = 256
S5_TILE = 256
WIDE_TILE = 512
SHORT_ROWS = 512
DECAY_SCALE = 0.6065306597126334
RWKV_INSTANCES = 16


def _sigmoid(x):
    return 0.5 * jnp.tanh(0.5 * x) + 0.5


def _silu(x):
    h = 0.5 * x
    return h * (jnp.tanh(h) + 1.0)


def _softplus(x):
    return jnp.maximum(x, 0.0) + jnp.log1p(jnp.exp(-jnp.abs(x)))


def _norm_mod(x, g, mod):
    ms = jnp.mean(x * x, axis=-1, keepdims=True)
    return (x * lax.rsqrt(ms + RMS_EPS)) * (g * (1.0 + mod[1:2, :])) + mod[0:1, :]


def _dot(a, b):
    return jnp.dot(a, b, preferred_element_type=F32)


def _dot_hi(a, b):
    return jnp.dot(a, b, preferred_element_type=F32, precision=HIGHEST)


def _split2(x):
    hi = x.astype(BF16)
    lo = (x - hi.astype(F32)).astype(BF16)
    return hi, lo


_NN = (((1,), (0,)), ((), ()))
_NT = (((1,), (1,)), ((), ()))
_TN = (((0,), (0,)), ((), ()))


def _mm(a, b, dims=_NN):
    return lax.dot_general(a.astype(BF16), b.astype(BF16), dims, preferred_element_type=F32)


def _head_sum(x, ones_blk):
    outs = []
    for s in range(D_MODEL // MXU_TILE):
        outs.append(_dot(x[:, MXU_TILE * s:MXU_TILE * (s + 1)].astype(BF16), ones_blk))
    return jnp.concatenate(outs, axis=1)


def _block_ones(n, blk):
    ri = lax.broadcasted_iota(jnp.int32, (n, n), 0)
    ci = lax.broadcasted_iota(jnp.int32, (n, n), 1)
    sh = blk.bit_length() - 1
    return ((ri >> sh) == (ci >> sh)).astype(F32).astype(BF16)


def _mod_kernel(c_ref, w_ref, b_ref, o_ref):
    s = _silu(c_ref[...])
    o_ref[...] = _dot_hi(s, w_ref[...]) + b_ref[...]


def _modulation(c_all, ada_w, ada_b):
    depth = ada_w.shape[0]
    rows = c_all.shape[0]
    return pl.pallas_call(
        _mod_kernel,
        grid=(depth, 3),
        in_specs=[
            pl.BlockSpec((rows, D_MODEL), lambda l, j: (0, 0)),
            pl.BlockSpec((None, D_MODEL, D_MODEL), lambda l, j: (l, 0, j)),
            pl.BlockSpec((None, 1, D_MODEL), lambda l, j: (l, 0, j)),
        ],
        out_specs=pl.BlockSpec((None, rows, D_MODEL), lambda l, j: (l, 0, j)),
        out_shape=jax.ShapeDtypeStruct((depth, rows, 3 * D_MODEL), F32),
        name="adaln_mod",
    )(c_all, ada_w, ada_b.reshape(depth, 1, 3 * D_MODEL))


def _s5_prep_kernel(are_ref, aim_ref, ls_ref, bre_ref, bim_ref, abr_ref, abi_ref, bbr_ref, bbi_ref):
    are = are_ref[...]
    aim = aim_ref[...]
    dt = jnp.exp(ls_ref[...])
    mag = jnp.exp(are * dt)
    abr = mag * jnp.cos(aim * dt)
    abi = mag * jnp.sin(aim * dt)
    abr_ref[...] = abr
    abi_ref[...] = abi
    nr = abr - 1.0
    ni = abi
    den = are * are + aim * aim
    cr = (nr * are + ni * aim) / den
    ci = (ni * are - nr * aim) / den
    bre = bre_ref[...]
    bim = bim_ref[...]
    bbr_ref[...] = cr[:, None, :] * bre - ci[:, None, :] * bim
    bbi_ref[...] = cr[:, None, :] * bim + ci[:, None, :] * bre


def _s5_prep(a_re, a_im, log_step, b_re, b_im):
    depth, g, p = a_re.shape
    i = b_re.shape[-1]
    layer = lambda *dims: pl.BlockSpec((None,) + dims, lambda l: (l,) + (0,) * len(dims))
    return pl.pallas_call(
        _s5_prep_kernel,
        grid=(depth,),
        in_specs=[layer(g, p), layer(g, p), layer(g, 1), layer(g, i, p), layer(g, i, p)],
        out_specs=(layer(g, p), layer(g, p), layer(g, i, p), layer(g, i, p)),
        out_shape=(
            jax.ShapeDtypeStruct((depth, g, p), F32),
            jax.ShapeDtypeStruct((depth, g, p), F32),
            jax.ShapeDtypeStruct((depth, g, i, p), F32),
            jax.ShapeDtypeStruct((depth, g, i, p), F32),
        ),
        name="s5_discretise",
    )(a_re, a_im, log_step.reshape(depth, g, 1), jnp.swapaxes(b_re, 2, 3), jnp.swapaxes(b_im, 2, 3))


def _rwkv_kernel(x_ref, mod_ref, g_ref, wr_ref, wg_ref, mu_ref, w0_ref, w2a_ref, a0_ref, kk_ref, ka_ref,
                 rk_ref, lng_ref, lnb_ref, shift0_ref, wkv0_ref,
                 yz_ref, shift_out_ref, wkv_out_ref,
                 carry_ref, st_ref, at_ref, rt_ref, bt_ref, kt_ref, bh_ref, kh_ref, v_ref, pl_ref, y_ref,
                 nh_ref, t_ref, aak_ref, ark_ref, rhs_ref, qg_ref, hc_ref, yc_ref, ones_ref,
                 *, tile, chunk, t_valid, group):
    t = pl.program_id(1)
    n_chunks = tile // chunk
    two = 2 * chunk
    n_inst = group * n_chunks
    chained = x_ref.shape[0] == 1

    def pair_state(s0):
        rh = lax.broadcasted_iota(jnp.int32, (LANES, LANES), 0) >> (HEAD.bit_length() - 1)
        ch = lax.broadcasted_iota(jnp.int32, (LANES, LANES), 1) >> (HEAD.bit_length() - 1)
        return jnp.where(rh == ch, jnp.concatenate([s0, s0], axis=1), 0.0).T

    def head_states(st):
        sv = st.T
        return sv[:, :HEAD] + sv[:, HEAD:]

    if chained:
        @pl.when(t == 0)
        def _():
            carry_ref[...] = shift0_ref[0]
            for q in range(PAIRS):
                st_ref[q] = pair_state(wkv0_ref[0, q])

    h = _pre_norm(x_ref, g_ref, mod_ref)
    p = _dot(h, wr_ref[...])
    zg = _silu(_dot(h, wg_ref[...]))

    rows = lax.broadcasted_iota(jnp.int32, (tile, 1), 0)
    rmod = rows & (chunk - 1)
    last = (t_valid - 1) % chunk
    if chained:
        prev = jnp.where(rows == 0, carry_ref[...], pltpu.roll(p, 1, 0))
        carry_ref[...] = p[tile - chunk + last:tile - chunk + last + 1, :]
    else:
        first = jnp.concatenate([jnp.broadcast_to(shift0_ref[s], (chunk, SHIFT_W)) for s in range(n_chunks)], axis=0)
        prev = jnp.where(rmod == 0, first, pltpu.roll(p, 1, 0))
        for s in range(n_chunks):
            shift_out_ref[s] = p[s * chunk + last:s * chunk + last + 1, :]
    pm = p + (prev - p) * mu_ref[...]

    r = pm[:, 0:D_MODEL]
    k = pm[:, D_MODEL:2 * D_MODEL]
    v = pm[:, 2 * D_MODEL:3 * D_MODEL]
    wa = pm[:, 3 * D_MODEL:SHIFT_W]
    lane = lax.broadcasted_iota(jnp.int32, (1, LANES), 1)
    wa = jnp.where(lane < LORA, jnp.tanh(wa), wa)
    lora = _dot(wa.astype(BF16), w2a_ref[...])
    logd = -DECAY_SCALE * _sigmoid(w0_ref[...] + lora[:, :D_MODEL])
    a_sig = _sigmoid(a0_ref[...] + lora[:, D_MODEL:])

    @pl.when(t == 0)
    def _():
        ones_ref[...] = _block_ones(MXU_TILE, HEAD)

    ones_blk = ones_ref[...]
    kk = k * kk_ref[...]
    kk = kk * jnp.minimum(lax.rsqrt(_head_sum(kk * kk, ones_blk)), 1e12)
    k2 = k * (1.0 + (a_sig - 1.0) * ka_ref[...])
    bvec = kk * a_sig
    if t_valid % chunk != 0:
        ok = rmod < t_valid % chunk
        logd = jnp.where(ok, logd, 0.0)
        bvec = jnp.where(ok, bvec, 0.0)
        k2 = jnp.where(ok, k2, 0.0)

    cum = logd
    s = 1
    while s < chunk:
        cum = cum + jnp.where(rmod >= s, pltpu.roll(cum, s, 0), 0.0)
        s *= 2
    cum_last = jnp.concatenate(
        [jnp.broadcast_to(cum[(j + 1) * chunk - 1:(j + 1) * chunk, :], (chunk, D_MODEL)) for j in range(n_chunks)],
        axis=0)
    p_inc = jnp.exp(cum)
    p_inv = jnp.exp(-cum)
    p_exc = jnp.exp(cum - logd)
    p_end = jnp.exp(cum_last - cum)

    def to_pairs(ref, val):
        for q in range(PAIRS):
            ref[q] = val[:, LANES * q:LANES * (q + 1)]

    to_pairs(at_ref, -kk * p_exc)
    to_pairs(rt_ref, r * p_inc)
    to_pairs(bt_ref, bvec * p_inv)
    to_pairs(kt_ref, k2 * p_inv)
    to_pairs(bh_ref, bvec * p_end)
    to_pairs(kh_ref, k2 * p_end)
    to_pairs(v_ref, v)
    for j in range(n_chunks):
        row = jnp.exp(cum[(j + 1) * chunk - 1:(j + 1) * chunk, :])
        for q in range(PAIRS):
            pl_ref[q, j] = jnp.broadcast_to(row[:, LANES * q:LANES * (q + 1)], (SUBLANES, LANES))

    m_a = (lane < HEAD).astype(F32)
    m_b = 1.0 - m_a
    ri = lax.broadcasted_iota(jnp.int32, (two, two), 0)
    ci = lax.broadcasted_iota(jnp.int32, (two, two), 1)
    csh = chunk.bit_length() - 1
    same_head = (ri >> csh) == (ci >> csh)
    strict = jnp.logical_and(same_head, (ri & (chunk - 1)) > (ci & (chunk - 1)))
    incl = jnp.logical_and(same_head, (ri & (chunk - 1)) >= (ci & (chunk - 1)))
    eye = ri == ci
    head_a = ri < chunk
    off_masks = []
    for lg in range(csh):
        off_masks.append(jnp.logical_and(strict, jnp.logical_and((ri >> (lg + 1)) == (ci >> (lg + 1)),
                                                                 (ri >> lg) != (ci >> lg))))

    def stack(x):
        return jnp.concatenate([x * m_a, x * m_b], axis=0)

    gsh = group.bit_length() - 1
    unroll = min(16, n_inst)

    def group_body(g, carry):
        def where_is(i):
            q = g * group + (i & (group - 1))
            j = i >> gsh
            return q, j, pl.ds(pl.multiple_of(j * chunk, chunk), chunk)

        def gram(i, c):
            q, _, sl = where_is(i)
            atm = stack(at_ref[q, sl, :])
            rtm = stack(rt_ref[q, sl, :])
            bt = bt_ref[q, sl, :]
            kt = kt_ref[q, sl, :]
            o1 = _mm(jnp.concatenate([atm, rtm], axis=0), jnp.concatenate([bt, kt], axis=0), _NT)
            o1_sw = pltpu.roll(o1, chunk, 1)
            n_ab = jnp.where(strict, jnp.where(head_a, o1[:two], o1_sw[:two]), 0.0)
            nh_ref[i] = n_ab.astype(BF16)
            t_ref[i] = jnp.where(eye, 1.0, jnp.where(off_masks[0], n_ab, 0.0)).astype(BF16)
            aak_ref[i] = jnp.where(strict, jnp.where(head_a, o1_sw[:two], o1[:two]), 0.0).astype(BF16)
            ark_ref[i] = jnp.concatenate([jnp.where(incl, jnp.where(head_a, o1[two:], o1_sw[two:]), 0.0),
                                          jnp.where(incl, jnp.where(head_a, o1_sw[two:], o1[two:]), 0.0)],
                                         axis=1).astype(BF16)
            return c

        lax.fori_loop(0, n_inst, gram, 0, unroll=unroll)

        for off in off_masks[1:]:
            def level(i, c, off=off):
                t_inv = t_ref[i]
                x = _dot(jnp.where(off, nh_ref[i], 0.0), t_inv).astype(BF16)
                t_ref[i] = t_inv + _dot(t_inv, x).astype(BF16)
                return c

            lax.fori_loop(0, n_inst, level, 0, unroll=unroll)

        def right_side(i, c):
            q, _, sl = where_is(i)
            atm = stack(at_ref[q, sl, :])
            vm = stack(v_ref[q, sl, :])
            z0_hi, z0_lo = _split2(jnp.concatenate([atm, _mm(aak_ref[i], vm)], axis=1))
            rhs_ref[i] = jnp.concatenate([z0_hi, z0_lo], axis=0)
            return c

        lax.fori_loop(0, n_inst, right_side, 0, unroll=unroll)

        def apply(i, c):
            q, _, sl = where_is(i)
            vm = stack(v_ref[q, sl, :])
            t_inv = t_ref[i]
            z = _dot(t_inv, rhs_ref[i, :two, :]) + _dot(t_inv, rhs_ref[i, two:, :])
            rhs_ref[i] = jnp.concatenate([z, jnp.concatenate([jnp.zeros_like(vm), vm], axis=1)], axis=0).astype(BF16)
            return c

        lax.fori_loop(0, n_inst, apply, 0, unroll=unroll)

        def operators(i, c):
            q, j, sl = where_is(i)
            rhs2 = rhs_ref[i]
            k_all = g * n_inst + i
            top = _mm(jnp.concatenate([stack(bh_ref[q, sl, :]), stack(kh_ref[q, sl, :])], axis=0), rhs2, _TN)
            bot = _dot(ark_ref[i], rhs2)
            g_mat = top[:, :two] + jnp.where(eye, jnp.broadcast_to(pl_ref[q, j][0:1, :], (two, two)), 0.0)
            q_eff = bot[:, :two] + stack(rt_ref[q, sl, :])
            qg_ref[k_all] = jnp.concatenate([q_eff, g_mat], axis=0).astype(BF16)
            hc_ref[k_all] = top[:, two:]
            yc_ref[k_all] = bot[:, two:]
            return c

        lax.fori_loop(0, n_inst, operators, 0, unroll=unroll)
        return carry

    lax.fori_loop(0, PAIRS // group, group_body, 0)

    if chained:
        states = [st_ref[q] for q in range(PAIRS)]
    for j in range(n_chunks):
        for q in range(PAIRS):
            k_all = (q // group) * n_inst + j * group + q % group
            st = (states[q] if chained else pair_state(wkv0_ref[j, q])).astype(BF16)
            both = _dot(qg_ref[k_all], st)
            ym = both[:two, :] + yc_ref[k_all]
            y_ref[q, j * chunk:(j + 1) * chunk, :] = ym[:chunk, :] + ym[chunk:, :]
            st_new = both[two:, :] + hc_ref[k_all]
            if chained:
                states[q] = st_new
            else:
                wkv_out_ref[j, q] = head_states(st_new)
    if chained:
        for q in range(PAIRS):
            st_ref[q] = states[q]

    ys = jnp.concatenate([y_ref[q] for q in range(PAIRS)], axis=1)
    mean = _head_sum(ys, ones_blk) * (1.0 / HEAD)
    yc = ys - mean
    var = _head_sum(yc * yc, ones_blk) * (1.0 / HEAD)
    yn = yc * lax.rsqrt(var + GN_EPS) * lng_ref[...] + lnb_ref[...]
    bonus = _head_sum(r * k2 * rk_ref[...], ones_blk) * v
    yz_ref[...] = ((yn + bonus) * zg).reshape(yz_ref.shape)

    if chained:
        @pl.when(t == pl.num_programs(1) - 1)
        def _():
            shift_out_ref[0] = carry_ref[...]
            for q in range(PAIRS):
                wkv_out_ref[0, q] = head_states(st_ref[q])


def _specs(layer, state_layer, mod_row, seqs=None):
    assert seqs is None or mod_row % seqs == 0
    per_step = 1 if seqs is None else seqs

    def weight(a):
        return pl.BlockSpec((None,) + a.shape[1:], lambda b, t: (layer,) + (0,) * (a.ndim - 1))

    def state(a):
        return pl.BlockSpec((None, seqs) + a.shape[2:], lambda b, t: (state_layer, b) + (0,) * (a.ndim - 2))

    def mod(a):
        return pl.BlockSpec((None, seqs) + a.shape[2:], lambda b, t: (layer, mod_row // per_step + b, 0, 0))

    return weight, state, mod


def _pre_norm(x_ref, g_ref, mod_ref):
    g = g_ref[...]
    rows = [_norm_mod(x_ref[s], g, mod_ref[s]) for s in range(x_ref.shape[0])]
    return jnp.concatenate(rows, axis=0).astype(BF16)


def _rwkv_mixer(x, mod, g_pre, wr, wg, vecs, w2a, shift0, wkv0, *, sel, tile, t_valid, seqs):
    bsz, tp, _ = x.shape
    chunk = RWKV_CHUNK
    assert tp % tile == 0 and tile % chunk == 0 and bsz % seqs == 0
    assert t_valid == tp or tp == tile
    assert seqs == 1 or tp == chunk
    nt = tp // tile
    block_rows = tile
    tile = seqs * tile
    mu, w0, a0, k_k, k_a, r_k, ln_g, ln_b = vecs
    weight, state, mod_spec = _specs(*sel, seqs=seqs)
    pair_buf = pltpu.VMEM((PAIRS, tile, LANES), F32)
    group = min(PAIRS, max(1, RWKV_INSTANCES // (tile // chunk)))
    n_inst = group * (tile // chunk)
    two = 2 * chunk
    inst = lambda cols, dtype: pltpu.VMEM((n_inst, two, cols), dtype)
    every = lambda dtype: pltpu.VMEM((PAIRS * (tile // chunk), two, two), dtype)
    kern = functools.partial(_rwkv_kernel, tile=tile, chunk=chunk, t_valid=t_valid, group=group)
    return pl.pallas_call(
        kern,
        grid=(bsz // seqs, nt),
        in_specs=[
            pl.BlockSpec((seqs, block_rows, D_MODEL), lambda b, t: (b, t, 0)),
            mod_spec(mod),
            weight(g_pre), weight(wr), weight(wg), weight(mu), weight(w0), weight(w2a), weight(a0),
            weight(k_k), weight(k_a), weight(r_k), weight(ln_g), weight(ln_b),
            state(shift0), state(wkv0),
        ],
        out_specs=[
            pl.BlockSpec((seqs, block_rows, D_MODEL), lambda b, t: (b, t, 0)),
            pl.BlockSpec((seqs, 1, SHIFT_W), lambda b, t: (b, 0, 0)),
            pl.BlockSpec((seqs, PAIRS, LANES, HEAD), lambda b, t: (b, 0, 0, 0)),
        ],
        out_shape=[
            jax.ShapeDtypeStruct((bsz, tp, D_MODEL), F32),
            jax.ShapeDtypeStruct((bsz, 1, SHIFT_W), F32),
            jax.ShapeDtypeStruct((bsz, PAIRS, LANES, HEAD), F32),
        ],
        scratch_shapes=[
            pltpu.VMEM((1, SHIFT_W), F32),
            pltpu.VMEM((PAIRS, LANES, LANES), F32),
            pair_buf, pair_buf, pair_buf, pair_buf, pair_buf, pair_buf, pair_buf,
            pltpu.VMEM((PAIRS, tile // chunk, SUBLANES, LANES), F32),
            pair_buf,
            inst(two, BF16), inst(two, BF16), inst(two, BF16), inst(2 * two, BF16),
            pltpu.VMEM((n_inst, 2 * two, 2 * two), BF16),
            pltpu.VMEM((PAIRS * (tile // chunk), 2 * two, two), BF16), every(F32), every(F32),
            pltpu.VMEM((MXU_TILE, MXU_TILE), BF16),
        ],
        compiler_params=pltpu.CompilerParams(
            dimension_semantics=("arbitrary", "arbitrary"), vmem_limit_bytes=VMEM_LIMIT),
        name="rwkv7_mixer",
    )(x, mod, g_pre, wr, wg, mu, w0, w2a, a0, k_k, k_a, r_k, ln_g, ln_b, shift0, wkv0)


def _s5_kernel(x_ref, mod_ref, g_ref, wu_ref, wg_ref, bbr_ref, bbi_ref, cre_ref, cmi_ref, abr_ref, abi_ref,
               d_ref, gluw_ref, glub_ref, s0r_ref, s0i_ref,
               yz_ref, sr_out_ref, si_out_ref,
               hr_ref, hi_ref, cr_ref, ci_ref, *, tile):
    t = pl.program_id(1)

    @pl.when(t == 0)
    def _():
        cr_ref[...] = s0r_ref[...]
        ci_ref[...] = s0i_ref[...]

    seqs = x_ref.shape[0]
    frames = seqs * tile
    h = _pre_norm(x_ref, g_ref, mod_ref)
    u = _dot(h, wu_ref[...])
    zg = _silu(_dot(h, wg_ref[...]))
    set_tiles = S5_TILES // S5_SETS

    def frame_rows(c):
        return pl.ds(c, frames, stride=S5_ROW_STRIDE)

    in_tiles = bbr_ref.shape[2] // LANES
    for s in range(D_MODEL // LANES):
        ub = u[:, LANES * s:LANES * (s + 1)].astype(BF16)
        bur = _dot(ub, bbr_ref[s])
        bui = _dot(ub, bbi_ref[s])
        for c in range(in_tiles):
            hr_ref[frame_rows(s * in_tiles + c), :] = bur[:, LANES * c:LANES * (c + 1)]
            hi_ref[frame_rows(s * in_tiles + c), :] = bui[:, LANES * c:LANES * (c + 1)]

    abr = abr_ref[...]
    abi = abi_ref[...]

    def frame(f, carry):
        sr, si = carry
        rows = pl.ds(pl.multiple_of(f * S5_ROW_STRIDE, SUBLANES), S5_TILES)
        nr = abr * sr - abi * si + hr_ref[rows, :]
        ni = abr * si + abi * sr + hi_ref[rows, :]
        hr_ref[rows, :] = nr
        hi_ref[rows, :] = ni
        return nr, ni

    for s in range(seqs):
        cr_ref[s], ci_ref[s] = lax.fori_loop(s * tile, (s + 1) * tile, frame, (cr_ref[s], ci_ref[s]), unroll=16)

    outs = []
    for s in range(S5_SETS):
        hr = jnp.concatenate([hr_ref[frame_rows(s * set_tiles + c), :] for c in range(set_tiles)], axis=1)
        hi = jnp.concatenate([hi_ref[frame_rows(s * set_tiles + c), :] for c in range(set_tiles)], axis=1)
        outs.append(_dot(hr.astype(BF16), cre_ref[s]) + _dot(hi.astype(BF16), cmi_ref[s]))
    y = jnp.concatenate(outs, axis=1) + d_ref[...] * u
    y = 0.5 * y * (1.0 + jnp.tanh(0.7978845608028654 * (y + 0.044715 * (y * y * y))))
    y = y * _sigmoid(_dot(y.astype(BF16), gluw_ref[...]) + glub_ref[...])
    yz_ref[...] = (y * zg).reshape(seqs, tile, D_MODEL)

    @pl.when(t == pl.num_programs(1) - 1)
    def _():
        sr_out_ref[...] = cr_ref[...]
        si_out_ref[...] = ci_ref[...]


def _s5_mixer(x, mod, g_pre, wu, wg, bbr, bbi, cre, cmi, abr, abi, d_skip, glu_w, glu_b, s0r, s0i, *, sel, tile,
              seqs):
    bsz, tp, _ = x.shape
    assert tp % tile == 0 and tile % SUBLANES == 0 and bsz % seqs == 0
    nt = tp // tile
    weight, state, mod_spec = _specs(*sel, seqs=seqs)
    state_out = pl.BlockSpec((seqs, S5_TILES, LANES), lambda b, t: (b, 0, 0))
    kern = functools.partial(_s5_kernel, tile=tile)
    return pl.pallas_call(
        kern,
        grid=(bsz // seqs, nt),
        in_specs=[
            pl.BlockSpec((seqs, tile, D_MODEL), lambda b, t: (b, t, 0)),
            mod_spec(mod),
            weight(g_pre), weight(wu), weight(wg), weight(bbr), weight(bbi), weight(cre), weight(cmi),
            weight(abr), weight(abi), weight(d_skip), weight(glu_w), weight(glu_b), state(s0r), state(s0i),
        ],
        out_specs=[pl.BlockSpec((seqs, tile, D_MODEL), lambda b, t: (b, t, 0)), state_out, state_out],
        out_shape=[
            jax.ShapeDtypeStruct((bsz, tp, D_MODEL), F32),
            jax.ShapeDtypeStruct((bsz, S5_TILES, LANES), F32),
            jax.ShapeDtypeStruct((bsz, S5_TILES, LANES), F32),
        ],
        scratch_shapes=[
            pltpu.VMEM((seqs * tile * S5_ROW_STRIDE, LANES), F32),
            pltpu.VMEM((seqs * tile * S5_ROW_STRIDE, LANES), F32),
            pltpu.VMEM((seqs, S5_TILES, LANES), F32),
            pltpu.VMEM((seqs, S5_TILES, LANES), F32),
        ],
        compiler_params=pltpu.CompilerParams(
            dimension_semantics=("arbitrary", "arbitrary"), vmem_limit_bytes=VMEM_LIMIT),
        name="s5_mixer",
    )(x, mod, g_pre, wu, wg, bbr, bbi, cre, cmi, abr, abi, d_skip, glu_w, glu_b, s0r, s0i)


def _lru_kernel(x_ref, mod_ref, g_ref, wx_ref, wg_ref, cw_ref, cb_ref, wa4_ref, ba_ref, wx4_ref, bx_ref, lam_ref,
                conv0_ref, h0_ref,
                yz_ref, conv_out_ref, h_out_ref,
                xbuf_ref, a_ref, b_ref, hc_ref, *, tile):
    t = pl.program_id(1)
    pad = SUBLANES
    hist = CONV_W - 1
    seqs = x_ref.shape[0]

    @pl.when(t == 0)
    def _():
        xbuf_ref[:, pad - hist:pad, :] = conv0_ref[...]
        hc_ref[...] = h0_ref[...]

    h = _pre_norm(x_ref, g_ref, mod_ref)
    xl = _dot(h, wx_ref[...])
    zg = _silu(_dot(h, wg_ref[...]))
    xcs, tails = [], []
    for s in range(seqs):
        xs = xl[s * tile:(s + 1) * tile, :]
        xbuf_ref[s, pad:pad + tile, :] = xs
        acc = cb_ref[...] + xs * cw_ref[CONV_W - 1:CONV_W, :]
        for j in range(hist):
            acc = acc + xbuf_ref[s, pad - hist + j:pad - hist + j + tile, :] * cw_ref[j:j + 1, :]
        tails.append(xbuf_ref[s, pad + tile - hist:pad + tile, :])
        xbuf_ref[s, pad - hist:pad, :] = tails[s]
        xcs.append(acc)
    xc = jnp.concatenate(xcs, axis=0)

    ga, gx = [], []
    blk = MXU_TILE
    for s in range(D_MODEL // blk):
        xb = xc[:, blk * s:blk * (s + 1)].astype(BF16)
        ga.append(_dot(xb, wa4_ref[s]))
        gx.append(_dot(xb, wx4_ref[s]))
    gate_r = _sigmoid(jnp.concatenate(ga, axis=1) + ba_ref[...])
    gate_i = _sigmoid(jnp.concatenate(gx, axis=1) + bx_ref[...])
    log_a = -LRU_C * gate_r * _softplus(-lam_ref[...])
    a = jnp.exp(log_a)
    a_ref[...] = a
    var = -jnp.tanh(log_a) * (a * a + 1.0)
    b_ref[...] = jnp.where(var > 0.0, var * lax.rsqrt(var), 0.0) * (gate_i * xc)

    row8 = lax.broadcasted_iota(jnp.int32, (SUBLANES, 1), 0)

    def row_body(g, carry):
        rs = pl.ds(pl.multiple_of(g * SUBLANES, SUBLANES), SUBLANES)
        av = a_ref[rs, :]
        bv = b_ref[rs, :]
        for sh in (1, 2, 4):
            keep = row8 >= sh
            ash = jnp.where(keep, pltpu.roll(av, sh, 0), 1.0)
            bsh = jnp.where(keep, pltpu.roll(bv, sh, 0), 0.0)
            bv = bv + av * bsh
            av = av * ash
        hs = bv + av * carry
        b_ref[rs, :] = hs
        return hs[SUBLANES - 1:SUBLANES, :]

    groups = tile // SUBLANES
    for s in range(seqs):
        hc_ref[s] = lax.fori_loop(s * groups, (s + 1) * groups, row_body, hc_ref[s], unroll=4)
    yz_ref[...] = (b_ref[...] * zg).reshape(seqs, tile, D_MODEL)

    @pl.when(t == pl.num_programs(1) - 1)
    def _():
        for s in range(seqs):
            conv_out_ref[s] = tails[s]
        h_out_ref[...] = hc_ref[...]


def _lru_mixer(x, mod, g_pre, wx, wg, conv_w, conv_b, wa4, ba, wx4, bx, lam, conv0, h0, *, sel, tile, seqs):
    bsz, tp, _ = x.shape
    assert tp % tile == 0 and tile % SUBLANES == 0 and tile >= SUBLANES and bsz % seqs == 0
    nt = tp // tile
    weight, state, mod_spec = _specs(*sel, seqs=seqs)
    kern = functools.partial(_lru_kernel, tile=tile)
    return pl.pallas_call(
        kern,
        grid=(bsz // seqs, nt),
        in_specs=[
            pl.BlockSpec((seqs, tile, D_MODEL), lambda b, t: (b, t, 0)),
            mod_spec(mod),
            weight(g_pre), weight(wx), weight(wg), weight(conv_w), weight(conv_b), weight(wa4), weight(ba),
            weight(wx4), weight(bx), weight(lam), state(conv0), state(h0),
        ],
        out_specs=[
            pl.BlockSpec((seqs, tile, D_MODEL), lambda b, t: (b, t, 0)),
            pl.BlockSpec((seqs, CONV_W - 1, D_MODEL), lambda b, t: (b, 0, 0)),
            pl.BlockSpec((seqs, 1, D_MODEL), lambda b, t: (b, 0, 0)),
        ],
        out_shape=[
            jax.ShapeDtypeStruct((bsz, tp, D_MODEL), F32),
            jax.ShapeDtypeStruct((bsz, CONV_W - 1, D_MODEL), F32),
            jax.ShapeDtypeStruct((bsz, 1, D_MODEL), F32),
        ],
        scratch_shapes=[
            pltpu.VMEM((seqs, tile + SUBLANES, D_MODEL), F32),
            pltpu.VMEM((seqs * tile, D_MODEL), F32),
            pltpu.VMEM((seqs * tile, D_MODEL), F32),
            pltpu.VMEM((seqs, 1, D_MODEL), F32),
        ],
        compiler_params=pltpu.CompilerParams(
            dimension_semantics=("arbitrary", "arbitrary"), vmem_limit_bytes=VMEM_LIMIT),
        name="rglru_mixer",
    )(x, mod, g_pre, wx, wg, conv_w, conv_b, wa4, ba, wx4, bx, lam, conv0, h0)


def _merge_kernel(x_ref, mod_ref, g_ref, gpost_ref, wm_ref, wo_ref, yr_ref, ys_ref, yl_ref, o_ref):
    seqs, tile, _ = x_ref.shape
    rows = lambda ref: ref[...].reshape(seqs * tile, D_MODEL)
    m = _sigmoid(_dot(_pre_norm(x_ref, g_ref, mod_ref), wm_ref[...]))
    merged = (m[:, :D_MODEL] * rows(yr_ref) + m[:, D_MODEL:2 * D_MODEL] * rows(ys_ref)
              + m[:, 2 * D_MODEL:] * rows(yl_ref))
    o = _dot(merged.astype(BF16), wo_ref[...])
    ms = jnp.mean(o * o, axis=-1, keepdims=True)
    o = (o * lax.rsqrt(ms + RMS_EPS)) * gpost_ref[...]
    for s in range(seqs):
        o_ref[s] = x_ref[s] + mod_ref[s][2:3, :] * o[s * tile:(s + 1) * tile, :]


def _merge_out(x, mod, g_pre, g_post, wm, wo, yr, ys, yl, *, sel, tile, seqs):
    bsz, tp, _ = x.shape
    assert tp % tile == 0 and bsz % seqs == 0
    nt = tp // tile
    act = pl.BlockSpec((seqs, tile, D_MODEL), lambda b, t: (b, t, 0))
    weight, _, mod_spec = _specs(*sel, seqs=seqs)
    return pl.pallas_call(
        _merge_kernel,
        grid=(bsz // seqs, nt),
        in_specs=[act, mod_spec(mod), weight(g_pre), weight(g_post), weight(wm), weight(wo), act, act, act],
        out_specs=act,
        out_shape=jax.ShapeDtypeStruct((bsz, tp, D_MODEL), F32),
        compiler_params=pltpu.CompilerParams(
            dimension_semantics=("arbitrary", "arbitrary"), vmem_limit_bytes=VMEM_LIMIT),
        name="merge_out",
    )(x, mod, g_pre, g_post, wm, wo, yr, ys, yl)


def _block_diag_expand(w, per):
    depth, n, a, b = w.shape
    w = w.reshape(depth, n // per, per, a, b)
    eye = jnp.eye(per, dtype=w.dtype)
    return jnp.einsum("lsgab,gh->lsgahb", w, eye).reshape(depth, n // per, per * a, per * b)


def _layer(x, mod, state, prm, *, sel, t_valid, tiles):
    shift_row, wkv, s_re, s_im, lru_h, lru_conv = state
    bsz = x.shape[0]
    g_pre = prm["g_pre"]

    xr = x
    if x.shape[1] % tiles["rwkv"] != 0:
        xr = jnp.pad(x, ((0, 0), (0, tiles["rwkv"] - x.shape[1] % tiles["rwkv"]), (0, 0)))
    yz_r, shift_new, wkv_new = _rwkv_mixer(
        xr, mod, g_pre, prm["w_rwkv"], prm["w_rwkv_g"], prm["rwkv_vecs"], prm["w2a"], shift_row, wkv,
        sel=sel, tile=tiles["rwkv"], t_valid=t_valid, seqs=tiles["rwkv_seqs"])
    yz_r = yz_r[:, :t_valid]
    yz_s, s_re_new, s_im_new = _s5_mixer(
        x, mod, g_pre, prm["w_s5"], prm["w_s5_g"], prm["bbr"], prm["bbi"], prm["cre"], prm["cmi"],
        prm["abr"], prm["abi"], prm["s5_d"], prm["glu_w"], prm["glu_b"], s_re, s_im, sel=sel, tile=tiles["s5"],
        seqs=tiles["seqs"])
    yz_l, conv_new, h_new = _lru_mixer(
        x, mod, g_pre, prm["w_lru"], prm["w_lru_g"], prm["conv_w"], prm["conv_b"], prm["wa4"], prm["ba"],
        prm["wx4"], prm["bx"], prm["lam"], lru_conv, lru_h, sel=sel, tile=tiles["lru"], seqs=tiles["seqs"])
    x_new = _merge_out(x, mod, g_pre, prm["g_post"], prm["w_merge"], prm["w_out"], yz_r, yz_s, yz_l,
                       sel=sel, tile=tiles["merge"], seqs=tiles["seqs"])
    new_state = (shift_new.reshape(bsz, SHIFT_W), wkv_new.reshape(bsz, HEADS, HEAD, HEAD),
                 s_re_new.reshape(bsz, S5_GROUPS, S5_STATES), s_im_new.reshape(bsz, S5_GROUPS, S5_STATES),
                 h_new.reshape(bsz, D_MODEL), conv_new)
    return x_new, new_state


def _stacked_params(w_in, w_out, norm_pre, norm_post, rwkv, s5, lru):
    depth = w_in.shape[0]
    d = D_MODEL
    wb = w_in.astype(BF16)
    o = SHIFT_W
    row = lambda a: a.reshape(depth, 1, -1)
    mu, w0, w2, a0, a2, k_k, k_a, r_k, ln_g, ln_b = rwkv
    a_re, a_im, log_step, b_re, b_im, c_re, c_im, s5_d, glu_w, glu_b = s5
    conv_w, conv_b, wa, ba, wx, bx, lam = lru
    zeros = jnp.zeros((depth, LORA, d), F32)
    w2a = jnp.concatenate([jnp.concatenate([w2, zeros], axis=2), jnp.concatenate([zeros, a2], axis=2)], axis=1)
    abr, abi, bbr, bbi = _s5_prep(a_re, a_im, log_step, b_re, b_im)
    per = S5_GROUPS // S5_SETS
    gate_per = MXU_TILE // LRU_BLOCK
    return dict(
        g_pre=row(norm_pre), g_post=row(norm_post),
        w_rwkv=wb[:, :, :o], w_rwkv_g=wb[:, :, o:o + d],
        w_s5=wb[:, :, o + d:o + 2 * d], w_s5_g=wb[:, :, o + 2 * d:o + 3 * d],
        w_lru=wb[:, :, o + 3 * d:o + 4 * d], w_lru_g=wb[:, :, o + 4 * d:o + 5 * d],
        w_merge=wb[:, :, o + 5 * d:], w_out=w_out.astype(BF16),
        rwkv_vecs=(row(mu), row(w0), row(a0), row(k_k), row(k_a), row(r_k), row(ln_g), row(ln_b)),
        w2a=w2a.astype(BF16),
        bbr=_block_diag_expand(bbr, S5_IN_GROUPS).astype(BF16),
        bbi=_block_diag_expand(bbi, S5_IN_GROUPS).astype(BF16),
        cre=_block_diag_expand(jnp.swapaxes(c_re, 2, 3), per).astype(BF16),
        cmi=_block_diag_expand(-jnp.swapaxes(c_im, 2, 3), per).astype(BF16),
        abr=abr.reshape(depth, S5_TILES, LANES), abi=abi.reshape(depth, S5_TILES, LANES),
        s5_d=row(s5_d), glu_w=glu_w.astype(BF16), glu_b=row(glu_b),
        conv_w=conv_w, conv_b=row(conv_b),
        wa4=_block_diag_expand(wa, gate_per).astype(BF16), ba=row(ba),
        wx4=_block_diag_expand(wx, gate_per).astype(BF16), bx=row(bx), lam=row(lam),
    )


def _state_layout(shift, wkv, s_re, s_im, lru_h, lru_conv):
    n, bsz = shift.shape[:2]
    return (shift.reshape(n, bsz, 1, SHIFT_W), wkv.reshape(n, bsz, PAIRS, LANES, HEAD),
            s_re.reshape(n, bsz, S5_TILES, LANES), s_im.reshape(n, bsz, S5_TILES, LANES),
            lru_h.reshape(n, bsz, 1, D_MODEL), lru_conv)


def _tiles(t, bsz):
    pick = lambda want: want if t % want == 0 else t
    share = lambda rows, limit: max(n for n in range(1, bsz + 1) if bsz % n == 0 and n * rows <= max(limit, rows))
    seqs = share(t, SHORT_ROWS) if t < SHORT_ROWS else 1
    rwkv = RWKV_TILE if t % RWKV_TILE == 0 else RWKV_CHUNK
    rwkv_seqs = share(RWKV_CHUNK, RWKV_TILE) if t <= RWKV_CHUNK else 1
    return dict(rwkv=rwkv, rwkv_seqs=rwkv_seqs, s5=pick(S5_TILE), lru=pick(WIDE_TILE), merge=pick(WIDE_TILE),
                seqs=seqs)


def kernel(x_prompt, x_sample, state_rwkv_shift, state_rwkv_wkv, state_s5_re, state_s5_im, state_lru_h,
           state_lru_conv, c_prompt, c_sample, ada_w, ada_b, norm_pre, norm_post, w_in, w_out, rwkv_mu, rwkv_w0,
           rwkv_w2, rwkv_a0, rwkv_a2, rwkv_k_k, rwkv_k_a, rwkv_r_k, rwkv_ln_g, rwkv_ln_b, s5_a_re, s5_a_im,
           s5_log_step, s5_b_re, s5_b_im, s5_c_re, s5_c_im, s5_d, s5_glu_w, s5_glu_b, lru_conv_w, lru_conv_b,
           lru_wa, lru_ba, lru_wx, lru_bx, lru_lambda):
    depth = w_in.shape[0]
    bp, tp, _ = x_prompt.shape
    bs, ts, _ = x_sample.shape
    c_all = jnp.concatenate([c_sample, c_prompt], axis=0)
    pad = (-c_all.shape[0]) % SUBLANES
    c_all = jnp.pad(c_all, ((0, pad), (0, 0)))
    mod_all = _modulation(c_all, ada_w, ada_b)
    mod_all = mod_all.reshape(depth, c_all.shape[0], 3, D_MODEL)
    rwkv = (rwkv_mu, rwkv_w0, rwkv_w2, rwkv_a0, rwkv_a2, rwkv_k_k, rwkv_k_a,
            rwkv_r_k.reshape(depth, D_MODEL), rwkv_ln_g, rwkv_ln_b)
    s5 = (s5_a_re, s5_a_im, s5_log_step, s5_b_re, s5_b_im, s5_c_re, s5_c_im, s5_d, s5_glu_w, s5_glu_b)
    lru = (lru_conv_w, lru_conv_b, lru_wa, lru_ba, lru_wx, lru_bx, lru_lambda)
    prm = _stacked_params(w_in, w_out, norm_pre, norm_post, rwkv, s5, lru)
    zero = _state_layout(jnp.zeros((1, bp, SHIFT_W), F32), jnp.zeros((1, bp, HEADS, HEAD, HEAD), F32),
                         jnp.zeros((1, bp, S5_GROUPS, S5_STATES), F32),
                         jnp.zeros((1, bp, S5_GROUPS, S5_STATES), F32),
                         jnp.zeros((1, bp, D_MODEL), F32), jnp.zeros((1, bp, CONV_W - 1, D_MODEL), F32))
    st_in = _state_layout(state_rwkv_shift, state_rwkv_wkv, state_s5_re, state_s5_im, state_lru_h,
                          state_lru_conv)

    xp, xs = x_prompt, x_sample
    new_p = [[] for _ in range(6)]
    new_s = [[] for _ in range(6)]
    for l in range(depth):
        xp, st_p = _layer(xp, mod_all, zero, prm, sel=(l, 0, bs), t_valid=tp, tiles=_tiles(tp, bp))
        xs, st_s = _layer(xs, mod_all, st_in, prm, sel=(l, l, 0), t_valid=ts, tiles=_tiles(ts, bs))
        for i in range(6):
            new_p[i].append(st_p[i])
            new_s[i].append(st_s[i])
    sp = [jnp.stack(z, axis=0) for z in new_p]
    ss = [jnp.stack(z, axis=0) for z in new_s]
    return (xp, xs, sp[0], sp[1], sp[2], sp[3], sp[4], sp[5], ss[0], ss[1], ss[2], ss[3], ss[4], ss[5])
```

```python
import functools

import jax
import jax.numpy as jnp
from jax import lax
from jax.experimental import pallas as pl
from jax.experimental.pallas import tpu as pltpu

F32 = jnp.float32
BF16 = jnp.bfloat16
HIGHEST = lax.Precision.HIGHEST

SUBLANES = 8
LANES = 128
MXU_TILE = 256
VMEM_LIMIT = 56 * 1024 * 1024

D_MODEL = 1024
HEAD = 64
HEADS = D_MODEL // HEAD
PAIRS = HEADS // 2
LORA = 64
SHIFT_W = 3 * D_MODEL + 2 * LORA
S5_GROUPS = 64
S5_STATES = 64
S5_STATE_W = S5_GROUPS * S5_STATES
LRU_BLOCK = 64
S5_SETS = 4
S5_IN_GROUPS = 8
S5_TILES = S5_STATE_W // LANES
S5_ROW_STRIDE = 40
CONV_W = 4
LRU_C = 8.0
RMS_EPS = 1e-6
GN_EPS = 64e-5
RWKV_CHUNK = 64
RWKV_TILE = 256
S5_TILE = 256
WIDE_TILE = 512
SHORT_ROWS = 512
DECAY_SCALE = 0.6065306597126334
RWKV_INSTANCES = 32


def _sigmoid(x):
    return 0.5 * jnp.tanh(0.5 * x) + 0.5


def _silu(x):
    h = 0.5 * x
    return h * (jnp.tanh(h) + 1.0)


def _softplus(x):
    return jnp.maximum(x, 0.0) + jnp.log1p(jnp.exp(-jnp.abs(x)))


def _norm_mod(x, g, mod):
    ms = jnp.mean(x * x, axis=-1, keepdims=True)
    return (x * lax.rsqrt(ms + RMS_EPS)) * (g * (1.0 + mod[1:2, :])) + mod[0:1, :]


def _dot(a, b):
    return jnp.dot(a, b, preferred_element_type=F32)


def _dot_hi(a, b):
    return jnp.dot(a, b, preferred_element_type=F32, precision=HIGHEST)


def _split2(x):
    hi = x.astype(BF16)
    lo = (x - hi.astype(F32)).astype(BF16)
    return hi, lo


_NN = (((1,), (0,)), ((), ()))
_NT = (((1,), (1,)), ((), ()))
_TN = (((0,), (0,)), ((), ()))


def _mm(a, b, dims=_NN):
    return lax.dot_general(a.astype(BF16), b.astype(BF16), dims, preferred_element_type=F32)


def _head_sum(x, ones_blk):
    outs = []
    for s in range(D_MODEL // MXU_TILE):
        outs.append(_dot(x[:, MXU_TILE * s:MXU_TILE * (s + 1)].astype(BF16), ones_blk))
    return jnp.concatenate(outs, axis=1)


def _block_ones(n, blk):
    ri = lax.broadcasted_iota(jnp.int32, (n, n), 0)
    ci = lax.broadcasted_iota(jnp.int32, (n, n), 1)
    sh = blk.bit_length() - 1
    return ((ri >> sh) == (ci >> sh)).astype(F32).astype(BF16)


def _mod_kernel(c_ref, w_ref, b_ref, o_ref):
    s = _silu(c_ref[...])
    o_ref[...] = _dot_hi(s, w_ref[...]) + b_ref[...]


def _modulation(c_all, ada_w, ada_b):
    depth = ada_w.shape[0]
    rows = c_all.shape[0]
    return pl.pallas_call(
        _mod_kernel,
        grid=(depth, 3),
        in_specs=[
            pl.BlockSpec((rows, D_MODEL), lambda l, j: (0, 0)),
            pl.BlockSpec((None, D_MODEL, D_MODEL), lambda l, j: (l, 0, j)),
            pl.BlockSpec((None, 1, D_MODEL), lambda l, j: (l, 0, j)),
        ],
        out_specs=pl.BlockSpec((None, rows, D_MODEL), lambda l, j: (l, 0, j)),
        out_shape=jax.ShapeDtypeStruct((depth, rows, 3 * D_MODEL), F32),
        name="adaln_mod",
    )(c_all, ada_w, ada_b.reshape(depth, 1, 3 * D_MODEL))


def _s5_prep_kernel(are_ref, aim_ref, ls_ref, bre_ref, bim_ref, abr_ref, abi_ref, bbr_ref, bbi_ref):
    are = are_ref[...]
    aim = aim_ref[...]
    dt = jnp.exp(ls_ref[...])
    mag = jnp.exp(are * dt)
    abr = mag * jnp.cos(aim * dt)
    abi = mag * jnp.sin(aim * dt)
    abr_ref[...] = abr
    abi_ref[...] = abi
    nr = abr - 1.0
    ni = abi
    den = are * are + aim * aim
    cr = (nr * are + ni * aim) / den
    ci = (ni * are - nr * aim) / den
    bre = bre_ref[...]
    bim = bim_ref[...]
    bbr_ref[...] = cr[:, None, :] * bre - ci[:, None, :] * bim
    bbi_ref[...] = cr[:, None, :] * bim + ci[:, None, :] * bre


def _s5_prep(a_re, a_im, log_step, b_re, b_im):
    depth, g, p = a_re.shape
    i = b_re.shape[-1]
    layer = lambda *dims: pl.BlockSpec((None,) + dims, lambda l: (l,) + (0,) * len(dims))
    return pl.pallas_call(
        _s5_prep_kernel,
        grid=(depth,),
        in_specs=[layer(g, p), layer(g, p), layer(g, 1), layer(g, i, p), layer(g, i, p)],
        out_specs=(layer(g, p), layer(g, p), layer(g, i, p), layer(g, i, p)),
        out_shape=(
            jax.ShapeDtypeStruct((depth, g, p), F32),
            jax.ShapeDtypeStruct((depth, g, p), F32),
            jax.ShapeDtypeStruct((depth, g, i, p), F32),
            jax.ShapeDtypeStruct((depth, g, i, p), F32),
        ),
        name="s5_discretise",
    )(a_re, a_im, log_step.reshape(depth, g, 1), jnp.swapaxes(b_re, 2, 3), jnp.swapaxes(b_im, 2, 3))


def _rwkv_kernel(x_ref, mod_ref, g_ref, wr_ref, wg_ref, mu_ref, w0_ref, w2a_ref, a0_ref, kk_ref, ka_ref,
                 rk_ref, lng_ref, lnb_ref, shift0_ref, wkv0_ref,
                 yz_ref, shift_out_ref, wkv_out_ref,
                 carry_ref, st_ref, at_ref, rt_ref, bt_ref, kt_ref, bh_ref, kh_ref, v_ref, pl_ref, y_ref,
                 nh_ref, t_ref, aak_ref, ark_ref, rhs_ref, qg_ref, hc_ref, yc_ref, ones_ref,
                 *, tile, chunk, t_valid, group):
    t = pl.program_id(1)
    n_chunks = tile // chunk
    two = 2 * chunk
    n_inst = group * n_chunks
    chained = x_ref.shape[0] == 1

    def pair_state(s0):
        rh = lax.broadcasted_iota(jnp.int32, (LANES, LANES), 0) >> (HEAD.bit_length() - 1)
        ch = lax.broadcasted_iota(jnp.int32, (LANES, LANES), 1) >> (HEAD.bit_length() - 1)
        return jnp.where(rh == ch, jnp.concatenate([s0, s0], axis=1), 0.0).T

    def head_states(st):
        sv = st.T
        return sv[:, :HEAD] + sv[:, HEAD:]

    if chained:
        @pl.when(t == 0)
        def _():
            carry_ref[...] = shift0_ref[0]
            for q in range(PAIRS):
                st_ref[q] = pair_state(wkv0_ref[0, q])

    h = _pre_norm(x_ref, g_ref, mod_ref)
    p = _dot(h, wr_ref[...])
    zg = _silu(_dot(h, wg_ref[...]))

    rows = lax.broadcasted_iota(jnp.int32, (tile, 1), 0)
    rmod = rows & (chunk - 1)
    last = (t_valid - 1) % chunk
    if chained:
        prev = jnp.where(rows == 0, carry_ref[...], pltpu.roll(p, 1, 0))
        carry_ref[...] = p[tile - chunk + last:tile - chunk + last + 1, :]
    else:
        first = jnp.concatenate([jnp.broadcast_to(shift0_ref[s], (chunk, SHIFT_W)) for s in range(n_chunks)], axis=0)
        prev = jnp.where(rmod == 0, first, pltpu.roll(p, 1, 0))
        for s in range(n_chunks):
            shift_out_ref[s] = p[s * chunk + last:s * chunk + last + 1, :]
    pm = p + (prev - p) * mu_ref[...]

    r = pm[:, 0:D_MODEL]
    k = pm[:, D_MODEL:2 * D_MODEL]
    v = pm[:, 2 * D_MODEL:3 * D_MODEL]
    wa = pm[:, 3 * D_MODEL:SHIFT_W]
    lane = lax.broadcasted_iota(jnp.int32, (1, LANES), 1)
    wa = jnp.where(lane < LORA, jnp.tanh(wa), wa)
    lora = _dot(wa.astype(BF16), w2a_ref[...])
    logd = -DECAY_SCALE * _sigmoid(w0_ref[...] + lora[:, :D_MODEL])
    a_sig = _sigmoid(a0_ref[...] + lora[:, D_MODEL:])

    @pl.when(t == 0)
    def _():
        ones_ref[...] = _block_ones(MXU_TILE, HEAD)

    ones_blk = ones_ref[...]
    kk = k * kk_ref[...]
    kk = kk * jnp.minimum(lax.rsqrt(_head_sum(kk * kk, ones_blk)), 1e12)
    k2 = k * (1.0 + (a_sig - 1.0) * ka_ref[...])
    bvec = kk * a_sig
    if t_valid % chunk != 0:
        ok = rmod < t_valid % chunk
        logd = jnp.where(ok, logd, 0.0)
        bvec = jnp.where(ok, bvec, 0.0)
        k2 = jnp.where(ok, k2, 0.0)

    cum = logd
    s = 1
    while s < chunk:
        cum = cum + jnp.where(rmod >= s, pltpu.roll(cum, s, 0), 0.0)
        s *= 2
    cum_last = jnp.concatenate(
        [jnp.broadcast_to(cum[(j + 1) * chunk - 1:(j + 1) * chunk, :], (chunk, D_MODEL)) for j in range(n_chunks)],
        axis=0)
    p_inc = jnp.exp(cum)
    p_inv = jnp.exp(-cum)
    p_exc = jnp.exp(cum - logd)
    p_end = jnp.exp(cum_last - cum)

    def to_pairs(ref, val):
        for q in range(PAIRS):
            ref[q] = val[:, LANES * q:LANES * (q + 1)]

    to_pairs(at_ref, -kk * p_exc)
    to_pairs(rt_ref, r * p_inc)
    to_pairs(bt_ref, bvec * p_inv)
    to_pairs(kt_ref, k2 * p_inv)
    to_pairs(bh_ref, bvec * p_end)
    to_pairs(kh_ref, k2 * p_end)
    to_pairs(v_ref, v)
    for j in range(n_chunks):
        row = jnp.exp(cum[(j + 1) * chunk - 1:(j + 1) * chunk, :])
        for q in range(PAIRS):
            pl_ref[q, j] = jnp.broadcast_to(row[:, LANES * q:LANES * (q + 1)], (SUBLANES, LANES))

    m_a = (lane < HEAD).astype(F32)
    m_b = 1.0 - m_a
    ri = lax.broadcasted_iota(jnp.int32, (two, two), 0)
    ci = lax.broadcasted_iota(jnp.int32, (two, two), 1)
    csh = chunk.bit_length() - 1
    same_head = (ri >> csh) == (ci >> csh)
    strict = jnp.logical_and(same_head, (ri & (chunk - 1)) > (ci & (chunk - 1)))
    incl = jnp.logical_and(same_head, (ri & (chunk - 1)) >= (ci & (chunk - 1)))
    eye = ri == ci
    head_a = ri < chunk
    off_masks = []
    for lg in range(csh):
        off_masks.append(jnp.logical_and(strict, jnp.logical_and((ri >> (lg + 1)) == (ci >> (lg + 1)),
                                                                 (ri >> lg) != (ci >> lg))))

    def stack(x):
        return jnp.concatenate([x * m_a, x * m_b], axis=0)

    gsh = group.bit_length() - 1
    unroll = n_inst

    def group_body(g, carry):
        def where_is(i):
            q = g * group + (i & (group - 1))
            j = i >> gsh
            return q, j, pl.ds(pl.multiple_of(j * chunk, chunk), chunk)

        def gram(i, c):
            q, _, sl = where_is(i)
            atm = stack(at_ref[q, sl, :])
            rtm = stack(rt_ref[q, sl, :])
            bt = bt_ref[q, sl, :]
            kt = kt_ref[q, sl, :]
            o1 = _mm(jnp.concatenate([atm, rtm], axis=0), jnp.concatenate([bt, kt], axis=0), _NT)
            o1_sw = pltpu.roll(o1, chunk, 1)
            n_ab = jnp.where(strict, jnp.where(head_a, o1[:two], o1_sw[:two]), 0.0)
            nh_ref[i] = n_ab.astype(BF16)
            t_ref[i] = jnp.where(eye, 1.0, jnp.where(off_masks[0], n_ab, 0.0)).astype(BF16)
            aak_ref[i] = jnp.where(strict, jnp.where(head_a, o1_sw[:two], o1[:two]), 0.0).astype(BF16)
            ark_ref[i] = jnp.concatenate([jnp.where(incl, jnp.where(head_a, o1[two:], o1_sw[two:]), 0.0),
                                          jnp.where(incl, jnp.where(head_a, o1_sw[two:], o1[two:]), 0.0)],
                                         axis=1).astype(BF16)
            return c

        lax.fori_loop(0, n_inst, gram, 0, unroll=unroll)

        for off in off_masks[1:]:
            def level(i, c, off=off):
                t_inv = t_ref[i]
                x = _dot(jnp.where(off, nh_ref[i], 0.0), t_inv).astype(BF16)
                t_ref[i] = t_inv + _dot(t_inv, x).astype(BF16)
                return c

            lax.fori_loop(0, n_inst, level, 0, unroll=unroll)

        def right_side(i, c):
            q, _, sl = where_is(i)
            atm = stack(at_ref[q, sl, :])
            vm = stack(v_ref[q, sl, :])
            z0_hi, z0_lo = _split2(jnp.concatenate([atm, _mm(aak_ref[i], vm)], axis=1))
            rhs_ref[i] = jnp.concatenate([z0_hi, z0_lo], axis=0)
            return c

        lax.fori_loop(0, n_inst, right_side, 0, unroll=unroll)

        def apply(i, c):
            q, _, sl = where_is(i)
            vm = stack(v_ref[q, sl, :])
            t_inv = t_ref[i]
            z = _dot(t_inv, rhs_ref[i, :two, :]) + _dot(t_inv, rhs_ref[i, two:, :])
            rhs_ref[i] = jnp.concatenate([z, jnp.concatenate([jnp.zeros_like(vm), vm], axis=1)], axis=0).astype(BF16)
            return c

        lax.fori_loop(0, n_inst, apply, 0, unroll=unroll)

        def operators(i, c):
            q, j, sl = where_is(i)
            rhs2 = rhs_ref[i]
            k_all = g * n_inst + i
            top = _mm(jnp.concatenate([stack(bh_ref[q, sl, :]), stack(kh_ref[q, sl, :])], axis=0), rhs2, _TN)
            bot = _dot(ark_ref[i], rhs2)
            g_mat = top[:, :two] + jnp.where(eye, jnp.broadcast_to(pl_ref[q, j][0:1, :], (two, two)), 0.0)
            q_eff = bot[:, :two] + stack(rt_ref[q, sl, :])
            qg_ref[k_all] = jnp.concatenate([q_eff, g_mat], axis=0).astype(BF16)
            hc_ref[k_all] = top[:, two:]
            yc_ref[k_all] = bot[:, two:]
            return c

        lax.fori_loop(0, n_inst, operators, 0, unroll=unroll)
        return carry

    lax.fori_loop(0, PAIRS // group, group_body, 0)

    if chained:
        states = [st_ref[q] for q in range(PAIRS)]
    for j in range(n_chunks):
        for q in range(PAIRS):
            k_all = (q // group) * n_inst + j * group + q % group
            st = (states[q] if chained else pair_state(wkv0_ref[j, q])).astype(BF16)
            both = _dot(qg_ref[k_all], st)
            ym = both[:two, :] + yc_ref[k_all]
            y_ref[q, j * chunk:(j + 1) * chunk, :] = ym[:chunk, :] + ym[chunk:, :]
            st_new = both[two:, :] + hc_ref[k_all]
            if chained:
                states[q] = st_new
            else:
                wkv_out_ref[j, q] = head_states(st_new)
    if chained:
        for q in range(PAIRS):
            st_ref[q] = states[q]

    ys = jnp.concatenate([y_ref[q] for q in range(PAIRS)], axis=1)
    mean = _head_sum(ys, ones_blk) * (1.0 / HEAD)
    yc = ys - mean
    var = _head_sum(yc * yc, ones_blk) * (1.0 / HEAD)
    yn = yc * lax.rsqrt(var + GN_EPS) * lng_ref[...] + lnb_ref[...]
    bonus = _head_sum(r * k2 * rk_ref[...], ones_blk) * v
    yz_ref[...] = ((yn + bonus) * zg).reshape(yz_ref.shape)

    if chained:
        @pl.when(t == pl.num_programs(1) - 1)
        def _():
            shift_out_ref[0] = carry_ref[...]
            for q in range(PAIRS):
                wkv_out_ref[0, q] = head_states(st_ref[q])


def _specs(layer, state_layer, mod_row, seqs=None):
    assert seqs is None or mod_row % seqs == 0
    per_step = 1 if seqs is None else seqs

    def weight(a):
        return pl.BlockSpec((None,) + a.shape[1:], lambda b, t: (layer,) + (0,) * (a.ndim - 1))

    def state(a):
        return pl.BlockSpec((None, seqs) + a.shape[2:], lambda b, t: (state_layer, b) + (0,) * (a.ndim - 2))

    def mod(a):
        return pl.BlockSpec((None, seqs) + a.shape[2:], lambda b, t: (layer, mod_row // per_step + b, 0, 0))

    return weight, state, mod


def _pre_norm(x_ref, g_ref, mod_ref):
    g = g_ref[...]
    rows = [_norm_mod(x_ref[s], g, mod_ref[s]) for s in range(x_ref.shape[0])]
    return jnp.concatenate(rows, axis=0).astype(BF16)


def _rwkv_mixer(x, mod, g_pre, wr, wg, vecs, w2a, shift0, wkv0, *, sel, tile, t_valid, seqs):
    bsz, tp, _ = x.shape
    chunk = RWKV_CHUNK
    assert tp % tile == 0 and tile % chunk == 0 and bsz % seqs == 0
    assert t_valid == tp or tp == tile
    assert seqs == 1 or tp == chunk
    nt = tp // tile
    block_rows = tile
    tile = seqs * tile
    mu, w0, a0, k_k, k_a, r_k, ln_g, ln_b = vecs
    weight, state, mod_spec = _specs(*sel, seqs=seqs)
    pair_buf = pltpu.VMEM((PAIRS, tile, LANES), F32)
    group = min(PAIRS, max(1, RWKV_INSTANCES // (tile // chunk)))
    n_inst = group * (tile // chunk)
    two = 2 * chunk
    inst = lambda cols, dtype: pltpu.VMEM((n_inst, two, cols), dtype)
    every = lambda dtype: pltpu.VMEM((PAIRS * (tile // chunk), two, two), dtype)
    kern = functools.partial(_rwkv_kernel, tile=tile, chunk=chunk, t_valid=t_valid, group=group)
    return pl.pallas_call(
        kern,
        grid=(bsz // seqs, nt),
        in_specs=[
            pl.BlockSpec((seqs, block_rows, D_MODEL), lambda b, t: (b, t, 0)),
            mod_spec(mod),
            weight(g_pre), weight(wr), weight(wg), weight(mu), weight(w0), weight(w2a), weight(a0),
            weight(k_k), weight(k_a), weight(r_k), weight(ln_g), weight(ln_b),
            state(shift0), state(wkv0),
        ],
        out_specs=[
            pl.BlockSpec((seqs, block_rows, D_MODEL), lambda b, t: (b, t, 0)),
            pl.BlockSpec((seqs, 1, SHIFT_W), lambda b, t: (b, 0, 0)),
            pl.BlockSpec((seqs, PAIRS, LANES, HEAD), lambda b, t: (b, 0, 0, 0)),
        ],
        out_shape=[
            jax.ShapeDtypeStruct((bsz, tp, D_MODEL), F32),
            jax.ShapeDtypeStruct((bsz, 1, SHIFT_W), F32),
            jax.ShapeDtypeStruct((bsz, PAIRS, LANES, HEAD), F32),
        ],
        scratch_shapes=[
            pltpu.VMEM((1, SHIFT_W), F32),
            pltpu.VMEM((PAIRS, LANES, LANES), F32),
            pair_buf, pair_buf, pair_buf, pair_buf, pair_buf, pair_buf, pair_buf,
            pltpu.VMEM((PAIRS, tile // chunk, SUBLANES, LANES), F32),
            pair_buf,
            inst(two, BF16), inst(two, BF16), inst(two, BF16), inst(2 * two, BF16),
            pltpu.VMEM((n_inst, 2 * two, 2 * two), BF16),
            pltpu.VMEM((PAIRS * (tile // chunk), 2 * two, two), BF16), every(F32), every(F32),
            pltpu.VMEM((MXU_TILE, MXU_TILE), BF16),
        ],
        compiler_params=pltpu.CompilerParams(
            dimension_semantics=("arbitrary", "arbitrary"), vmem_limit_bytes=VMEM_LIMIT),
        name="rwkv7_mixer",
    )(x, mod, g_pre, wr, wg, mu, w0, w2a, a0, k_k, k_a, r_k, ln_g, ln_b, shift0, wkv0)


def _s5_kernel(x_ref, mod_ref, g_ref, wu_ref, wg_ref, bbr_ref, bbi_ref, cre_ref, cmi_ref, abr_ref, abi_ref,
               d_ref, gluw_ref, glub_ref, s0r_ref, s0i_ref,
               yz_ref, sr_out_ref, si_out_ref,
               hr_ref, hi_ref, cr_ref, ci_ref, *, tile):
    t = pl.program_id(1)

    @pl.when(t == 0)
    def _():
        cr_ref[...] = s0r_ref[...]
        ci_ref[...] = s0i_ref[...]

    seqs = x_ref.shape[0]
    frames = seqs * tile
    h = _pre_norm(x_ref, g_ref, mod_ref)
    u = _dot(h, wu_ref[...])
    zg = _silu(_dot(h, wg_ref[...]))
    set_tiles = S5_TILES // S5_SETS

    def frame_rows(c):
        return pl.ds(c, frames, stride=S5_ROW_STRIDE)

    in_tiles = bbr_ref.shape[2] // LANES
    for s in range(D_MODEL // LANES):
        ub = u[:, LANES * s:LANES * (s + 1)].astype(BF16)
        bur = _dot(ub, bbr_ref[s])
        bui = _dot(ub, bbi_ref[s])
        for c in range(in_tiles):
            hr_ref[frame_rows(s * in_tiles + c), :] = bur[:, LANES * c:LANES * (c + 1)]
            hi_ref[frame_rows(s * in_tiles + c), :] = bui[:, LANES * c:LANES * (c + 1)]

    abr = abr_ref[...]
    abi = abi_ref[...]

    def frame(f, carry):
        sr, si = carry
        rows = pl.ds(pl.multiple_of(f * S5_ROW_STRIDE, SUBLANES), S5_TILES)
        nr = abr * sr - abi * si + hr_ref[rows, :]
        ni = abr * si + abi * sr + hi_ref[rows, :]
        hr_ref[rows, :] = nr
        hi_ref[rows, :] = ni
        return nr, ni

    for s in range(seqs):
        cr_ref[s], ci_ref[s] = lax.fori_loop(s * tile, (s + 1) * tile, frame, (cr_ref[s], ci_ref[s]), unroll=16)

    outs = []
    for s in range(S5_SETS):
        hr = jnp.concatenate([hr_ref[frame_rows(s * set_tiles + c), :] for c in range(set_tiles)], axis=1)
        hi = jnp.concatenate([hi_ref[frame_rows(s * set_tiles + c), :] for c in range(set_tiles)], axis=1)
        outs.append(_dot(hr.astype(BF16), cre_ref[s]) + _dot(hi.astype(BF16), cmi_ref[s]))
    y = jnp.concatenate(outs, axis=1) + d_ref[...] * u
    y = 0.5 * y * (1.0 + jnp.tanh(0.7978845608028654 * (y + 0.044715 * (y * y * y))))
    y = y * _sigmoid(_dot(y.astype(BF16), gluw_ref[...]) + glub_ref[...])
    yz_ref[...] = (y * zg).reshape(seqs, tile, D_MODEL)

    @pl.when(t == pl.num_programs(1) - 1)
    def _():
        sr_out_ref[...] = cr_ref[...]
        si_out_ref[...] = ci_ref[...]


def _s5_mixer(x, mod, g_pre, wu, wg, bbr, bbi, cre, cmi, abr, abi, d_skip, glu_w, glu_b, s0r, s0i, *, sel, tile,
              seqs):
    bsz, tp, _ = x.shape
    assert tp % tile == 0 and tile % SUBLANES == 0 and bsz % seqs == 0
    nt = tp // tile
    weight, state, mod_spec = _specs(*sel, seqs=seqs)
    state_out = pl.BlockSpec((seqs, S5_TILES, LANES), lambda b, t: (b, 0, 0))
    kern = functools.partial(_s5_kernel, tile=tile)
    return pl.pallas_call(
        kern,
        grid=(bsz // seqs, nt),
        in_specs=[
            pl.BlockSpec((seqs, tile, D_MODEL), lambda b, t: (b, t, 0)),
            mod_spec(mod),
            weight(g_pre), weight(wu), weight(wg), weight(bbr), weight(bbi), weight(cre), weight(cmi),
            weight(abr), weight(abi), weight(d_skip), weight(glu_w), weight(glu_b), state(s0r), state(s0i),
        ],
        out_specs=[pl.BlockSpec((seqs, tile, D_MODEL), lambda b, t: (b, t, 0)), state_out, state_out],
        out_shape=[
            jax.ShapeDtypeStruct((bsz, tp, D_MODEL), F32),
            jax.ShapeDtypeStruct((bsz, S5_TILES, LANES), F32),
            jax.ShapeDtypeStruct((bsz, S5_TILES, LANES), F32),
        ],
        scratch_shapes=[
            pltpu.VMEM((seqs * tile * S5_ROW_STRIDE, LANES), F32),
            pltpu.VMEM((seqs * tile * S5_ROW_STRIDE, LANES), F32),
            pltpu.VMEM((seqs, S5_TILES, LANES), F32),
            pltpu.VMEM((seqs, S5_TILES, LANES), F32),
        ],
        compiler_params=pltpu.CompilerParams(
            dimension_semantics=("arbitrary", "arbitrary"), vmem_limit_bytes=VMEM_LIMIT),
        name="s5_mixer",
    )(x, mod, g_pre, wu, wg, bbr, bbi, cre, cmi, abr, abi, d_skip, glu_w, glu_b, s0r, s0i)


def _lru_kernel(x_ref, mod_ref, g_ref, wx_ref, wg_ref, cw_ref, cb_ref, wa4_ref, ba_ref, wx4_ref, bx_ref, lam_ref,
                conv0_ref, h0_ref,
                yz_ref, conv_out_ref, h_out_ref,
                xbuf_ref, a_ref, b_ref, hc_ref, *, tile):
    t = pl.program_id(1)
    pad = SUBLANES
    hist = CONV_W - 1
    seqs = x_ref.shape[0]

    @pl.when(t == 0)
    def _():
        xbuf_ref[:, pad - hist:pad, :] = conv0_ref[...]
        hc_ref[...] = h0_ref[...]

    h = _pre_norm(x_ref, g_ref, mod_ref)
    xl = _dot(h, wx_ref[...])
    zg = _silu(_dot(h, wg_ref[...]))
    xcs, tails = [], []
    for s in range(seqs):
        xs = xl[s * tile:(s + 1) * tile, :]
        xbuf_ref[s, pad:pad + tile, :] = xs
        acc = cb_ref[...] + xs * cw_ref[CONV_W - 1:CONV_W, :]
        for j in range(hist):
            acc = acc + xbuf_ref[s, pad - hist + j:pad - hist + j + tile, :] * cw_ref[j:j + 1, :]
        tails.append(xbuf_ref[s, pad + tile - hist:pad + tile, :])
        xbuf_ref[s, pad - hist:pad, :] = tails[s]
        xcs.append(acc)
    xc = jnp.concatenate(xcs, axis=0)

    ga, gx = [], []
    blk = MXU_TILE
    for s in range(D_MODEL // blk):
        xb = xc[:, blk * s:blk * (s + 1)].astype(BF16)
        ga.append(_dot(xb, wa4_ref[s]))
        gx.append(_dot(xb, wx4_ref[s]))
    gate_r = _sigmoid(jnp.concatenate(ga, axis=1) + ba_ref[...])
    gate_i = _sigmoid(jnp.concatenate(gx, axis=1) + bx_ref[...])
    log_a = -LRU_C * gate_r * _softplus(-lam_ref[...])
    a = jnp.exp(log_a)
    a_ref[...] = a
    var = -jnp.tanh(log_a) * (a * a + 1.0)
    b_ref[...] = jnp.where(var > 0.0, var * lax.rsqrt(var), 0.0) * (gate_i * xc)

    row8 = lax.broadcasted_iota(jnp.int32, (SUBLANES, 1), 0)

    def row_body(g, carry):
        rs = pl.ds(pl.multiple_of(g * SUBLANES, SUBLANES), SUBLANES)
        av = a_ref[rs, :]
        bv = b_ref[rs, :]
        for sh in (1, 2, 4):
            keep = row8 >= sh
            ash = jnp.where(keep, pltpu.roll(av, sh, 0), 1.0)
            bsh = jnp.where(keep, pltpu.roll(bv, sh, 0), 0.0)
            bv = bv + av * bsh
            av = av * ash
        hs = bv + av * carry
        b_ref[rs, :] = hs
        return hs[SUBLANES - 1:SUBLANES, :]

    groups = tile // SUBLANES
    for s in range(seqs):
        hc_ref[s] = lax.fori_loop(s * groups, (s + 1) * groups, row_body, hc_ref[s], unroll=4)
    yz_ref[...] = (b_ref[...] * zg).reshape(seqs, tile, D_MODEL)

    @pl.when(t == pl.num_programs(1) - 1)
    def _():
        for s in range(seqs):
            conv_out_ref[s] = tails[s]
        h_out_ref[...] = hc_ref[...]


def _lru_mixer(x, mod, g_pre, wx, wg, conv_w, conv_b, wa4, ba, wx4, bx, lam, conv0, h0, *, sel, tile, seqs):
    bsz, tp, _ = x.shape
    assert tp % tile == 0 and tile % SUBLANES == 0 and tile >= SUBLANES and bsz % seqs == 0
    nt = tp // tile
    weight, state, mod_spec = _specs(*sel, seqs=seqs)
    kern = functools.partial(_lru_kernel, tile=tile)
    return pl.pallas_call(
        kern,
        grid=(bsz // seqs, nt),
        in_specs=[
            pl.BlockSpec((seqs, tile, D_MODEL), lambda b, t: (b, t, 0)),
            mod_spec(mod),
            weight(g_pre), weight(wx), weight(wg), weight(conv_w), weight(conv_b), weight(wa4), weight(ba),
            weight(wx4), weight(bx), weight(lam), state(conv0), state(h0),
        ],
        out_specs=[
            pl.BlockSpec((seqs, tile, D_MODEL), lambda b, t: (b, t, 0)),
            pl.BlockSpec((seqs, CONV_W - 1, D_MODEL), lambda b, t: (b, 0, 0)),
            pl.BlockSpec((seqs, 1, D_MODEL), lambda b, t: (b, 0, 0)),
        ],
        out_shape=[
            jax.ShapeDtypeStruct((bsz, tp, D_MODEL), F32),
            jax.ShapeDtypeStruct((bsz, CONV_W - 1, D_MODEL), F32),
            jax.ShapeDtypeStruct((bsz, 1, D_MODEL), F32),
        ],
        scratch_shapes=[
            pltpu.VMEM((seqs, tile + SUBLANES, D_MODEL), F32),
            pltpu.VMEM((seqs * tile, D_MODEL), F32),
            pltpu.VMEM((seqs * tile, D_MODEL), F32),
            pltpu.VMEM((seqs, 1, D_MODEL), F32),
        ],
        compiler_params=pltpu.CompilerParams(
            dimension_semantics=("arbitrary", "arbitrary"), vmem_limit_bytes=VMEM_LIMIT),
        name="rglru_mixer",
    )(x, mod, g_pre, wx, wg, conv_w, conv_b, wa4, ba, wx4, bx, lam, conv0, h0)


def _merge_kernel(x_ref, mod_ref, g_ref, gpost_ref, wm_ref, wo_ref, yr_ref, ys_ref, yl_ref, o_ref):
    seqs, tile, _ = x_ref.shape
    rows = lambda ref: ref[...].reshape(seqs * tile, D_MODEL)
    m = _sigmoid(_dot(_pre_norm(x_ref, g_ref, mod_ref), wm_ref[...]))
    merged = (m[:, :D_MODEL] * rows(yr_ref) + m[:, D_MODEL:2 * D_MODEL] * rows(ys_ref)
              + m[:, 2 * D_MODEL:] * rows(yl_ref))
    o = _dot(merged.astype(BF16), wo_ref[...])
    ms = jnp.mean(o * o, axis=-1, keepdims=True)
    o = (o * lax.rsqrt(ms + RMS_EPS)) * gpost_ref[...]
    for s in range(seqs):
        o_ref[s] = x_ref[s] + mod_ref[s][2:3, :] * o[s * tile:(s + 1) * tile, :]


def _merge_out(x, mod, g_pre, g_post, wm, wo, yr, ys, yl, *, sel, tile, seqs):
    bsz, tp, _ = x.shape
    assert tp % tile == 0 and bsz % seqs == 0
    nt = tp // tile
    act = pl.BlockSpec((seqs, tile, D_MODEL), lambda b, t: (b, t, 0))
    weight, _, mod_spec = _specs(*sel, seqs=seqs)
    return pl.pallas_call(
        _merge_kernel,
        grid=(bsz // seqs, nt),
        in_specs=[act, mod_spec(mod), weight(g_pre), weight(g_post), weight(wm), weight(wo), act, act, act],
        out_specs=act,
        out_shape=jax.ShapeDtypeStruct((bsz, tp, D_MODEL), F32),
        compiler_params=pltpu.CompilerParams(
            dimension_semantics=("arbitrary", "arbitrary"), vmem_limit_bytes=VMEM_LIMIT),
        name="merge_out",
    )(x, mod, g_pre, g_post, wm, wo, yr, ys, yl)


def _block_diag_expand(w, per):
    depth, n, a, b = w.shape
    w = w.reshape(depth, n // per, per, a, b)
    eye = jnp.eye(per, dtype=w.dtype)
    return jnp.einsum("lsgab,gh->lsgahb", w, eye).reshape(depth, n // per, per * a, per * b)


def _layer(x, mod, state, prm, *, sel, t_valid, tiles):
    shift_row, wkv, s_re, s_im, lru_h, lru_conv = state
    bsz = x.shape[0]
    g_pre = prm["g_pre"]

    xr = x
    if x.shape[1] % tiles["rwkv"] != 0:
        xr = jnp.pad(x, ((0, 0), (0, tiles["rwkv"] - x.shape[1] % tiles["rwkv"]), (0, 0)))
    yz_r, shift_new, wkv_new = _rwkv_mixer(
        xr, mod, g_pre, prm["w_rwkv"], prm["w_rwkv_g"], prm["rwkv_vecs"], prm["w2a"], shift_row, wkv,
        sel=sel, tile=tiles["rwkv"], t_valid=t_valid, seqs=tiles["rwkv_seqs"])
    yz_r = yz_r[:, :t_valid]
    yz_s, s_re_new, s_im_new = _s5_mixer(
        x, mod, g_pre, prm["w_s5"], prm["w_s5_g"], prm["bbr"], prm["bbi"], prm["cre"], prm["cmi"],
        prm["abr"], prm["abi"], prm["s5_d"], prm["glu_w"], prm["glu_b"], s_re, s_im, sel=sel, tile=tiles["s5"],
        seqs=tiles["seqs"])
    yz_l, conv_new, h_new = _lru_mixer(
        x, mod, g_pre, prm["w_lru"], prm["w_lru_g"], prm["conv_w"], prm["conv_b"], prm["wa4"], prm["ba"],
        prm["wx4"], prm["bx"], prm["lam"], lru_conv, lru_h, sel=sel, tile=tiles["lru"], seqs=tiles["seqs"])
    x_new = _merge_out(x, mod, g_pre, prm["g_post"], prm["w_merge"], prm["w_out"], yz_r, yz_s, yz_l,
                       sel=sel, tile=tiles["merge"], seqs=tiles["seqs"])
    new_state = (shift_new.reshape(bsz, SHIFT_W), wkv_new.reshape(bsz, HEADS, HEAD, HEAD),
                 s_re_new.reshape(bsz, S5_GROUPS, S5_STATES), s_im_new.reshape(bsz, S5_GROUPS, S5_STATES),
                 h_new.reshape(bsz, D_MODEL), conv_new)
    return x_new, new_state


def _stacked_params(w_in, w_out, norm_pre, norm_post, rwkv, s5, lru):
    depth = w_in.shape[0]
    d = D_MODEL
    wb = w_in.astype(BF16)
    o = SHIFT_W
    row = lambda a: a.reshape(depth, 1, -1)
    mu, w0, w2, a0, a2, k_k, k_a, r_k, ln_g, ln_b = rwkv
    a_re, a_im, log_step, b_re, b_im, c_re, c_im, s5_d, glu_w, glu_b = s5
    conv_w, conv_b, wa, ba, wx, bx, lam = lru
    zeros = jnp.zeros((depth, LORA, d), F32)
    w2a = jnp.concatenate([jnp.concatenate([w2, zeros], axis=2), jnp.concatenate([zeros, a2], axis=2)], axis=1)
    abr, abi, bbr, bbi = _s5_prep(a_re, a_im, log_step, b_re, b_im)
    per = S5_GROUPS // S5_SETS
    gate_per = MXU_TILE // LRU_BLOCK
    return dict(
        g_pre=row(norm_pre), g_post=row(norm_post),
        w_rwkv=wb[:, :, :o], w_rwkv_g=wb[:, :, o:o + d],
        w_s5=wb[:, :, o + d:o + 2 * d], w_s5_g=wb[:, :, o + 2 * d:o + 3 * d],
        w_lru=wb[:, :, o + 3 * d:o + 4 * d], w_lru_g=wb[:, :, o + 4 * d:o + 5 * d],
        w_merge=wb[:, :, o + 5 * d:], w_out=w_out.astype(BF16),
        rwkv_vecs=(row(mu), row(w0), row(a0), row(k_k), row(k_a), row(r_k), row(ln_g), row(ln_b)),
        w2a=w2a.astype(BF16),
        bbr=_block_diag_expand(bbr, S5_IN_GROUPS).astype(BF16),
        bbi=_block_diag_expand(bbi, S5_IN_GROUPS).astype(BF16),
        cre=_block_diag_expand(jnp.swapaxes(c_re, 2, 3), per).astype(BF16),
        cmi=_block_diag_expand(-jnp.swapaxes(c_im, 2, 3), per).astype(BF16),
        abr=abr.reshape(depth, S5_TILES, LANES), abi=abi.reshape(depth, S5_TILES, LANES),
        s5_d=row(s5_d), glu_w=glu_w.astype(BF16), glu_b=row(glu_b),
        conv_w=conv_w, conv_b=row(conv_b),
        wa4=_block_diag_expand(wa, gate_per).astype(BF16), ba=row(ba),
        wx4=_block_diag_expand(wx, gate_per).astype(BF16), bx=row(bx), lam=row(lam),
    )


def _state_layout(shift, wkv, s_re, s_im, lru_h, lru_conv):
    n, bsz = shift.shape[:2]
    return (shift.reshape(n, bsz, 1, SHIFT_W), wkv.reshape(n, bsz, PAIRS, LANES, HEAD),
            s_re.reshape(n, bsz, S5_TILES, LANES), s_im.reshape(n, bsz, S5_TILES, LANES),
            lru_h.reshape(n, bsz, 1, D_MODEL), lru_conv)


def _tiles(t, bsz):
    pick = lambda want: want if t % want == 0 else t
    share = lambda rows, limit: max(n for n in range(1, bsz + 1) if bsz % n == 0 and n * rows <= max(limit, rows))
    seqs = share(t, SHORT_ROWS) if t < SHORT_ROWS else 1
    rwkv = RWKV_TILE if t % RWKV_TILE == 0 else RWKV_CHUNK
    rwkv_seqs = share(RWKV_CHUNK, RWKV_TILE) if t <= RWKV_CHUNK else 1
    return dict(rwkv=rwkv, rwkv_seqs=rwkv_seqs, s5=pick(S5_TILE), lru=pick(WIDE_TILE), merge=pick(WIDE_TILE),
                seqs=seqs)


def kernel(x_prompt, x_sample, state_rwkv_shift, state_rwkv_wkv, state_s5_re, state_s5_im, state_lru_h,
           state_lru_conv, c_prompt, c_sample, ada_w, ada_b, norm_pre, norm_post, w_in, w_out, rwkv_mu, rwkv_w0,
           rwkv_w2, rwkv_a0, rwkv_a2, rwkv_k_k, rwkv_k_a, rwkv_r_k, rwkv_ln_g, rwkv_ln_b, s5_a_re, s5_a_im,
           s5_log_step, s5_b_re, s5_b_im, s5_c_re, s5_c_im, s5_d, s5_glu_w, s5_glu_b, lru_conv_w, lru_conv_b,
           lru_wa, lru_ba, lru_wx, lru_bx, lru_lambda):
    depth = w_in.shape[0]
    bp, tp, _ = x_prompt.shape
    bs, ts, _ = x_sample.shape
    c_all = jnp.concatenate([c_sample, c_prompt], axis=0)
    pad = (-c_all.shape[0]) % SUBLANES
    c_all = jnp.pad(c_all, ((0, pad), (0, 0)))
    mod_all = _modulation(c_all, ada_w, ada_b)
    mod_all = mod_all.reshape(depth, c_all.shape[0], 3, D_MODEL)
    rwkv = (rwkv_mu, rwkv_w0, rwkv_w2, rwkv_a0, rwkv_a2, rwkv_k_k, rwkv_k_a,
            rwkv_r_k.reshape(depth, D_MODEL), rwkv_ln_g, rwkv_ln_b)
    s5 = (s5_a_re, s5_a_im, s5_log_step, s5_b_re, s5_b_im, s5_c_re, s5_c_im, s5_d, s5_glu_w, s5_glu_b)
    lru = (lru_conv_w, lru_conv_b, lru_wa, lru_ba, lru_wx, lru_bx, lru_lambda)
    prm = _stacked_params(w_in, w_out, norm_pre, norm_post, rwkv, s5, lru)
    zero = _state_layout(jnp.zeros((1, bp, SHIFT_W), F32), jnp.zeros((1, bp, HEADS, HEAD, HEAD), F32),
                         jnp.zeros((1, bp, S5_GROUPS, S5_STATES), F32),
                         jnp.zeros((1, bp, S5_GROUPS, S5_STATES), F32),
                         jnp.zeros((1, bp, D_MODEL), F32), jnp.zeros((1, bp, CONV_W - 1, D_MODEL), F32))
    st_in = _state_layout(state_rwkv_shift, state_rwkv_wkv, state_s5_re, state_s5_im, state_lru_h,
                          state_lru_conv)

    xp, xs = x_prompt, x_sample
    new_p = [[] for _ in range(6)]
    new_s = [[] for _ in range(6)]
    for l in range(depth):
        xp, st_p = _layer(xp, mod_all, zero, prm, sel=(l, 0, bs), t_valid=tp, tiles=_tiles(tp, bp))
        xs, st_s = _layer(xs, mod_all, st_in, prm, sel=(l, l, 0), t_valid=ts, tiles=_tiles(ts, bs))
        for i in range(6):
            new_p[i].append(st_p[i])
            new_s[i].append(st_s[i])
    sp = [jnp.stack(z, axis=0) for z in new_p]
    ss = [jnp.stack(z, axis=0) for z in new_s]
    return (xp, xs, sp[0], sp[1], sp[2], sp[3], sp[4], sp[5], ss[0], ss[1], ss[2], ss[3], ss[4], ss[5])
```

```python
import functools

import jax
import jax.numpy as jnp
from jax import lax
from jax.experimental import pallas as pl
from jax.experimental.pallas import tpu as pltpu

F32 = jnp.float32
BF16 = jnp.bfloat16
HIGHEST = lax.Precision.HIGHEST

SUBLANES = 8
LANES = 128
MXU_TILE = 256
VMEM_LIMIT = 56 * 1024 * 1024

D_MODEL = 1024
HEAD = 64
HEADS = D_MODEL // HEAD
PAIRS = HEADS // 2
LORA = 64
SHIFT_W = 3 * D_MODEL + 2 * LORA
S5_GROUPS = 64
S5_STATES = 64
S5_STATE_W = S5_GROUPS * S5_STATES
LRU_BLOCK = 64
S5_SETS = 4
S5_IN_GROUPS = 8
S5_TILES = S5_STATE_W // LANES
S5_ROW_STRIDE = 40
CONV_W = 4
LRU_C = 8.0
RMS_EPS = 1e-6
GN_EPS = 64e-5
RWKV_CHUNK = 64
RWKV_TILE = 256
S5_TILE = 256
WIDE_TILE = 512
SHORT_ROWS = 512
DECAY_SCALE = 0.6065306597126334
GELU_C = 0.7978845608028654
LOG2_E = 1.4426950408889634
RWKV_INSTANCES = 32


def _sigmoid(x):
    return 0.5 * jnp.tanh(0.5 * x) + 0.5


def _silu(x):
    h = 0.5 * x
    return h * (jnp.tanh(h) + 1.0)


def _softplus(x):
    return jnp.maximum(x, 0.0) + jnp.log1p(jnp.exp(-jnp.abs(x)))


def _norm_mod(x, g, mod):
    ms = jnp.mean(x * x, axis=-1, keepdims=True)
    return (x * lax.rsqrt(ms + RMS_EPS)) * (g * (1.0 + mod[1:2, :])) + mod[0:1, :]


def _dot(a, b):
    return jnp.dot(a, b, preferred_element_type=F32)


def _dot_hi(a, b):
    return jnp.dot(a, b, preferred_element_type=F32, precision=HIGHEST)


def _split2(x):
    hi = x.astype(BF16)
    lo = (x - hi.astype(F32)).astype(BF16)
    return hi, lo


_NN = (((1,), (0,)), ((), ()))
_NT = (((1,), (1,)), ((), ()))
_TN = (((0,), (0,)), ((), ()))


def _mm(a, b, dims=_NN):
    return lax.dot_general(a.astype(BF16), b.astype(BF16), dims, preferred_element_type=F32)


def _head_sum(x, ones_blk):
    outs = []
    for s in range(D_MODEL // MXU_TILE):
        outs.append(_dot(x[:, MXU_TILE * s:MXU_TILE * (s + 1)].astype(BF16), ones_blk))
    return jnp.concatenate(outs, axis=1)


def _block_ones(n, blk):
    ri = lax.broadcasted_iota(jnp.int32, (n, n), 0)
    ci = lax.broadcasted_iota(jnp.int32, (n, n), 1)
    sh = blk.bit_length() - 1
    return ((ri >> sh) == (ci >> sh)).astype(F32).astype(BF16)


def _mod_kernel(c_ref, w_ref, b_ref, o_ref):
    s = _silu(c_ref[...])
    o_ref[...] = _dot_hi(s, w_ref[...]) + b_ref[...]


def _modulation(c_all, ada_w, ada_b):
    depth = ada_w.shape[0]
    rows = c_all.shape[0]
    return pl.pallas_call(
        _mod_kernel,
        grid=(depth, 3),
        in_specs=[
            pl.BlockSpec((rows, D_MODEL), lambda l, j: (0, 0)),
            pl.BlockSpec((None, D_MODEL, D_MODEL), lambda l, j: (l, 0, j)),
            pl.BlockSpec((None, 1, D_MODEL), lambda l, j: (l, 0, j)),
        ],
        out_specs=pl.BlockSpec((None, rows, D_MODEL), lambda l, j: (l, 0, j)),
        out_shape=jax.ShapeDtypeStruct((depth, rows, 3 * D_MODEL), F32),
        name="adaln_mod",
    )(c_all, ada_w, ada_b.reshape(depth, 1, 3 * D_MODEL))


def _s5_prep_kernel(are_ref, aim_ref, ls_ref, bre_ref, bim_ref, abr_ref, abi_ref, bbr_ref, bbi_ref):
    are = are_ref[...]
    aim = aim_ref[...]
    dt = jnp.exp(ls_ref[...])
    mag = jnp.exp(are * dt)
    abr = mag * jnp.cos(aim * dt)
    abi = mag * jnp.sin(aim * dt)
    abr_ref[...] = abr
    abi_ref[...] = abi
    nr = abr - 1.0
    ni = abi
    den = are * are + aim * aim
    cr = (nr * are + ni * aim) / den
    ci = (ni * are - nr * aim) / den
    bre = bre_ref[...]
    bim = bim_ref[...]
    bbr_ref[...] = cr[:, None, :] * bre - ci[:, None, :] * bim
    bbi_ref[...] = cr[:, None, :] * bim + ci[:, None, :] * bre


def _s5_prep(a_re, a_im, log_step, b_re, b_im):
    depth, g, p = a_re.shape
    i = b_re.shape[-1]
    layer = lambda *dims: pl.BlockSpec((None,) + dims, lambda l: (l,) + (0,) * len(dims))
    return pl.pallas_call(
        _s5_prep_kernel,
        grid=(depth,),
        in_specs=[layer(g, p), layer(g, p), layer(g, 1), layer(g, i, p), layer(g, i, p)],
        out_specs=(layer(g, p), layer(g, p), layer(g, i, p), layer(g, i, p)),
        out_shape=(
            jax.ShapeDtypeStruct((depth, g, p), F32),
            jax.ShapeDtypeStruct((depth, g, p), F32),
            jax.ShapeDtypeStruct((depth, g, i, p), F32),
            jax.ShapeDtypeStruct((depth, g, i, p), F32),
        ),
        name="s5_discretise",
    )(a_re, a_im, log_step.reshape(depth, g, 1), jnp.swapaxes(b_re, 2, 3), jnp.swapaxes(b_im, 2, 3))


def _rwkv_kernel(x_ref, mod_ref, g_ref, wr_ref, wg_ref, mu_ref, w0_ref, w2a_ref, a0_ref, kk_ref, ka_ref,
                 rk_ref, lng_ref, lnb_ref, shift0_ref, wkv0_ref,
                 yz_ref, shift_out_ref, wkv_out_ref,
                 carry_ref, st_ref, at_ref, rt_ref, bt_ref, kt_ref, bh_ref, kh_ref, v_ref, pl_ref, y_ref,
                 nh_ref, t_ref, aak_ref, ark_ref, rhs_ref, qg_ref, hc_ref, yc_ref, ones_ref,
                 *, tile, chunk, t_valid, group):
    t = pl.program_id(1)
    n_chunks = tile // chunk
    two = 2 * chunk
    n_inst = group * n_chunks
    chained = x_ref.shape[0] == 1

    def pair_state(s0):
        rh = lax.broadcasted_iota(jnp.int32, (LANES, LANES), 0) >> (HEAD.bit_length() - 1)
        ch = lax.broadcasted_iota(jnp.int32, (LANES, LANES), 1) >> (HEAD.bit_length() - 1)
        return jnp.where(rh == ch, jnp.concatenate([s0, s0], axis=1), 0.0).T

    def head_states(st):
        sv = st.T
        return sv[:, :HEAD] + sv[:, HEAD:]

    if chained:
        @pl.when(t == 0)
        def _():
            carry_ref[...] = shift0_ref[0]
            for q in range(PAIRS):
                st_ref[q] = pair_state(wkv0_ref[0, q])

    h = _pre_norm(x_ref, g_ref, mod_ref)
    p = _dot(h, wr_ref[...])
    zg = _silu(_dot(h, wg_ref[...]))

    rows = lax.broadcasted_iota(jnp.int32, (tile, 1), 0)
    rmod = rows & (chunk - 1)
    last = (t_valid - 1) % chunk
    shifted = pltpu.roll(p, 1, 0)
    top_row = lax.broadcasted_iota(jnp.int32, (SUBLANES, 1), 0) == 0
    if chained:
        starts = {0: carry_ref[...]}
        carry_ref[...] = p[tile - chunk + last:tile - chunk + last + 1, :]
    else:
        starts = {s * chunk: shift0_ref[s] for s in range(n_chunks)}
        for s in range(n_chunks):
            shift_out_ref[s] = p[s * chunk + last:s * chunk + last + 1, :]
    pieces, at = [], 0
    for r0 in sorted(starts):
        if r0 > at:
            pieces.append(shifted[at:r0, :])
        pieces.append(jnp.where(top_row, starts[r0], shifted[r0:r0 + SUBLANES, :]))
        at = r0 + SUBLANES
    pieces.append(shifted[at:, :])
    prev = jnp.concatenate(pieces, axis=0)
    pm = p + (prev - p) * mu_ref[...]

    r = pm[:, 0:D_MODEL]
    k = pm[:, D_MODEL:2 * D_MODEL]
    v = pm[:, 2 * D_MODEL:3 * D_MODEL]
    wa = pm[:, 3 * D_MODEL:SHIFT_W]
    lane = lax.broadcasted_iota(jnp.int32, (1, LANES), 1)
    wa = jnp.where(lane < LORA, jnp.tanh(wa), wa)
    lora = _dot(wa.astype(BF16), w2a_ref[...])
    logd = -(DECAY_SCALE * LOG2_E) * _sigmoid(w0_ref[...] + lora[:, :D_MODEL])
    a_sig = _sigmoid(a0_ref[...] + lora[:, D_MODEL:])

    @pl.when(t == 0)
    def _():
        ones_ref[...] = _block_ones(MXU_TILE, HEAD)

    ones_blk = ones_ref[...]
    kk = k * kk_ref[...]
    kk = kk * jnp.minimum(lax.rsqrt(_head_sum(kk * kk, ones_blk)), 1e12)
    k2 = k * (1.0 + (a_sig - 1.0) * ka_ref[...])
    bvec = kk * a_sig
    if t_valid % chunk != 0:
        ok = rmod < t_valid % chunk
        logd = jnp.where(ok, logd, 0.0)
        bvec = jnp.where(ok, bvec, 0.0)
        k2 = jnp.where(ok, k2, 0.0)

    cum = logd
    s = 1
    while s < chunk:
        cum = cum + jnp.where(rmod >= s, pltpu.roll(cum, s, 0), 0.0)
        s *= 2
    cum_last = jnp.concatenate(
        [jnp.broadcast_to(cum[(j + 1) * chunk - 1:(j + 1) * chunk, :], (chunk, D_MODEL)) for j in range(n_chunks)],
        axis=0)
    p_inc = jnp.exp2(cum)
    p_inv = jnp.exp2(-cum)
    p_exc = jnp.exp2(cum - logd)
    p_end = jnp.exp2(cum_last - cum)

    def to_pairs(ref, val):
        for q in range(PAIRS):
            ref[q] = val[:, LANES * q:LANES * (q + 1)]

    to_pairs(at_ref, -kk * p_exc)
    to_pairs(rt_ref, r * p_inc)
    to_pairs(bt_ref, bvec * p_inv)
    to_pairs(kt_ref, k2 * p_inv)
    to_pairs(bh_ref, bvec * p_end)
    to_pairs(kh_ref, k2 * p_end)
    to_pairs(v_ref, v)
    for j in range(n_chunks):
        row = jnp.exp2(cum[(j + 1) * chunk - 1:(j + 1) * chunk, :])
        for q in range(PAIRS):
            pl_ref[q, j] = jnp.broadcast_to(row[:, LANES * q:LANES * (q + 1)], (SUBLANES, LANES))

    m_a = (lane < HEAD).astype(F32)
    m_b = 1.0 - m_a
    ri = lax.broadcasted_iota(jnp.int32, (two, two), 0)
    ci = lax.broadcasted_iota(jnp.int32, (two, two), 1)
    csh = chunk.bit_length() - 1
    same_head = (ri >> csh) == (ci >> csh)
    strict = jnp.logical_and(same_head, (ri & (chunk - 1)) > (ci & (chunk - 1)))
    incl = jnp.logical_and(same_head, (ri & (chunk - 1)) >= (ci & (chunk - 1)))
    eye = ri == ci
    head_a = ri < chunk
    off_masks = []
    for lg in range(csh):
        off_masks.append(jnp.logical_and(strict, jnp.logical_and((ri >> (lg + 1)) == (ci >> (lg + 1)),
                                                                 (ri >> lg) != (ci >> lg))))

    def stack(x):
        return jnp.concatenate([x * m_a, x * m_b], axis=0)

    gsh = group.bit_length() - 1
    unroll = n_inst

    def group_body(g, carry):
        def where_is(i):
            q = g * group + (i & (group - 1))
            j = i >> gsh
            return q, j, pl.ds(pl.multiple_of(j * chunk, chunk), chunk)

        def gram(i, c):
            q, _, sl = where_is(i)
            atm = stack(at_ref[q, sl, :])
            rtm = stack(rt_ref[q, sl, :])
            bt = bt_ref[q, sl, :]
            kt = kt_ref[q, sl, :]
            o1 = _mm(jnp.concatenate([atm, rtm], axis=0), jnp.concatenate([bt, kt], axis=0), _NT)
            o1_sw = pltpu.roll(o1, chunk, 1)
            n_ab = jnp.where(strict, jnp.where(head_a, o1[:two], o1_sw[:two]), 0.0)
            nh_ref[i] = n_ab.astype(BF16)
            t_ref[i] = jnp.where(eye, 1.0, jnp.where(off_masks[0], n_ab, 0.0)).astype(BF16)
            aak_ref[i] = jnp.where(strict, jnp.where(head_a, o1_sw[:two], o1[:two]), 0.0).astype(BF16)
            ark_ref[i] = jnp.concatenate([jnp.where(incl, jnp.where(head_a, o1[two:], o1_sw[two:]), 0.0),
                                          jnp.where(incl, jnp.where(head_a, o1_sw[two:], o1[two:]), 0.0)],
                                         axis=1).astype(BF16)
            return c

        lax.fori_loop(0, n_inst, gram, 0, unroll=unroll)

        for off in off_masks[1:]:
            def level(i, c, off=off):
                t_inv = t_ref[i]
                x = _dot(jnp.where(off, nh_ref[i], 0.0), t_inv).astype(BF16)
                t_ref[i] = t_inv + _dot(t_inv, x).astype(BF16)
                return c

            lax.fori_loop(0, n_inst, level, 0, unroll=unroll)

        def right_side(i, c):
            q, _, sl = where_is(i)
            atm = stack(at_ref[q, sl, :])
            vm = stack(v_ref[q, sl, :])
            z0_hi, z0_lo = _split2(jnp.concatenate([atm, _mm(aak_ref[i], vm)], axis=1))
            rhs_ref[i] = jnp.concatenate([z0_hi, z0_lo], axis=0)
            return c

        lax.fori_loop(0, n_inst, right_side, 0, unroll=unroll)

        def apply(i, c):
            q, _, sl = where_is(i)
            vm = stack(v_ref[q, sl, :])
            t_inv = t_ref[i]
            z = _dot(t_inv, rhs_ref[i, :two, :]) + _dot(t_inv, rhs_ref[i, two:, :])
            rhs_ref[i] = jnp.concatenate([z, jnp.concatenate([jnp.zeros_like(vm), vm], axis=1)], axis=0).astype(BF16)
            return c

        lax.fori_loop(0, n_inst, apply, 0, unroll=unroll)

        def operators(i, c):
            q, j, sl = where_is(i)
            rhs2 = rhs_ref[i]
            k_all = g * n_inst + i
            top = _mm(jnp.concatenate([stack(bh_ref[q, sl, :]), stack(kh_ref[q, sl, :])], axis=0), rhs2, _TN)
            bot = _dot(ark_ref[i], rhs2)
            g_mat = top[:, :two] + jnp.where(eye, jnp.broadcast_to(pl_ref[q, j][0:1, :], (two, two)), 0.0)
            q_eff = bot[:, :two] + stack(rt_ref[q, sl, :])
            qg_ref[k_all] = jnp.concatenate([q_eff, g_mat], axis=0).astype(BF16)
            hc_ref[k_all] = top[:, two:]
            yc_ref[k_all] = bot[:, two:]
            return c

        lax.fori_loop(0, n_inst, operators, 0, unroll=unroll)
        return carry

    lax.fori_loop(0, PAIRS // group, group_body, 0)

    if chained:
        states = [st_ref[q] for q in range(PAIRS)]
    for j in range(n_chunks):
        for q in range(PAIRS):
            k_all = (q // group) * n_inst + j * group + q % group
            st = (states[q] if chained else pair_state(wkv0_ref[j, q])).astype(BF16)
            both = _dot(qg_ref[k_all], st)
            ym = both[:two, :] + yc_ref[k_all]
            y_ref[q, j * chunk:(j + 1) * chunk, :] = ym[:chunk, :] + ym[chunk:, :]
            st_new = both[two:, :] + hc_ref[k_all]
            if chained:
                states[q] = st_new
            else:
                wkv_out_ref[j, q] = head_states(st_new)
    if chained:
        for q in range(PAIRS):
            st_ref[q] = states[q]

    ys = jnp.concatenate([y_ref[q] for q in range(PAIRS)], axis=1)
    mean_blk = ones_blk * (1.0 / HEAD)
    mean = _head_sum(ys, mean_blk)
    yc = ys - mean
    var = _head_sum(yc * yc, mean_blk)
    yn = yc * lax.rsqrt(var + GN_EPS) * lng_ref[...] + lnb_ref[...]
    bonus = _head_sum(r * k2 * rk_ref[...], ones_blk) * v
    yz_ref[...] = ((yn + bonus) * zg).reshape(yz_ref.shape)

    if chained:
        @pl.when(t == pl.num_programs(1) - 1)
        def _():
            shift_out_ref[0] = carry_ref[...]
            for q in range(PAIRS):
                wkv_out_ref[0, q] = head_states(st_ref[q])


def _specs(layer, state_layer, mod_row, seqs=None):
    assert seqs is None or mod_row % seqs == 0
    per_step = 1 if seqs is None else seqs

    def weight(a):
        return pl.BlockSpec((None,) + a.shape[1:], lambda b, t: (layer,) + (0,) * (a.ndim - 1))

    def state(a):
        return pl.BlockSpec((None, seqs) + a.shape[2:], lambda b, t: (state_layer, b) + (0,) * (a.ndim - 2))

    def mod(a):
        return pl.BlockSpec((None, seqs) + a.shape[2:], lambda b, t: (layer, mod_row // per_step + b, 0, 0))

    return weight, state, mod


def _pre_norm(x_ref, g_ref, mod_ref):
    g = g_ref[...]
    rows = [_norm_mod(x_ref[s], g, mod_ref[s]) for s in range(x_ref.shape[0])]
    return jnp.concatenate(rows, axis=0).astype(BF16)


def _rwkv_mixer(x, mod, g_pre, wr, wg, vecs, w2a, shift0, wkv0, *, sel, tile, t_valid, seqs):
    bsz, tp, _ = x.shape
    chunk = RWKV_CHUNK
    assert tp % tile == 0 and tile % chunk == 0 and bsz % seqs == 0
    assert t_valid == tp or tp == tile
    assert seqs == 1 or tp == chunk
    nt = tp // tile
    block_rows = tile
    tile = seqs * tile
    mu, w0, a0, k_k, k_a, r_k, ln_g, ln_b = vecs
    weight, state, mod_spec = _specs(*sel, seqs=seqs)
    pair_buf = pltpu.VMEM((PAIRS, tile, LANES), F32)
    group = min(PAIRS, max(1, RWKV_INSTANCES // (tile // chunk)))
    n_inst = group * (tile // chunk)
    two = 2 * chunk
    inst = lambda cols, dtype: pltpu.VMEM((n_inst, two, cols), dtype)
    every = lambda dtype: pltpu.VMEM((PAIRS * (tile // chunk), two, two), dtype)
    kern = functools.partial(_rwkv_kernel, tile=tile, chunk=chunk, t_valid=t_valid, group=group)
    return pl.pallas_call(
        kern,
        grid=(bsz // seqs, nt),
        in_specs=[
            pl.BlockSpec((seqs, block_rows, D_MODEL), lambda b, t: (b, t, 0)),
            mod_spec(mod),
            weight(g_pre), weight(wr), weight(wg), weight(mu), weight(w0), weight(w2a), weight(a0),
            weight(k_k), weight(k_a), weight(r_k), weight(ln_g), weight(ln_b),
            state(shift0), state(wkv0),
        ],
        out_specs=[
            pl.BlockSpec((seqs, block_rows, D_MODEL), lambda b, t: (b, t, 0)),
            pl.BlockSpec((seqs, 1, SHIFT_W), lambda b, t: (b, 0, 0)),
            pl.BlockSpec((seqs, PAIRS, LANES, HEAD), lambda b, t: (b, 0, 0, 0)),
        ],
        out_shape=[
            jax.ShapeDtypeStruct((bsz, tp, D_MODEL), F32),
            jax.ShapeDtypeStruct((bsz, 1, SHIFT_W), F32),
            jax.ShapeDtypeStruct((bsz, PAIRS, LANES, HEAD), F32),
        ],
        scratch_shapes=[
            pltpu.VMEM((1, SHIFT_W), F32),
            pltpu.VMEM((PAIRS, LANES, LANES), F32),
            pair_buf, pair_buf, pair_buf, pair_buf, pair_buf, pair_buf, pair_buf,
            pltpu.VMEM((PAIRS, tile // chunk, SUBLANES, LANES), F32),
            pair_buf,
            inst(two, BF16), inst(two, BF16), inst(two, BF16), inst(2 * two, BF16),
            pltpu.VMEM((n_inst, 2 * two, 2 * two), BF16),
            pltpu.VMEM((PAIRS * (tile // chunk), 2 * two, two), BF16), every(F32), every(F32),
            pltpu.VMEM((MXU_TILE, MXU_TILE), BF16),
        ],
        compiler_params=pltpu.CompilerParams(
            dimension_semantics=("arbitrary", "arbitrary"), vmem_limit_bytes=VMEM_LIMIT),
        name="rwkv7_mixer",
    )(x, mod, g_pre, wr, wg, mu, w0, w2a, a0, k_k, k_a, r_k, ln_g, ln_b, shift0, wkv0)


def _s5_kernel(x_ref, mod_ref, g_ref, wu_ref, wg_ref, bbr_ref, bbi_ref, cre_ref, cmi_ref, abr_ref, abi_ref,
               d_ref, gluw_ref, glub_ref, s0r_ref, s0i_ref,
               yz_ref, sr_out_ref, si_out_ref,
               hr_ref, hi_ref, cr_ref, ci_ref, *, tile):
    t = pl.program_id(1)

    @pl.when(t == 0)
    def _():
        cr_ref[...] = s0r_ref[...]
        ci_ref[...] = s0i_ref[...]

    seqs = x_ref.shape[0]
    frames = seqs * tile
    h = _pre_norm(x_ref, g_ref, mod_ref)
    u = _dot(h, wu_ref[...])
    zg = _silu(_dot(h, wg_ref[...]))
    set_tiles = S5_TILES // S5_SETS

    def frame_rows(c):
        return pl.ds(c, frames, stride=S5_ROW_STRIDE)

    in_tiles = bbr_ref.shape[2] // LANES
    for s in range(D_MODEL // LANES):
        ub = u[:, LANES * s:LANES * (s + 1)].astype(BF16)
        bur = _dot(ub, bbr_ref[s])
        bui = _dot(ub, bbi_ref[s])
        for c in range(in_tiles):
            hr_ref[frame_rows(s * in_tiles + c), :] = bur[:, LANES * c:LANES * (c + 1)]
            hi_ref[frame_rows(s * in_tiles + c), :] = bui[:, LANES * c:LANES * (c + 1)]

    abr = abr_ref[...]
    abi = abi_ref[...]

    def frame(f, carry):
        sr, si = carry
        rows = pl.ds(pl.multiple_of(f * S5_ROW_STRIDE, SUBLANES), S5_TILES)
        nr = abr * sr - abi * si + hr_ref[rows, :]
        ni = abr * si + abi * sr + hi_ref[rows, :]
        hr_ref[rows, :] = nr
        hi_ref[rows, :] = ni
        return nr, ni

    for s in range(seqs):
        cr_ref[s], ci_ref[s] = lax.fori_loop(s * tile, (s + 1) * tile, frame, (cr_ref[s], ci_ref[s]), unroll=16)

    outs = []
    for s in range(S5_SETS):
        hr = jnp.concatenate([hr_ref[frame_rows(s * set_tiles + c), :] for c in range(set_tiles)], axis=1)
        hi = jnp.concatenate([hi_ref[frame_rows(s * set_tiles + c), :] for c in range(set_tiles)], axis=1)
        outs.append(_dot(hr.astype(BF16), cre_ref[s]) + _dot(hi.astype(BF16), cmi_ref[s]))
    y = jnp.concatenate(outs, axis=1) + d_ref[...] * u
    y = (0.5 * y) * (1.0 + jnp.tanh(y * (GELU_C + (GELU_C * 0.044715) * (y * y))))
    y = y * _sigmoid(_dot(y.astype(BF16), gluw_ref[...]) + glub_ref[...])
    yz_ref[...] = (y * zg).reshape(seqs, tile, D_MODEL)

    @pl.when(t == pl.num_programs(1) - 1)
    def _():
        sr_out_ref[...] = cr_ref[...]
        si_out_ref[...] = ci_ref[...]


def _s5_mixer(x, mod, g_pre, wu, wg, bbr, bbi, cre, cmi, abr, abi, d_skip, glu_w, glu_b, s0r, s0i, *, sel, tile,
              seqs):
    bsz, tp, _ = x.shape
    assert tp % tile == 0 and tile % SUBLANES == 0 and bsz % seqs == 0
    nt = tp // tile
    weight, state, mod_spec = _specs(*sel, seqs=seqs)
    state_out = pl.BlockSpec((seqs, S5_TILES, LANES), lambda b, t: (b, 0, 0))
    kern = functools.partial(_s5_kernel, tile=tile)
    return pl.pallas_call(
        kern,
        grid=(bsz // seqs, nt),
        in_specs=[
            pl.BlockSpec((seqs, tile, D_MODEL), lambda b, t: (b, t, 0)),
            mod_spec(mod),
            weight(g_pre), weight(wu), weight(wg), weight(bbr), weight(bbi), weight(cre), weight(cmi),
            weight(abr), weight(abi), weight(d_skip), weight(glu_w), weight(glu_b), state(s0r), state(s0i),
        ],
        out_specs=[pl.BlockSpec((seqs, tile, D_MODEL), lambda b, t: (b, t, 0)), state_out, state_out],
        out_shape=[
            jax.ShapeDtypeStruct((bsz, tp, D_MODEL), F32),
            jax.ShapeDtypeStruct((bsz, S5_TILES, LANES), F32),
            jax.ShapeDtypeStruct((bsz, S5_TILES, LANES), F32),
        ],
        scratch_shapes=[
            pltpu.VMEM((seqs * tile * S5_ROW_STRIDE, LANES), F32),
            pltpu.VMEM((seqs * tile * S5_ROW_STRIDE, LANES), F32),
            pltpu.VMEM((seqs, S5_TILES, LANES), F32),
            pltpu.VMEM((seqs, S5_TILES, LANES), F32),
        ],
        compiler_params=pltpu.CompilerParams(
            dimension_semantics=("arbitrary", "arbitrary"), vmem_limit_bytes=VMEM_LIMIT),
        name="s5_mixer",
    )(x, mod, g_pre, wu, wg, bbr, bbi, cre, cmi, abr, abi, d_skip, glu_w, glu_b, s0r, s0i)


def _lru_kernel(x_ref, mod_ref, g_ref, wx_ref, wg_ref, cw_ref, cb_ref, wa4_ref, ba_ref, wx4_ref, bx_ref, lam_ref,
                conv0_ref, h0_ref,
                yz_ref, conv_out_ref, h_out_ref,
                xbuf_ref, a_ref, b_ref, hc_ref, *, tile):
    t = pl.program_id(1)
    pad = SUBLANES
    hist = CONV_W - 1
    seqs = x_ref.shape[0]

    @pl.when(t == 0)
    def _():
        xbuf_ref[:, pad - hist:pad, :] = conv0_ref[...]
        hc_ref[...] = h0_ref[...]

    h = _pre_norm(x_ref, g_ref, mod_ref)
    xl = _dot(h, wx_ref[...])
    zg = _silu(_dot(h, wg_ref[...]))
    xcs, tails = [], []
    for s in range(seqs):
        xs = xl[s * tile:(s + 1) * tile, :]
        xbuf_ref[s, pad:pad + tile, :] = xs
        acc = cb_ref[...] + xs * cw_ref[CONV_W - 1:CONV_W, :]
        for j in range(hist):
            acc = acc + xbuf_ref[s, pad - hist + j:pad - hist + j + tile, :] * cw_ref[j:j + 1, :]
        tails.append(xbuf_ref[s, pad + tile - hist:pad + tile, :])
        xbuf_ref[s, pad - hist:pad, :] = tails[s]
        xcs.append(acc)
    xc = jnp.concatenate(xcs, axis=0)

    ga, gx = [], []
    blk = MXU_TILE
    for s in range(D_MODEL // blk):
        xb = xc[:, blk * s:blk * (s + 1)].astype(BF16)
        ga.append(_dot(xb, wa4_ref[s]))
        gx.append(_dot(xb, wx4_ref[s]))
    gate_r = _sigmoid(jnp.concatenate(ga, axis=1) + ba_ref[...])
    gate_i = _sigmoid(jnp.concatenate(gx, axis=1) + bx_ref[...])
    log_a = -LRU_C * gate_r * _softplus(-lam_ref[...])
    a = jnp.exp(log_a)
    a_ref[...] = a
    var = -jnp.tanh(log_a) * (a * a + 1.0)
    b_ref[...] = jnp.where(var > 0.0, var * lax.rsqrt(var), 0.0) * (gate_i * xc)

    row8 = lax.broadcasted_iota(jnp.int32, (SUBLANES, 1), 0)

    def row_body(g, carry):
        rs = pl.ds(pl.multiple_of(g * SUBLANES, SUBLANES), SUBLANES)
        av = a_ref[rs, :]
        bv = b_ref[rs, :]
        for sh in (1, 2, 4):
            keep = row8 >= sh
            ash = jnp.where(keep, pltpu.roll(av, sh, 0), 1.0)
            bsh = jnp.where(keep, pltpu.roll(bv, sh, 0), 0.0)
            bv = bv + av * bsh
            av = av * ash
        hs = bv + av * carry
        b_ref[rs, :] = hs
        return hs[SUBLANES - 1:SUBLANES, :]

    groups = tile // SUBLANES
    for s in range(seqs):
        hc_ref[s] = lax.fori_loop(s * groups, (s + 1) * groups, row_body, hc_ref[s], unroll=4)
    yz_ref[...] = (b_ref[...] * zg).reshape(seqs, tile, D_MODEL)

    @pl.when(t == pl.num_programs(1) - 1)
    def _():
        for s in range(seqs):
            conv_out_ref[s] = tails[s]
        h_out_ref[...] = hc_ref[...]


def _lru_mixer(x, mod, g_pre, wx, wg, conv_w, conv_b, wa4, ba, wx4, bx, lam, conv0, h0, *, sel, tile, seqs):
    bsz, tp, _ = x.shape
    assert tp % tile == 0 and tile % SUBLANES == 0 and tile >= SUBLANES and bsz % seqs == 0
    nt = tp // tile
    weight, state, mod_spec = _specs(*sel, seqs=seqs)
    kern = functools.partial(_lru_kernel, tile=tile)
    return pl.pallas_call(
        kern,
        grid=(bsz // seqs, nt),
        in_specs=[
            pl.BlockSpec((seqs, tile, D_MODEL), lambda b, t: (b, t, 0)),
            mod_spec(mod),
            weight(g_pre), weight(wx), weight(wg), weight(conv_w), weight(conv_b), weight(wa4), weight(ba),
            weight(wx4), weight(bx), weight(lam), state(conv0), state(h0),
        ],
        out_specs=[
            pl.BlockSpec((seqs, tile, D_MODEL), lambda b, t: (b, t, 0)),
            pl.BlockSpec((seqs, CONV_W - 1, D_MODEL), lambda b, t: (b, 0, 0)),
            pl.BlockSpec((seqs, 1, D_MODEL), lambda b, t: (b, 0, 0)),
        ],
        out_shape=[
            jax.ShapeDtypeStruct((bsz, tp, D_MODEL), F32),
            jax.ShapeDtypeStruct((bsz, CONV_W - 1, D_MODEL), F32),
            jax.ShapeDtypeStruct((bsz, 1, D_MODEL), F32),
        ],
        scratch_shapes=[
            pltpu.VMEM((seqs, tile + SUBLANES, D_MODEL), F32),
            pltpu.VMEM((seqs * tile, D_MODEL), F32),
            pltpu.VMEM((seqs * tile, D_MODEL), F32),
            pltpu.VMEM((seqs, 1, D_MODEL), F32),
        ],
        compiler_params=pltpu.CompilerParams(
            dimension_semantics=("arbitrary", "arbitrary"), vmem_limit_bytes=VMEM_LIMIT),
        name="rglru_mixer",
    )(x, mod, g_pre, wx, wg, conv_w, conv_b, wa4, ba, wx4, bx, lam, conv0, h0)


def _merge_kernel(x_ref, mod_ref, g_ref, gpost_ref, wm_ref, wo_ref, yr_ref, ys_ref, yl_ref, o_ref):
    seqs, tile, _ = x_ref.shape
    rows = lambda ref: ref[...].reshape(seqs * tile, D_MODEL)
    m = _sigmoid(_dot(_pre_norm(x_ref, g_ref, mod_ref), wm_ref[...]))
    merged = (m[:, :D_MODEL] * rows(yr_ref) + m[:, D_MODEL:2 * D_MODEL] * rows(ys_ref)
              + m[:, 2 * D_MODEL:] * rows(yl_ref))
    o = _dot(merged.astype(BF16), wo_ref[...])
    ms = jnp.mean(o * o, axis=-1, keepdims=True)
    o = (o * lax.rsqrt(ms + RMS_EPS)) * gpost_ref[...]
    for s in range(seqs):
        o_ref[s] = x_ref[s] + mod_ref[s][2:3, :] * o[s * tile:(s + 1) * tile, :]


def _merge_out(x, mod, g_pre, g_post, wm, wo, yr, ys, yl, *, sel, tile, seqs):
    bsz, tp, _ = x.shape
    assert tp % tile == 0 and bsz % seqs == 0
    nt = tp // tile
    act = pl.BlockSpec((seqs, tile, D_MODEL), lambda b, t: (b, t, 0))
    weight, _, mod_spec = _specs(*sel, seqs=seqs)
    return pl.pallas_call(
        _merge_kernel,
        grid=(bsz // seqs, nt),
        in_specs=[act, mod_spec(mod), weight(g_pre), weight(g_post), weight(wm), weight(wo), act, act, act],
        out_specs=act,
        out_shape=jax.ShapeDtypeStruct((bsz, tp, D_MODEL), F32),
        compiler_params=pltpu.CompilerParams(
            dimension_semantics=("arbitrary", "arbitrary"), vmem_limit_bytes=VMEM_LIMIT),
        name="merge_out",
    )(x, mod, g_pre, g_post, wm, wo, yr, ys, yl)


def _block_diag_expand(w, per):
    depth, n, a, b = w.shape
    w = w.reshape(depth, n // per, per, a, b)
    eye = jnp.eye(per, dtype=w.dtype)
    return jnp.einsum("lsgab,gh->lsgahb", w, eye).reshape(depth, n // per, per * a, per * b)


def _layer(x, mod, state, prm, *, sel, t_valid, tiles):
    shift_row, wkv, s_re, s_im, lru_h, lru_conv = state
    bsz = x.shape[0]
    g_pre = prm["g_pre"]

    xr = x
    if x.shape[1] % tiles["rwkv"] != 0:
        xr = jnp.pad(x, ((0, 0), (0, tiles["rwkv"] - x.shape[1] % tiles["rwkv"]), (0, 0)))
    yz_r, shift_new, wkv_new = _rwkv_mixer(
        xr, mod, g_pre, prm["w_rwkv"], prm["w_rwkv_g"], prm["rwkv_vecs"], prm["w2a"], shift_row, wkv,
        sel=sel, tile=tiles["rwkv"], t_valid=t_valid, seqs=tiles["rwkv_seqs"])
    yz_r = yz_r[:, :t_valid]
    yz_s, s_re_new, s_im_new = _s5_mixer(
        x, mod, g_pre, prm["w_s5"], prm["w_s5_g"], prm["bbr"], prm["bbi"], prm["cre"], prm["cmi"],
        prm["abr"], prm["abi"], prm["s5_d"], prm["glu_w"], prm["glu_b"], s_re, s_im, sel=sel, tile=tiles["s5"],
        seqs=tiles["seqs"])
    yz_l, conv_new, h_new = _lru_mixer(
        x, mod, g_pre, prm["w_lru"], prm["w_lru_g"], prm["conv_w"], prm["conv_b"], prm["wa4"], prm["ba"],
        prm["wx4"], prm["bx"], prm["lam"], lru_conv, lru_h, sel=sel, tile=tiles["lru"], seqs=tiles["seqs"])
    x_new = _merge_out(x, mod, g_pre, prm["g_post"], prm["w_merge"], prm["w_out"], yz_r, yz_s, yz_l,
                       sel=sel, tile=tiles["merge"], seqs=tiles["seqs"])
    new_state = (shift_new.reshape(bsz, SHIFT_W), wkv_new.reshape(bsz, HEADS, HEAD, HEAD),
                 s_re_new.reshape(bsz, S5_GROUPS, S5_STATES), s_im_new.reshape(bsz, S5_GROUPS, S5_STATES),
                 h_new.reshape(bsz, D_MODEL), conv_new)
    return x_new, new_state


def _stacked_params(w_in, w_out, norm_pre, norm_post, rwkv, s5, lru):
    depth = w_in.shape[0]
    d = D_MODEL
    wb = w_in.astype(BF16)
    o = SHIFT_W
    row = lambda a: a.reshape(depth, 1, -1)
    mu, w0, w2, a0, a2, k_k, k_a, r_k, ln_g, ln_b = rwkv
    a_re, a_im, log_step, b_re, b_im, c_re, c_im, s5_d, glu_w, glu_b = s5
    conv_w, conv_b, wa, ba, wx, bx, lam = lru
    zeros = jnp.zeros((depth, LORA, d), F32)
    w2a = jnp.concatenate([jnp.concatenate([w2, zeros], axis=2), jnp.concatenate([zeros, a2], axis=2)], axis=1)
    abr, abi, bbr, bbi = _s5_prep(a_re, a_im, log_step, b_re, b_im)
    per = S5_GROUPS // S5_SETS
    gate_per = MXU_TILE // LRU_BLOCK
    return dict(
        g_pre=row(norm_pre), g_post=row(norm_post),
        w_rwkv=wb[:, :, :o], w_rwkv_g=wb[:, :, o:o + d],
        w_s5=wb[:, :, o + d:o + 2 * d], w_s5_g=wb[:, :, o + 2 * d:o + 3 * d],
        w_lru=wb[:, :, o + 3 * d:o + 4 * d], w_lru_g=wb[:, :, o + 4 * d:o + 5 * d],
        w_merge=wb[:, :, o + 5 * d:], w_out=w_out.astype(BF16),
        rwkv_vecs=(row(mu), row(w0), row(a0), row(k_k), row(k_a), row(r_k), row(ln_g), row(ln_b)),
        w2a=w2a.astype(BF16),
        bbr=_block_diag_expand(bbr, S5_IN_GROUPS).astype(BF16),
        bbi=_block_diag_expand(bbi, S5_IN_GROUPS).astype(BF16),
        cre=_block_diag_expand(jnp.swapaxes(c_re, 2, 3), per).astype(BF16),
        cmi=_block_diag_expand(-jnp.swapaxes(c_im, 2, 3), per).astype(BF16),
        abr=abr.reshape(depth, S5_TILES, LANES), abi=abi.reshape(depth, S5_TILES, LANES),
        s5_d=row(s5_d), glu_w=glu_w.astype(BF16), glu_b=row(glu_b),
        conv_w=conv_w, conv_b=row(conv_b),
        wa4=_block_diag_expand(wa, gate_per).astype(BF16), ba=row(ba),
        wx4=_block_diag_expand(wx, gate_per).astype(BF16), bx=row(bx), lam=row(lam),
    )


def _state_layout(shift, wkv, s_re, s_im, lru_h, lru_conv):
    n, bsz = shift.shape[:2]
    return (shift.reshape(n, bsz, 1, SHIFT_W), wkv.reshape(n, bsz, PAIRS, LANES, HEAD),
            s_re.reshape(n, bsz, S5_TILES, LANES), s_im.reshape(n, bsz, S5_TILES, LANES),
            lru_h.reshape(n, bsz, 1, D_MODEL), lru_conv)


def _tiles(t, bsz):
    pick = lambda want: want if t % want == 0 else t
    share = lambda rows, limit: max(n for n in range(1, bsz + 1) if bsz % n == 0 and n * rows <= max(limit, rows))
    seqs = share(t, SHORT_ROWS) if t < SHORT_ROWS else 1
    rwkv = RWKV_TILE if t % RWKV_TILE == 0 else RWKV_CHUNK
    rwkv_seqs = share(RWKV_CHUNK, RWKV_TILE) if t <= RWKV_CHUNK else 1
    return dict(rwkv=rwkv, rwkv_seqs=rwkv_seqs, s5=pick(S5_TILE), lru=pick(WIDE_TILE), merge=pick(WIDE_TILE),
                seqs=seqs)


def kernel(x_prompt, x_sample, state_rwkv_shift, state_rwkv_wkv, state_s5_re, state_s5_im, state_lru_h,
           state_lru_conv, c_prompt, c_sample, ada_w, ada_b, norm_pre, norm_post, w_in, w_out, rwkv_mu, rwkv_w0,
           rwkv_w2, rwkv_a0, rwkv_a2, rwkv_k_k, rwkv_k_a, rwkv_r_k, rwkv_ln_g, rwkv_ln_b, s5_a_re, s5_a_im,
           s5_log_step, s5_b_re, s5_b_im, s5_c_re, s5_c_im, s5_d, s5_glu_w, s5_glu_b, lru_conv_w, lru_conv_b,
           lru_wa, lru_ba, lru_wx, lru_bx, lru_lambda):
    depth = w_in.shape[0]
    bp, tp, _ = x_prompt.shape
    bs, ts, _ = x_sample.shape
    c_all = jnp.concatenate([c_sample, c_prompt], axis=0)
    pad = (-c_all.shape[0]) % SUBLANES
    c_all = jnp.pad(c_all, ((0, pad), (0, 0)))
    mod_all = _modulation(c_all, ada_w, ada_b)
    mod_all = mod_all.reshape(depth, c_all.shape[0], 3, D_MODEL)
    rwkv = (rwkv_mu, rwkv_w0, rwkv_w2, rwkv_a0, rwkv_a2, rwkv_k_k, rwkv_k_a,
            rwkv_r_k.reshape(depth, D_MODEL), rwkv_ln_g, rwkv_ln_b)
    s5 = (s5_a_re, s5_a_im, s5_log_step, s5_b_re, s5_b_im, s5_c_re, s5_c_im, s5_d, s5_glu_w, s5_glu_b)
    lru = (lru_conv_w, lru_conv_b, lru_wa, lru_ba, lru_wx, lru_bx, lru_lambda)
    prm = _stacked_params(w_in, w_out, norm_pre, norm_post, rwkv, s5, lru)
    zero = _state_layout(jnp.zeros((1, bp, SHIFT_W), F32), jnp.zeros((1, bp, HEADS, HEAD, HEAD), F32),
                         jnp.zeros((1, bp, S5_GROUPS, S5_STATES), F32),
                         jnp.zeros((1, bp, S5_GROUPS, S5_STATES), F32),
                         jnp.zeros((1, bp, D_MODEL), F32), jnp.zeros((1, bp, CONV_W - 1, D_MODEL), F32))
    st_in = _state_layout(state_rwkv_shift, state_rwkv_wkv, state_s5_re, state_s5_im, state_lru_h,
                          state_lru_conv)

    xp, xs = x_prompt, x_sample
    new_p = [[] for _ in range(6)]
    new_s = [[] for _ in range(6)]
    for l in range(depth):
        xp, st_p = _layer(xp, mod_all, zero, prm, sel=(l, 0, bs), t_valid=tp, tiles=_tiles(tp, bp))
        xs, st_s = _layer(xs, mod_all, st_in, prm, sel=(l, l, 0), t_valid=ts, tiles=_tiles(ts, bs))
        for i in range(6):
            new_p[i].append(st_p[i])
            new_s[i].append(st_s[i])
    sp = [jnp.stack(z, axis=0) for z in new_p]
    ss = [jnp.stack(z, axis=0) for z in new_s]
    return (xp, xs, sp[0], sp[1], sp[2], sp[3], sp[4], sp[5], ss[0], ss[1], ss[2], ss[3], ss[4], ss[5])
```

```python
import functools

import jax
import jax.numpy as jnp
from jax import lax
from jax.experimental import pallas as pl
from jax.experimental.pallas import tpu as pltpu

F32 = jnp.float32
BF16 = jnp.bfloat16
HIGHEST = lax.Precision.HIGHEST

SUBLANES = 8
LANES = 128
MXU_TILE = 256
VMEM_LIMIT = 56 * 1024 * 1024

D_MODEL = 1024
HEAD = 64
HEADS = D_MODEL // HEAD
PAIRS = HEADS // 2
LORA = 64
SHIFT_W = 3 * D_MODEL + 2 * LORA
S5_GROUPS = 64
S5_STATES = 64
S5_STATE_W = S5_GROUPS * S5_STATES
LRU_BLOCK = 64
S5_SETS = 4
S5_IN_GROUPS = 8
S5_TILES = S5_STATE_W // LANES
S5_ROW_STRIDE = 40
CONV_W = 4
LRU_C = 8.0
RMS_EPS = 1e-6
GN_EPS = 64e-5
RWKV_CHUNK = 64
RWKV_TILE = 256
S5_TILE = 256
WIDE_TILE = 512
SHORT_ROWS = 512
DECAY_SCALE = 0.6065306597126334
GELU_C = 0.7978845608028654
LOG2_E = 1.4426950408889634
RWKV_INSTANCES = 32


def _sigmoid(x):
    return 0.5 * jnp.tanh(0.5 * x) + 0.5


def _silu(x):
    h = 0.5 * x
    return h * (jnp.tanh(h) + 1.0)


def _softplus(x):
    return jnp.maximum(x, 0.0) + jnp.log1p(jnp.exp(-jnp.abs(x)))


def _norm_mod(x, g, mod):
    ms = jnp.mean(x * x, axis=-1, keepdims=True)
    return (x * lax.rsqrt(ms + RMS_EPS)) * (g * (1.0 + mod[1:2, :])) + mod[0:1, :]


def _dot(a, b):
    return jnp.dot(a, b, preferred_element_type=F32)


def _dot_hi(a, b):
    return jnp.dot(a, b, preferred_element_type=F32, precision=HIGHEST)


def _split2(x):
    hi = x.astype(BF16)
    lo = (x - hi.astype(F32)).astype(BF16)
    return hi, lo


_NN = (((1,), (0,)), ((), ()))
_NT = (((1,), (1,)), ((), ()))
_TN = (((0,), (0,)), ((), ()))


def _mm(a, b, dims=_NN):
    return lax.dot_general(a.astype(BF16), b.astype(BF16), dims, preferred_element_type=F32)


def _head_sum(x, ones_blk):
    outs = []
    for s in range(D_MODEL // MXU_TILE):
        outs.append(_dot(x[:, MXU_TILE * s:MXU_TILE * (s + 1)].astype(BF16), ones_blk))
    return jnp.concatenate(outs, axis=1)


def _block_ones(n, blk):
    ri = lax.broadcasted_iota(jnp.int32, (n, n), 0)
    ci = lax.broadcasted_iota(jnp.int32, (n, n), 1)
    sh = blk.bit_length() - 1
    return ((ri >> sh) == (ci >> sh)).astype(F32).astype(BF16)


def _mod_kernel(c_ref, w_ref, b_ref, o_ref):
    s = _silu(c_ref[...])
    o_ref[...] = _dot_hi(s, w_ref[...]) + b_ref[...]


def _modulation(c_all, ada_w, ada_b):
    depth = ada_w.shape[0]
    rows = c_all.shape[0]
    return pl.pallas_call(
        _mod_kernel,
        grid=(depth, 3),
        in_specs=[
            pl.BlockSpec((rows, D_MODEL), lambda l, j: (0, 0)),
            pl.BlockSpec((None, D_MODEL, D_MODEL), lambda l, j: (l, 0, j)),
            pl.BlockSpec((None, 1, D_MODEL), lambda l, j: (l, 0, j)),
        ],
        out_specs=pl.BlockSpec((None, rows, D_MODEL), lambda l, j: (l, 0, j)),
        out_shape=jax.ShapeDtypeStruct((depth, rows, 3 * D_MODEL), F32),
        name="adaln_mod",
    )(c_all, ada_w, ada_b.reshape(depth, 1, 3 * D_MODEL))


def _s5_prep_kernel(are_ref, aim_ref, ls_ref, bre_ref, bim_ref, abr_ref, abi_ref, bbr_ref, bbi_ref):
    are = are_ref[...]
    aim = aim_ref[...]
    dt = jnp.exp(ls_ref[...])
    mag = jnp.exp(are * dt)
    abr = mag * jnp.cos(aim * dt)
    abi = mag * jnp.sin(aim * dt)
    abr_ref[...] = abr
    abi_ref[...] = abi
    nr = abr - 1.0
    ni = abi
    den = are * are + aim * aim
    cr = (nr * are + ni * aim) / den
    ci = (ni * are - nr * aim) / den
    bre = bre_ref[...]
    bim = bim_ref[...]
    bbr_ref[...] = cr[:, None, :] * bre - ci[:, None, :] * bim
    bbi_ref[...] = cr[:, None, :] * bim + ci[:, None, :] * bre


def _s5_prep(a_re, a_im, log_step, b_re, b_im):
    depth, g, p = a_re.shape
    i = b_re.shape[-1]
    layer = lambda *dims: pl.BlockSpec((None,) + dims, lambda l: (l,) + (0,) * len(dims))
    return pl.pallas_call(
        _s5_prep_kernel,
        grid=(depth,),
        in_specs=[layer(g, p), layer(g, p), layer(g, 1), layer(g, i, p), layer(g, i, p)],
        out_specs=(layer(g, p), layer(g, p), layer(g, i, p), layer(g, i, p)),
        out_shape=(
            jax.ShapeDtypeStruct((depth, g, p), F32),
            jax.ShapeDtypeStruct((depth, g, p), F32),
            jax.ShapeDtypeStruct((depth, g, i, p), F32),
            jax.ShapeDtypeStruct((depth, g, i, p), F32),
        ),
        name="s5_discretise",
    )(a_re, a_im, log_step.reshape(depth, g, 1), jnp.swapaxes(b_re, 2, 3), jnp.swapaxes(b_im, 2, 3))


def _rwkv_kernel(x_ref, mod_ref, g_ref, wr_ref, wg_ref, mu_ref, w0_ref, w2a_ref, a0_ref, kk_ref, ka_ref,
                 rk_ref, lng_ref, lnb_ref, shift0_ref, wkv0_ref,
                 yz_ref, shift_out_ref, wkv_out_ref,
                 carry_ref, st_ref, at_ref, rt_ref, bt_ref, kt_ref, v_ref, pl_ref, y_ref,
                 nh_ref, t_ref, aak_ref, ark_ref, rhs_ref, qg_ref, hc_ref, yc_ref, ones_ref,
                 *, tile, chunk, t_valid, group):
    t = pl.program_id(1)
    n_chunks = tile // chunk
    two = 2 * chunk
    n_inst = group * n_chunks
    chained = x_ref.shape[0] == 1

    def pair_state(s0):
        rh = lax.broadcasted_iota(jnp.int32, (LANES, LANES), 0) >> (HEAD.bit_length() - 1)
        ch = lax.broadcasted_iota(jnp.int32, (LANES, LANES), 1) >> (HEAD.bit_length() - 1)
        return jnp.where(rh == ch, jnp.concatenate([s0, s0], axis=1), 0.0).T

    def head_states(st):
        sv = st.T
        return sv[:, :HEAD] + sv[:, HEAD:]

    if chained:
        @pl.when(t == 0)
        def _():
            carry_ref[...] = shift0_ref[0]
            for q in range(PAIRS):
                st_ref[q] = pair_state(wkv0_ref[0, q])

    h = _pre_norm(x_ref, g_ref, mod_ref)
    p = _dot(h, wr_ref[...])
    zg = _silu(_dot(h, wg_ref[...]))

    rows = lax.broadcasted_iota(jnp.int32, (tile, 1), 0)
    rmod = rows & (chunk - 1)
    last = (t_valid - 1) % chunk
    shifted = pltpu.roll(p, 1, 0)
    top_row = lax.broadcasted_iota(jnp.int32, (SUBLANES, 1), 0) == 0
    if chained:
        starts = {0: carry_ref[...]}
        carry_ref[...] = p[tile - chunk + last:tile - chunk + last + 1, :]
    else:
        starts = {s * chunk: shift0_ref[s] for s in range(n_chunks)}
        for s in range(n_chunks):
            shift_out_ref[s] = p[s * chunk + last:s * chunk + last + 1, :]
    pieces, at = [], 0
    for r0 in sorted(starts):
        if r0 > at:
            pieces.append(shifted[at:r0, :])
        pieces.append(jnp.where(top_row, starts[r0], shifted[r0:r0 + SUBLANES, :]))
        at = r0 + SUBLANES
    pieces.append(shifted[at:, :])
    prev = jnp.concatenate(pieces, axis=0)
    pm = p + (prev - p) * mu_ref[...]

    r = pm[:, 0:D_MODEL]
    k = pm[:, D_MODEL:2 * D_MODEL]
    v = pm[:, 2 * D_MODEL:3 * D_MODEL]
    wa = pm[:, 3 * D_MODEL:SHIFT_W]
    lane = lax.broadcasted_iota(jnp.int32, (1, LANES), 1)
    wa = jnp.where(lane < LORA, jnp.tanh(wa), wa)
    lora = _dot(wa.astype(BF16), w2a_ref[...])
    logd = -(DECAY_SCALE * LOG2_E) * _sigmoid(w0_ref[...] + lora[:, :D_MODEL])
    a_sig = _sigmoid(a0_ref[...] + lora[:, D_MODEL:])

    @pl.when(t == 0)
    def _():
        ones_ref[...] = _block_ones(MXU_TILE, HEAD)

    ones_blk = ones_ref[...]
    kk = k * kk_ref[...]
    kk = kk * jnp.minimum(lax.rsqrt(_head_sum(kk * kk, ones_blk)), 1e12)
    k2 = k * (1.0 + (a_sig - 1.0) * ka_ref[...])
    bvec = kk * a_sig
    if t_valid % chunk != 0:
        ok = rmod < t_valid % chunk
        logd = jnp.where(ok, logd, 0.0)
        bvec = jnp.where(ok, bvec, 0.0)
        k2 = jnp.where(ok, k2, 0.0)

    cum = logd
    s = 1
    while s < chunk:
        cum = cum + jnp.where(rmod >= s, pltpu.roll(cum, s, 0), 0.0)
        s *= 2
    p_inc = jnp.exp2(cum)
    p_inv = jnp.exp2(-cum)
    p_exc = jnp.exp2(cum - logd)

    def to_pairs(ref, val):
        for q in range(PAIRS):
            ref[q] = val[:, LANES * q:LANES * (q + 1)]

    to_pairs(at_ref, -kk * p_exc)
    to_pairs(rt_ref, r * p_inc)
    to_pairs(bt_ref, bvec * p_inv)
    to_pairs(kt_ref, k2 * p_inv)
    to_pairs(v_ref, v)
    for j in range(n_chunks):
        row = jnp.exp2(cum[(j + 1) * chunk - 1:(j + 1) * chunk, :])
        for q in range(PAIRS):
            pl_ref[q, j] = jnp.broadcast_to(row[:, LANES * q:LANES * (q + 1)], (SUBLANES, LANES))

    m_a = (lane < HEAD).astype(F32)
    m_b = 1.0 - m_a
    ri = lax.broadcasted_iota(jnp.int32, (two, two), 0)
    ci = lax.broadcasted_iota(jnp.int32, (two, two), 1)
    csh = chunk.bit_length() - 1
    same_head = (ri >> csh) == (ci >> csh)
    strict = jnp.logical_and(same_head, (ri & (chunk - 1)) > (ci & (chunk - 1)))
    incl = jnp.logical_and(same_head, (ri & (chunk - 1)) >= (ci & (chunk - 1)))
    eye = ri == ci
    head_a = ri < chunk
    off_masks = []
    for lg in range(csh):
        off_masks.append(jnp.logical_and(strict, jnp.logical_and((ri >> (lg + 1)) == (ci >> (lg + 1)),
                                                                 (ri >> lg) != (ci >> lg))))

    def stack(x):
        return jnp.concatenate([x * m_a, x * m_b], axis=0)

    gsh = group.bit_length() - 1
    unroll = n_inst

    def group_body(g, carry):
        def where_is(i):
            q = g * group + (i & (group - 1))
            j = i >> gsh
            return q, j, pl.ds(pl.multiple_of(j * chunk, chunk), chunk)

        def gram(i, c):
            q, _, sl = where_is(i)
            atm = stack(at_ref[q, sl, :])
            rtm = stack(rt_ref[q, sl, :])
            bt = bt_ref[q, sl, :]
            kt = kt_ref[q, sl, :]
            o1 = _mm(jnp.concatenate([atm, rtm], axis=0), jnp.concatenate([bt, kt], axis=0), _NT)
            o1_sw = pltpu.roll(o1, chunk, 1)
            n_ab = jnp.where(strict, jnp.where(head_a, o1[:two], o1_sw[:two]), 0.0)
            nh_ref[i] = n_ab.astype(BF16)
            t_ref[i] = jnp.where(eye, 1.0, jnp.where(off_masks[0], n_ab, 0.0)).astype(BF16)
            aak_ref[i] = jnp.where(strict, jnp.where(head_a, o1_sw[:two], o1[:two]), 0.0).astype(BF16)
            ark_ref[i] = jnp.concatenate([jnp.where(incl, jnp.where(head_a, o1[two:], o1_sw[two:]), 0.0),
                                          jnp.where(incl, jnp.where(head_a, o1_sw[two:], o1[two:]), 0.0)],
                                         axis=1).astype(BF16)
            return c

        lax.fori_loop(0, n_inst, gram, 0, unroll=unroll)

        for off in off_masks[1:]:
            def level(i, c, off=off):
                t_inv = t_ref[i]
                x = _dot(jnp.where(off, nh_ref[i], 0.0), t_inv).astype(BF16)
                t_ref[i] = t_inv + _dot(t_inv, x).astype(BF16)
                return c

            lax.fori_loop(0, n_inst, level, 0, unroll=unroll)

        def right_side(i, c):
            q, _, sl = where_is(i)
            atm = stack(at_ref[q, sl, :])
            vm = stack(v_ref[q, sl, :])
            z0_hi, z0_lo = _split2(jnp.concatenate([atm, _mm(aak_ref[i], vm)], axis=1))
            rhs_ref[i] = jnp.concatenate([z0_hi, z0_lo], axis=0)
            return c

        lax.fori_loop(0, n_inst, right_side, 0, unroll=unroll)

        def apply(i, c):
            q, _, sl = where_is(i)
            vm = stack(v_ref[q, sl, :])
            t_inv = t_ref[i]
            z = _dot(t_inv, rhs_ref[i, :two, :]) + _dot(t_inv, rhs_ref[i, two:, :])
            rhs_ref[i] = jnp.concatenate([z, jnp.concatenate([jnp.zeros_like(vm), vm], axis=1)], axis=0).astype(BF16)
            return c

        lax.fori_loop(0, n_inst, apply, 0, unroll=unroll)

        def operators(i, c):
            q, j, sl = where_is(i)
            rhs2 = rhs_ref[i]
            k_all = g * n_inst + i
            p_last = pl_ref[q, j][0:1, :]
            top = _mm(jnp.concatenate([stack(bt_ref[q, sl, :] * p_last), stack(kt_ref[q, sl, :] * p_last)], axis=0),
                      rhs2, _TN)
            bot = _dot(ark_ref[i], rhs2)
            g_mat = top[:, :two] + jnp.where(eye, jnp.broadcast_to(p_last, (two, two)), 0.0)
            q_eff = bot[:, :two] + stack(rt_ref[q, sl, :])
            qg_ref[k_all] = jnp.concatenate([q_eff, g_mat], axis=0).astype(BF16)
            hc_ref[k_all] = top[:, two:]
            yc_ref[k_all] = bot[:, two:]
            return c

        lax.fori_loop(0, n_inst, operators, 0, unroll=unroll)
        return carry

    lax.fori_loop(0, PAIRS // group, group_body, 0)

    if chained:
        states = [st_ref[q] for q in range(PAIRS)]
    for j in range(n_chunks):
        for q in range(PAIRS):
            k_all = (q // group) * n_inst + j * group + q % group
            st = (states[q] if chained else pair_state(wkv0_ref[j, q])).astype(BF16)
            both = _dot(qg_ref[k_all], st)
            ym = both[:two, :] + yc_ref[k_all]
            y_ref[q, j * chunk:(j + 1) * chunk, :] = ym[:chunk, :] + ym[chunk:, :]
            st_new = both[two:, :] + hc_ref[k_all]
            if chained:
                states[q] = st_new
            else:
                wkv_out_ref[j, q] = head_states(st_new)
    if chained:
        for q in range(PAIRS):
            st_ref[q] = states[q]

    ys = jnp.concatenate([y_ref[q] for q in range(PAIRS)], axis=1)
    mean_blk = ones_blk * (1.0 / HEAD)
    mean = _head_sum(ys, mean_blk)
    yc = ys - mean
    var = _head_sum(yc * yc, mean_blk)
    yn = yc * lax.rsqrt(var + GN_EPS) * lng_ref[...] + lnb_ref[...]
    bonus = _head_sum(r * k2 * rk_ref[...], ones_blk) * v
    yz_ref[...] = ((yn + bonus) * zg).reshape(yz_ref.shape)

    if chained:
        @pl.when(t == pl.num_programs(1) - 1)
        def _():
            shift_out_ref[0] = carry_ref[...]
            for q in range(PAIRS):
                wkv_out_ref[0, q] = head_states(st_ref[q])


def _specs(layer, state_layer, mod_row, seqs=None):
    assert seqs is None or mod_row % seqs == 0
    per_step = 1 if seqs is None else seqs

    def weight(a):
        return pl.BlockSpec((None,) + a.shape[1:], lambda b, t: (layer,) + (0,) * (a.ndim - 1))

    def state(a):
        return pl.BlockSpec((None, seqs) + a.shape[2:], lambda b, t: (state_layer, b) + (0,) * (a.ndim - 2))

    def mod(a):
        return pl.BlockSpec((None, seqs) + a.shape[2:], lambda b, t: (layer, mod_row // per_step + b, 0, 0))

    return weight, state, mod


def _pre_norm(x_ref, g_ref, mod_ref):
    g = g_ref[...]
    rows = [_norm_mod(x_ref[s], g, mod_ref[s]) for s in range(x_ref.shape[0])]
    return jnp.concatenate(rows, axis=0).astype(BF16)


def _rwkv_mixer(x, mod, g_pre, wr, wg, vecs, w2a, shift0, wkv0, *, sel, tile, t_valid, seqs):
    bsz, tp, _ = x.shape
    chunk = RWKV_CHUNK
    assert tp % tile == 0 and tile % chunk == 0 and bsz % seqs == 0
    assert t_valid == tp or tp == tile
    assert seqs == 1 or tp == chunk
    nt = tp // tile
    block_rows = tile
    tile = seqs * tile
    mu, w0, a0, k_k, k_a, r_k, ln_g, ln_b = vecs
    weight, state, mod_spec = _specs(*sel, seqs=seqs)
    pair_buf = pltpu.VMEM((PAIRS, tile, LANES), F32)
    group = min(PAIRS, max(1, RWKV_INSTANCES // (tile // chunk)))
    n_inst = group * (tile // chunk)
    two = 2 * chunk
    inst = lambda cols, dtype: pltpu.VMEM((n_inst, two, cols), dtype)
    every = lambda dtype: pltpu.VMEM((PAIRS * (tile // chunk), two, two), dtype)
    kern = functools.partial(_rwkv_kernel, tile=tile, chunk=chunk, t_valid=t_valid, group=group)
    return pl.pallas_call(
        kern,
        grid=(bsz // seqs, nt),
        in_specs=[
            pl.BlockSpec((seqs, block_rows, D_MODEL), lambda b, t: (b, t, 0)),
            mod_spec(mod),
            weight(g_pre), weight(wr), weight(wg), weight(mu), weight(w0), weight(w2a), weight(a0),
            weight(k_k), weight(k_a), weight(r_k), weight(ln_g), weight(ln_b),
            state(shift0), state(wkv0),
        ],
        out_specs=[
            pl.BlockSpec((seqs, block_rows, D_MODEL), lambda b, t: (b, t, 0)),
            pl.BlockSpec((seqs, 1, SHIFT_W), lambda b, t: (b, 0, 0)),
            pl.BlockSpec((seqs, PAIRS, LANES, HEAD), lambda b, t: (b, 0, 0, 0)),
        ],
        out_shape=[
            jax.ShapeDtypeStruct((bsz, tp, D_MODEL), F32),
            jax.ShapeDtypeStruct((bsz, 1, SHIFT_W), F32),
            jax.ShapeDtypeStruct((bsz, PAIRS, LANES, HEAD), F32),
        ],
        scratch_shapes=[
            pltpu.VMEM((1, SHIFT_W), F32),
            pltpu.VMEM((PAIRS, LANES, LANES), F32),
            pair_buf, pair_buf, pair_buf, pair_buf, pair_buf,
            pltpu.VMEM((PAIRS, tile // chunk, SUBLANES, LANES), F32),
            pair_buf,
            inst(two, BF16), inst(two, BF16), inst(two, BF16), inst(2 * two, BF16),
            pltpu.VMEM((n_inst, 2 * two, 2 * two), BF16),
            pltpu.VMEM((PAIRS * (tile // chunk), 2 * two, two), BF16), every(F32), every(F32),
            pltpu.VMEM((MXU_TILE, MXU_TILE), BF16),
        ],
        compiler_params=pltpu.CompilerParams(
            dimension_semantics=("arbitrary", "arbitrary"), vmem_limit_bytes=VMEM_LIMIT),
        name="rwkv7_mixer",
    )(x, mod, g_pre, wr, wg, mu, w0, w2a, a0, k_k, k_a, r_k, ln_g, ln_b, shift0, wkv0)


def _s5_kernel(x_ref, mod_ref, g_ref, wu_ref, wg_ref, bbr_ref, bbi_ref, cre_ref, cmi_ref, abr_ref, abi_ref,
               d_ref, gluw_ref, glub_ref, s0r_ref, s0i_ref,
               yz_ref, sr_out_ref, si_out_ref,
               hr_ref, hi_ref, cr_ref, ci_ref, *, tile):
    t = pl.program_id(1)

    @pl.when(t == 0)
    def _():
        cr_ref[...] = s0r_ref[...]
        ci_ref[...] = s0i_ref[...]

    seqs = x_ref.shape[0]
    frames = seqs * tile
    h = _pre_norm(x_ref, g_ref, mod_ref)
    u = _dot(h, wu_ref[...])
    zg = _silu(_dot(h, wg_ref[...]))
    set_tiles = S5_TILES // S5_SETS

    def frame_rows(c):
        return pl.ds(c, frames, stride=S5_ROW_STRIDE)

    in_tiles = bbr_ref.shape[2] // LANES
    for s in range(D_MODEL // LANES):
        ub = u[:, LANES * s:LANES * (s + 1)].astype(BF16)
        bur = _dot(ub, bbr_ref[s])
        bui = _dot(ub, bbi_ref[s])
        for c in range(in_tiles):
            hr_ref[frame_rows(s * in_tiles + c), :] = bur[:, LANES * c:LANES * (c + 1)]
            hi_ref[frame_rows(s * in_tiles + c), :] = bui[:, LANES * c:LANES * (c + 1)]

    abr = abr_ref[...]
    abi = abi_ref[...]

    def frame(f, carry):
        sr, si = carry
        rows = pl.ds(pl.multiple_of(f * S5_ROW_STRIDE, SUBLANES), S5_TILES)
        nr = abr * sr - abi * si + hr_ref[rows, :]
        ni = abr * si + abi * sr + hi_ref[rows, :]
        hr_ref[rows, :] = nr
        hi_ref[rows, :] = ni
        return nr, ni

    for s in range(seqs):
        cr_ref[s], ci_ref[s] = lax.fori_loop(s * tile, (s + 1) * tile, frame, (cr_ref[s], ci_ref[s]), unroll=16)

    outs = []
    for s in range(S5_SETS):
        hr = jnp.concatenate([hr_ref[frame_rows(s * set_tiles + c), :] for c in range(set_tiles)], axis=1)
        hi = jnp.concatenate([hi_ref[frame_rows(s * set_tiles + c), :] for c in range(set_tiles)], axis=1)
        outs.append(_dot(hr.astype(BF16), cre_ref[s]) + _dot(hi.astype(BF16), cmi_ref[s]))
    y = jnp.concatenate(outs, axis=1) + d_ref[...] * u
    y = (0.5 * y) * (1.0 + jnp.tanh(y * (GELU_C + (GELU_C * 0.044715) * (y * y))))
    y = y * _sigmoid(_dot(y.astype(BF16), gluw_ref[...]) + glub_ref[...])
    yz_ref[...] = (y * zg).reshape(seqs, tile, D_MODEL)

    @pl.when(t == pl.num_programs(1) - 1)
    def _():
        sr_out_ref[...] = cr_ref[...]
        si_out_ref[...] = ci_ref[...]


def _s5_mixer(x, mod, g_pre, wu, wg, bbr, bbi, cre, cmi, abr, abi, d_skip, glu_w, glu_b, s0r, s0i, *, sel, tile,
              seqs):
    bsz, tp, _ = x.shape
    assert tp % tile == 0 and tile % SUBLANES == 0 and bsz % seqs == 0
    nt = tp // tile
    weight, state, mod_spec = _specs(*sel, seqs=seqs)
    state_out = pl.BlockSpec((seqs, S5_TILES, LANES), lambda b, t: (b, 0, 0))
    kern = functools.partial(_s5_kernel, tile=tile)
    return pl.pallas_call(
        kern,
        grid=(bsz // seqs, nt),
        in_specs=[
            pl.BlockSpec((seqs, tile, D_MODEL), lambda b, t: (b, t, 0)),
            mod_spec(mod),
            weight(g_pre), weight(wu), weight(wg), weight(bbr), weight(bbi), weight(cre), weight(cmi),
            weight(abr), weight(abi), weight(d_skip), weight(glu_w), weight(glu_b), state(s0r), state(s0i),
        ],
        out_specs=[pl.BlockSpec((seqs, tile, D_MODEL), lambda b, t: (b, t, 0)), state_out, state_out],
        out_shape=[
            jax.ShapeDtypeStruct((bsz, tp, D_MODEL), F32),
            jax.ShapeDtypeStruct((bsz, S5_TILES, LANES), F32),
            jax.ShapeDtypeStruct((bsz, S5_TILES, LANES), F32),
        ],
        scratch_shapes=[
            pltpu.VMEM((seqs * tile * S5_ROW_STRIDE, LANES), F32),
            pltpu.VMEM((seqs * tile * S5_ROW_STRIDE, LANES), F32),
            pltpu.VMEM((seqs, S5_TILES, LANES), F32),
            pltpu.VMEM((seqs, S5_TILES, LANES), F32),
        ],
        compiler_params=pltpu.CompilerParams(
            dimension_semantics=("arbitrary", "arbitrary"), vmem_limit_bytes=VMEM_LIMIT),
        name="s5_mixer",
    )(x, mod, g_pre, wu, wg, bbr, bbi, cre, cmi, abr, abi, d_skip, glu_w, glu_b, s0r, s0i)


def _lru_kernel(x_ref, mod_ref, g_ref, wx_ref, wg_ref, cw_ref, cb_ref, wa4_ref, ba_ref, wx4_ref, bx_ref, lam_ref,
                conv0_ref, h0_ref,
                yz_ref, conv_out_ref, h_out_ref,
                xbuf_ref, a_ref, b_ref, hc_ref, *, tile):
    t = pl.program_id(1)
    pad = SUBLANES
    hist = CONV_W - 1
    seqs = x_ref.shape[0]

    @pl.when(t == 0)
    def _():
        xbuf_ref[:, pad - hist:pad, :] = conv0_ref[...]
        hc_ref[...] = h0_ref[...]

    h = _pre_norm(x_ref, g_ref, mod_ref)
    xl = _dot(h, wx_ref[...])
    zg = _silu(_dot(h, wg_ref[...]))
    xcs, tails = [], []
    for s in range(seqs):
        xs = xl[s * tile:(s + 1) * tile, :]
        xbuf_ref[s, pad:pad + tile, :] = xs
        acc = cb_ref[...] + xs * cw_ref[CONV_W - 1:CONV_W, :]
        for j in range(hist):
            acc = acc + xbuf_ref[s, pad - hist + j:pad - hist + j + tile, :] * cw_ref[j:j + 1, :]
        tails.append(xbuf_ref[s, pad + tile - hist:pad + tile, :])
        xbuf_ref[s, pad - hist:pad, :] = tails[s]
        xcs.append(acc)
    xc = jnp.concatenate(xcs, axis=0)

    ga, gx = [], []
    blk = MXU_TILE
    for s in range(D_MODEL // blk):
        xb = xc[:, blk * s:blk * (s + 1)].astype(BF16)
        ga.append(_dot(xb, wa4_ref[s]))
        gx.append(_dot(xb, wx4_ref[s]))
    gate_r = _sigmoid(jnp.concatenate(ga, axis=1) + ba_ref[...])
    gate_i = _sigmoid(jnp.concatenate(gx, axis=1) + bx_ref[...])
    log_a = -LRU_C * gate_r * _softplus(-lam_ref[...])
    a = jnp.exp(log_a)
    a_ref[...] = a
    var = -jnp.tanh(log_a) * (a * a + 1.0)
    b_ref[...] = jnp.where(var > 0.0, var * lax.rsqrt(var), 0.0) * (gate_i * xc)

    row8 = lax.broadcasted_iota(jnp.int32, (SUBLANES, 1), 0)

    def row_body(g, carry):
        rs = pl.ds(pl.multiple_of(g * SUBLANES, SUBLANES), SUBLANES)
        av = a_ref[rs, :]
        bv = b_ref[rs, :]
        for sh in (1, 2, 4):
            keep = row8 >= sh
            ash = jnp.where(keep, pltpu.roll(av, sh, 0), 1.0)
            bsh = jnp.where(keep, pltpu.roll(bv, sh, 0), 0.0)
            bv = bv + av * bsh
            av = av * ash
        hs = bv + av * carry
        b_ref[rs, :] = hs
        return hs[SUBLANES - 1:SUBLANES, :]

    groups = tile // SUBLANES
    for s in range(seqs):
        hc_ref[s] = lax.fori_loop(s * groups, (s + 1) * groups, row_body, hc_ref[s], unroll=4)
    yz_ref[...] = (b_ref[...] * zg).reshape(seqs, tile, D_MODEL)

    @pl.when(t == pl.num_programs(1) - 1)
    def _():
        for s in range(seqs):
            conv_out_ref[s] = tails[s]
        h_out_ref[...] = hc_ref[...]


def _lru_mixer(x, mod, g_pre, wx, wg, conv_w, conv_b, wa4, ba, wx4, bx, lam, conv0, h0, *, sel, tile, seqs):
    bsz, tp, _ = x.shape
    assert tp % tile == 0 and tile % SUBLANES == 0 and tile >= SUBLANES and bsz % seqs == 0
    nt = tp // tile
    weight, state, mod_spec = _specs(*sel, seqs=seqs)
    kern = functools.partial(_lru_kernel, tile=tile)
    return pl.pallas_call(
        kern,
        grid=(bsz // seqs, nt),
        in_specs=[
            pl.BlockSpec((seqs, tile, D_MODEL), lambda b, t: (b, t, 0)),
            mod_spec(mod),
            weight(g_pre), weight(wx), weight(wg), weight(conv_w), weight(conv_b), weight(wa4), weight(ba),
            weight(wx4), weight(bx), weight(lam), state(conv0), state(h0),
        ],
        out_specs=[
            pl.BlockSpec((seqs, tile, D_MODEL), lambda b, t: (b, t, 0)),
            pl.BlockSpec((seqs, CONV_W - 1, D_MODEL), lambda b, t: (b, 0, 0)),
            pl.BlockSpec((seqs, 1, D_MODEL), lambda b, t: (b, 0, 0)),
        ],
        out_shape=[
            jax.ShapeDtypeStruct((bsz, tp, D_MODEL), F32),
            jax.ShapeDtypeStruct((bsz, CONV_W - 1, D_MODEL), F32),
            jax.ShapeDtypeStruct((bsz, 1, D_MODEL), F32),
        ],
        scratch_shapes=[
            pltpu.VMEM((seqs, tile + SUBLANES, D_MODEL), F32),
            pltpu.VMEM((seqs * tile, D_MODEL), F32),
            pltpu.VMEM((seqs * tile, D_MODEL), F32),
            pltpu.VMEM((seqs, 1, D_MODEL), F32),
        ],
        compiler_params=pltpu.CompilerParams(
            dimension_semantics=("arbitrary", "arbitrary"), vmem_limit_bytes=VMEM_LIMIT),
        name="rglru_mixer",
    )(x, mod, g_pre, wx, wg, conv_w, conv_b, wa4, ba, wx4, bx, lam, conv0, h0)


def _merge_kernel(x_ref, mod_ref, g_ref, gpost_ref, wm_ref, wo_ref, yr_ref, ys_ref, yl_ref, o_ref):
    seqs, tile, _ = x_ref.shape
    rows = lambda ref: ref[...].reshape(seqs * tile, D_MODEL)
    m = _sigmoid(_dot(_pre_norm(x_ref, g_ref, mod_ref), wm_ref[...]))
    merged = (m[:, :D_MODEL] * rows(yr_ref) + m[:, D_MODEL:2 * D_MODEL] * rows(ys_ref)
              + m[:, 2 * D_MODEL:] * rows(yl_ref))
    o = _dot(merged.astype(BF16), wo_ref[...])
    ms = jnp.mean(o * o, axis=-1, keepdims=True)
    o = (o * lax.rsqrt(ms + RMS_EPS)) * gpost_ref[...]
    for s in range(seqs):
        o_ref[s] = x_ref[s] + mod_ref[s][2:3, :] * o[s * tile:(s + 1) * tile, :]


def _merge_out(x, mod, g_pre, g_post, wm, wo, yr, ys, yl, *, sel, tile, seqs):
    bsz, tp, _ = x.shape
    assert tp % tile == 0 and bsz % seqs == 0
    nt = tp // tile
    act = pl.BlockSpec((seqs, tile, D_MODEL), lambda b, t: (b, t, 0))
    weight, _, mod_spec = _specs(*sel, seqs=seqs)
    return pl.pallas_call(
        _merge_kernel,
        grid=(bsz // seqs, nt),
        in_specs=[act, mod_spec(mod), weight(g_pre), weight(g_post), weight(wm), weight(wo), act, act, act],
        out_specs=act,
        out_shape=jax.ShapeDtypeStruct((bsz, tp, D_MODEL), F32),
        compiler_params=pltpu.CompilerParams(
            dimension_semantics=("arbitrary", "arbitrary"), vmem_limit_bytes=VMEM_LIMIT),
        name="merge_out",
    )(x, mod, g_pre, g_post, wm, wo, yr, ys, yl)


def _block_diag_expand(w, per):
    depth, n, a, b = w.shape
    w = w.reshape(depth, n // per, per, a, b)
    eye = jnp.eye(per, dtype=w.dtype)
    return jnp.einsum("lsgab,gh->lsgahb", w, eye).reshape(depth, n // per, per * a, per * b)


def _layer(x, mod, state, prm, *, sel, t_valid, tiles):
    shift_row, wkv, s_re, s_im, lru_h, lru_conv = state
    bsz = x.shape[0]
    g_pre = prm["g_pre"]

    xr = x
    if x.shape[1] % tiles["rwkv"] != 0:
        xr = jnp.pad(x, ((0, 0), (0, tiles["rwkv"] - x.shape[1] % tiles["rwkv"]), (0, 0)))
    yz_r, shift_new, wkv_new = _rwkv_mixer(
        xr, mod, g_pre, prm["w_rwkv"], prm["w_rwkv_g"], prm["rwkv_vecs"], prm["w2a"], shift_row, wkv,
        sel=sel, tile=tiles["rwkv"], t_valid=t_valid, seqs=tiles["rwkv_seqs"])
    yz_r = yz_r[:, :t_valid]
    yz_s, s_re_new, s_im_new = _s5_mixer(
        x, mod, g_pre, prm["w_s5"], prm["w_s5_g"], prm["bbr"], prm["bbi"], prm["cre"], prm["cmi"],
        prm["abr"], prm["abi"], prm["s5_d"], prm["glu_w"], prm["glu_b"], s_re, s_im, sel=sel, tile=tiles["s5"],
        seqs=tiles["seqs"])
    yz_l, conv_new, h_new = _lru_mixer(
        x, mod, g_pre, prm["w_lru"], prm["w_lru_g"], prm["conv_w"], prm["conv_b"], prm["wa4"], prm["ba"],
        prm["wx4"], prm["bx"], prm["lam"], lru_conv, lru_h, sel=sel, tile=tiles["lru"], seqs=tiles["seqs"])
    x_new = _merge_out(x, mod, g_pre, prm["g_post"], prm["w_merge"], prm["w_out"], yz_r, yz_s, yz_l,
                       sel=sel, tile=tiles["merge"], seqs=tiles["seqs"])
    new_state = (shift_new.reshape(bsz, SHIFT_W), wkv_new.reshape(bsz, HEADS, HEAD, HEAD),
                 s_re_new.reshape(bsz, S5_GROUPS, S5_STATES), s_im_new.reshape(bsz, S5_GROUPS, S5_STATES),
                 h_new.reshape(bsz, D_MODEL), conv_new)
    return x_new, new_state


def _stacked_params(w_in, w_out, norm_pre, norm_post, rwkv, s5, lru):
    depth = w_in.shape[0]
    d = D_MODEL
    wb = w_in.astype(BF16)
    o = SHIFT_W
    row = lambda a: a.reshape(depth, 1, -1)
    mu, w0, w2, a0, a2, k_k, k_a, r_k, ln_g, ln_b = rwkv
    a_re, a_im, log_step, b_re, b_im, c_re, c_im, s5_d, glu_w, glu_b = s5
    conv_w, conv_b, wa, ba, wx, bx, lam = lru
    zeros = jnp.zeros((depth, LORA, d), F32)
    w2a = jnp.concatenate([jnp.concatenate([w2, zeros], axis=2), jnp.concatenate([zeros, a2], axis=2)], axis=1)
    abr, abi, bbr, bbi = _s5_prep(a_re, a_im, log_step, b_re, b_im)
    per = S5_GROUPS // S5_SETS
    gate_per = MXU_TILE // LRU_BLOCK
    return dict(
        g_pre=row(norm_pre), g_post=row(norm_post),
        w_rwkv=wb[:, :, :o], w_rwkv_g=wb[:, :, o:o + d],
        w_s5=wb[:, :, o + d:o + 2 * d], w_s5_g=wb[:, :, o + 2 * d:o + 3 * d],
        w_lru=wb[:, :, o + 3 * d:o + 4 * d], w_lru_g=wb[:, :, o + 4 * d:o + 5 * d],
        w_merge=wb[:, :, o + 5 * d:], w_out=w_out.astype(BF16),
        rwkv_vecs=(row(mu), row(w0), row(a0), row(k_k), row(k_a), row(r_k), row(ln_g), row(ln_b)),
        w2a=w2a.astype(BF16),
        bbr=_block_diag_expand(bbr, S5_IN_GROUPS).astype(BF16),
        bbi=_block_diag_expand(bbi, S5_IN_GROUPS).astype(BF16),
        cre=_block_diag_expand(jnp.swapaxes(c_re, 2, 3), per).astype(BF16),
        cmi=_block_diag_expand(-jnp.swapaxes(c_im, 2, 3), per).astype(BF16),
        abr=abr.reshape(depth, S5_TILES, LANES), abi=abi.reshape(depth, S5_TILES, LANES),
        s5_d=row(s5_d), glu_w=glu_w.astype(BF16), glu_b=row(glu_b),
        conv_w=conv_w, conv_b=row(conv_b),
        wa4=_block_diag_expand(wa, gate_per).astype(BF16), ba=row(ba),
        wx4=_block_diag_expand(wx, gate_per).astype(BF16), bx=row(bx), lam=row(lam),
    )


def _state_layout(shift, wkv, s_re, s_im, lru_h, lru_conv):
    n, bsz = shift.shape[:2]
    return (shift.reshape(n, bsz, 1, SHIFT_W), wkv.reshape(n, bsz, PAIRS, LANES, HEAD),
            s_re.reshape(n, bsz, S5_TILES, LANES), s_im.reshape(n, bsz, S5_TILES, LANES),
            lru_h.reshape(n, bsz, 1, D_MODEL), lru_conv)


def _tiles(t, bsz):
    pick = lambda want: want if t % want == 0 else t
    share = lambda rows, limit: max(n for n in range(1, bsz + 1) if bsz % n == 0 and n * rows <= max(limit, rows))
    seqs = share(t, SHORT_ROWS) if t < SHORT_ROWS else 1
    rwkv = RWKV_TILE if t % RWKV_TILE == 0 else RWKV_CHUNK
    rwkv_seqs = share(RWKV_CHUNK, RWKV_TILE) if t <= RWKV_CHUNK else 1
    return dict(rwkv=rwkv, rwkv_seqs=rwkv_seqs, s5=pick(S5_TILE), lru=pick(WIDE_TILE), merge=pick(WIDE_TILE),
                seqs=seqs)


def kernel(x_prompt, x_sample, state_rwkv_shift, state_rwkv_wkv, state_s5_re, state_s5_im, state_lru_h,
           state_lru_conv, c_prompt, c_sample, ada_w, ada_b, norm_pre, norm_post, w_in, w_out, rwkv_mu, rwkv_w0,
           rwkv_w2, rwkv_a0, rwkv_a2, rwkv_k_k, rwkv_k_a, rwkv_r_k, rwkv_ln_g, rwkv_ln_b, s5_a_re, s5_a_im,
           s5_log_step, s5_b_re, s5_b_im, s5_c_re, s5_c_im, s5_d, s5_glu_w, s5_glu_b, lru_conv_w, lru_conv_b,
           lru_wa, lru_ba, lru_wx, lru_bx, lru_lambda):
    depth = w_in.shape[0]
    bp, tp, _ = x_prompt.shape
    bs, ts, _ = x_sample.shape
    c_all = jnp.concatenate([c_sample, c_prompt], axis=0)
    pad = (-c_all.shape[0]) % SUBLANES
    c_all = jnp.pad(c_all, ((0, pad), (0, 0)))
    mod_all = _modulation(c_all, ada_w, ada_b)
    mod_all = mod_all.reshape(depth, c_all.shape[0], 3, D_MODEL)
    rwkv = (rwkv_mu, rwkv_w0, rwkv_w2, rwkv_a0, rwkv_a2, rwkv_k_k, rwkv_k_a,
            rwkv_r_k.reshape(depth, D_MODEL), rwkv_ln_g, rwkv_ln_b)
    s5 = (s5_a_re, s5_a_im, s5_log_step, s5_b_re, s5_b_im, s5_c_re, s5_c_im, s5_d, s5_glu_w, s5_glu_b)
    lru = (lru_conv_w, lru_conv_b, lru_wa, lru_ba, lru_wx, lru_bx, lru_lambda)
    prm = _stacked_params(w_in, w_out, norm_pre, norm_post, rwkv, s5, lru)
    zero = _state_layout(jnp.zeros((1, bp, SHIFT_W), F32), jnp.zeros((1, bp, HEADS, HEAD, HEAD), F32),
                         jnp.zeros((1, bp, S5_GROUPS, S5_STATES), F32),
                         jnp.zeros((1, bp, S5_GROUPS, S5_STATES), F32),
                         jnp.zeros((1, bp, D_MODEL), F32), jnp.zeros((1, bp, CONV_W - 1, D_MODEL), F32))
    st_in = _state_layout(state_rwkv_shift, state_rwkv_wkv, state_s5_re, state_s5_im, state_lru_h,
                          state_lru_conv)

    xp, xs = x_prompt, x_sample
    new_p = [[] for _ in range(6)]
    new_s = [[] for _ in range(6)]
    for l in range(depth):
        xp, st_p = _layer(xp, mod_all, zero, prm, sel=(l, 0, bs), t_valid=tp, tiles=_tiles(tp, bp))
        xs, st_s = _layer(xs, mod_all, st_in, prm, sel=(l, l, 0), t_valid=ts, tiles=_tiles(ts, bs))
        for i in range(6):
            new_p[i].append(st_p[i])
            new_s[i].append(st_s[i])
    sp = [jnp.stack(z, axis=0) for z in new_p]
    ss = [jnp.stack(z, axis=0) for z in new_s]
    return (xp, xs, sp[0], sp[1], sp[2], sp[3], sp[4], sp[5], ss[0], ss[1], ss[2], ss[3], ss[4], ss[5])
```

```python
import functools

import jax
import jax.numpy as jnp
from jax import lax
from jax.experimental import pallas as pl
from jax.experimental.pallas import tpu as pltpu

F32 = jnp.float32
BF16 = jnp.bfloat16
HIGHEST = lax.Precision.HIGHEST

SUBLANES = 8
LANES = 128
MXU_TILE = 256
VMEM_LIMIT = 56 * 1024 * 1024

D_MODEL = 1024
HEAD = 64
HEADS = D_MODEL // HEAD
PAIRS = HEADS // 2
LORA = 64
SHIFT_W = 3 * D_MODEL + 2 * LORA
S5_GROUPS = 64
S5_STATES = 64
S5_STATE_W = S5_GROUPS * S5_STATES
LRU_BLOCK = 64
S5_SETS = 4
S5_IN_GROUPS = 8
S5_TILES = S5_STATE_W // LANES
S5_ROW_STRIDE = 40
CONV_W = 4
LRU_C = 8.0
RMS_EPS = 1e-6
GN_EPS = 64e-5
RWKV_CHUNK = 64
RWKV_TILE = 256
S5_TILE = 256
WIDE_TILE = 512
SHORT_ROWS = 512
DECAY_SCALE = 0.6065306597126334
GELU_C = 0.7978845608028654
LOG2_E = 1.4426950408889634
RWKV_INSTANCES = 32


def _sigmoid(x):
    return 0.5 * jnp.tanh(0.5 * x) + 0.5


def _silu(x):
    h = 0.5 * x
    return h * (jnp.tanh(h) + 1.0)


def _softplus(x):
    return jnp.maximum(x, 0.0) + jnp.log1p(jnp.exp(-jnp.abs(x)))


def _norm_mod(x, g, mod):
    ms = jnp.mean(x * x, axis=-1, keepdims=True)
    return (x * lax.rsqrt(ms + RMS_EPS)) * (g * (1.0 + mod[1:2, :])) + mod[0:1, :]


def _dot(a, b):
    return jnp.dot(a, b, preferred_element_type=F32)


def _dot_hi(a, b):
    return jnp.dot(a, b, preferred_element_type=F32, precision=HIGHEST)


def _split2(x):
    hi = x.astype(BF16)
    lo = (x - hi.astype(F32)).astype(BF16)
    return hi, lo


_NN = (((1,), (0,)), ((), ()))
_NT = (((1,), (1,)), ((), ()))
_TN = (((0,), (0,)), ((), ()))


def _mm(a, b, dims=_NN):
    return lax.dot_general(a.astype(BF16), b.astype(BF16), dims, preferred_element_type=F32)


def _head_sum(x, ones_blk):
    outs = []
    for s in range(D_MODEL // MXU_TILE):
        outs.append(_dot(x[:, MXU_TILE * s:MXU_TILE * (s + 1)].astype(BF16), ones_blk))
    return jnp.concatenate(outs, axis=1)


def _block_ones(n, blk):
    ri = lax.broadcasted_iota(jnp.int32, (n, n), 0)
    ci = lax.broadcasted_iota(jnp.int32, (n, n), 1)
    sh = blk.bit_length() - 1
    return ((ri >> sh) == (ci >> sh)).astype(F32).astype(BF16)


def _mod_kernel(c_ref, w_ref, b_ref, o_ref):
    s = _silu(c_ref[...])
    o_ref[...] = _dot_hi(s, w_ref[...]) + b_ref[...]


def _modulation(c_all, ada_w, ada_b):
    depth = ada_w.shape[0]
    rows = c_all.shape[0]
    return pl.pallas_call(
        _mod_kernel,
        grid=(depth, 3),
        in_specs=[
            pl.BlockSpec((rows, D_MODEL), lambda l, j: (0, 0)),
            pl.BlockSpec((None, D_MODEL, D_MODEL), lambda l, j: (l, 0, j)),
            pl.BlockSpec((None, 1, D_MODEL), lambda l, j: (l, 0, j)),
        ],
        out_specs=pl.BlockSpec((None, rows, D_MODEL), lambda l, j: (l, 0, j)),
        out_shape=jax.ShapeDtypeStruct((depth, rows, 3 * D_MODEL), F32),
        name="adaln_mod",
    )(c_all, ada_w, ada_b.reshape(depth, 1, 3 * D_MODEL))


def _s5_prep_kernel(are_ref, aim_ref, ls_ref, bre_ref, bim_ref, abr_ref, abi_ref, bbr_ref, bbi_ref):
    are = are_ref[...]
    aim = aim_ref[...]
    dt = jnp.exp(ls_ref[...])
    mag = jnp.exp(are * dt)
    abr = mag * jnp.cos(aim * dt)
    abi = mag * jnp.sin(aim * dt)
    abr_ref[...] = abr
    abi_ref[...] = abi
    nr = abr - 1.0
    ni = abi
    den = are * are + aim * aim
    cr = (nr * are + ni * aim) / den
    ci = (ni * are - nr * aim) / den
    bre = bre_ref[...]
    bim = bim_ref[...]
    bbr_ref[...] = cr[:, None, :] * bre - ci[:, None, :] * bim
    bbi_ref[...] = cr[:, None, :] * bim + ci[:, None, :] * bre


def _s5_prep(a_re, a_im, log_step, b_re, b_im):
    depth, g, p = a_re.shape
    i = b_re.shape[-1]
    layer = lambda *dims: pl.BlockSpec((None,) + dims, lambda l: (l,) + (0,) * len(dims))
    return pl.pallas_call(
        _s5_prep_kernel,
        grid=(depth,),
        in_specs=[layer(g, p), layer(g, p), layer(g, 1), layer(g, i, p), layer(g, i, p)],
        out_specs=(layer(g, p), layer(g, p), layer(g, i, p), layer(g, i, p)),
        out_shape=(
            jax.ShapeDtypeStruct((depth, g, p), F32),
            jax.ShapeDtypeStruct((depth, g, p), F32),
            jax.ShapeDtypeStruct((depth, g, i, p), F32),
            jax.ShapeDtypeStruct((depth, g, i, p), F32),
        ),
        name="s5_discretise",
    )(a_re, a_im, log_step.reshape(depth, g, 1), jnp.swapaxes(b_re, 2, 3), jnp.swapaxes(b_im, 2, 3))


def _rwkv_kernel(x_ref, mod_ref, g_ref, wr_ref, wg_ref, mu_ref, w0_ref, w2a_ref, a0_ref, kk_ref, ka_ref,
                 rk_ref, lng_ref, lnb_ref, shift0_ref, wkv0_ref,
                 yz_ref, shift_out_ref, wkv_out_ref,
                 carry_ref, st_ref, at_ref, rt_ref, bt_ref, kt_ref, v_ref, pl_ref, y_ref,
                 nh_ref, t_ref, aak_ref, ark_ref, rhs_ref, qg_ref, hc_ref, yc_ref, ones_ref,
                 *, tile, chunk, t_valid, group):
    t = pl.program_id(1)
    n_chunks = tile // chunk
    two = 2 * chunk
    n_inst = group * n_chunks
    chained = x_ref.shape[0] == 1

    def pair_state(s0):
        rh = lax.broadcasted_iota(jnp.int32, (LANES, LANES), 0) >> (HEAD.bit_length() - 1)
        ch = lax.broadcasted_iota(jnp.int32, (LANES, LANES), 1) >> (HEAD.bit_length() - 1)
        return jnp.where(rh == ch, jnp.concatenate([s0, s0], axis=1), 0.0).T

    def head_states(st):
        sv = st.T
        return sv[:, :HEAD] + sv[:, HEAD:]

    if chained:
        @pl.when(t == 0)
        def _():
            carry_ref[...] = shift0_ref[0]
            for q in range(PAIRS):
                st_ref[q] = pair_state(wkv0_ref[0, q])

    h = _pre_norm(x_ref, g_ref, mod_ref)
    p = _dot(h, wr_ref[...])
    zg = _silu(_dot(h, wg_ref[...]))

    rows = lax.broadcasted_iota(jnp.int32, (tile, 1), 0)
    rmod = rows & (chunk - 1)
    last = (t_valid - 1) % chunk
    shifted = pltpu.roll(p, 1, 0)
    top_row = lax.broadcasted_iota(jnp.int32, (SUBLANES, 1), 0) == 0
    if chained:
        starts = {0: carry_ref[...]}
        carry_ref[...] = p[tile - chunk + last:tile - chunk + last + 1, :]
    else:
        starts = {s * chunk: shift0_ref[s] for s in range(n_chunks)}
        for s in range(n_chunks):
            shift_out_ref[s] = p[s * chunk + last:s * chunk + last + 1, :]
    pieces, at = [], 0
    for r0 in sorted(starts):
        if r0 > at:
            pieces.append(shifted[at:r0, :])
        pieces.append(jnp.where(top_row, starts[r0], shifted[r0:r0 + SUBLANES, :]))
        at = r0 + SUBLANES
    pieces.append(shifted[at:, :])
    prev = jnp.concatenate(pieces, axis=0)
    pm = p + (prev - p) * mu_ref[...]

    r = pm[:, 0:D_MODEL]
    k = pm[:, D_MODEL:2 * D_MODEL]
    v = pm[:, 2 * D_MODEL:3 * D_MODEL]
    wa = pm[:, 3 * D_MODEL:SHIFT_W]
    lane = lax.broadcasted_iota(jnp.int32, (1, LANES), 1)
    wa = jnp.where(lane < LORA, jnp.tanh(wa), wa)
    lora = _dot(wa.astype(BF16), w2a_ref[...])
    logd = -(DECAY_SCALE * LOG2_E) * _sigmoid(w0_ref[...] + lora[:, :D_MODEL])
    a_sig = _sigmoid(a0_ref[...] + lora[:, D_MODEL:])

    @pl.when(t == 0)
    def _():
        ones_ref[...] = _block_ones(MXU_TILE, HEAD)

    ones_blk = ones_ref[...]
    kk = k * kk_ref[...]
    kk = kk * jnp.minimum(lax.rsqrt(_head_sum(kk * kk, ones_blk)), 1e12)
    k2 = k * (1.0 + (a_sig - 1.0) * ka_ref[...])
    bvec = kk * a_sig
    if t_valid % chunk != 0:
        ok = rmod < t_valid % chunk
        logd = jnp.where(ok, logd, 0.0)
        bvec = jnp.where(ok, bvec, 0.0)
        k2 = jnp.where(ok, k2, 0.0)

    cum = logd
    s = 1
    while s < chunk:
        cum = cum + jnp.where(rmod >= s, pltpu.roll(cum, s, 0), 0.0)
        s *= 2
    p_inc = jnp.exp2(cum)
    p_inv = jnp.exp2(-cum)
    p_exc = jnp.exp2(cum - logd)

    def to_pairs(ref, val):
        for q in range(PAIRS):
            ref[q] = val[:, LANES * q:LANES * (q + 1)]

    to_pairs(at_ref, -kk * p_exc)
    to_pairs(rt_ref, r * p_inc)
    to_pairs(bt_ref, bvec * p_inv)
    to_pairs(kt_ref, k2 * p_inv)
    to_pairs(v_ref, v)
    for j in range(n_chunks):
        row = jnp.exp2(cum[(j + 1) * chunk - 1:(j + 1) * chunk, :])
        for q in range(PAIRS):
            pl_ref[q, j] = jnp.broadcast_to(row[:, LANES * q:LANES * (q + 1)], (SUBLANES, LANES))

    m_a = (lane < HEAD).astype(F32)
    m_b = 1.0 - m_a
    ri = lax.broadcasted_iota(jnp.int32, (two, two), 0)
    ci = lax.broadcasted_iota(jnp.int32, (two, two), 1)
    csh = chunk.bit_length() - 1
    same_head = (ri >> csh) == (ci >> csh)
    strict = jnp.logical_and(same_head, (ri & (chunk - 1)) > (ci & (chunk - 1)))
    incl = jnp.logical_and(same_head, (ri & (chunk - 1)) >= (ci & (chunk - 1)))
    eye = ri == ci
    head_a = ri < chunk
    off_masks = []
    for lg in range(csh):
        off_masks.append(jnp.logical_and(strict, jnp.logical_and((ri >> (lg + 1)) == (ci >> (lg + 1)),
                                                                 (ri >> lg) != (ci >> lg))))

    def stack(x):
        return jnp.concatenate([x * m_a, x * m_b], axis=0)

    gsh = group.bit_length() - 1
    unroll = n_inst

    def group_body(g, carry):
        def where_is(i):
            q = g * group + (i & (group - 1))
            j = i >> gsh
            return q, j, pl.ds(pl.multiple_of(j * chunk, chunk), chunk)

        def gram(i, c):
            q, _, sl = where_is(i)
            atm = stack(at_ref[q, sl, :])
            rtm = stack(rt_ref[q, sl, :])
            bt = bt_ref[q, sl, :]
            kt = kt_ref[q, sl, :]
            o1 = _mm(jnp.concatenate([atm, rtm], axis=0), jnp.concatenate([bt, kt], axis=0), _NT)
            o1_sw = pltpu.roll(o1, chunk, 1)
            n_ab = jnp.where(strict, jnp.where(head_a, o1[:two], o1_sw[:two]), 0.0)
            nh_ref[i] = n_ab.astype(BF16)
            t_ref[i] = jnp.where(eye, 1.0, jnp.where(off_masks[0], n_ab, 0.0)).astype(BF16)
            aak_ref[i] = jnp.where(strict, jnp.where(head_a, o1_sw[:two], o1[:two]), 0.0).astype(BF16)
            ark_ref[i] = jnp.concatenate([jnp.where(incl, jnp.where(head_a, o1[two:], o1_sw[two:]), 0.0),
                                          jnp.where(incl, jnp.where(head_a, o1_sw[two:], o1[two:]), 0.0)],
                                         axis=1).astype(BF16)
            return c

        lax.fori_loop(0, n_inst, gram, 0, unroll=unroll)

        for off in off_masks[1:]:
            def level(i, c, off=off):
                t_inv = t_ref[i]
                x = _dot(jnp.where(off, nh_ref[i], 0.0), t_inv).astype(BF16)
                t_ref[i] = t_inv + _dot(t_inv, x).astype(BF16)
                return c

            lax.fori_loop(0, n_inst, level, 0, unroll=unroll)

        def right_side(i, c):
            q, _, sl = where_is(i)
            atm = stack(at_ref[q, sl, :])
            vm = stack(v_ref[q, sl, :])
            z0_hi, z0_lo = _split2(jnp.concatenate([atm, _mm(aak_ref[i], vm)], axis=1))
            rhs_ref[i] = jnp.concatenate([z0_hi, z0_lo], axis=0)
            return c

        lax.fori_loop(0, n_inst, right_side, 0, unroll=unroll)

        def apply(i, c):
            q, _, sl = where_is(i)
            vm = stack(v_ref[q, sl, :])
            t_inv = t_ref[i]
            z = _dot(t_inv, rhs_ref[i, :two, :]) + _dot(t_inv, rhs_ref[i, two:, :])
            rhs_ref[i] = jnp.concatenate([z, jnp.concatenate([jnp.zeros_like(vm), vm], axis=1)], axis=0).astype(BF16)
            return c

        lax.fori_loop(0, n_inst, apply, 0, unroll=unroll)

        def operators(i, c):
            q, j, sl = where_is(i)
            rhs2 = rhs_ref[i]
            k_all = g * n_inst + i
            p_last = pl_ref[q, j][0:1, :]
            top = _mm(jnp.concatenate([stack(bt_ref[q, sl, :] * p_last), stack(kt_ref[q, sl, :] * p_last)], axis=0),
                      rhs2, _TN)
            bot = _dot(ark_ref[i], rhs2)
            g_mat = top[:, :two] + jnp.where(eye, jnp.broadcast_to(p_last, (two, two)), 0.0)
            q_eff = bot[:, :two] + stack(rt_ref[q, sl, :])
            qg_ref[k_all] = jnp.concatenate([q_eff, g_mat], axis=0).astype(BF16)
            hc_ref[k_all] = top[:, two:]
            yc_ref[k_all] = bot[:, two:]
            return c

        lax.fori_loop(0, n_inst, operators, 0, unroll=unroll)
        return carry

    lax.fori_loop(0, PAIRS // group, group_body, 0)

    if chained:
        states = [st_ref[q] for q in range(PAIRS)]
    for j in range(n_chunks):
        for q in range(PAIRS):
            k_all = (q // group) * n_inst + j * group + q % group
            st = (states[q] if chained else pair_state(wkv0_ref[j, q])).astype(BF16)
            both = _dot(qg_ref[k_all], st)
            ym = both[:two, :] + yc_ref[k_all]
            y_ref[q, j * chunk:(j + 1) * chunk, :] = ym[:chunk, :] + ym[chunk:, :]
            st_new = both[two:, :] + hc_ref[k_all]
            if chained:
                states[q] = st_new
            else:
                wkv_out_ref[j, q] = head_states(st_new)
    if chained:
        for q in range(PAIRS):
            st_ref[q] = states[q]

    ys = jnp.concatenate([y_ref[q] for q in range(PAIRS)], axis=1)
    mean_blk = ones_blk * (1.0 / HEAD)
    mean = _head_sum(ys, mean_blk)
    yc = ys - mean
    var = _head_sum(yc * yc, mean_blk)
    yn = yc * lax.rsqrt(var + GN_EPS) * lng_ref[...] + lnb_ref[...]
    bonus = _head_sum(r * k2 * rk_ref[...], ones_blk) * v
    yz_ref[...] = ((yn + bonus) * zg).reshape(yz_ref.shape)

    if chained:
        @pl.when(t == pl.num_programs(1) - 1)
        def _():
            shift_out_ref[0] = carry_ref[...]
            for q in range(PAIRS):
                wkv_out_ref[0, q] = head_states(st_ref[q])


def _specs(layer, state_layer, mod_row, seqs=None):
    assert seqs is None or mod_row % seqs == 0
    per_step = 1 if seqs is None else seqs

    def weight(a):
        return pl.BlockSpec((None,) + a.shape[1:], lambda b, t: (layer,) + (0,) * (a.ndim - 1))

    def state(a):
        return pl.BlockSpec((None, seqs) + a.shape[2:], lambda b, t: (state_layer, b) + (0,) * (a.ndim - 2))

    def mod(a):
        return pl.BlockSpec((None, seqs) + a.shape[2:], lambda b, t: (layer, mod_row // per_step + b, 0, 0))

    return weight, state, mod


def _pre_norm(x_ref, g_ref, mod_ref):
    g = g_ref[...]
    rows = [_norm_mod(x_ref[s], g, mod_ref[s]) for s in range(x_ref.shape[0])]
    return jnp.concatenate(rows, axis=0).astype(BF16)


def _rwkv_mixer(x, mod, g_pre, wr, wg, vecs, w2a, shift0, wkv0, *, sel, tile, t_valid, seqs):
    bsz, tp, _ = x.shape
    chunk = RWKV_CHUNK
    assert tp % tile == 0 and tile % chunk == 0 and bsz % seqs == 0
    assert t_valid == tp or tp == tile
    assert seqs == 1 or tp == chunk
    nt = tp // tile
    block_rows = tile
    tile = seqs * tile
    mu, w0, a0, k_k, k_a, r_k, ln_g, ln_b = vecs
    weight, state, mod_spec = _specs(*sel, seqs=seqs)
    pair_buf = pltpu.VMEM((PAIRS, tile, LANES), F32)
    group = min(PAIRS, max(1, RWKV_INSTANCES // (tile // chunk)))
    n_inst = group * (tile // chunk)
    two = 2 * chunk
    inst = lambda cols, dtype: pltpu.VMEM((n_inst, two, cols), dtype)
    every = lambda dtype: pltpu.VMEM((PAIRS * (tile // chunk), two, two), dtype)
    kern = functools.partial(_rwkv_kernel, tile=tile, chunk=chunk, t_valid=t_valid, group=group)
    return pl.pallas_call(
        kern,
        grid=(bsz // seqs, nt),
        in_specs=[
            pl.BlockSpec((seqs, block_rows, D_MODEL), lambda b, t: (b, t, 0)),
            mod_spec(mod),
            weight(g_pre), weight(wr), weight(wg), weight(mu), weight(w0), weight(w2a), weight(a0),
            weight(k_k), weight(k_a), weight(r_k), weight(ln_g), weight(ln_b),
            state(shift0), state(wkv0),
        ],
        out_specs=[
            pl.BlockSpec((seqs, block_rows, D_MODEL), lambda b, t: (b, t, 0)),
            pl.BlockSpec((seqs, 1, SHIFT_W), lambda b, t: (b, 0, 0)),
            pl.BlockSpec((seqs, PAIRS, LANES, HEAD), lambda b, t: (b, 0, 0, 0)),
        ],
        out_shape=[
            jax.ShapeDtypeStruct((bsz, tp, D_MODEL), F32),
            jax.ShapeDtypeStruct((bsz, 1, SHIFT_W), F32),
            jax.ShapeDtypeStruct((bsz, PAIRS, LANES, HEAD), F32),
        ],
        scratch_shapes=[
            pltpu.VMEM((1, SHIFT_W), F32),
            pltpu.VMEM((PAIRS, LANES, LANES), F32),
            pair_buf, pair_buf, pair_buf, pair_buf, pair_buf,
            pltpu.VMEM((PAIRS, tile // chunk, SUBLANES, LANES), F32),
            pair_buf,
            inst(two, BF16), inst(two, BF16), inst(two, BF16), inst(2 * two, BF16),
            pltpu.VMEM((n_inst, 2 * two, 2 * two), BF16),
            pltpu.VMEM((PAIRS * (tile // chunk), 2 * two, two), BF16), every(F32), every(F32),
            pltpu.VMEM((MXU_TILE, MXU_TILE), BF16),
        ],
        compiler_params=pltpu.CompilerParams(
            dimension_semantics=("arbitrary", "arbitrary"), vmem_limit_bytes=VMEM_LIMIT),
        name="rwkv7_mixer",
    )(x, mod, g_pre, wr, wg, mu, w0, w2a, a0, k_k, k_a, r_k, ln_g, ln_b, shift0, wkv0)


def _s5_kernel(x_ref, mod_ref, g_ref, wu_ref, wg_ref, bbr_ref, bbi_ref, cre_ref, cmi_ref, abr_ref, abi_ref,
               d_ref, gluw_ref, glub_ref, s0r_ref, s0i_ref,
               yz_ref, sr_out_ref, si_out_ref,
               hr_ref, hi_ref, cr_ref, ci_ref, *, tile):
    t = pl.program_id(1)

    @pl.when(t == 0)
    def _():
        cr_ref[...] = s0r_ref[...]
        ci_ref[...] = s0i_ref[...]

    seqs = x_ref.shape[0]
    frames = seqs * tile
    h = _pre_norm(x_ref, g_ref, mod_ref)
    u = _dot(h, wu_ref[...])
    zg = _silu(_dot(h, wg_ref[...]))
    set_tiles = S5_TILES // S5_SETS

    def frame_rows(c):
        return pl.ds(c, frames, stride=S5_ROW_STRIDE)

    in_tiles = bbr_ref.shape[2] // LANES
    for s in range(D_MODEL // LANES):
        ub = u[:, LANES * s:LANES * (s + 1)].astype(BF16)
        bur = _dot(ub, bbr_ref[s])
        bui = _dot(ub, bbi_ref[s])
        for c in range(in_tiles):
            hr_ref[frame_rows(s * in_tiles + c), :] = bur[:, LANES * c:LANES * (c + 1)]
            hi_ref[frame_rows(s * in_tiles + c), :] = bui[:, LANES * c:LANES * (c + 1)]

    abr = abr_ref[...]
    abi = abi_ref[...]

    def frame(f, carry):
        sr, si = carry
        rows = pl.ds(pl.multiple_of(f * S5_ROW_STRIDE, SUBLANES), S5_TILES)
        nr = abr * sr - abi * si + hr_ref[rows, :]
        ni = abr * si + abi * sr + hi_ref[rows, :]
        hr_ref[rows, :] = nr
        hi_ref[rows, :] = ni
        return nr, ni

    for s in range(seqs):
        cr_ref[s], ci_ref[s] = lax.fori_loop(s * tile, (s + 1) * tile, frame, (cr_ref[s], ci_ref[s]), unroll=True)

    outs = []
    for s in range(S5_SETS):
        hr = jnp.concatenate([hr_ref[frame_rows(s * set_tiles + c), :] for c in range(set_tiles)], axis=1)
        hi = jnp.concatenate([hi_ref[frame_rows(s * set_tiles + c), :] for c in range(set_tiles)], axis=1)
        outs.append(_dot(hr.astype(BF16), cre_ref[s]) + _dot(hi.astype(BF16), cmi_ref[s]))
    y = jnp.concatenate(outs, axis=1) + d_ref[...] * u
    y = (0.5 * y) * (1.0 + jnp.tanh(y * (GELU_C + (GELU_C * 0.044715) * (y * y))))
    y = y * _sigmoid(_dot(y.astype(BF16), gluw_ref[...]) + glub_ref[...])
    yz_ref[...] = (y * zg).reshape(seqs, tile, D_MODEL)

    @pl.when(t == pl.num_programs(1) - 1)
    def _():
        sr_out_ref[...] = cr_ref[...]
        si_out_ref[...] = ci_ref[...]


def _s5_mixer(x, mod, g_pre, wu, wg, bbr, bbi, cre, cmi, abr, abi, d_skip, glu_w, glu_b, s0r, s0i, *, sel, tile,
              seqs):
    bsz, tp, _ = x.shape
    assert tp % tile == 0 and tile % SUBLANES == 0 and bsz % seqs == 0
    nt = tp // tile
    weight, state, mod_spec = _specs(*sel, seqs=seqs)
    state_out = pl.BlockSpec((seqs, S5_TILES, LANES), lambda b, t: (b, 0, 0))
    kern = functools.partial(_s5_kernel, tile=tile)
    return pl.pallas_call(
        kern,
        grid=(bsz // seqs, nt),
        in_specs=[
            pl.BlockSpec((seqs, tile, D_MODEL), lambda b, t: (b, t, 0)),
            mod_spec(mod),
            weight(g_pre), weight(wu), weight(wg), weight(bbr), weight(bbi), weight(cre), weight(cmi),
            weight(abr), weight(abi), weight(d_skip), weight(glu_w), weight(glu_b), state(s0r), state(s0i),
        ],
        out_specs=[pl.BlockSpec((seqs, tile, D_MODEL), lambda b, t: (b, t, 0)), state_out, state_out],
        out_shape=[
            jax.ShapeDtypeStruct((bsz, tp, D_MODEL), F32),
            jax.ShapeDtypeStruct((bsz, S5_TILES, LANES), F32),
            jax.ShapeDtypeStruct((bsz, S5_TILES, LANES), F32),
        ],
        scratch_shapes=[
            pltpu.VMEM((seqs * tile * S5_ROW_STRIDE, LANES), F32),
            pltpu.VMEM((seqs * tile * S5_ROW_STRIDE, LANES), F32),
            pltpu.VMEM((seqs, S5_TILES, LANES), F32),
            pltpu.VMEM((seqs, S5_TILES, LANES), F32),
        ],
        compiler_params=pltpu.CompilerParams(
            dimension_semantics=("arbitrary", "arbitrary"), vmem_limit_bytes=VMEM_LIMIT),
        name="s5_mixer",
    )(x, mod, g_pre, wu, wg, bbr, bbi, cre, cmi, abr, abi, d_skip, glu_w, glu_b, s0r, s0i)


def _lru_kernel(x_ref, mod_ref, g_ref, wx_ref, wg_ref, cw_ref, cb_ref, wa4_ref, ba_ref, wx4_ref, bx_ref, lam_ref,
                conv0_ref, h0_ref,
                yz_ref, conv_out_ref, h_out_ref,
                xbuf_ref, a_ref, b_ref, hc_ref, *, tile):
    t = pl.program_id(1)
    pad = SUBLANES
    hist = CONV_W - 1
    seqs = x_ref.shape[0]

    @pl.when(t == 0)
    def _():
        xbuf_ref[:, pad - hist:pad, :] = conv0_ref[...]
        hc_ref[...] = h0_ref[...]

    h = _pre_norm(x_ref, g_ref, mod_ref)
    xl = _dot(h, wx_ref[...])
    zg = _silu(_dot(h, wg_ref[...]))
    xcs, tails = [], []
    for s in range(seqs):
        xs = xl[s * tile:(s + 1) * tile, :]
        xbuf_ref[s, pad:pad + tile, :] = xs
        acc = cb_ref[...] + xs * cw_ref[CONV_W - 1:CONV_W, :]
        for j in range(hist):
            acc = acc + xbuf_ref[s, pad - hist + j:pad - hist + j + tile, :] * cw_ref[j:j + 1, :]
        tails.append(xbuf_ref[s, pad + tile - hist:pad + tile, :])
        xbuf_ref[s, pad - hist:pad, :] = tails[s]
        xcs.append(acc)
    xc = jnp.concatenate(xcs, axis=0)

    ga, gx = [], []
    blk = MXU_TILE
    for s in range(D_MODEL // blk):
        xb = xc[:, blk * s:blk * (s + 1)].astype(BF16)
        ga.append(_dot(xb, wa4_ref[s]))
        gx.append(_dot(xb, wx4_ref[s]))
    gate_r = _sigmoid(jnp.concatenate(ga, axis=1) + ba_ref[...])
    gate_i = _sigmoid(jnp.concatenate(gx, axis=1) + bx_ref[...])
    log_a = -LRU_C * gate_r * _softplus(-lam_ref[...])
    a = jnp.exp(log_a)
    a_ref[...] = a
    var = -jnp.tanh(log_a) * (a * a + 1.0)
    b_ref[...] = jnp.where(var > 0.0, var * lax.rsqrt(var), 0.0) * (gate_i * xc)

    row8 = lax.broadcasted_iota(jnp.int32, (SUBLANES, 1), 0)

    def row_body(g, carry):
        rs = pl.ds(pl.multiple_of(g * SUBLANES, SUBLANES), SUBLANES)
        av = a_ref[rs, :]
        bv = b_ref[rs, :]
        for sh in (1, 2, 4):
            keep = row8 >= sh
            ash = jnp.where(keep, pltpu.roll(av, sh, 0), 1.0)
            bsh = jnp.where(keep, pltpu.roll(bv, sh, 0), 0.0)
            bv = bv + av * bsh
            av = av * ash
        hs = bv + av * carry
        b_ref[rs, :] = hs
        return hs[SUBLANES - 1:SUBLANES, :]

    groups = tile // SUBLANES
    for s in range(seqs):
        hc_ref[s] = lax.fori_loop(s * groups, (s + 1) * groups, row_body, hc_ref[s], unroll=True)
    yz_ref[...] = (b_ref[...] * zg).reshape(seqs, tile, D_MODEL)

    @pl.when(t == pl.num_programs(1) - 1)
    def _():
        for s in range(seqs):
            conv_out_ref[s] = tails[s]
        h_out_ref[...] = hc_ref[...]


def _lru_mixer(x, mod, g_pre, wx, wg, conv_w, conv_b, wa4, ba, wx4, bx, lam, conv0, h0, *, sel, tile, seqs):
    bsz, tp, _ = x.shape
    assert tp % tile == 0 and tile % SUBLANES == 0 and tile >= SUBLANES and bsz % seqs == 0
    nt = tp // tile
    weight, state, mod_spec = _specs(*sel, seqs=seqs)
    kern = functools.partial(_lru_kernel, tile=tile)
    return pl.pallas_call(
        kern,
        grid=(bsz // seqs, nt),
        in_specs=[
            pl.BlockSpec((seqs, tile, D_MODEL), lambda b, t: (b, t, 0)),
            mod_spec(mod),
            weight(g_pre), weight(wx), weight(wg), weight(conv_w), weight(conv_b), weight(wa4), weight(ba),
            weight(wx4), weight(bx), weight(lam), state(conv0), state(h0),
        ],
        out_specs=[
            pl.BlockSpec((seqs, tile, D_MODEL), lambda b, t: (b, t, 0)),
            pl.BlockSpec((seqs, CONV_W - 1, D_MODEL), lambda b, t: (b, 0, 0)),
            pl.BlockSpec((seqs, 1, D_MODEL), lambda b, t: (b, 0, 0)),
        ],
        out_shape=[
            jax.ShapeDtypeStruct((bsz, tp, D_MODEL), F32),
            jax.ShapeDtypeStruct((bsz, CONV_W - 1, D_MODEL), F32),
            jax.ShapeDtypeStruct((bsz, 1, D_MODEL), F32),
        ],
        scratch_shapes=[
            pltpu.VMEM((seqs, tile + SUBLANES, D_MODEL), F32),
            pltpu.VMEM((seqs * tile, D_MODEL), F32),
            pltpu.VMEM((seqs * tile, D_MODEL), F32),
            pltpu.VMEM((seqs, 1, D_MODEL), F32),
        ],
        compiler_params=pltpu.CompilerParams(
            dimension_semantics=("arbitrary", "arbitrary"), vmem_limit_bytes=VMEM_LIMIT),
        name="rglru_mixer",
    )(x, mod, g_pre, wx, wg, conv_w, conv_b, wa4, ba, wx4, bx, lam, conv0, h0)


def _merge_kernel(x_ref, mod_ref, g_ref, gpost_ref, wm_ref, wo_ref, yr_ref, ys_ref, yl_ref, o_ref):
    seqs, tile, _ = x_ref.shape
    rows = lambda ref: ref[...].reshape(seqs * tile, D_MODEL)
    m = _sigmoid(_dot(_pre_norm(x_ref, g_ref, mod_ref), wm_ref[...]))
    merged = (m[:, :D_MODEL] * rows(yr_ref) + m[:, D_MODEL:2 * D_MODEL] * rows(ys_ref)
              + m[:, 2 * D_MODEL:] * rows(yl_ref))
    o = _dot(merged.astype(BF16), wo_ref[...])
    ms = jnp.mean(o * o, axis=-1, keepdims=True)
    o = (o * lax.rsqrt(ms + RMS_EPS)) * gpost_ref[...]
    for s in range(seqs):
        o_ref[s] = x_ref[s] + mod_ref[s][2:3, :] * o[s * tile:(s + 1) * tile, :]


def _merge_out(x, mod, g_pre, g_post, wm, wo, yr, ys, yl, *, sel, tile, seqs):
    bsz, tp, _ = x.shape
    assert tp % tile == 0 and bsz % seqs == 0
    nt = tp // tile
    act = pl.BlockSpec((seqs, tile, D_MODEL), lambda b, t: (b, t, 0))
    weight, _, mod_spec = _specs(*sel, seqs=seqs)
    return pl.pallas_call(
        _merge_kernel,
        grid=(bsz // seqs, nt),
        in_specs=[act, mod_spec(mod), weight(g_pre), weight(g_post), weight(wm), weight(wo), act, act, act],
        out_specs=act,
        out_shape=jax.ShapeDtypeStruct((bsz, tp, D_MODEL), F32),
        compiler_params=pltpu.CompilerParams(
            dimension_semantics=("arbitrary", "arbitrary"), vmem_limit_bytes=VMEM_LIMIT),
        name="merge_out",
    )(x, mod, g_pre, g_post, wm, wo, yr, ys, yl)


def _block_diag_expand(w, per):
    depth, n, a, b = w.shape
    w = w.reshape(depth, n // per, per, a, b)
    eye = jnp.eye(per, dtype=w.dtype)
    return jnp.einsum("lsgab,gh->lsgahb", w, eye).reshape(depth, n // per, per * a, per * b)


def _layer(x, mod, state, prm, *, sel, t_valid, tiles):
    shift_row, wkv, s_re, s_im, lru_h, lru_conv = state
    bsz = x.shape[0]
    g_pre = prm["g_pre"]

    xr = x
    if x.shape[1] % tiles["rwkv"] != 0:
        xr = jnp.pad(x, ((0, 0), (0, tiles["rwkv"] - x.shape[1] % tiles["rwkv"]), (0, 0)))
    yz_r, shift_new, wkv_new = _rwkv_mixer(
        xr, mod, g_pre, prm["w_rwkv"], prm["w_rwkv_g"], prm["rwkv_vecs"], prm["w2a"], shift_row, wkv,
        sel=sel, tile=tiles["rwkv"], t_valid=t_valid, seqs=tiles["rwkv_seqs"])
    yz_r = yz_r[:, :t_valid]
    yz_s, s_re_new, s_im_new = _s5_mixer(
        x, mod, g_pre, prm["w_s5"], prm["w_s5_g"], prm["bbr"], prm["bbi"], prm["cre"], prm["cmi"],
        prm["abr"], prm["abi"], prm["s5_d"], prm["glu_w"], prm["glu_b"], s_re, s_im, sel=sel, tile=tiles["s5"],
        seqs=tiles["seqs"])
    yz_l, conv_new, h_new = _lru_mixer(
        x, mod, g_pre, prm["w_lru"], prm["w_lru_g"], prm["conv_w"], prm["conv_b"], prm["wa4"], prm["ba"],
        prm["wx4"], prm["bx"], prm["lam"], lru_conv, lru_h, sel=sel, tile=tiles["lru"], seqs=tiles["seqs"])
    x_new = _merge_out(x, mod, g_pre, prm["g_post"], prm["w_merge"], prm["w_out"], yz_r, yz_s, yz_l,
                       sel=sel, tile=tiles["merge"], seqs=tiles["seqs"])
    new_state = (shift_new.reshape(bsz, SHIFT_W), wkv_new.reshape(bsz, HEADS, HEAD, HEAD),
                 s_re_new.reshape(bsz, S5_GROUPS, S5_STATES), s_im_new.reshape(bsz, S5_GROUPS, S5_STATES),
                 h_new.reshape(bsz, D_MODEL), conv_new)
    return x_new, new_state


def _stacked_params(w_in, w_out, norm_pre, norm_post, rwkv, s5, lru):
    depth = w_in.shape[0]
    d = D_MODEL
    wb = w_in.astype(BF16)
    o = SHIFT_W
    row = lambda a: a.reshape(depth, 1, -1)
    mu, w0, w2, a0, a2, k_k, k_a, r_k, ln_g, ln_b = rwkv
    a_re, a_im, log_step, b_re, b_im, c_re, c_im, s5_d, glu_w, glu_b = s5
    conv_w, conv_b, wa, ba, wx, bx, lam = lru
    zeros = jnp.zeros((depth, LORA, d), F32)
    w2a = jnp.concatenate([jnp.concatenate([w2, zeros], axis=2), jnp.concatenate([zeros, a2], axis=2)], axis=1)
    abr, abi, bbr, bbi = _s5_prep(a_re, a_im, log_step, b_re, b_im)
    per = S5_GROUPS // S5_SETS
    gate_per = MXU_TILE // LRU_BLOCK
    return dict(
        g_pre=row(norm_pre), g_post=row(norm_post),
        w_rwkv=wb[:, :, :o], w_rwkv_g=wb[:, :, o:o + d],
        w_s5=wb[:, :, o + d:o + 2 * d], w_s5_g=wb[:, :, o + 2 * d:o + 3 * d],
        w_lru=wb[:, :, o + 3 * d:o + 4 * d], w_lru_g=wb[:, :, o + 4 * d:o + 5 * d],
        w_merge=wb[:, :, o + 5 * d:], w_out=w_out.astype(BF16),
        rwkv_vecs=(row(mu), row(w0), row(a0), row(k_k), row(k_a), row(r_k), row(ln_g), row(ln_b)),
        w2a=w2a.astype(BF16),
        bbr=_block_diag_expand(bbr, S5_IN_GROUPS).astype(BF16),
        bbi=_block_diag_expand(bbi, S5_IN_GROUPS).astype(BF16),
        cre=_block_diag_expand(jnp.swapaxes(c_re, 2, 3), per).astype(BF16),
        cmi=_block_diag_expand(-jnp.swapaxes(c_im, 2, 3), per).astype(BF16),
        abr=abr.reshape(depth, S5_TILES, LANES), abi=abi.reshape(depth, S5_TILES, LANES),
        s5_d=row(s5_d), glu_w=glu_w.astype(BF16), glu_b=row(glu_b),
        conv_w=conv_w, conv_b=row(conv_b),
        wa4=_block_diag_expand(wa, gate_per).astype(BF16), ba=row(ba),
        wx4=_block_diag_expand(wx, gate_per).astype(BF16), bx=row(bx), lam=row(lam),
    )


def _state_layout(shift, wkv, s_re, s_im, lru_h, lru_conv):
    n, bsz = shift.shape[:2]
    return (shift.reshape(n, bsz, 1, SHIFT_W), wkv.reshape(n, bsz, PAIRS, LANES, HEAD),
            s_re.reshape(n, bsz, S5_TILES, LANES), s_im.reshape(n, bsz, S5_TILES, LANES),
            lru_h.reshape(n, bsz, 1, D_MODEL), lru_conv)


def _tiles(t, bsz):
    pick = lambda want: want if t % want == 0 else t
    share = lambda rows, limit: max(n for n in range(1, bsz + 1) if bsz % n == 0 and n * rows <= max(limit, rows))
    seqs = share(t, SHORT_ROWS) if t < SHORT_ROWS else 1
    rwkv = RWKV_TILE if t % RWKV_TILE == 0 else RWKV_CHUNK
    rwkv_seqs = share(RWKV_CHUNK, RWKV_TILE) if t <= RWKV_CHUNK else 1
    return dict(rwkv=rwkv, rwkv_seqs=rwkv_seqs, s5=pick(S5_TILE), lru=pick(WIDE_TILE), merge=pick(WIDE_TILE),
                seqs=seqs)


def kernel(x_prompt, x_sample, state_rwkv_shift, state_rwkv_wkv, state_s5_re, state_s5_im, state_lru_h,
           state_lru_conv, c_prompt, c_sample, ada_w, ada_b, norm_pre, norm_post, w_in, w_out, rwkv_mu, rwkv_w0,
           rwkv_w2, rwkv_a0, rwkv_a2, rwkv_k_k, rwkv_k_a, rwkv_r_k, rwkv_ln_g, rwkv_ln_b, s5_a_re, s5_a_im,
           s5_log_step, s5_b_re, s5_b_im, s5_c_re, s5_c_im, s5_d, s5_glu_w, s5_glu_b, lru_conv_w, lru_conv_b,
           lru_wa, lru_ba, lru_wx, lru_bx, lru_lambda):
    depth = w_in.shape[0]
    bp, tp, _ = x_prompt.shape
    bs, ts, _ = x_sample.shape
    c_all = jnp.concatenate([c_sample, c_prompt], axis=0)
    pad = (-c_all.shape[0]) % SUBLANES
    c_all = jnp.pad(c_all, ((0, pad), (0, 0)))
    mod_all = _modulation(c_all, ada_w, ada_b)
    mod_all = mod_all.reshape(depth, c_all.shape[0], 3, D_MODEL)
    rwkv = (rwkv_mu, rwkv_w0, rwkv_w2, rwkv_a0, rwkv_a2, rwkv_k_k, rwkv_k_a,
            rwkv_r_k.reshape(depth, D_MODEL), rwkv_ln_g, rwkv_ln_b)
    s5 = (s5_a_re, s5_a_im, s5_log_step, s5_b_re, s5_b_im, s5_c_re, s5_c_im, s5_d, s5_glu_w, s5_glu_b)
    lru = (lru_conv_w, lru_conv_b, lru_wa, lru_ba, lru_wx, lru_bx, lru_lambda)
    prm = _stacked_params(w_in, w_out, norm_pre, norm_post, rwkv, s5, lru)
    zero = _state_layout(jnp.zeros((1, bp, SHIFT_W), F32), jnp.zeros((1, bp, HEADS, HEAD, HEAD), F32),
                         jnp.zeros((1, bp, S5_GROUPS, S5_STATES), F32),
                         jnp.zeros((1, bp, S5_GROUPS, S5_STATES), F32),
                         jnp.zeros((1, bp, D_MODEL), F32), jnp.zeros((1, bp, CONV_W - 1, D_MODEL), F32))
    st_in = _state_layout(state_rwkv_shift, state_rwkv_wkv, state_s5_re, state_s5_im, state_lru_h,
                          state_lru_conv)

    xp, xs = x_prompt, x_sample
    new_p = [[] for _ in range(6)]
    new_s = [[] for _ in range(6)]
    for l in range(depth):
        xp, st_p = _layer(xp, mod_all, zero, prm, sel=(l, 0, bs), t_valid=tp, tiles=_tiles(tp, bp))
        xs, st_s = _layer(xs, mod_all, st_in, prm, sel=(l, l, 0), t_valid=ts, tiles=_tiles(ts, bs))
        for i in range(6):
            new_p[i].append(st_p[i])
            new_s[i].append(st_s[i])
    sp = [jnp.stack(z, axis=0) for z in new_p]
    ss = [jnp.stack(z, axis=0) for z in new_s]
    return (xp, xs, sp[0], sp[1], sp[2], sp[3], sp[4], sp[5], ss[0], ss[1], ss[2], ss[3], ss[4], ss[5])
```

```python
import functools

import jax
import jax.numpy as jnp
from jax import lax
from jax.experimental import pallas as pl
from jax.experimental.pallas import tpu as pltpu

F32 = jnp.float32
BF16 = jnp.bfloat16
HIGHEST = lax.Precision.HIGHEST

SUBLANES = 8
LANES = 128
MXU_TILE = 256
VMEM_LIMIT = 56 * 1024 * 1024

D_MODEL = 1024
HEAD = 64
HEADS = D_MODEL // HEAD
PAIRS = HEADS // 2
LORA = 64
SHIFT_W = 3 * D_MODEL + 2 * LORA
S5_GROUPS = 64
S5_STATES = 64
S5_STATE_W = S5_GROUPS * S5_STATES
LRU_BLOCK = 64
S5_SETS = 4
S5_IN_GROUPS = 8
S5_TILES = S5_STATE_W // LANES
S5_ROW_STRIDE = 40
CONV_W = 4
LRU_C = 8.0
RMS_EPS = 1e-6
GN_EPS = 64e-5
RWKV_CHUNK = 64
RWKV_TILE = 256
S5_TILE = 256
WIDE_TILE = 512
SHORT_ROWS = 512
DECAY_SCALE = 0.6065306597126334
GELU_C = 0.7978845608028654
LOG2_E = 1.4426950408889634
RWKV_INSTANCES = 32


def _sigmoid(x):
    return 0.5 * jnp.tanh(0.5 * x) + 0.5


def _silu(x):
    h = 0.5 * x
    return h * (jnp.tanh(h) + 1.0)


def _softplus(x):
    return jnp.maximum(x, 0.0) + jnp.log1p(jnp.exp(-jnp.abs(x)))


def _norm_mod(x, g, mod):
    ms = jnp.mean(x * x, axis=-1, keepdims=True)
    return (x * lax.rsqrt(ms + RMS_EPS)) * (g * (1.0 + mod[1:2, :])) + mod[0:1, :]


def _dot(a, b):
    return jnp.dot(a, b, preferred_element_type=F32)


def _dot_hi(a, b):
    return jnp.dot(a, b, preferred_element_type=F32, precision=HIGHEST)


def _split2(x):
    hi = x.astype(BF16)
    lo = (x - hi.astype(F32)).astype(BF16)
    return hi, lo


_NN = (((1,), (0,)), ((), ()))
_NT = (((1,), (1,)), ((), ()))
_TN = (((0,), (0,)), ((), ()))


def _mm(a, b, dims=_NN):
    return lax.dot_general(a.astype(BF16), b.astype(BF16), dims, preferred_element_type=F32)


def _head_sum(x, ones_blk):
    outs = []
    for s in range(D_MODEL // MXU_TILE):
        outs.append(_dot(x[:, MXU_TILE * s:MXU_TILE * (s + 1)].astype(BF16), ones_blk))
    return jnp.concatenate(outs, axis=1)


def _block_ones(n, blk):
    ri = lax.broadcasted_iota(jnp.int32, (n, n), 0)
    ci = lax.broadcasted_iota(jnp.int32, (n, n), 1)
    sh = blk.bit_length() - 1
    return ((ri >> sh) == (ci >> sh)).astype(F32).astype(BF16)


def _mod_kernel(c_ref, w_ref, b_ref, o_ref):
    s = _silu(c_ref[...])
    o_ref[...] = _dot_hi(s, w_ref[...]) + b_ref[...]


def _modulation(c_all, ada_w, ada_b):
    depth = ada_w.shape[0]
    rows = c_all.shape[0]
    return pl.pallas_call(
        _mod_kernel,
        grid=(depth, 3),
        in_specs=[
            pl.BlockSpec((rows, D_MODEL), lambda l, j: (0, 0)),
            pl.BlockSpec((None, D_MODEL, D_MODEL), lambda l, j: (l, 0, j)),
            pl.BlockSpec((None, 1, D_MODEL), lambda l, j: (l, 0, j)),
        ],
        out_specs=pl.BlockSpec((None, rows, D_MODEL), lambda l, j: (l, 0, j)),
        out_shape=jax.ShapeDtypeStruct((depth, rows, 3 * D_MODEL), F32),
        name="adaln_mod",
    )(c_all, ada_w, ada_b.reshape(depth, 1, 3 * D_MODEL))


def _s5_prep_kernel(are_ref, aim_ref, ls_ref, bre_ref, bim_ref, abr_ref, abi_ref, bbr_ref, bbi_ref):
    are = are_ref[...]
    aim = aim_ref[...]
    dt = jnp.exp(ls_ref[...])
    mag = jnp.exp(are * dt)
    abr = mag * jnp.cos(aim * dt)
    abi = mag * jnp.sin(aim * dt)
    abr_ref[...] = abr
    abi_ref[...] = abi
    nr = abr - 1.0
    ni = abi
    den = are * are + aim * aim
    cr = (nr * are + ni * aim) / den
    ci = (ni * are - nr * aim) / den
    bre = bre_ref[...]
    bim = bim_ref[...]
    bbr_ref[...] = cr[:, None, :] * bre - ci[:, None, :] * bim
    bbi_ref[...] = cr[:, None, :] * bim + ci[:, None, :] * bre


def _s5_prep(a_re, a_im, log_step, b_re, b_im):
    depth, g, p = a_re.shape
    i = b_re.shape[-1]
    layer = lambda *dims: pl.BlockSpec((None,) + dims, lambda l: (l,) + (0,) * len(dims))
    return pl.pallas_call(
        _s5_prep_kernel,
        grid=(depth,),
        in_specs=[layer(g, p), layer(g, p), layer(g, 1), layer(g, i, p), layer(g, i, p)],
        out_specs=(layer(g, p), layer(g, p), layer(g, i, p), layer(g, i, p)),
        out_shape=(
            jax.ShapeDtypeStruct((depth, g, p), F32),
            jax.ShapeDtypeStruct((depth, g, p), F32),
            jax.ShapeDtypeStruct((depth, g, i, p), F32),
            jax.ShapeDtypeStruct((depth, g, i, p), F32),
        ),
        name="s5_discretise",
    )(a_re, a_im, log_step.reshape(depth, g, 1), jnp.swapaxes(b_re, 2, 3), jnp.swapaxes(b_im, 2, 3))


def _rwkv_kernel(x_ref, mod_ref, g_ref, wr_ref, wg_ref, mu_ref, w0_ref, w2a_ref, a0_ref, kk_ref, ka_ref,
                 rk_ref, lng_ref, lnb_ref, shift0_ref, wkv0_ref,
                 yz_ref, shift_out_ref, wkv_out_ref,
                 carry_ref, st_ref, at_ref, rt_ref, bt_ref, kt_ref, v_ref, pl_ref, y_ref,
                 nh_ref, t_ref, aak_ref, ark_ref, rhs_ref, qg_ref, hc_ref, yc_ref, ones_ref,
                 *, tile, chunk, t_valid, group):
    t = pl.program_id(1)
    n_chunks = tile // chunk
    two = 2 * chunk
    n_inst = group * n_chunks
    chained = x_ref.shape[0] == 1

    def pair_state(s0):
        rh = lax.broadcasted_iota(jnp.int32, (LANES, LANES), 0) >> (HEAD.bit_length() - 1)
        ch = lax.broadcasted_iota(jnp.int32, (LANES, LANES), 1) >> (HEAD.bit_length() - 1)
        return jnp.where(rh == ch, jnp.concatenate([s0, s0], axis=1), 0.0).T

    def head_states(st):
        sv = st.T
        return sv[:, :HEAD] + sv[:, HEAD:]

    if chained:
        @pl.when(t == 0)
        def _():
            carry_ref[...] = shift0_ref[0]
            for q in range(PAIRS):
                st_ref[q] = pair_state(wkv0_ref[0, q])

    h = _pre_norm(x_ref, g_ref, mod_ref)
    p = _dot(h, wr_ref[...])
    zg = _silu(_dot(h, wg_ref[...]))

    rows = lax.broadcasted_iota(jnp.int32, (tile, 1), 0)
    rmod = rows & (chunk - 1)
    last = (t_valid - 1) % chunk
    shifted = pltpu.roll(p, 1, 0)
    top_row = lax.broadcasted_iota(jnp.int32, (SUBLANES, 1), 0) == 0
    if chained:
        starts = {0: carry_ref[...]}
        carry_ref[...] = p[tile - chunk + last:tile - chunk + last + 1, :]
    else:
        starts = {s * chunk: shift0_ref[s] for s in range(n_chunks)}
        for s in range(n_chunks):
            shift_out_ref[s] = p[s * chunk + last:s * chunk + last + 1, :]
    pieces, at = [], 0
    for r0 in sorted(starts):
        if r0 > at:
            pieces.append(shifted[at:r0, :])
        pieces.append(jnp.where(top_row, starts[r0], shifted[r0:r0 + SUBLANES, :]))
        at = r0 + SUBLANES
    pieces.append(shifted[at:, :])
    prev = jnp.concatenate(pieces, axis=0)
    pm = p + (prev - p) * mu_ref[...]

    r = pm[:, 0:D_MODEL]
    k = pm[:, D_MODEL:2 * D_MODEL]
    v = pm[:, 2 * D_MODEL:3 * D_MODEL]
    wa = pm[:, 3 * D_MODEL:SHIFT_W]
    lane = lax.broadcasted_iota(jnp.int32, (1, LANES), 1)
    wa = jnp.where(lane < LORA, jnp.tanh(wa), wa)
    lora = _dot(wa.astype(BF16), w2a_ref[...])
    logd = -(DECAY_SCALE * LOG2_E) * _sigmoid(w0_ref[...] + lora[:, :D_MODEL])
    a_sig = _sigmoid(a0_ref[...] + lora[:, D_MODEL:])

    @pl.when(t == 0)
    def _():
        ones_ref[...] = _block_ones(MXU_TILE, HEAD)

    ones_blk = ones_ref[...]
    kk = k * kk_ref[...]
    kk = kk * jnp.minimum(lax.rsqrt(_head_sum(kk * kk, ones_blk)), 1e12)
    k2 = k * (1.0 + (a_sig - 1.0) * ka_ref[...])
    bvec = kk * a_sig
    if t_valid % chunk != 0:
        ok = rmod < t_valid % chunk
        logd = jnp.where(ok, logd, 0.0)
        bvec = jnp.where(ok, bvec, 0.0)
        k2 = jnp.where(ok, k2, 0.0)

    cum = logd
    s = 1
    while s < chunk:
        cum = cum + jnp.where(rmod >= s, pltpu.roll(cum, s, 0), 0.0)
        s *= 2
    p_inc = jnp.exp2(cum)
    p_inv = jnp.exp2(-cum)
    p_exc = jnp.exp2(cum - logd)

    def to_pairs(ref, val):
        for q in range(PAIRS):
            ref[q] = val[:, LANES * q:LANES * (q + 1)]

    to_pairs(at_ref, -kk * p_exc)
    to_pairs(rt_ref, r * p_inc)
    to_pairs(bt_ref, bvec * p_inv)
    to_pairs(kt_ref, k2 * p_inv)
    to_pairs(v_ref, v)
    for j in range(n_chunks):
        row = jnp.exp2(cum[(j + 1) * chunk - 1:(j + 1) * chunk, :])
        for q in range(PAIRS):
            pl_ref[q, j] = jnp.broadcast_to(row[:, LANES * q:LANES * (q + 1)], (SUBLANES, LANES))

    m_a = (lane < HEAD).astype(F32)
    m_b = 1.0 - m_a
    ri = lax.broadcasted_iota(jnp.int32, (two, two), 0)
    ci = lax.broadcasted_iota(jnp.int32, (two, two), 1)
    csh = chunk.bit_length() - 1
    same_head = (ri >> csh) == (ci >> csh)
    strict = jnp.logical_and(same_head, (ri & (chunk - 1)) > (ci & (chunk - 1)))
    incl = jnp.logical_and(same_head, (ri & (chunk - 1)) >= (ci & (chunk - 1)))
    eye = ri == ci
    head_a = ri < chunk
    off_masks = []
    for lg in range(csh):
        off_masks.append(jnp.logical_and(strict, jnp.logical_and((ri >> (lg + 1)) == (ci >> (lg + 1)),
                                                                 (ri >> lg) != (ci >> lg))))

    def stack(x):
        return jnp.concatenate([x * m_a, x * m_b], axis=0)

    gsh = group.bit_length() - 1
    unroll = n_inst

    def group_body(g, carry):
        def where_is(i):
            q = g * group + (i & (group - 1))
            j = i >> gsh
            return q, j, pl.ds(pl.multiple_of(j * chunk, chunk), chunk)

        def gram(i, c):
            q, _, sl = where_is(i)
            atm = stack(at_ref[q, sl, :])
            rtm = stack(rt_ref[q, sl, :])
            bt = bt_ref[q, sl, :]
            kt = kt_ref[q, sl, :]
            o1 = _mm(jnp.concatenate([atm, rtm], axis=0), jnp.concatenate([bt, kt], axis=0), _NT)
            o1_sw = pltpu.roll(o1, chunk, 1)
            n_ab = jnp.where(strict, jnp.where(head_a, o1[:two], o1_sw[:two]), 0.0)
            nh_ref[i] = n_ab.astype(BF16)
            t_ref[i] = jnp.where(eye, 1.0, jnp.where(off_masks[0], n_ab, 0.0)).astype(BF16)
            aak_ref[i] = jnp.where(strict, jnp.where(head_a, o1_sw[:two], o1[:two]), 0.0).astype(BF16)
            ark_ref[i] = jnp.concatenate([jnp.where(incl, jnp.where(head_a, o1[two:], o1_sw[two:]), 0.0),
                                          jnp.where(incl, jnp.where(head_a, o1_sw[two:], o1[two:]), 0.0)],
                                         axis=1).astype(BF16)
            return c

        lax.fori_loop(0, n_inst, gram, 0, unroll=unroll)

        for off in off_masks[1:]:
            def level(i, c, off=off):
                t_inv = t_ref[i]
                x = _dot(jnp.where(off, nh_ref[i], 0.0), t_inv).astype(BF16)
                t_ref[i] = t_inv + _dot(t_inv, x).astype(BF16)
                return c

            lax.fori_loop(0, n_inst, level, 0, unroll=unroll)

        def right_side(i, c):
            q, _, sl = where_is(i)
            atm = stack(at_ref[q, sl, :])
            vm = stack(v_ref[q, sl, :])
            z0_hi, z0_lo = _split2(jnp.concatenate([atm, _mm(aak_ref[i], vm)], axis=1))
            rhs_ref[i] = jnp.concatenate([z0_hi, z0_lo], axis=0)
            return c

        lax.fori_loop(0, n_inst, right_side, 0, unroll=unroll)

        def apply(i, c):
            q, _, sl = where_is(i)
            vm = stack(v_ref[q, sl, :])
            t_inv = t_ref[i]
            z = _dot(t_inv, rhs_ref[i, :two, :]) + _dot(t_inv, rhs_ref[i, two:, :])
            rhs_ref[i] = jnp.concatenate([z, jnp.concatenate([jnp.zeros_like(vm), vm], axis=1)], axis=0).astype(BF16)
            return c

        lax.fori_loop(0, n_inst, apply, 0, unroll=unroll)

        def operators(i, c):
            q, j, sl = where_is(i)
            rhs2 = rhs_ref[i]
            k_all = g * n_inst + i
            p_last = pl_ref[q, j][0:1, :]
            top = _mm(jnp.concatenate([stack(bt_ref[q, sl, :] * p_last), stack(kt_ref[q, sl, :] * p_last)], axis=0),
                      rhs2, _TN)
            bot = _dot(ark_ref[i], rhs2)
            g_mat = top[:, :two] + jnp.where(eye, jnp.broadcast_to(p_last, (two, two)), 0.0)
            q_eff = bot[:, :two] + stack(rt_ref[q, sl, :])
            qg_ref[k_all] = jnp.concatenate([q_eff, g_mat], axis=0).astype(BF16)
            hc_ref[k_all] = top[:, two:]
            yc_ref[k_all] = bot[:, two:]
            return c

        lax.fori_loop(0, n_inst, operators, 0, unroll=unroll)
        return carry

    lax.fori_loop(0, PAIRS // group, group_body, 0)

    if chained:
        states = [st_ref[q] for q in range(PAIRS)]
    for j in range(n_chunks):
        for q in range(PAIRS):
            k_all = (q // group) * n_inst + j * group + q % group
            st = (states[q] if chained else pair_state(wkv0_ref[j, q])).astype(BF16)
            both = _dot(qg_ref[k_all], st)
            ym = both[:two, :] + yc_ref[k_all]
            y_ref[q, j * chunk:(j + 1) * chunk, :] = ym[:chunk, :] + ym[chunk:, :]
            st_new = both[two:, :] + hc_ref[k_all]
            if chained:
                states[q] = st_new
            else:
                wkv_out_ref[j, q] = head_states(st_new)
    if chained:
        for q in range(PAIRS):
            st_ref[q] = states[q]

    ys = jnp.concatenate([y_ref[q] for q in range(PAIRS)], axis=1)
    mean_blk = ones_blk * (1.0 / HEAD)
    mean = _head_sum(ys, mean_blk)
    yc = ys - mean
    var = _head_sum(yc * yc, mean_blk)
    yn = yc * lax.rsqrt(var + GN_EPS) * lng_ref[...] + lnb_ref[...]
    bonus = _head_sum(r * k2 * rk_ref[...], ones_blk) * v
    yz_ref[...] = ((yn + bonus) * zg).reshape(yz_ref.shape)

    if chained:
        @pl.when(t == pl.num_programs(1) - 1)
        def _():
            shift_out_ref[0] = carry_ref[...]
            for q in range(PAIRS):
                wkv_out_ref[0, q] = head_states(st_ref[q])


def _specs(layer, state_layer, mod_row, seqs=None):
    assert seqs is None or mod_row % seqs == 0
    per_step = 1 if seqs is None else seqs

    def weight(a):
        return pl.BlockSpec((None,) + a.shape[1:], lambda b, t: (layer,) + (0,) * (a.ndim - 1))

    def state(a):
        return pl.BlockSpec((None, seqs) + a.shape[2:], lambda b, t: (state_layer, b) + (0,) * (a.ndim - 2))

    def mod(a):
        return pl.BlockSpec((None, seqs) + a.shape[2:], lambda b, t: (layer, mod_row // per_step + b, 0, 0))

    return weight, state, mod


def _pre_norm(x_ref, g_ref, mod_ref):
    g = g_ref[0:1, :]
    rows = [_norm_mod(x_ref[s], g, mod_ref[s]) for s in range(x_ref.shape[0])]
    return jnp.concatenate(rows, axis=0).astype(BF16)


def _rwkv_mixer(x, mod, g_pre, wr, wg, vecs, w2a, shift0, wkv0, *, sel, tile, t_valid, seqs):
    bsz, tp, _ = x.shape
    chunk = RWKV_CHUNK
    assert tp % tile == 0 and tile % chunk == 0 and bsz % seqs == 0
    assert t_valid == tp or tp == tile
    assert seqs == 1 or tp == chunk
    nt = tp // tile
    block_rows = tile
    tile = seqs * tile
    mu, w0, a0, k_k, k_a, r_k, ln_g, ln_b = vecs
    weight, state, mod_spec = _specs(*sel, seqs=seqs)
    pair_buf = pltpu.VMEM((PAIRS, tile, LANES), F32)
    group = min(PAIRS, max(1, RWKV_INSTANCES // (tile // chunk)))
    n_inst = group * (tile // chunk)
    two = 2 * chunk
    inst = lambda cols, dtype: pltpu.VMEM((n_inst, two, cols), dtype)
    every = lambda dtype: pltpu.VMEM((PAIRS * (tile // chunk), two, two), dtype)
    kern = functools.partial(_rwkv_kernel, tile=tile, chunk=chunk, t_valid=t_valid, group=group)
    return pl.pallas_call(
        kern,
        grid=(bsz // seqs, nt),
        in_specs=[
            pl.BlockSpec((seqs, block_rows, D_MODEL), lambda b, t: (b, t, 0)),
            mod_spec(mod),
            weight(g_pre), weight(wr), weight(wg), weight(mu), weight(w0), weight(w2a), weight(a0),
            weight(k_k), weight(k_a), weight(r_k), weight(ln_g), weight(ln_b),
            state(shift0), state(wkv0),
        ],
        out_specs=[
            pl.BlockSpec((seqs, block_rows, D_MODEL), lambda b, t: (b, t, 0)),
            pl.BlockSpec((seqs, 1, SHIFT_W), lambda b, t: (b, 0, 0)),
            pl.BlockSpec((seqs, PAIRS, LANES, HEAD), lambda b, t: (b, 0, 0, 0)),
        ],
        out_shape=[
            jax.ShapeDtypeStruct((bsz, tp, D_MODEL), F32),
            jax.ShapeDtypeStruct((bsz, 1, SHIFT_W), F32),
            jax.ShapeDtypeStruct((bsz, PAIRS, LANES, HEAD), F32),
        ],
        scratch_shapes=[
            pltpu.VMEM((1, SHIFT_W), F32),
            pltpu.VMEM((PAIRS, LANES, LANES), F32),
            pair_buf, pair_buf, pair_buf, pair_buf, pair_buf,
            pltpu.VMEM((PAIRS, tile // chunk, SUBLANES, LANES), F32),
            pair_buf,
            inst(two, BF16), inst(two, BF16), inst(two, BF16), inst(2 * two, BF16),
            pltpu.VMEM((n_inst, 2 * two, 2 * two), BF16),
            pltpu.VMEM((PAIRS * (tile // chunk), 2 * two, two), BF16), every(F32), every(F32),
            pltpu.VMEM((MXU_TILE, MXU_TILE), BF16),
        ],
        compiler_params=pltpu.CompilerParams(
            dimension_semantics=("arbitrary", "arbitrary"), vmem_limit_bytes=VMEM_LIMIT),
        name="rwkv7_mixer",
    )(x, mod, g_pre, wr, wg, mu, w0, w2a, a0, k_k, k_a, r_k, ln_g, ln_b, shift0, wkv0)


def _s5_kernel(x_ref, mod_ref, g_ref, wu_ref, wg_ref, bbr_ref, bbi_ref, cre_ref, cmi_ref, abr_ref, abi_ref,
               d_ref, gluw_ref, glub_ref, s0r_ref, s0i_ref,
               yz_ref, sr_out_ref, si_out_ref,
               hr_ref, hi_ref, cr_ref, ci_ref, *, tile):
    t = pl.program_id(1)

    @pl.when(t == 0)
    def _():
        cr_ref[...] = s0r_ref[...]
        ci_ref[...] = s0i_ref[...]

    seqs = x_ref.shape[0]
    frames = seqs * tile
    h = _pre_norm(x_ref, g_ref, mod_ref)
    u = _dot(h, wu_ref[...])
    zg = _silu(_dot(h, wg_ref[...]))
    set_tiles = S5_TILES // S5_SETS

    def frame_rows(c):
        return pl.ds(c, frames, stride=S5_ROW_STRIDE)

    in_tiles = bbr_ref.shape[2] // LANES
    for s in range(D_MODEL // LANES):
        ub = u[:, LANES * s:LANES * (s + 1)].astype(BF16)
        bur = _dot(ub, bbr_ref[s])
        bui = _dot(ub, bbi_ref[s])
        for c in range(in_tiles):
            hr_ref[frame_rows(s * in_tiles + c), :] = bur[:, LANES * c:LANES * (c + 1)]
            hi_ref[frame_rows(s * in_tiles + c), :] = bui[:, LANES * c:LANES * (c + 1)]

    abr = abr_ref[...]
    abi = abi_ref[...]

    def frame(f, carry):
        sr, si = carry
        rows = pl.ds(pl.multiple_of(f * S5_ROW_STRIDE, SUBLANES), S5_TILES)
        nr = abr * sr - abi * si + hr_ref[rows, :]
        ni = abr * si + abi * sr + hi_ref[rows, :]
        hr_ref[rows, :] = nr
        hi_ref[rows, :] = ni
        return nr, ni

    for s in range(seqs):
        cr_ref[s], ci_ref[s] = lax.fori_loop(s * tile, (s + 1) * tile, frame, (cr_ref[s], ci_ref[s]), unroll=True)

    outs = []
    for s in range(S5_SETS):
        hr = jnp.concatenate([hr_ref[frame_rows(s * set_tiles + c), :] for c in range(set_tiles)], axis=1)
        hi = jnp.concatenate([hi_ref[frame_rows(s * set_tiles + c), :] for c in range(set_tiles)], axis=1)
        outs.append(_dot(hr.astype(BF16), cre_ref[s]) + _dot(hi.astype(BF16), cmi_ref[s]))
    y = jnp.concatenate(outs, axis=1) + d_ref[0:1, :] * u
    y = (0.5 * y) * (1.0 + jnp.tanh(y * (GELU_C + (GELU_C * 0.044715) * (y * y))))
    y = y * _sigmoid(_dot(y.astype(BF16), gluw_ref[...]) + glub_ref[0:1, :])
    yz_ref[...] = (y * zg).reshape(seqs, tile, D_MODEL)

    @pl.when(t == pl.num_programs(1) - 1)
    def _():
        sr_out_ref[...] = cr_ref[...]
        si_out_ref[...] = ci_ref[...]


def _s5_mixer(x, mod, g_pre, wu, wg, bbr, bbi, cre, cmi, abr, abi, d_skip, glu_w, glu_b, s0r, s0i, *, sel, tile,
              seqs):
    bsz, tp, _ = x.shape
    assert tp % tile == 0 and tile % SUBLANES == 0 and bsz % seqs == 0
    nt = tp // tile
    weight, state, mod_spec = _specs(*sel, seqs=seqs)
    state_out = pl.BlockSpec((seqs, S5_TILES, LANES), lambda b, t: (b, 0, 0))
    kern = functools.partial(_s5_kernel, tile=tile)
    return pl.pallas_call(
        kern,
        grid=(bsz // seqs, nt),
        in_specs=[
            pl.BlockSpec((seqs, tile, D_MODEL), lambda b, t: (b, t, 0)),
            mod_spec(mod),
            weight(g_pre), weight(wu), weight(wg), weight(bbr), weight(bbi), weight(cre), weight(cmi),
            weight(abr), weight(abi), weight(d_skip), weight(glu_w), weight(glu_b), state(s0r), state(s0i),
        ],
        out_specs=[pl.BlockSpec((seqs, tile, D_MODEL), lambda b, t: (b, t, 0)), state_out, state_out],
        out_shape=[
            jax.ShapeDtypeStruct((bsz, tp, D_MODEL), F32),
            jax.ShapeDtypeStruct((bsz, S5_TILES, LANES), F32),
            jax.ShapeDtypeStruct((bsz, S5_TILES, LANES), F32),
        ],
        scratch_shapes=[
            pltpu.VMEM((seqs * tile * S5_ROW_STRIDE, LANES), F32),
            pltpu.VMEM((seqs * tile * S5_ROW_STRIDE, LANES), F32),
            pltpu.VMEM((seqs, S5_TILES, LANES), F32),
            pltpu.VMEM((seqs, S5_TILES, LANES), F32),
        ],
        compiler_params=pltpu.CompilerParams(
            dimension_semantics=("arbitrary", "arbitrary"), vmem_limit_bytes=VMEM_LIMIT),
        name="s5_mixer",
    )(x, mod, g_pre, wu, wg, bbr, bbi, cre, cmi, abr, abi, d_skip, glu_w, glu_b, s0r, s0i)


def _lru_kernel(x_ref, mod_ref, g_ref, wx_ref, wg_ref, cw_ref, cb_ref, wa4_ref, ba_ref, wx4_ref, bx_ref, lam_ref,
                conv0_ref, h0_ref,
                yz_ref, conv_out_ref, h_out_ref,
                xbuf_ref, a_ref, b_ref, hc_ref, *, tile):
    t = pl.program_id(1)
    pad = SUBLANES
    hist = CONV_W - 1
    seqs = x_ref.shape[0]

    @pl.when(t == 0)
    def _():
        xbuf_ref[:, pad - hist:pad, :] = conv0_ref[...]
        hc_ref[...] = h0_ref[...]

    h = _pre_norm(x_ref, g_ref, mod_ref)
    xl = _dot(h, wx_ref[...])
    zg = _silu(_dot(h, wg_ref[...]))
    xcs, tails = [], []
    for s in range(seqs):
        xs = xl[s * tile:(s + 1) * tile, :]
        xbuf_ref[s, pad:pad + tile, :] = xs
        acc = cb_ref[0:1, :] + xs * cw_ref[CONV_W - 1:CONV_W, :]
        for j in range(hist):
            acc = acc + xbuf_ref[s, pad - hist + j:pad - hist + j + tile, :] * cw_ref[j:j + 1, :]
        tails.append(xbuf_ref[s, pad + tile - hist:pad + tile, :])
        xbuf_ref[s, pad - hist:pad, :] = tails[s]
        xcs.append(acc)
    xc = jnp.concatenate(xcs, axis=0)

    ga, gx = [], []
    blk = MXU_TILE
    for s in range(D_MODEL // blk):
        xb = xc[:, blk * s:blk * (s + 1)].astype(BF16)
        ga.append(_dot(xb, wa4_ref[s]))
        gx.append(_dot(xb, wx4_ref[s]))
    gate_r = _sigmoid(jnp.concatenate(ga, axis=1) + ba_ref[0:1, :])
    gate_i = _sigmoid(jnp.concatenate(gx, axis=1) + bx_ref[0:1, :])
    log_a = -LRU_C * gate_r * _softplus(-lam_ref[0:1, :])
    a = jnp.exp(log_a)
    a_ref[...] = a
    var = -jnp.tanh(log_a) * (a * a + 1.0)
    b_ref[...] = jnp.where(var > 0.0, var * lax.rsqrt(var), 0.0) * (gate_i * xc)

    row8 = lax.broadcasted_iota(jnp.int32, (SUBLANES, 1), 0)

    def row_body(g, carry):
        rs = pl.ds(pl.multiple_of(g * SUBLANES, SUBLANES), SUBLANES)
        av = a_ref[rs, :]
        bv = b_ref[rs, :]
        for sh in (1, 2, 4):
            keep = row8 >= sh
            ash = jnp.where(keep, pltpu.roll(av, sh, 0), 1.0)
            bsh = jnp.where(keep, pltpu.roll(bv, sh, 0), 0.0)
            bv = bv + av * bsh
            av = av * ash
        hs = bv + av * carry
        b_ref[rs, :] = hs
        return hs[SUBLANES - 1:SUBLANES, :]

    groups = tile // SUBLANES
    for s in range(seqs):
        hc_ref[s] = lax.fori_loop(s * groups, (s + 1) * groups, row_body, hc_ref[s], unroll=True)
    yz_ref[...] = (b_ref[...] * zg).reshape(seqs, tile, D_MODEL)

    @pl.when(t == pl.num_programs(1) - 1)
    def _():
        for s in range(seqs):
            conv_out_ref[s] = tails[s]
        h_out_ref[...] = hc_ref[...]


def _lru_mixer(x, mod, g_pre, wx, wg, conv_w, conv_b, wa4, ba, wx4, bx, lam, conv0, h0, *, sel, tile, seqs):
    bsz, tp, _ = x.shape
    assert tp % tile == 0 and tile % SUBLANES == 0 and tile >= SUBLANES and bsz % seqs == 0
    nt = tp // tile
    weight, state, mod_spec = _specs(*sel, seqs=seqs)
    kern = functools.partial(_lru_kernel, tile=tile)
    return pl.pallas_call(
        kern,
        grid=(bsz // seqs, nt),
        in_specs=[
            pl.BlockSpec((seqs, tile, D_MODEL), lambda b, t: (b, t, 0)),
            mod_spec(mod),
            weight(g_pre), weight(wx), weight(wg), weight(conv_w), weight(conv_b), weight(wa4), weight(ba),
            weight(wx4), weight(bx), weight(lam), state(conv0), state(h0),
        ],
        out_specs=[
            pl.BlockSpec((seqs, tile, D_MODEL), lambda b, t: (b, t, 0)),
            pl.BlockSpec((seqs, CONV_W - 1, D_MODEL), lambda b, t: (b, 0, 0)),
            pl.BlockSpec((seqs, 1, D_MODEL), lambda b, t: (b, 0, 0)),
        ],
        out_shape=[
            jax.ShapeDtypeStruct((bsz, tp, D_MODEL), F32),
            jax.ShapeDtypeStruct((bsz, CONV_W - 1, D_MODEL), F32),
            jax.ShapeDtypeStruct((bsz, 1, D_MODEL), F32),
        ],
        scratch_shapes=[
            pltpu.VMEM((seqs, tile + SUBLANES, D_MODEL), F32),
            pltpu.VMEM((seqs * tile, D_MODEL), F32),
            pltpu.VMEM((seqs * tile, D_MODEL), F32),
            pltpu.VMEM((seqs, 1, D_MODEL), F32),
        ],
        compiler_params=pltpu.CompilerParams(
            dimension_semantics=("arbitrary", "arbitrary"), vmem_limit_bytes=VMEM_LIMIT),
        name="rglru_mixer",
    )(x, mod, g_pre, wx, wg, conv_w, conv_b, wa4, ba, wx4, bx, lam, conv0, h0)


def _merge_kernel(x_ref, mod_ref, g_ref, gpost_ref, wm_ref, wo_ref, yr_ref, ys_ref, yl_ref, o_ref):
    seqs, tile, _ = x_ref.shape
    rows = lambda ref: ref[...].reshape(seqs * tile, D_MODEL)
    m = _sigmoid(_dot(_pre_norm(x_ref, g_ref, mod_ref), wm_ref[...]))
    merged = (m[:, :D_MODEL] * rows(yr_ref) + m[:, D_MODEL:2 * D_MODEL] * rows(ys_ref)
              + m[:, 2 * D_MODEL:] * rows(yl_ref))
    o = _dot(merged.astype(BF16), wo_ref[...])
    ms = jnp.mean(o * o, axis=-1, keepdims=True)
    o = (o * lax.rsqrt(ms + RMS_EPS)) * gpost_ref[...]
    for s in range(seqs):
        o_ref[s] = x_ref[s] + mod_ref[s][2:3, :] * o[s * tile:(s + 1) * tile, :]


def _merge_out(x, mod, g_pre, g_post, wm, wo, yr, ys, yl, *, sel, tile, seqs):
    bsz, tp, _ = x.shape
    assert tp % tile == 0 and bsz % seqs == 0
    nt = tp // tile
    act = pl.BlockSpec((seqs, tile, D_MODEL), lambda b, t: (b, t, 0))
    weight, _, mod_spec = _specs(*sel, seqs=seqs)
    return pl.pallas_call(
        _merge_kernel,
        grid=(bsz // seqs, nt),
        in_specs=[act, mod_spec(mod), weight(g_pre), weight(g_post), weight(wm), weight(wo), act, act, act],
        out_specs=act,
        out_shape=jax.ShapeDtypeStruct((bsz, tp, D_MODEL), F32),
        compiler_params=pltpu.CompilerParams(
            dimension_semantics=("arbitrary", "arbitrary"), vmem_limit_bytes=VMEM_LIMIT),
        name="merge_out",
    )(x, mod, g_pre, g_post, wm, wo, yr, ys, yl)


def _block_diag_expand(w, per):
    depth, n, a, b = w.shape
    w = w.reshape(depth, n // per, per, a, b)
    eye = jnp.eye(per, dtype=w.dtype)
    return jnp.einsum("lsgab,gh->lsgahb", w, eye).reshape(depth, n // per, per * a, per * b)


def _layer(x, mod, state, prm, *, sel, t_valid, tiles):
    shift_row, wkv, s_re, s_im, lru_h, lru_conv = state
    bsz = x.shape[0]
    g_pre = prm["g_pre"]

    xr = x
    if x.shape[1] % tiles["rwkv"] != 0:
        xr = jnp.pad(x, ((0, 0), (0, tiles["rwkv"] - x.shape[1] % tiles["rwkv"]), (0, 0)))
    yz_r, shift_new, wkv_new = _rwkv_mixer(
        xr, mod, g_pre, prm["w_rwkv"], prm["w_rwkv_g"], prm["rwkv_vecs"], prm["w2a"], shift_row, wkv,
        sel=sel, tile=tiles["rwkv"], t_valid=t_valid, seqs=tiles["rwkv_seqs"])
    yz_r = yz_r[:, :t_valid]
    yz_s, s_re_new, s_im_new = _s5_mixer(
        x, mod, g_pre, prm["w_s5"], prm["w_s5_g"], prm["bbr"], prm["bbi"], prm["cre"], prm["cmi"],
        prm["abr"], prm["abi"], prm["s5_d"], prm["glu_w"], prm["glu_b"], s_re, s_im, sel=sel, tile=tiles["s5"],
        seqs=tiles["seqs"])
    yz_l, conv_new, h_new = _lru_mixer(
        x, mod, g_pre, prm["w_lru"], prm["w_lru_g"], prm["conv_w"], prm["conv_b"], prm["wa4"], prm["ba"],
        prm["wx4"], prm["bx"], prm["lam"], lru_conv, lru_h, sel=sel, tile=tiles["lru"], seqs=tiles["seqs"])
    x_new = _merge_out(x, mod, g_pre, prm["g_post"], prm["w_merge"], prm["w_out"], yz_r, yz_s, yz_l,
                       sel=sel, tile=tiles["merge"], seqs=tiles["seqs"])
    new_state = (shift_new.reshape(bsz, SHIFT_W), wkv_new.reshape(bsz, HEADS, HEAD, HEAD),
                 s_re_new.reshape(bsz, S5_GROUPS, S5_STATES), s_im_new.reshape(bsz, S5_GROUPS, S5_STATES),
                 h_new.reshape(bsz, D_MODEL), conv_new)
    return x_new, new_state


def _stacked_params(w_in, w_out, norm_pre, norm_post, rwkv, s5, lru):
    depth = w_in.shape[0]
    d = D_MODEL
    wb = w_in.astype(BF16)
    o = SHIFT_W
    row = lambda a: a.reshape(depth, 1, -1)
    row8 = lambda a: jnp.pad(row(a), ((0, 0), (0, SUBLANES - 1), (0, 0)))
    mu, w0, w2, a0, a2, k_k, k_a, r_k, ln_g, ln_b = rwkv
    a_re, a_im, log_step, b_re, b_im, c_re, c_im, s5_d, glu_w, glu_b = s5
    conv_w, conv_b, wa, ba, wx, bx, lam = lru
    zeros = jnp.zeros((depth, LORA, d), F32)
    w2a = jnp.concatenate([jnp.concatenate([w2, zeros], axis=2), jnp.concatenate([zeros, a2], axis=2)], axis=1)
    abr, abi, bbr, bbi = _s5_prep(a_re, a_im, log_step, b_re, b_im)
    per = S5_GROUPS // S5_SETS
    gate_per = MXU_TILE // LRU_BLOCK
    return dict(
        g_pre=row8(norm_pre), g_post=row(norm_post),
        w_rwkv=wb[:, :, :o], w_rwkv_g=wb[:, :, o:o + d],
        w_s5=wb[:, :, o + d:o + 2 * d], w_s5_g=wb[:, :, o + 2 * d:o + 3 * d],
        w_lru=wb[:, :, o + 3 * d:o + 4 * d], w_lru_g=wb[:, :, o + 4 * d:o + 5 * d],
        w_merge=wb[:, :, o + 5 * d:], w_out=w_out.astype(BF16),
        rwkv_vecs=(row(mu), row(w0), row(a0), row(k_k), row(k_a), row(r_k), row(ln_g), row(ln_b)),
        w2a=w2a.astype(BF16),
        bbr=_block_diag_expand(bbr, S5_IN_GROUPS).astype(BF16),
        bbi=_block_diag_expand(bbi, S5_IN_GROUPS).astype(BF16),
        cre=_block_diag_expand(jnp.swapaxes(c_re, 2, 3), per).astype(BF16),
        cmi=_block_diag_expand(-jnp.swapaxes(c_im, 2, 3), per).astype(BF16),
        abr=abr.reshape(depth, S5_TILES, LANES), abi=abi.reshape(depth, S5_TILES, LANES),
        s5_d=row8(s5_d), glu_w=glu_w.astype(BF16), glu_b=row8(glu_b),
        conv_w=conv_w, conv_b=row8(conv_b),
        wa4=_block_diag_expand(wa, gate_per).astype(BF16), ba=row8(ba),
        wx4=_block_diag_expand(wx, gate_per).astype(BF16), bx=row8(bx), lam=row8(lam),
    )


def _state_layout(shift, wkv, s_re, s_im, lru_h, lru_conv):
    n, bsz = shift.shape[:2]
    return (shift.reshape(n, bsz, 1, SHIFT_W), wkv.reshape(n, bsz, PAIRS, LANES, HEAD),
            s_re.reshape(n, bsz, S5_TILES, LANES), s_im.reshape(n, bsz, S5_TILES, LANES),
            lru_h.reshape(n, bsz, 1, D_MODEL), lru_conv)


def _tiles(t, bsz):
    pick = lambda want: want if t % want == 0 else t
    share = lambda rows, limit: max(n for n in range(1, bsz + 1) if bsz % n == 0 and n * rows <= max(limit, rows))
    seqs = share(t, SHORT_ROWS) if t < SHORT_ROWS else 1
    rwkv = RWKV_TILE if t % RWKV_TILE == 0 else RWKV_CHUNK
    rwkv_seqs = share(RWKV_CHUNK, RWKV_TILE) if t <= RWKV_CHUNK else 1
    return dict(rwkv=rwkv, rwkv_seqs=rwkv_seqs, s5=pick(S5_TILE), lru=pick(WIDE_TILE), merge=pick(WIDE_TILE),
                seqs=seqs)


def kernel(x_prompt, x_sample, state_rwkv_shift, state_rwkv_wkv, state_s5_re, state_s5_im, state_lru_h,
           state_lru_conv, c_prompt, c_sample, ada_w, ada_b, norm_pre, norm_post, w_in, w_out, rwkv_mu, rwkv_w0,
           rwkv_w2, rwkv_a0, rwkv_a2, rwkv_k_k, rwkv_k_a, rwkv_r_k, rwkv_ln_g, rwkv_ln_b, s5_a_re, s5_a_im,
           s5_log_step, s5_b_re, s5_b_im, s5_c_re, s5_c_im, s5_d, s5_glu_w, s5_glu_b, lru_conv_w, lru_conv_b,
           lru_wa, lru_ba, lru_wx, lru_bx, lru_lambda):
    depth = w_in.shape[0]
    bp, tp, _ = x_prompt.shape
    bs, ts, _ = x_sample.shape
    c_all = jnp.concatenate([c_sample, c_prompt], axis=0)
    pad = (-c_all.shape[0]) % SUBLANES
    c_all = jnp.pad(c_all, ((0, pad), (0, 0)))
    mod_all = _modulation(c_all, ada_w, ada_b)
    mod_all = mod_all.reshape(depth, c_all.shape[0], 3, D_MODEL)
    rwkv = (rwkv_mu, rwkv_w0, rwkv_w2, rwkv_a0, rwkv_a2, rwkv_k_k, rwkv_k_a,
            rwkv_r_k.reshape(depth, D_MODEL), rwkv_ln_g, rwkv_ln_b)
    s5 = (s5_a_re, s5_a_im, s5_log_step, s5_b_re, s5_b_im, s5_c_re, s5_c_im, s5_d, s5_glu_w, s5_glu_b)
    lru = (lru_conv_w, lru_conv_b, lru_wa, lru_ba, lru_wx, lru_bx, lru_lambda)
    prm = _stacked_params(w_in, w_out, norm_pre, norm_post, rwkv, s5, lru)
    zero = _state_layout(jnp.zeros((1, bp, SHIFT_W), F32), jnp.zeros((1, bp, HEADS, HEAD, HEAD), F32),
                         jnp.zeros((1, bp, S5_GROUPS, S5_STATES), F32),
                         jnp.zeros((1, bp, S5_GROUPS, S5_STATES), F32),
                         jnp.zeros((1, bp, D_MODEL), F32), jnp.zeros((1, bp, CONV_W - 1, D_MODEL), F32))
    st_in = _state_layout(state_rwkv_shift, state_rwkv_wkv, state_s5_re, state_s5_im, state_lru_h,
                          state_lru_conv)

    xp, xs = x_prompt, x_sample
    new_p = [[] for _ in range(6)]
    new_s = [[] for _ in range(6)]
    for l in range(depth):
        xp, st_p = _layer(xp, mod_all, zero, prm, sel=(l, 0, bs), t_valid=tp, tiles=_tiles(tp, bp))
        xs, st_s = _layer(xs, mod_all, st_in, prm, sel=(l, l, 0), t_valid=ts, tiles=_tiles(ts, bs))
        for i in range(6):
            new_p[i].append(st_p[i])
            new_s[i].append(st_s[i])
    sp = [jnp.stack(z, axis=0) for z in new_p]
    ss = [jnp.stack(z, axis=0) for z in new_s]
    return (xp, xs, sp[0], sp[1], sp[2], sp[3], sp[4], sp[5], ss[0], ss[1], ss[2], ss[3], ss[4], ss[5])
```

```python
import functools

import jax
import jax.numpy as jnp
from jax import lax
from jax.experimental import pallas as pl
from jax.experimental.pallas import tpu as pltpu

F32 = jnp.float32
BF16 = jnp.bfloat16
HIGHEST = lax.Precision.HIGHEST

SUBLANES = 8
LANES = 128
MXU_TILE = 256
VMEM_LIMIT = 56 * 1024 * 1024

D_MODEL = 1024
HEAD = 64
HEADS = D_MODEL // HEAD
PAIRS = HEADS // 2
LORA = 64
SHIFT_W = 3 * D_MODEL + 2 * LORA
S5_GROUPS = 64
S5_STATES = 64
S5_STATE_W = S5_GROUPS * S5_STATES
LRU_BLOCK = 64
S5_SETS = 4
S5_IN_GROUPS = 8
S5_TILES = S5_STATE_W // LANES
S5_ROW_STRIDE = 40
CONV_W = 4
LRU_C = 8.0
RMS_EPS = 1e-6
GN_EPS = 64e-5
RWKV_CHUNK = 64
RWKV_TILE = 256
S5_TILE = 256
WIDE_TILE = 512
SHORT_ROWS = 512
DECAY_SCALE = 0.6065306597126334
GELU_C = 0.7978845608028654
LOG2_E = 1.4426950408889634
RWKV_INSTANCES = 32


def _sigmoid(x):
    return 0.5 * jnp.tanh(0.5 * x) + 0.5


def _silu(x):
    h = 0.5 * x
    return h * (jnp.tanh(h) + 1.0)


def _softplus(x):
    return jnp.maximum(x, 0.0) + jnp.log1p(jnp.exp(-jnp.abs(x)))


def _norm_mod(x, g, mod):
    ms = jnp.mean(x * x, axis=-1, keepdims=True)
    return (x * lax.rsqrt(ms + RMS_EPS)) * (g * (1.0 + mod[1:2, :])) + mod[0:1, :]


def _dot(a, b):
    return jnp.dot(a, b, preferred_element_type=F32)


def _dot_hi(a, b):
    return jnp.dot(a, b, preferred_element_type=F32, precision=HIGHEST)


def _split2(x):
    hi = x.astype(BF16)
    lo = (x - hi.astype(F32)).astype(BF16)
    return hi, lo


_NN = (((1,), (0,)), ((), ()))
_NT = (((1,), (1,)), ((), ()))
_TN = (((0,), (0,)), ((), ()))


def _mm(a, b, dims=_NN):
    return lax.dot_general(a.astype(BF16), b.astype(BF16), dims, preferred_element_type=F32)


def _head_sum(x, ones_blk):
    outs = []
    for s in range(D_MODEL // MXU_TILE):
        outs.append(_dot(x[:, MXU_TILE * s:MXU_TILE * (s + 1)].astype(BF16), ones_blk))
    return jnp.concatenate(outs, axis=1)


def _block_ones(n, blk):
    ri = lax.broadcasted_iota(jnp.int32, (n, n), 0)
    ci = lax.broadcasted_iota(jnp.int32, (n, n), 1)
    sh = blk.bit_length() - 1
    return ((ri >> sh) == (ci >> sh)).astype(F32).astype(BF16)


def _mod_kernel(c_ref, w_ref, b_ref, o_ref):
    s = _silu(c_ref[...])
    o_ref[...] = _dot_hi(s, w_ref[...]) + b_ref[...]


def _modulation(c_all, ada_w, ada_b):
    depth = ada_w.shape[0]
    rows = c_all.shape[0]
    return pl.pallas_call(
        _mod_kernel,
        grid=(depth, 3),
        in_specs=[
            pl.BlockSpec((rows, D_MODEL), lambda l, j: (0, 0)),
            pl.BlockSpec((None, D_MODEL, D_MODEL), lambda l, j: (l, 0, j)),
            pl.BlockSpec((None, 1, D_MODEL), lambda l, j: (l, 0, j)),
        ],
        out_specs=pl.BlockSpec((None, rows, D_MODEL), lambda l, j: (l, 0, j)),
        out_shape=jax.ShapeDtypeStruct((depth, rows, 3 * D_MODEL), F32),
        name="adaln_mod",
    )(c_all, ada_w, ada_b.reshape(depth, 1, 3 * D_MODEL))


def _s5_prep_kernel(are_ref, aim_ref, ls_ref, bre_ref, bim_ref, abr_ref, abi_ref, bbr_ref, bbi_ref):
    are = are_ref[...]
    aim = aim_ref[...]
    dt = jnp.exp(ls_ref[...])
    mag = jnp.exp(are * dt)
    abr = mag * jnp.cos(aim * dt)
    abi = mag * jnp.sin(aim * dt)
    abr_ref[...] = abr
    abi_ref[...] = abi
    nr = abr - 1.0
    ni = abi
    den = are * are + aim * aim
    cr = (nr * are + ni * aim) / den
    ci = (ni * are - nr * aim) / den
    bre = bre_ref[...]
    bim = bim_ref[...]
    bbr_ref[...] = cr[:, None, :] * bre - ci[:, None, :] * bim
    bbi_ref[...] = cr[:, None, :] * bim + ci[:, None, :] * bre


def _s5_prep(a_re, a_im, log_step, b_re, b_im):
    depth, g, p = a_re.shape
    i = b_re.shape[-1]
    layer = lambda *dims: pl.BlockSpec((None,) + dims, lambda l: (l,) + (0,) * len(dims))
    return pl.pallas_call(
        _s5_prep_kernel,
        grid=(depth,),
        in_specs=[layer(g, p), layer(g, p), layer(g, 1), layer(g, i, p), layer(g, i, p)],
        out_specs=(layer(g, p), layer(g, p), layer(g, i, p), layer(g, i, p)),
        out_shape=(
            jax.ShapeDtypeStruct((depth, g, p), F32),
            jax.ShapeDtypeStruct((depth, g, p), F32),
            jax.ShapeDtypeStruct((depth, g, i, p), F32),
            jax.ShapeDtypeStruct((depth, g, i, p), F32),
        ),
        name="s5_discretise",
    )(a_re, a_im, log_step.reshape(depth, g, 1), jnp.swapaxes(b_re, 2, 3), jnp.swapaxes(b_im, 2, 3))


def _rwkv_kernel(x_ref, mod_ref, g_ref, wr_ref, wg_ref, mu_ref, w0_ref, w2a_ref, a0_ref, kk_ref, ka_ref,
                 rk_ref, lng_ref, lnb_ref, shift0_ref, wkv0_ref,
                 yz_ref, shift_out_ref, wkv_out_ref,
                 carry_ref, st_ref, at_ref, rt_ref, bt_ref, kt_ref, v_ref, pl_ref, y_ref,
                 nh_ref, t_ref, aak_ref, ark_ref, rhs_ref, qg_ref, hc_ref, yc_ref, ones_ref,
                 *, tile, chunk, t_valid, group):
    t = pl.program_id(1)
    n_chunks = tile // chunk
    two = 2 * chunk
    n_inst = group * n_chunks
    chained = x_ref.shape[0] == 1

    def pair_state(s0):
        rh = lax.broadcasted_iota(jnp.int32, (LANES, LANES), 0) >> (HEAD.bit_length() - 1)
        ch = lax.broadcasted_iota(jnp.int32, (LANES, LANES), 1) >> (HEAD.bit_length() - 1)
        return jnp.where(rh == ch, jnp.concatenate([s0, s0], axis=1), 0.0).T

    def head_states(st):
        sv = st.T
        return sv[:, :HEAD] + sv[:, HEAD:]

    if chained:
        @pl.when(t == 0)
        def _():
            carry_ref[...] = shift0_ref[0]
            for q in range(PAIRS):
                st_ref[q] = pair_state(wkv0_ref[0, q])

    h = _pre_norm(x_ref, g_ref, mod_ref)
    p = _dot(h, wr_ref[...])
    zg = _silu(_dot(h, wg_ref[...]))

    rows = lax.broadcasted_iota(jnp.int32, (tile, 1), 0)
    rmod = rows & (chunk - 1)
    last = (t_valid - 1) % chunk
    shifted = pltpu.roll(p, 1, 0)
    top_row = lax.broadcasted_iota(jnp.int32, (SUBLANES, 1), 0) == 0
    if chained:
        starts = {0: carry_ref[...]}
        carry_ref[...] = p[tile - chunk + last:tile - chunk + last + 1, :]
    else:
        starts = {s * chunk: shift0_ref[s] for s in range(n_chunks)}
        for s in range(n_chunks):
            shift_out_ref[s] = p[s * chunk + last:s * chunk + last + 1, :]
    pieces, at = [], 0
    for r0 in sorted(starts):
        if r0 > at:
            pieces.append(shifted[at:r0, :])
        pieces.append(jnp.where(top_row, starts[r0], shifted[r0:r0 + SUBLANES, :]))
        at = r0 + SUBLANES
    pieces.append(shifted[at:, :])
    prev = jnp.concatenate(pieces, axis=0)
    pm = p + (prev - p) * mu_ref[...]

    r = pm[:, 0:D_MODEL]
    k = pm[:, D_MODEL:2 * D_MODEL]
    v = pm[:, 2 * D_MODEL:3 * D_MODEL]
    wa = pm[:, 3 * D_MODEL:SHIFT_W]
    lane = lax.broadcasted_iota(jnp.int32, (1, LANES), 1)
    wa = jnp.where(lane < LORA, jnp.tanh(wa), wa)
    lora = _dot(wa.astype(BF16), w2a_ref[...])
    logd = -(DECAY_SCALE * LOG2_E) * _sigmoid(w0_ref[...] + lora[:, :D_MODEL])
    a_sig = _sigmoid(a0_ref[...] + lora[:, D_MODEL:])

    @pl.when(t == 0)
    def _():
        ones_ref[...] = _block_ones(MXU_TILE, HEAD)

    ones_blk = ones_ref[...]
    kk = k * kk_ref[...]
    kk = kk * jnp.minimum(lax.rsqrt(_head_sum(kk * kk, ones_blk)), 1e12)
    k2 = k * (1.0 + (a_sig - 1.0) * ka_ref[...])
    bvec = kk * a_sig
    if t_valid % chunk != 0:
        ok = rmod < t_valid % chunk
        logd = jnp.where(ok, logd, 0.0)
        bvec = jnp.where(ok, bvec, 0.0)
        k2 = jnp.where(ok, k2, 0.0)

    cum = logd
    s = 1
    while s < chunk:
        cum = cum + jnp.where(rmod >= s, pltpu.roll(cum, s, 0), 0.0)
        s *= 2
    p_inc = jnp.exp2(cum)
    p_inv = jnp.exp2(-cum)
    p_exc = jnp.exp2(cum - logd)

    def to_pairs(ref, val):
        for q in range(PAIRS):
            ref[q] = val[:, LANES * q:LANES * (q + 1)]

    to_pairs(at_ref, -kk * p_exc)
    to_pairs(rt_ref, r * p_inc)
    to_pairs(bt_ref, bvec * p_inv)
    to_pairs(kt_ref, k2 * p_inv)
    to_pairs(v_ref, v)
    for j in range(n_chunks):
        row = jnp.exp2(cum[(j + 1) * chunk - 1:(j + 1) * chunk, :])
        for q in range(PAIRS):
            pl_ref[q, j] = jnp.broadcast_to(row[:, LANES * q:LANES * (q + 1)], (SUBLANES, LANES))

    m_a = (lane < HEAD).astype(F32)
    m_b = 1.0 - m_a
    ri = lax.broadcasted_iota(jnp.int32, (two, two), 0)
    ci = lax.broadcasted_iota(jnp.int32, (two, two), 1)
    csh = chunk.bit_length() - 1
    same_head = (ri >> csh) == (ci >> csh)
    strict = jnp.logical_and(same_head, (ri & (chunk - 1)) > (ci & (chunk - 1)))
    incl = jnp.logical_and(same_head, (ri & (chunk - 1)) >= (ci & (chunk - 1)))
    eye = ri == ci
    head_a = ri < chunk
    off_masks = []
    for lg in range(csh):
        off_masks.append(jnp.logical_and(strict, jnp.logical_and((ri >> (lg + 1)) == (ci >> (lg + 1)),
                                                                 (ri >> lg) != (ci >> lg))))

    def stack(x):
        return jnp.concatenate([x * m_a, x * m_b], axis=0)

    gsh = group.bit_length() - 1
    unroll = n_inst

    def group_body(g, carry):
        def where_is(i):
            q = g * group + (i & (group - 1))
            j = i >> gsh
            return q, j, pl.ds(pl.multiple_of(j * chunk, chunk), chunk)

        def gram(i, c):
            q, _, sl = where_is(i)
            atm = stack(at_ref[q, sl, :])
            rtm = stack(rt_ref[q, sl, :])
            bt = bt_ref[q, sl, :]
            kt = kt_ref[q, sl, :]
            o1 = _mm(jnp.concatenate([atm, rtm], axis=0), jnp.concatenate([bt, kt], axis=0), _NT)
            o1_sw = pltpu.roll(o1, chunk, 1)
            n_ab = jnp.where(strict, jnp.where(head_a, o1[:two], o1_sw[:two]), 0.0)
            nh_ref[i] = n_ab.astype(BF16)
            t_ref[i] = jnp.where(eye, 1.0, jnp.where(off_masks[0], n_ab, 0.0)).astype(BF16)
            aak_ref[i] = jnp.where(strict, jnp.where(head_a, o1_sw[:two], o1[:two]), 0.0).astype(BF16)
            ark_ref[i] = jnp.concatenate([jnp.where(incl, jnp.where(head_a, o1[two:], o1_sw[two:]), 0.0),
                                          jnp.where(incl, jnp.where(head_a, o1_sw[two:], o1[two:]), 0.0)],
                                         axis=1).astype(BF16)
            return c

        lax.fori_loop(0, n_inst, gram, 0, unroll=unroll)

        for off in off_masks[1:]:
            def level(i, c, off=off):
                t_inv = t_ref[i]
                x = _dot(jnp.where(off, nh_ref[i], 0.0), t_inv).astype(BF16)
                t_ref[i] = t_inv + _dot(t_inv, x).astype(BF16)
                return c

            lax.fori_loop(0, n_inst, level, 0, unroll=unroll)

        def right_side(i, c):
            q, _, sl = where_is(i)
            atm = stack(at_ref[q, sl, :])
            vm = stack(v_ref[q, sl, :])
            z0_hi, z0_lo = _split2(jnp.concatenate([atm, _mm(aak_ref[i], vm)], axis=1))
            rhs_ref[i] = jnp.concatenate([z0_hi, z0_lo], axis=0)
            return c

        lax.fori_loop(0, n_inst, right_side, 0, unroll=unroll)

        def apply(i, c):
            q, _, sl = where_is(i)
            vm = stack(v_ref[q, sl, :])
            t_inv = t_ref[i]
            z = _dot(t_inv, rhs_ref[i, :two, :]) + _dot(t_inv, rhs_ref[i, two:, :])
            rhs_ref[i] = jnp.concatenate([z, jnp.concatenate([jnp.zeros_like(vm), vm], axis=1)], axis=0).astype(BF16)
            return c

        lax.fori_loop(0, n_inst, apply, 0, unroll=unroll)

        def operators(i, c):
            q, j, sl = where_is(i)
            rhs2 = rhs_ref[i]
            k_all = g * n_inst + i
            p_last = pl_ref[q, j][0:1, :]
            top = _mm(jnp.concatenate([stack(bt_ref[q, sl, :] * p_last), stack(kt_ref[q, sl, :] * p_last)], axis=0),
                      rhs2, _TN)
            bot = _dot(ark_ref[i], rhs2)
            g_mat = top[:, :two] + jnp.where(eye, jnp.broadcast_to(p_last, (two, two)), 0.0)
            q_eff = bot[:, :two] + stack(rt_ref[q, sl, :])
            qg_ref[k_all] = jnp.concatenate([q_eff, g_mat], axis=0).astype(BF16)
            hc_ref[k_all] = top[:, two:]
            yc_ref[k_all] = bot[:, two:]
            return c

        lax.fori_loop(0, n_inst, operators, 0, unroll=unroll)
        return carry

    lax.fori_loop(0, PAIRS // group, group_body, 0)

    if chained:
        states = [st_ref[q] for q in range(PAIRS)]
    for j in range(n_chunks):
        for q in range(PAIRS):
            k_all = (q // group) * n_inst + j * group + q % group
            st = (states[q] if chained else pair_state(wkv0_ref[j, q])).astype(BF16)
            both = _dot(qg_ref[k_all], st)
            ym = both[:two, :] + yc_ref[k_all]
            y_ref[q, j * chunk:(j + 1) * chunk, :] = ym[:chunk, :] + ym[chunk:, :]
            st_new = both[two:, :] + hc_ref[k_all]
            if chained:
                states[q] = st_new
            else:
                wkv_out_ref[j, q] = head_states(st_new)
    if chained:
        for q in range(PAIRS):
            st_ref[q] = states[q]

    ys = jnp.concatenate([y_ref[q] for q in range(PAIRS)], axis=1)
    mean_blk = ones_blk * (1.0 / HEAD)
    mean = _head_sum(ys, mean_blk)
    yc = ys - mean
    var = _head_sum(yc * yc, mean_blk)
    yn = yc * lax.rsqrt(var + GN_EPS) * lng_ref[...] + lnb_ref[...]
    bonus = _head_sum(r * k2 * rk_ref[...], ones_blk) * v
    yz_ref[...] = ((yn + bonus) * zg).reshape(yz_ref.shape)

    if chained:
        @pl.when(t == pl.num_programs(1) - 1)
        def _():
            shift_out_ref[0] = carry_ref[...]
            for q in range(PAIRS):
                wkv_out_ref[0, q] = head_states(st_ref[q])


def _specs(layer, state_layer, mod_row, seqs=None):
    assert seqs is None or mod_row % seqs == 0
    per_step = 1 if seqs is None else seqs

    def weight(a):
        return pl.BlockSpec((None,) + a.shape[1:], lambda b, t: (layer,) + (0,) * (a.ndim - 1),
                            pipeline_mode=pl.Buffered(1))

    def state(a):
        return pl.BlockSpec((None, seqs) + a.shape[2:], lambda b, t: (state_layer, b) + (0,) * (a.ndim - 2))

    def mod(a):
        return pl.BlockSpec((None, seqs) + a.shape[2:], lambda b, t: (layer, mod_row // per_step + b, 0, 0))

    return weight, state, mod


def _pre_norm(x_ref, g_ref, mod_ref):
    g = g_ref[0:1, :]
    rows = [_norm_mod(x_ref[s], g, mod_ref[s]) for s in range(x_ref.shape[0])]
    return jnp.concatenate(rows, axis=0).astype(BF16)


def _rwkv_mixer(x, mod, g_pre, wr, wg, vecs, w2a, shift0, wkv0, *, sel, tile, t_valid, seqs):
    bsz, tp, _ = x.shape
    chunk = RWKV_CHUNK
    assert tp % tile == 0 and tile % chunk == 0 and bsz % seqs == 0
    assert t_valid == tp or tp == tile
    assert seqs == 1 or tp == chunk
    nt = tp // tile
    block_rows = tile
    tile = seqs * tile
    mu, w0, a0, k_k, k_a, r_k, ln_g, ln_b = vecs
    weight, state, mod_spec = _specs(*sel, seqs=seqs)
    pair_buf = pltpu.VMEM((PAIRS, tile, LANES), F32)
    group = min(PAIRS, max(1, RWKV_INSTANCES // (tile // chunk)))
    n_inst = group * (tile // chunk)
    two = 2 * chunk
    inst = lambda cols, dtype: pltpu.VMEM((n_inst, two, cols), dtype)
    every = lambda dtype: pltpu.VMEM((PAIRS * (tile // chunk), two, two), dtype)
    kern = functools.partial(_rwkv_kernel, tile=tile, chunk=chunk, t_valid=t_valid, group=group)
    return pl.pallas_call(
        kern,
        grid=(bsz // seqs, nt),
        in_specs=[
            pl.BlockSpec((seqs, block_rows, D_MODEL), lambda b, t: (b, t, 0)),
            mod_spec(mod),
            weight(g_pre), weight(wr), weight(wg), weight(mu), weight(w0), weight(w2a), weight(a0),
            weight(k_k), weight(k_a), weight(r_k), weight(ln_g), weight(ln_b),
            state(shift0), state(wkv0),
        ],
        out_specs=[
            pl.BlockSpec((seqs, block_rows, D_MODEL), lambda b, t: (b, t, 0)),
            pl.BlockSpec((seqs, 1, SHIFT_W), lambda b, t: (b, 0, 0)),
            pl.BlockSpec((seqs, PAIRS, LANES, HEAD), lambda b, t: (b, 0, 0, 0)),
        ],
        out_shape=[
            jax.ShapeDtypeStruct((bsz, tp, D_MODEL), F32),
            jax.ShapeDtypeStruct((bsz, 1, SHIFT_W), F32),
            jax.ShapeDtypeStruct((bsz, PAIRS, LANES, HEAD), F32),
        ],
        scratch_shapes=[
            pltpu.VMEM((1, SHIFT_W), F32),
            pltpu.VMEM((PAIRS, LANES, LANES), F32),
            pair_buf, pair_buf, pair_buf, pair_buf, pair_buf,
            pltpu.VMEM((PAIRS, tile // chunk, SUBLANES, LANES), F32),
            pair_buf,
            inst(two, BF16), inst(two, BF16), inst(two, BF16), inst(2 * two, BF16),
            pltpu.VMEM((n_inst, 2 * two, 2 * two), BF16),
            pltpu.VMEM((PAIRS * (tile // chunk), 2 * two, two), BF16), every(F32), every(F32),
            pltpu.VMEM((MXU_TILE, MXU_TILE), BF16),
        ],
        compiler_params=pltpu.CompilerParams(
            dimension_semantics=("arbitrary", "arbitrary"), vmem_limit_bytes=VMEM_LIMIT),
        name="rwkv7_mixer",
    )(x, mod, g_pre, wr, wg, mu, w0, w2a, a0, k_k, k_a, r_k, ln_g, ln_b, shift0, wkv0)


def _s5_kernel(x_ref, mod_ref, g_ref, wu_ref, wg_ref, bbr_ref, bbi_ref, cre_ref, cmi_ref, abr_ref, abi_ref,
               d_ref, gluw_ref, glub_ref, s0r_ref, s0i_ref,
               yz_ref, sr_out_ref, si_out_ref,
               hr_ref, hi_ref, cr_ref, ci_ref, *, tile):
    t = pl.program_id(1)

    @pl.when(t == 0)
    def _():
        cr_ref[...] = s0r_ref[...]
        ci_ref[...] = s0i_ref[...]

    seqs = x_ref.shape[0]
    frames = seqs * tile
    h = _pre_norm(x_ref, g_ref, mod_ref)
    u = _dot(h, wu_ref[...])
    zg = _silu(_dot(h, wg_ref[...]))
    set_tiles = S5_TILES // S5_SETS

    def frame_rows(c):
        return pl.ds(c, frames, stride=S5_ROW_STRIDE)

    in_tiles = bbr_ref.shape[2] // LANES
    for s in range(D_MODEL // LANES):
        ub = u[:, LANES * s:LANES * (s + 1)].astype(BF16)
        bur = _dot(ub, bbr_ref[s])
        bui = _dot(ub, bbi_ref[s])
        for c in range(in_tiles):
            hr_ref[frame_rows(s * in_tiles + c), :] = bur[:, LANES * c:LANES * (c + 1)]
            hi_ref[frame_rows(s * in_tiles + c), :] = bui[:, LANES * c:LANES * (c + 1)]

    abr = abr_ref[...]
    abi = abi_ref[...]

    def frame(f, carry):
        sr, si = carry
        rows = pl.ds(pl.multiple_of(f * S5_ROW_STRIDE, SUBLANES), S5_TILES)
        nr = abr * sr - abi * si + hr_ref[rows, :]
        ni = abr * si + abi * sr + hi_ref[rows, :]
        hr_ref[rows, :] = nr
        hi_ref[rows, :] = ni
        return nr, ni

    for s in range(seqs):
        cr_ref[s], ci_ref[s] = lax.fori_loop(s * tile, (s + 1) * tile, frame, (cr_ref[s], ci_ref[s]), unroll=True)

    outs = []
    for s in range(S5_SETS):
        hr = jnp.concatenate([hr_ref[frame_rows(s * set_tiles + c), :] for c in range(set_tiles)], axis=1)
        hi = jnp.concatenate([hi_ref[frame_rows(s * set_tiles + c), :] for c in range(set_tiles)], axis=1)
        outs.append(_dot(hr.astype(BF16), cre_ref[s]) + _dot(hi.astype(BF16), cmi_ref[s]))
    y = jnp.concatenate(outs, axis=1) + d_ref[0:1, :] * u
    y = (0.5 * y) * (1.0 + jnp.tanh(y * (GELU_C + (GELU_C * 0.044715) * (y * y))))
    y = y * _sigmoid(_dot(y.astype(BF16), gluw_ref[...]) + glub_ref[0:1, :])
    yz_ref[...] = (y * zg).reshape(seqs, tile, D_MODEL)

    @pl.when(t == pl.num_programs(1) - 1)
    def _():
        sr_out_ref[...] = cr_ref[...]
        si_out_ref[...] = ci_ref[...]


def _s5_mixer(x, mod, g_pre, wu, wg, bbr, bbi, cre, cmi, abr, abi, d_skip, glu_w, glu_b, s0r, s0i, *, sel, tile,
              seqs):
    bsz, tp, _ = x.shape
    assert tp % tile == 0 and tile % SUBLANES == 0 and bsz % seqs == 0
    nt = tp // tile
    weight, state, mod_spec = _specs(*sel, seqs=seqs)
    state_out = pl.BlockSpec((seqs, S5_TILES, LANES), lambda b, t: (b, 0, 0))
    kern = functools.partial(_s5_kernel, tile=tile)
    return pl.pallas_call(
        kern,
        grid=(bsz // seqs, nt),
        in_specs=[
            pl.BlockSpec((seqs, tile, D_MODEL), lambda b, t: (b, t, 0)),
            mod_spec(mod),
            weight(g_pre), weight(wu), weight(wg), weight(bbr), weight(bbi), weight(cre), weight(cmi),
            weight(abr), weight(abi), weight(d_skip), weight(glu_w), weight(glu_b), state(s0r), state(s0i),
        ],
        out_specs=[pl.BlockSpec((seqs, tile, D_MODEL), lambda b, t: (b, t, 0)), state_out, state_out],
        out_shape=[
            jax.ShapeDtypeStruct((bsz, tp, D_MODEL), F32),
            jax.ShapeDtypeStruct((bsz, S5_TILES, LANES), F32),
            jax.ShapeDtypeStruct((bsz, S5_TILES, LANES), F32),
        ],
        scratch_shapes=[
            pltpu.VMEM((seqs * tile * S5_ROW_STRIDE, LANES), F32),
            pltpu.VMEM((seqs * tile * S5_ROW_STRIDE, LANES), F32),
            pltpu.VMEM((seqs, S5_TILES, LANES), F32),
            pltpu.VMEM((seqs, S5_TILES, LANES), F32),
        ],
        compiler_params=pltpu.CompilerParams(
            dimension_semantics=("arbitrary", "arbitrary"), vmem_limit_bytes=VMEM_LIMIT),
        name="s5_mixer",
    )(x, mod, g_pre, wu, wg, bbr, bbi, cre, cmi, abr, abi, d_skip, glu_w, glu_b, s0r, s0i)


def _lru_kernel(x_ref, mod_ref, g_ref, wx_ref, wg_ref, cw_ref, cb_ref, wa4_ref, ba_ref, wx4_ref, bx_ref, lam_ref,
                conv0_ref, h0_ref,
                yz_ref, conv_out_ref, h_out_ref,
                xbuf_ref, a_ref, b_ref, hc_ref, *, tile):
    t = pl.program_id(1)
    pad = SUBLANES
    hist = CONV_W - 1
    seqs = x_ref.shape[0]

    @pl.when(t == 0)
    def _():
        xbuf_ref[:, pad - hist:pad, :] = conv0_ref[...]
        hc_ref[...] = h0_ref[...]

    h = _pre_norm(x_ref, g_ref, mod_ref)
    xl = _dot(h, wx_ref[...])
    zg = _silu(_dot(h, wg_ref[...]))
    xcs, tails = [], []
    for s in range(seqs):
        xs = xl[s * tile:(s + 1) * tile, :]
        xbuf_ref[s, pad:pad + tile, :] = xs
        acc = cb_ref[0:1, :] + xs * cw_ref[CONV_W - 1:CONV_W, :]
        for j in range(hist):
            acc = acc + xbuf_ref[s, pad - hist + j:pad - hist + j + tile, :] * cw_ref[j:j + 1, :]
        tails.append(xbuf_ref[s, pad + tile - hist:pad + tile, :])
        xbuf_ref[s, pad - hist:pad, :] = tails[s]
        xcs.append(acc)
    xc = jnp.concatenate(xcs, axis=0)

    ga, gx = [], []
    blk = MXU_TILE
    for s in range(D_MODEL // blk):
        xb = xc[:, blk * s:blk * (s + 1)].astype(BF16)
        ga.append(_dot(xb, wa4_ref[s]))
        gx.append(_dot(xb, wx4_ref[s]))
    gate_r = _sigmoid(jnp.concatenate(ga, axis=1) + ba_ref[0:1, :])
    gate_i = _sigmoid(jnp.concatenate(gx, axis=1) + bx_ref[0:1, :])
    log_a = -LRU_C * gate_r * _softplus(-lam_ref[0:1, :])
    a = jnp.exp(log_a)
    a_ref[...] = a
    var = -jnp.tanh(log_a) * (a * a + 1.0)
    b_ref[...] = jnp.where(var > 0.0, var * lax.rsqrt(var), 0.0) * (gate_i * xc)

    row8 = lax.broadcasted_iota(jnp.int32, (SUBLANES, 1), 0)

    def row_body(g, carry):
        rs = pl.ds(pl.multiple_of(g * SUBLANES, SUBLANES), SUBLANES)
        av = a_ref[rs, :]
        bv = b_ref[rs, :]
        for sh in (1, 2, 4):
            keep = row8 >= sh
            ash = jnp.where(keep, pltpu.roll(av, sh, 0), 1.0)
            bsh = jnp.where(keep, pltpu.roll(bv, sh, 0), 0.0)
            bv = bv + av * bsh
            av = av * ash
        hs = bv + av * carry
        b_ref[rs, :] = hs
        return hs[SUBLANES - 1:SUBLANES, :]

    groups = tile // SUBLANES
    for s in range(seqs):
        hc_ref[s] = lax.fori_loop(s * groups, (s + 1) * groups, row_body, hc_ref[s], unroll=True)
    yz_ref[...] = (b_ref[...] * zg).reshape(seqs, tile, D_MODEL)

    @pl.when(t == pl.num_programs(1) - 1)
    def _():
        for s in range(seqs):
            conv_out_ref[s] = tails[s]
        h_out_ref[...] = hc_ref[...]


def _lru_mixer(x, mod, g_pre, wx, wg, conv_w, conv_b, wa4, ba, wx4, bx, lam, conv0, h0, *, sel, tile, seqs):
    bsz, tp, _ = x.shape
    assert tp % tile == 0 and tile % SUBLANES == 0 and tile >= SUBLANES and bsz % seqs == 0
    nt = tp // tile
    weight, state, mod_spec = _specs(*sel, seqs=seqs)
    kern = functools.partial(_lru_kernel, tile=tile)
    return pl.pallas_call(
        kern,
        grid=(bsz // seqs, nt),
        in_specs=[
            pl.BlockSpec((seqs, tile, D_MODEL), lambda b, t: (b, t, 0)),
            mod_spec(mod),
            weight(g_pre), weight(wx), weight(wg), weight(conv_w), weight(conv_b), weight(wa4), weight(ba),
            weight(wx4), weight(bx), weight(lam), state(conv0), state(h0),
        ],
        out_specs=[
            pl.BlockSpec((seqs, tile, D_MODEL), lambda b, t: (b, t, 0)),
            pl.BlockSpec((seqs, CONV_W - 1, D_MODEL), lambda b, t: (b, 0, 0)),
            pl.BlockSpec((seqs, 1, D_MODEL), lambda b, t: (b, 0, 0)),
        ],
        out_shape=[
            jax.ShapeDtypeStruct((bsz, tp, D_MODEL), F32),
            jax.ShapeDtypeStruct((bsz, CONV_W - 1, D_MODEL), F32),
            jax.ShapeDtypeStruct((bsz, 1, D_MODEL), F32),
        ],
        scratch_shapes=[
            pltpu.VMEM((seqs, tile + SUBLANES, D_MODEL), F32),
            pltpu.VMEM((seqs * tile, D_MODEL), F32),
            pltpu.VMEM((seqs * tile, D_MODEL), F32),
            pltpu.VMEM((seqs, 1, D_MODEL), F32),
        ],
        compiler_params=pltpu.CompilerParams(
            dimension_semantics=("arbitrary", "arbitrary"), vmem_limit_bytes=VMEM_LIMIT),
        name="rglru_mixer",
    )(x, mod, g_pre, wx, wg, conv_w, conv_b, wa4, ba, wx4, bx, lam, conv0, h0)


def _merge_kernel(x_ref, mod_ref, g_ref, gpost_ref, wm_ref, wo_ref, yr_ref, ys_ref, yl_ref, o_ref):
    seqs, tile, _ = x_ref.shape
    rows = lambda ref: ref[...].reshape(seqs * tile, D_MODEL)
    m = _sigmoid(_dot(_pre_norm(x_ref, g_ref, mod_ref), wm_ref[...]))
    merged = (m[:, :D_MODEL] * rows(yr_ref) + m[:, D_MODEL:2 * D_MODEL] * rows(ys_ref)
              + m[:, 2 * D_MODEL:] * rows(yl_ref))
    o = _dot(merged.astype(BF16), wo_ref[...])
    ms = jnp.mean(o * o, axis=-1, keepdims=True)
    o = (o * lax.rsqrt(ms + RMS_EPS)) * gpost_ref[...]
    for s in range(seqs):
        o_ref[s] = x_ref[s] + mod_ref[s][2:3, :] * o[s * tile:(s + 1) * tile, :]


def _merge_out(x, mod, g_pre, g_post, wm, wo, yr, ys, yl, *, sel, tile, seqs):
    bsz, tp, _ = x.shape
    assert tp % tile == 0 and bsz % seqs == 0
    nt = tp // tile
    act = pl.BlockSpec((seqs, tile, D_MODEL), lambda b, t: (b, t, 0))
    weight, _, mod_spec = _specs(*sel, seqs=seqs)
    return pl.pallas_call(
        _merge_kernel,
        grid=(bsz // seqs, nt),
        in_specs=[act, mod_spec(mod), weight(g_pre), weight(g_post), weight(wm), weight(wo), act, act, act],
        out_specs=act,
        out_shape=jax.ShapeDtypeStruct((bsz, tp, D_MODEL), F32),
        compiler_params=pltpu.CompilerParams(
            dimension_semantics=("arbitrary", "arbitrary"), vmem_limit_bytes=VMEM_LIMIT),
        name="merge_out",
    )(x, mod, g_pre, g_post, wm, wo, yr, ys, yl)


def _block_diag_expand(w, per):
    depth, n, a, b = w.shape
    w = w.reshape(depth, n // per, per, a, b)
    eye = jnp.eye(per, dtype=w.dtype)
    return jnp.einsum("lsgab,gh->lsgahb", w, eye).reshape(depth, n // per, per * a, per * b)


def _layer(x, mod, state, prm, *, sel, t_valid, tiles):
    shift_row, wkv, s_re, s_im, lru_h, lru_conv = state
    bsz = x.shape[0]
    g_pre = prm["g_pre"]

    xr = x
    if x.shape[1] % tiles["rwkv"] != 0:
        xr = jnp.pad(x, ((0, 0), (0, tiles["rwkv"] - x.shape[1] % tiles["rwkv"]), (0, 0)))
    yz_r, shift_new, wkv_new = _rwkv_mixer(
        xr, mod, g_pre, prm["w_rwkv"], prm["w_rwkv_g"], prm["rwkv_vecs"], prm["w2a"], shift_row, wkv,
        sel=sel, tile=tiles["rwkv"], t_valid=t_valid, seqs=tiles["rwkv_seqs"])
    yz_r = yz_r[:, :t_valid]
    yz_s, s_re_new, s_im_new = _s5_mixer(
        x, mod, g_pre, prm["w_s5"], prm["w_s5_g"], prm["bbr"], prm["bbi"], prm["cre"], prm["cmi"],
        prm["abr"], prm["abi"], prm["s5_d"], prm["glu_w"], prm["glu_b"], s_re, s_im, sel=sel, tile=tiles["s5"],
        seqs=tiles["seqs"])
    yz_l, conv_new, h_new = _lru_mixer(
        x, mod, g_pre, prm["w_lru"], prm["w_lru_g"], prm["conv_w"], prm["conv_b"], prm["wa4"], prm["ba"],
        prm["wx4"], prm["bx"], prm["lam"], lru_conv, lru_h, sel=sel, tile=tiles["lru"], seqs=tiles["seqs"])
    x_new = _merge_out(x, mod, g_pre, prm["g_post"], prm["w_merge"], prm["w_out"], yz_r, yz_s, yz_l,
                       sel=sel, tile=tiles["merge"], seqs=tiles["seqs"])
    new_state = (shift_new.reshape(bsz, SHIFT_W), wkv_new.reshape(bsz, HEADS, HEAD, HEAD),
                 s_re_new.reshape(bsz, S5_GROUPS, S5_STATES), s_im_new.reshape(bsz, S5_GROUPS, S5_STATES),
                 h_new.reshape(bsz, D_MODEL), conv_new)
    return x_new, new_state


def _stacked_params(w_in, w_out, norm_pre, norm_post, rwkv, s5, lru):
    depth = w_in.shape[0]
    d = D_MODEL
    wb = w_in.astype(BF16)
    o = SHIFT_W
    row = lambda a: a.reshape(depth, 1, -1)
    row8 = lambda a: jnp.pad(row(a), ((0, 0), (0, SUBLANES - 1), (0, 0)))
    mu, w0, w2, a0, a2, k_k, k_a, r_k, ln_g, ln_b = rwkv
    a_re, a_im, log_step, b_re, b_im, c_re, c_im, s5_d, glu_w, glu_b = s5
    conv_w, conv_b, wa, ba, wx, bx, lam = lru
    zeros = jnp.zeros((depth, LORA, d), F32)
    w2a = jnp.concatenate([jnp.concatenate([w2, zeros], axis=2), jnp.concatenate([zeros, a2], axis=2)], axis=1)
    abr, abi, bbr, bbi = _s5_prep(a_re, a_im, log_step, b_re, b_im)
    per = S5_GROUPS // S5_SETS
    gate_per = MXU_TILE // LRU_BLOCK
    return dict(
        g_pre=row8(norm_pre), g_post=row(norm_post),
        w_rwkv=wb[:, :, :o], w_rwkv_g=wb[:, :, o:o + d],
        w_s5=wb[:, :, o + d:o + 2 * d], w_s5_g=wb[:, :, o + 2 * d:o + 3 * d],
        w_lru=wb[:, :, o + 3 * d:o + 4 * d], w_lru_g=wb[:, :, o + 4 * d:o + 5 * d],
        w_merge=wb[:, :, o + 5 * d:], w_out=w_out.astype(BF16),
        rwkv_vecs=(row(mu), row(w0), row(a0), row(k_k), row(k_a), row(r_k), row(ln_g), row(ln_b)),
        w2a=w2a.astype(BF16),
        bbr=_block_diag_expand(bbr, S5_IN_GROUPS).astype(BF16),
        bbi=_block_diag_expand(bbi, S5_IN_GROUPS).astype(BF16),
        cre=_block_diag_expand(jnp.swapaxes(c_re, 2, 3), per).astype(BF16),
        cmi=_block_diag_expand(-jnp.swapaxes(c_im, 2, 3), per).astype(BF16),
        abr=abr.reshape(depth, S5_TILES, LANES), abi=abi.reshape(depth, S5_TILES, LANES),
        s5_d=row8(s5_d), glu_w=glu_w.astype(BF16), glu_b=row8(glu_b),
        conv_w=conv_w, conv_b=row8(conv_b),
        wa4=_block_diag_expand(wa, gate_per).astype(BF16), ba=row8(ba),
        wx4=_block_diag_expand(wx, gate_per).astype(BF16), bx=row8(bx), lam=row8(lam),
    )


def _state_layout(shift, wkv, s_re, s_im, lru_h, lru_conv):
    n, bsz = shift.shape[:2]
    return (shift.reshape(n, bsz, 1, SHIFT_W), wkv.reshape(n, bsz, PAIRS, LANES, HEAD),
            s_re.reshape(n, bsz, S5_TILES, LANES), s_im.reshape(n, bsz, S5_TILES, LANES),
            lru_h.reshape(n, bsz, 1, D_MODEL), lru_conv)


def _tiles(t, bsz):
    pick = lambda want: want if t % want == 0 else t
    share = lambda rows, limit: max(n for n in range(1, bsz + 1) if bsz % n == 0 and n * rows <= max(limit, rows))
    seqs = share(t, SHORT_ROWS) if t < SHORT_ROWS else 1
    rwkv = RWKV_TILE if t % RWKV_TILE == 0 else RWKV_CHUNK
    rwkv_seqs = share(RWKV_CHUNK, RWKV_TILE) if t <= RWKV_CHUNK else 1
    return dict(rwkv=rwkv, rwkv_seqs=rwkv_seqs, s5=pick(S5_TILE), lru=pick(WIDE_TILE), merge=pick(WIDE_TILE),
                seqs=seqs)


def kernel(x_prompt, x_sample, state_rwkv_shift, state_rwkv_wkv, state_s5_re, state_s5_im, state_lru_h,
           state_lru_conv, c_prompt, c_sample, ada_w, ada_b, norm_pre, norm_post, w_in, w_out, rwkv_mu, rwkv_w0,
           rwkv_w2, rwkv_a0, rwkv_a2, rwkv_k_k, rwkv_k_a, rwkv_r_k, rwkv_ln_g, rwkv_ln_b, s5_a_re, s5_a_im,
           s5_log_step, s5_b_re, s5_b_im, s5_c_re, s5_c_im, s5_d, s5_glu_w, s5_glu_b, lru_conv_w, lru_conv_b,
           lru_wa, lru_ba, lru_wx, lru_bx, lru_lambda):
    depth = w_in.shape[0]
    bp, tp, _ = x_prompt.shape
    bs, ts, _ = x_sample.shape
    c_all = jnp.concatenate([c_sample, c_prompt], axis=0)
    pad = (-c_all.shape[0]) % SUBLANES
    c_all = jnp.pad(c_all, ((0, pad), (0, 0)))
    mod_all = _modulation(c_all, ada_w, ada_b)
    mod_all = mod_all.reshape(depth, c_all.shape[0], 3, D_MODEL)
    rwkv = (rwkv_mu, rwkv_w0, rwkv_w2, rwkv_a0, rwkv_a2, rwkv_k_k, rwkv_k_a,
            rwkv_r_k.reshape(depth, D_MODEL), rwkv_ln_g, rwkv_ln_b)
    s5 = (s5_a_re, s5_a_im, s5_log_step, s5_b_re, s5_b_im, s5_c_re, s5_c_im, s5_d, s5_glu_w, s5_glu_b)
    lru = (lru_conv_w, lru_conv_b, lru_wa, lru_ba, lru_wx, lru_bx, lru_lambda)
    prm = _stacked_params(w_in, w_out, norm_pre, norm_post, rwkv, s5, lru)
    zero = _state_layout(jnp.zeros((1, bp, SHIFT_W), F32), jnp.zeros((1, bp, HEADS, HEAD, HEAD), F32),
                         jnp.zeros((1, bp, S5_GROUPS, S5_STATES), F32),
                         jnp.zeros((1, bp, S5_GROUPS, S5_STATES), F32),
                         jnp.zeros((1, bp, D_MODEL), F32), jnp.zeros((1, bp, CONV_W - 1, D_MODEL), F32))
    st_in = _state_layout(state_rwkv_shift, state_rwkv_wkv, state_s5_re, state_s5_im, state_lru_h,
                          state_lru_conv)

    xp, xs = x_prompt, x_sample
    new_p = [[] for _ in range(6)]
    new_s = [[] for _ in range(6)]
    for l in range(depth):
        xp, st_p = _layer(xp, mod_all, zero, prm, sel=(l, 0, bs), t_valid=tp, tiles=_tiles(tp, bp))
        xs, st_s = _layer(xs, mod_all, st_in, prm, sel=(l, l, 0), t_valid=ts, tiles=_tiles(ts, bs))
        for i in range(6):
            new_p[i].append(st_p[i])
            new_s[i].append(st_s[i])
    sp = [jnp.stack(z, axis=0) for z in new_p]
    ss = [jnp.stack(z, axis=0) for z in new_s]
    return (xp, xs, sp[0], sp[1], sp[2], sp[3], sp[4], sp[5], ss[0], ss[1], ss[2], ss[3], ss[4], ss[5])
```
